```python
import math
import jax, jax.numpy as jnp
from jax import lax
import numpy as np

D_MODEL = 1024
BATCH = 16
SEQ = 2048
DEPTH = 4

MEM_LEN = 256
RMS_EPS = 1e-6
SB_HEADS = 16
SB_HEAD_DIM = 64
SB_WIDTH = SB_HEADS * SB_HEAD_DIM
SB_BLOCK = 128
SSM_EXPAND = 2
SSM_INNER = SSM_EXPAND * D_MODEL
SSM_HEAD_DIM = 64
SSM_HEADS = SSM_INNER // SSM_HEAD_DIM
SSM_GROUPS = 4
SSM_STATE = 128
SSM_CONV = 4
SSM_CHUNK = 128
SSM_CONV_DIM = SSM_INNER + 2 * SSM_GROUPS * SSM_STATE
XA_HEADS = 4
XA_HEAD_DIM = D_MODEL // XA_HEADS
FFN_HIDDEN = ((8 * D_MODEL + 767) // 768) * 256
IN_SIZES = (SB_WIDTH, SB_WIDTH, SB_WIDTH, SSM_INNER, SSM_CONV_DIM, SSM_HEADS, D_MODEL, D_MODEL)
IN_WIDTH = 3 * SB_WIDTH + SSM_INNER + SSM_CONV_DIM + SSM_HEADS + 2 * D_MODEL

kernel_name = "hybrid_stickbreak_ssd_gated_block"


def _split(a, sizes):
    idx = tuple(int(i) for i in np.cumsum(sizes)[:-1])
    return jnp.split(a, idx, axis=-1)


def rms_norm(x, g):
    xf = x.astype(jnp.float32)
    y = xf * lax.rsqrt(jnp.mean(xf * xf, axis=-1, keepdims=True) + RMS_EPS)
    return (y * g.astype(jnp.float32)).astype(x.dtype)


def stick_breaking_attention(q, k, v):
    bsz, seq = q.shape[0], q.shape[1]
    qf = jnp.swapaxes(q.astype(jnp.float32), 1, 2) * (SB_HEAD_DIM ** -0.5)
    kf = jnp.swapaxes(k.astype(jnp.float32), 1, 2)
    vf = jnp.swapaxes(v.astype(jnp.float32), 1, 2)
    outs = []
    for blk in range(seq // SB_BLOCK):
        t0 = blk * SB_BLOCK
        t1 = t0 + SB_BLOCK
        z = jnp.einsum('bhtd,bhsd->bhts', qf[:, :, t0:t1], kf[:, :, :t1])
        causal = jnp.arange(t1)[None, :] < (t0 + jnp.arange(SB_BLOCK))[:, None]
        log_1mb = jnp.where(causal, jax.nn.log_sigmoid(-z), 0.0)
        tail = lax.cumsum(log_1mb, axis=3, reverse=True)
        w = jnp.where(causal, jnp.exp(z + tail), 0.0)
        outs.append(jnp.einsum('bhts,bhsd->bhtd', w, vf[:, :, :t1]))
    o = jnp.concatenate(outs, axis=2)
    return jnp.swapaxes(o, 1, 2).reshape(bsz, seq, SB_WIDTH)


def ssd_chunked(x, dt, a, bm, cm):
    b, l, h, p = x.shape
    g, n = bm.shape[2], bm.shape[3]
    e = h // g
    c = l // SSM_CHUNK
    xc = (x * dt[..., None]).reshape(b, c, SSM_CHUNK, g, e, p)
    a_dt = (dt * a).reshape(b, c, SSM_CHUNK, g, e).transpose(0, 1, 3, 4, 2)
    a_cs = jnp.cumsum(a_dt, axis=-1)
    bc = bm.reshape(b, c, SSM_CHUNK, g, n)
    cc = cm.reshape(b, c, SSM_CHUNK, g, n)
    tril = jnp.tril(jnp.ones((SSM_CHUNK, SSM_CHUNK), dtype=bool))
    decay = jnp.exp(jnp.where(tril, a_cs[..., :, None] - a_cs[..., None, :], -jnp.inf))
    cb = jnp.einsum('bclgn,bcsgn->bcgls', cc, bc)
    y_diag = jnp.einsum('bcgls,bcgels,bcsgep->bclgep', cb, decay, xc)
    decay_to_end = jnp.exp(a_cs[..., -1:] - a_cs)
    states = jnp.einsum('bclgn,bcgel,bclgep->bcgepn', bc, decay_to_end, xc)
    chunk_decay = jnp.exp(a_cs[..., -1])

    def step(prev, inp):
        st, dec = inp
        return prev * dec[..., None, None] + st, prev

    init = jnp.zeros((b, g, e, p, n), dtype=states.dtype)
    _, prev_states = lax.scan(step, init, (jnp.moveaxis(states, 1, 0), jnp.moveaxis(chunk_decay, 1, 0)))
    prev_states = jnp.moveaxis(prev_states, 0, 1)
    y_off = jnp.einsum('bclgn,bcgepn,bcgel->bclgep', cc, prev_states, jnp.exp(a_cs))
    return (y_diag + y_off).reshape(b, l, h, p)


def ssd_branch(z, xbc, dt_raw, conv_w, conv_b, dt_bias, a_log, d_skip, g_norm):
    bsz, seq = xbc.shape[0], xbc.shape[1]
    xbc = lax.conv_general_dilated(
        xbc, conv_w[:, None, :].astype(xbc.dtype), window_strides=(1,),
        padding=[(SSM_CONV - 1, 0)], dimension_numbers=('NWC', 'WIO', 'NWC'),
        feature_group_count=SSM_CONV_DIM)
    xbc = jax.nn.silu(xbc.astype(jnp.float32) + conv_b.astype(jnp.float32))
    xs, bm, cm = _split(xbc, (SSM_INNER, SSM_GROUPS * SSM_STATE, SSM_GROUPS * SSM_STATE))
    dt = jax.nn.softplus(dt_raw.astype(jnp.float32) + dt_bias.astype(jnp.float32))
    a = -jnp.exp(a_log.astype(jnp.float32))
    xh = xs.reshape(bsz, seq, SSM_HEADS, SSM_HEAD_DIM)
    y = ssd_chunked(xh, dt, a,
                    bm.reshape(bsz, seq, SSM_GROUPS, SSM_STATE),
                    cm.reshape(bsz, seq, SSM_GROUPS, SSM_STATE))
    y = y + d_skip.astype(jnp.float32)[:, None] * xh
    y = y.reshape(bsz, seq, SSM_INNER) * jax.nn.silu(z.astype(jnp.float32))
    yg = y.reshape(bsz, seq, SSM_GROUPS, SSM_INNER // SSM_GROUPS)
    yg = yg * lax.rsqrt(jnp.mean(yg * yg, axis=-1, keepdims=True) + RMS_EPS)
    return yg.reshape(bsz, seq, SSM_INNER) * g_norm.astype(jnp.float32)


def memory_cross_attention(h, mem_n, w_xq, w_xkv, w_xo):
    bsz, seq = h.shape[0], h.shape[1]
    q = (h @ w_xq).reshape(bsz, seq, XA_HEADS, XA_HEAD_DIM)
    k, v = _split(mem_n @ w_xkv, (D_MODEL, D_MODEL))
    k = k.reshape(bsz, MEM_LEN, XA_HEADS, XA_HEAD_DIM)
    v = v.reshape(bsz, MEM_LEN, XA_HEADS, XA_HEAD_DIM)
    s = jnp.einsum('bshd,bmhd->bhsm', q.astype(jnp.float32), k.astype(jnp.float32)) * (XA_HEAD_DIM ** -0.5)
    p = jax.nn.softmax(s, axis=-1)
    o = jnp.einsum('bhsm,bmhd->bshd', p, v.astype(jnp.float32)).reshape(bsz, seq, D_MODEL)
    return o.astype(h.dtype) @ w_xo


def swiglu(h, w_gu, w_down):
    gate, up = _split(h @ w_gu, (FFN_HIDDEN, FFN_HIDDEN))
    return (jax.nn.silu(gate) * up) @ w_down


def _fwd_setup_inputs(seed: int = 0) -> dict:
    key = jax.random.key(seed)
    ks = jax.random.split(key, 32)
    f32 = jnp.float32

    def dense(k, shape, fan_in):
        return jax.random.normal(k, shape, f32) * (fan_in ** -0.5)

    def gain(k, shape):
        return 1.0 + 0.02 * jax.random.normal(k, shape, f32)

    dt0 = jnp.exp(jax.random.uniform(ks[5], (DEPTH, SSM_HEADS), f32,
                                     minval=math.log(1e-3), maxval=math.log(1e-1)))
    dt_bias = dt0 + jnp.log(-jnp.expm1(-dt0))
    a_log = jnp.log(jax.random.uniform(ks[6], (DEPTH, SSM_HEADS), f32, minval=1.0, maxval=16.0))
    return {
        "x": jax.random.normal(ks[0], (BATCH, SEQ, D_MODEL), f32),
        "mem": jax.random.normal(ks[1], (BATCH, MEM_LEN, D_MODEL), f32),
        "g_pre_mix": gain(ks[2], (DEPTH, D_MODEL)),
        "w_in": dense(ks[3], (DEPTH, D_MODEL, IN_WIDTH), D_MODEL),
        "conv_w": dense(ks[4], (DEPTH, SSM_CONV, SSM_CONV_DIM), SSM_CONV),
        "conv_b": 0.01 * jax.random.normal(ks[7], (DEPTH, SSM_CONV_DIM), f32),
        "dt_bias": dt_bias,
        "a_log": a_log,
        "d_skip": 1.0 + 0.1 * jax.random.normal(ks[8], (DEPTH, SSM_HEADS), f32),
        "g_ssm_norm": gain(ks[9], (DEPTH, SSM_INNER)),
        "w_br_att": dense(ks[10], (DEPTH, SB_WIDTH, D_MODEL), SB_WIDTH),
        "w_br_ssm": dense(ks[11], (DEPTH, SSM_INNER, D_MODEL), SSM_INNER),
        "w_mix_out": dense(ks[12], (DEPTH, D_MODEL, D_MODEL), D_MODEL),
        "g_post_mix": gain(ks[13], (DEPTH, D_MODEL)),
        "g_pre_xa": gain(ks[14], (DEPTH, D_MODEL)),
        "g_mem": gain(ks[15], (DEPTH, D_MODEL)),
        "w_xq": dense(ks[16], (DEPTH, D_MODEL, D_MODEL), D_MODEL),
        "w_xkv": dense(ks[17], (DEPTH, D_MODEL, 2 * D_MODEL), D_MODEL),
        "w_xo": dense(ks[18], (DEPTH, D_MODEL, D_MODEL), D_MODEL),
        "g_post_xa": gain(ks[19], (DEPTH, D_MODEL)),
        "g_pre_ffn": gain(ks[20], (DEPTH, D_MODEL)),
        "w_gu": dense(ks[21], (DEPTH, D_MODEL, 2 * FFN_HIDDEN), D_MODEL),
        "w_down": dense(ks[22], (DEPTH, FFN_HIDDEN, D_MODEL), FFN_HIDDEN),
        "g_post_ffn": gain(ks[23], (DEPTH, D_MODEL)),
    }


def _fwd_reference(x, mem, g_pre_mix, w_in, conv_w, conv_b, dt_bias, a_log, d_skip, g_ssm_norm,
              w_br_att, w_br_ssm, w_mix_out, g_post_mix, g_pre_xa, g_mem, w_xq, w_xkv, w_xo,
              g_post_xa, g_pre_ffn, w_gu, w_down, g_post_ffn):
    bsz, seq = x.shape[0], x.shape[1]
    for l in range(DEPTH):
        h = rms_norm(x, g_pre_mix[l])
        q, k, v, z, xbc, dt_raw, ga, gs = _split(h @ w_in[l], IN_SIZES)
        o_att = stick_breaking_attention(
            q.reshape(bsz, seq, SB_HEADS, SB_HEAD_DIM),
            k.reshape(bsz, seq, SB_HEADS, SB_HEAD_DIM),
            v.reshape(bsz, seq, SB_HEADS, SB_HEAD_DIM)).astype(x.dtype)
        o_ssm = ssd_branch(z, xbc, dt_raw, conv_w[l], conv_b[l], dt_bias[l], a_log[l],
                           d_skip[l], g_ssm_norm[l]).astype(x.dtype)
        merged = jax.nn.sigmoid(ga) * (o_att @ w_br_att[l]) + jax.nn.sigmoid(gs) * (o_ssm @ w_br_ssm[l])
        x = x + rms_norm(merged @ w_mix_out[l], g_post_mix[l])
        h = rms_norm(x, g_pre_xa[l])
        mem_n = rms_norm(mem, g_mem[l])
        x = x + rms_norm(memory_cross_attention(h, mem_n, w_xq[l], w_xkv[l], w_xo[l]), g_post_xa[l])
        h = rms_norm(x, g_pre_ffn[l])
        x = x + rms_norm(swiglu(h, w_gu[l], w_down[l]), g_post_ffn[l])
    return x


import jax as _jax
import jax.numpy as _jnp

TWIN_FORMAT = 'train_step'
FWD_PARAMS = ['x', 'mem', 'g_pre_mix', 'w_in', 'conv_w', 'conv_b', 'dt_bias', 'a_log', 'd_skip', 'g_ssm_norm', 'w_br_att', 'w_br_ssm', 'w_mix_out', 'g_post_mix', 'g_pre_xa', 'g_mem', 'w_xq', 'w_xkv', 'w_xo', 'g_post_xa', 'g_pre_ffn', 'w_gu', 'w_down', 'g_post_ffn']
TWIN_WEIGHTS = ['g_pre_mix', 'w_in', 'conv_w', 'conv_b', 'dt_bias', 'a_log', 'd_skip', 'g_ssm_norm', 'w_br_att', 'w_br_ssm', 'w_mix_out', 'g_post_mix', 'g_pre_xa', 'g_mem', 'w_xq', 'w_xkv', 'w_xo', 'g_post_xa', 'g_pre_ffn', 'w_gu', 'w_down', 'g_post_ffn']
TWIN_DIFF_INPUT = 'x'
TWIN_INPUTS = ['x', 'mem', 'g_pre_mix', 'w_in', 'conv_w', 'conv_b', 'dt_bias', 'a_log', 'd_skip', 'g_ssm_norm', 'w_br_att', 'w_br_ssm', 'w_mix_out', 'g_post_mix', 'g_pre_xa', 'g_mem', 'w_xq', 'w_xkv', 'w_xo', 'g_post_xa', 'g_pre_ffn', 'w_gu', 'w_down', 'g_post_ffn', 'loss_target', 'm_g_pre_mix', 'm_w_in', 'm_conv_w', 'm_conv_b', 'm_dt_bias', 'm_a_log', 'm_d_skip', 'm_g_ssm_norm', 'm_w_br_att', 'm_w_br_ssm', 'm_w_mix_out', 'm_g_post_mix', 'm_g_pre_xa', 'm_g_mem', 'm_w_xq', 'm_w_xkv', 'm_w_xo', 'm_g_post_xa', 'm_g_pre_ffn', 'm_w_gu', 'm_w_down', 'm_g_post_ffn', 'v_g_pre_mix', 'v_w_in', 'v_conv_w', 'v_conv_b', 'v_dt_bias', 'v_a_log', 'v_d_skip', 'v_g_ssm_norm', 'v_w_br_att', 'v_w_br_ssm', 'v_w_mix_out', 'v_g_post_mix', 'v_g_pre_xa', 'v_g_mem', 'v_w_xq', 'v_w_xkv', 'v_w_xo', 'v_g_post_xa', 'v_g_pre_ffn', 'v_w_gu', 'v_w_down', 'v_g_post_ffn']
TWIN_OUTPUTS = ['loss', 'grad_x', 'grad_g_pre_mix', 'grad_w_in', 'grad_conv_w', 'grad_conv_b', 'grad_dt_bias', 'grad_a_log', 'grad_d_skip', 'grad_g_ssm_norm', 'grad_w_br_att', 'grad_w_br_ssm', 'grad_w_mix_out', 'grad_g_post_mix', 'grad_g_pre_xa', 'grad_g_mem', 'grad_w_xq', 'grad_w_xkv', 'grad_w_xo', 'grad_g_post_xa', 'grad_g_pre_ffn', 'grad_w_gu', 'grad_w_down', 'grad_g_post_ffn', 'delta_g_pre_mix', 'delta_w_in', 'delta_conv_w', 'delta_conv_b', 'delta_dt_bias', 'delta_a_log', 'delta_d_skip', 'delta_g_ssm_norm', 'delta_w_br_att', 'delta_w_br_ssm', 'delta_w_mix_out', 'delta_g_post_mix', 'delta_g_pre_xa', 'delta_g_mem', 'delta_w_xq', 'delta_w_xkv', 'delta_w_xo', 'delta_g_post_xa', 'delta_g_pre_ffn', 'delta_w_gu', 'delta_w_down', 'delta_g_post_ffn', 'new_m_g_pre_mix', 'new_m_w_in', 'new_m_conv_w', 'new_m_conv_b', 'new_m_dt_bias', 'new_m_a_log', 'new_m_d_skip', 'new_m_g_ssm_norm', 'new_m_w_br_att', 'new_m_w_br_ssm', 'new_m_w_mix_out', 'new_m_g_post_mix', 'new_m_g_pre_xa', 'new_m_g_mem', 'new_m_w_xq', 'new_m_w_xkv', 'new_m_w_xo', 'new_m_g_post_xa', 'new_m_g_pre_ffn', 'new_m_w_gu', 'new_m_w_down', 'new_m_g_post_ffn', 'new_v_g_pre_mix', 'new_v_w_in', 'new_v_conv_w', 'new_v_conv_b', 'new_v_dt_bias', 'new_v_a_log', 'new_v_d_skip', 'new_v_g_ssm_norm', 'new_v_w_br_att', 'new_v_w_br_ssm', 'new_v_w_mix_out', 'new_v_g_post_mix', 'new_v_g_pre_xa', 'new_v_g_mem', 'new_v_w_xq', 'new_v_w_xkv', 'new_v_w_xo', 'new_v_g_post_xa', 'new_v_g_pre_ffn', 'new_v_w_gu', 'new_v_w_down', 'new_v_g_post_ffn']
TWIN_LEAF_KINDS = {'loss': 'loss', 'grad_x': 'grad_x', 'grad_g_pre_mix': 'grad_w', 'grad_w_in': 'grad_w', 'grad_conv_w': 'grad_w', 'grad_conv_b': 'grad_w', 'grad_dt_bias': 'grad_w', 'grad_a_log': 'grad_w', 'grad_d_skip': 'grad_w', 'grad_g_ssm_norm': 'grad_w', 'grad_w_br_att': 'grad_w', 'grad_w_br_ssm': 'grad_w', 'grad_w_mix_out': 'grad_w', 'grad_g_post_mix': 'grad_w', 'grad_g_pre_xa': 'grad_w', 'grad_g_mem': 'grad_w', 'grad_w_xq': 'grad_w', 'grad_w_xkv': 'grad_w', 'grad_w_xo': 'grad_w', 'grad_g_post_xa': 'grad_w', 'grad_g_pre_ffn': 'grad_w', 'grad_w_gu': 'grad_w', 'grad_w_down': 'grad_w', 'grad_g_post_ffn': 'grad_w', 'delta_g_pre_mix': 'delta_w', 'delta_w_in': 'delta_w', 'delta_conv_w': 'delta_w', 'delta_conv_b': 'delta_w', 'delta_dt_bias': 'delta_w', 'delta_a_log': 'delta_w', 'delta_d_skip': 'delta_w', 'delta_g_ssm_norm': 'delta_w', 'delta_w_br_att': 'delta_w', 'delta_w_br_ssm': 'delta_w', 'delta_w_mix_out': 'delta_w', 'delta_g_post_mix': 'delta_w', 'delta_g_pre_xa': 'delta_w', 'delta_g_mem': 'delta_w', 'delta_w_xq': 'delta_w', 'delta_w_xkv': 'delta_w', 'delta_w_xo': 'delta_w', 'delta_g_post_xa': 'delta_w', 'delta_g_pre_ffn': 'delta_w', 'delta_w_gu': 'delta_w', 'delta_w_down': 'delta_w', 'delta_g_post_ffn': 'delta_w', 'new_m_g_pre_mix': 'new_m', 'new_m_w_in': 'new_m', 'new_m_conv_w': 'new_m', 'new_m_conv_b': 'new_m', 'new_m_dt_bias': 'new_m', 'new_m_a_log': 'new_m', 'new_m_d_skip': 'new_m', 'new_m_g_ssm_norm': 'new_m', 'new_m_w_br_att': 'new_m', 'new_m_w_br_ssm': 'new_m', 'new_m_w_mix_out': 'new_m', 'new_m_g_post_mix': 'new_m', 'new_m_g_pre_xa': 'new_m', 'new_m_g_mem': 'new_m', 'new_m_w_xq': 'new_m', 'new_m_w_xkv': 'new_m', 'new_m_w_xo': 'new_m', 'new_m_g_post_xa': 'new_m', 'new_m_g_pre_ffn': 'new_m', 'new_m_w_gu': 'new_m', 'new_m_w_down': 'new_m', 'new_m_g_post_ffn': 'new_m', 'new_v_g_pre_mix': 'new_v', 'new_v_w_in': 'new_v', 'new_v_conv_w': 'new_v', 'new_v_conv_b': 'new_v', 'new_v_dt_bias': 'new_v', 'new_v_a_log': 'new_v', 'new_v_d_skip': 'new_v', 'new_v_g_ssm_norm': 'new_v', 'new_v_w_br_att': 'new_v', 'new_v_w_br_ssm': 'new_v', 'new_v_w_mix_out': 'new_v', 'new_v_g_post_mix': 'new_v', 'new_v_g_pre_xa': 'new_v', 'new_v_g_mem': 'new_v', 'new_v_w_xq': 'new_v', 'new_v_w_xkv': 'new_v', 'new_v_w_xo': 'new_v', 'new_v_g_post_xa': 'new_v', 'new_v_g_pre_ffn': 'new_v', 'new_v_w_gu': 'new_v', 'new_v_w_down': 'new_v', 'new_v_g_post_ffn': 'new_v'}


def _forward(args):
    return _fwd_reference(*[args[k] for k in FWD_PARAMS])


def _output_shape():
    out = _jax.eval_shape(lambda: _forward(_fwd_setup_inputs(0)))
    return out.shape, out.dtype

N_MICROBATCH = 1
ADAM_LR = 0.001
ADAM_B1 = 0.9
ADAM_B2 = 0.999
ADAM_EPS = 1e-08
ADAM_WD = 0.01
ADAM_STEP = 10
PER_EXAMPLE_BATCH_AXIS = {'x': 0, 'mem': 0, 'loss_target': 0}
SHARED_INPUTS = []
_WEIGHT_DTYPES = {'g_pre_mix': _jnp.float32, 'w_in': _jnp.float32, 'conv_w': _jnp.float32, 'conv_b': _jnp.float32, 'dt_bias': _jnp.float32, 'a_log': _jnp.float32, 'd_skip': _jnp.float32, 'g_ssm_norm': _jnp.float32, 'w_br_att': _jnp.float32, 'w_br_ssm': _jnp.float32, 'w_mix_out': _jnp.float32, 'g_post_mix': _jnp.float32, 'g_pre_xa': _jnp.float32, 'g_mem': _jnp.float32, 'w_xq': _jnp.float32, 'w_xkv': _jnp.float32, 'w_xo': _jnp.float32, 'g_post_xa': _jnp.float32, 'g_pre_ffn': _jnp.float32, 'w_gu': _jnp.float32, 'w_down': _jnp.float32, 'g_post_ffn': _jnp.float32}
MOMENT_SCALE = {'g_pre_mix': 5.573269e+00, 'w_in': 1.705942e+00, 'conv_w': 1.942312e+00, 'conv_b': 5.445661e+00, 'dt_bias': 2.481202e+00, 'a_log': 1.018363e+01, 'd_skip': 9.341952e+00, 'g_ssm_norm': 3.183555e+00, 'w_br_att': 4.775136e+00, 'w_br_ssm': 4.477723e+00, 'w_mix_out': 6.305893e+00, 'g_post_mix': 3.221296e+01, 'g_pre_xa': 3.997305e+00, 'g_mem': 1.529148e+01, 'w_xq': 3.842002e+00, 'w_xkv': 1.046465e+01, 'w_xo': 1.453434e+01, 'g_post_xa': 3.518280e+01, 'g_pre_ffn': 4.477373e+00, 'w_gu': 1.937619e+00, 'w_down': 3.834252e+00, 'g_post_ffn': 3.196050e+01}


def _to_microbatches(a, axis):
    t = _jnp.moveaxis(a, axis, 0)
    t = t.reshape((N_MICROBATCH, t.shape[0] // N_MICROBATCH) + t.shape[1:])
    return _jnp.moveaxis(t, 1, axis + 1)


def setup_inputs(seed: int = 0) -> dict:
    inp = _fwd_setup_inputs(seed)
    key = _jax.random.fold_in(_jax.random.key(seed), 7919)
    shape, _ = _output_shape()
    out = dict(inp)
    out["loss_target"] = _jax.random.normal(_jax.random.fold_in(key, 0), shape, _jnp.float32)
    for i, name in enumerate(TWIN_WEIGHTS):
        w = inp[name].astype(_jnp.float32)
        if MOMENT_SCALE is None:
            s = _jnp.sqrt(_jnp.mean(_jnp.square(w)) + 1e-30)
        else:
            s = MOMENT_SCALE[name]
        km, kv = _jax.random.split(_jax.random.fold_in(key, i + 1))
        out[name] = w
        out["m_" + name] = s * _jax.random.normal(km, w.shape, _jnp.float32)
        out["v_" + name] = (s * s) * _jax.random.uniform(kv, w.shape, _jnp.float32, 0.5, 1.5)
    if N_MICROBATCH > 1:
        for name, axis in PER_EXAMPLE_BATCH_AXIS.items():
            out[name] = _to_microbatches(out[name], axis)
    return {'x': out['x'], 'mem': out['mem'], 'g_pre_mix': out['g_pre_mix'], 'w_in': out['w_in'], 'conv_w': out['conv_w'], 'conv_b': out['conv_b'], 'dt_bias': out['dt_bias'], 'a_log': out['a_log'], 'd_skip': out['d_skip'], 'g_ssm_norm': out['g_ssm_norm'], 'w_br_att': out['w_br_att'], 'w_br_ssm': out['w_br_ssm'], 'w_mix_out': out['w_mix_out'], 'g_post_mix': out['g_post_mix'], 'g_pre_xa': out['g_pre_xa'], 'g_mem': out['g_mem'], 'w_xq': out['w_xq'], 'w_xkv': out['w_xkv'], 'w_xo': out['w_xo'], 'g_post_xa': out['g_post_xa'], 'g_pre_ffn': out['g_pre_ffn'], 'w_gu': out['w_gu'], 'w_down': out['w_down'], 'g_post_ffn': out['g_post_ffn'], 'loss_target': out['loss_target'], 'm_g_pre_mix': out['m_g_pre_mix'], 'm_w_in': out['m_w_in'], 'm_conv_w': out['m_conv_w'], 'm_conv_b': out['m_conv_b'], 'm_dt_bias': out['m_dt_bias'], 'm_a_log': out['m_a_log'], 'm_d_skip': out['m_d_skip'], 'm_g_ssm_norm': out['m_g_ssm_norm'], 'm_w_br_att': out['m_w_br_att'], 'm_w_br_ssm': out['m_w_br_ssm'], 'm_w_mix_out': out['m_w_mix_out'], 'm_g_post_mix': out['m_g_post_mix'], 'm_g_pre_xa': out['m_g_pre_xa'], 'm_g_mem': out['m_g_mem'], 'm_w_xq': out['m_w_xq'], 'm_w_xkv': out['m_w_xkv'], 'm_w_xo': out['m_w_xo'], 'm_g_post_xa': out['m_g_post_xa'], 'm_g_pre_ffn': out['m_g_pre_ffn'], 'm_w_gu': out['m_w_gu'], 'm_w_down': out['m_w_down'], 'm_g_post_ffn': out['m_g_post_ffn'], 'v_g_pre_mix': out['v_g_pre_mix'], 'v_w_in': out['v_w_in'], 'v_conv_w': out['v_conv_w'], 'v_conv_b': out['v_conv_b'], 'v_dt_bias': out['v_dt_bias'], 'v_a_log': out['v_a_log'], 'v_d_skip': out['v_d_skip'], 'v_g_ssm_norm': out['v_g_ssm_norm'], 'v_w_br_att': out['v_w_br_att'], 'v_w_br_ssm': out['v_w_br_ssm'], 'v_w_mix_out': out['v_w_mix_out'], 'v_g_post_mix': out['v_g_post_mix'], 'v_g_pre_xa': out['v_g_pre_xa'], 'v_g_mem': out['v_g_mem'], 'v_w_xq': out['v_w_xq'], 'v_w_xkv': out['v_w_xkv'], 'v_w_xo': out['v_w_xo'], 'v_g_post_xa': out['v_g_post_xa'], 'v_g_pre_ffn': out['v_g_pre_ffn'], 'v_w_gu': out['v_w_gu'], 'v_w_down': out['v_w_down'], 'v_g_post_ffn': out['v_g_post_ffn']}


def _loss(weights, diff, rest, loss_target):
    with _jax.named_scope("forward"):
        args = {**rest, TWIN_DIFF_INPUT: diff, **{k: w.astype(_WEIGHT_DTYPES[k]) for k, w in weights.items()}}
        y = _forward(args)
    with _jax.named_scope("loss_head"):
        err = _jnp.square(y.astype(_jnp.float32) - loss_target)
        return 0.5 * _jnp.sum(_jnp.mean(err, axis=-1)) if err.ndim else 0.5 * err


def _adamw(w, g, m, v):
    m = ADAM_B1 * m + (1.0 - ADAM_B1) * g
    v = ADAM_B2 * v + (1.0 - ADAM_B2) * _jnp.square(g)
    m_hat = m / (1.0 - ADAM_B1 ** ADAM_STEP)
    v_hat = v / (1.0 - ADAM_B2 ** ADAM_STEP)
    delta = -ADAM_LR * (m_hat / (_jnp.sqrt(v_hat) + ADAM_EPS) + ADAM_WD * w)
    return delta, m, v


def reference(x, mem, g_pre_mix, w_in, conv_w, conv_b, dt_bias, a_log, d_skip, g_ssm_norm, w_br_att, w_br_ssm, w_mix_out, g_post_mix, g_pre_xa, g_mem, w_xq, w_xkv, w_xo, g_post_xa, g_pre_ffn, w_gu, w_down, g_post_ffn, loss_target, m_g_pre_mix, m_w_in, m_conv_w, m_conv_b, m_dt_bias, m_a_log, m_d_skip, m_g_ssm_norm, m_w_br_att, m_w_br_ssm, m_w_mix_out, m_g_post_mix, m_g_pre_xa, m_g_mem, m_w_xq, m_w_xkv, m_w_xo, m_g_post_xa, m_g_pre_ffn, m_w_gu, m_w_down, m_g_post_ffn, v_g_pre_mix, v_w_in, v_conv_w, v_conv_b, v_dt_bias, v_a_log, v_d_skip, v_g_ssm_norm, v_w_br_att, v_w_br_ssm, v_w_mix_out, v_g_post_mix, v_g_pre_xa, v_g_mem, v_w_xq, v_w_xkv, v_w_xo, v_g_post_xa, v_g_pre_ffn, v_w_gu, v_w_down, v_g_post_ffn):
    given = dict(x=x, mem=mem, g_pre_mix=g_pre_mix, w_in=w_in, conv_w=conv_w, conv_b=conv_b, dt_bias=dt_bias, a_log=a_log, d_skip=d_skip, g_ssm_norm=g_ssm_norm, w_br_att=w_br_att, w_br_ssm=w_br_ssm, w_mix_out=w_mix_out, g_post_mix=g_post_mix, g_pre_xa=g_pre_xa, g_mem=g_mem, w_xq=w_xq, w_xkv=w_xkv, w_xo=w_xo, g_post_xa=g_post_xa, g_pre_ffn=g_pre_ffn, w_gu=w_gu, w_down=w_down, g_post_ffn=g_post_ffn, loss_target=loss_target, m_g_pre_mix=m_g_pre_mix, m_w_in=m_w_in, m_conv_w=m_conv_w, m_conv_b=m_conv_b, m_dt_bias=m_dt_bias, m_a_log=m_a_log, m_d_skip=m_d_skip, m_g_ssm_norm=m_g_ssm_norm, m_w_br_att=m_w_br_att, m_w_br_ssm=m_w_br_ssm, m_w_mix_out=m_w_mix_out, m_g_post_mix=m_g_post_mix, m_g_pre_xa=m_g_pre_xa, m_g_mem=m_g_mem, m_w_xq=m_w_xq, m_w_xkv=m_w_xkv, m_w_xo=m_w_xo, m_g_post_xa=m_g_post_xa, m_g_pre_ffn=m_g_pre_ffn, m_w_gu=m_w_gu, m_w_down=m_w_down, m_g_post_ffn=m_g_post_ffn, v_g_pre_mix=v_g_pre_mix, v_w_in=v_w_in, v_conv_w=v_conv_w, v_conv_b=v_conv_b, v_dt_bias=v_dt_bias, v_a_log=v_a_log, v_d_skip=v_d_skip, v_g_ssm_norm=v_g_ssm_norm, v_w_br_att=v_w_br_att, v_w_br_ssm=v_w_br_ssm, v_w_mix_out=v_w_mix_out, v_g_post_mix=v_g_post_mix, v_g_pre_xa=v_g_pre_xa, v_g_mem=v_g_mem, v_w_xq=v_w_xq, v_w_xkv=v_w_xkv, v_w_xo=v_w_xo, v_g_post_xa=v_g_post_xa, v_g_pre_ffn=v_g_pre_ffn, v_w_gu=v_w_gu, v_w_down=v_w_down, v_g_post_ffn=v_g_post_ffn)
    weights = {n: given[n] for n in TWIN_WEIGHTS}
    shared = {n: given[n] for n in SHARED_INPUTS}
    per_example = {n: given[n] for n in ['x', 'mem']}
    grad_fn = _jax.value_and_grad(_loss, argnums=(0, 1))

    def one_microbatch(ex, loss_target):
        ex = dict(ex)
        diff = ex.pop(TWIN_DIFF_INPUT)
        return grad_fn(weights, diff, {**shared, **ex}, loss_target)

    if N_MICROBATCH == 1:
        loss, (grad_w, grad_x) = one_microbatch(per_example, given["loss_target"])
    else:
        def body(carry, xs):
            loss_sum, grad_sum = carry
            l_k, (gw_k, gx_k) = one_microbatch(xs[0], xs[1])
            with _jax.named_scope("update"):
                return (loss_sum + l_k, _jax.tree.map(_jnp.add, grad_sum, gw_k)), gx_k

        init = (_jnp.zeros((), _jnp.float32), _jax.tree.map(_jnp.zeros_like, weights))
        (loss, grad_w), grad_x = _jax.lax.scan(body, init, (per_example, given["loss_target"]))
    with _jax.named_scope("update"):
        delta_w, new_m, new_v = {}, {}, {}
        for n in TWIN_WEIGHTS:
            delta_w[n], new_m[n], new_v[n] = _adamw(weights[n], grad_w[n], given["m_" + n], given["v_" + n])
    return (loss, grad_x, *[grad_w[n] for n in TWIN_WEIGHTS], *[delta_w[n] for n in TWIN_WEIGHTS],
            *[new_m[n] for n in TWIN_WEIGHTS], *[new_v[n] for n in TWIN_WEIGHTS])
```

```python
import functools
import math

import jax
import jax.numpy as jnp
from jax import lax
from jax.experimental import pallas as pl
from jax.experimental.pallas import tpu as pltpu

F32 = jnp.float32
BF16 = jnp.bfloat16
MXU_DTYPE = BF16
ACT_DTYPE = BF16
WIRE_DTYPE = BF16

D_MODEL = 1024
SEQ = 2048
DEPTH = 4
MEM_LEN = 256
RMS_EPS = 1e-6
SB_HEADS = 16
SB_HEAD_DIM = 64
SB_BLOCK = 128
SSM_INNER = 2 * D_MODEL
SSM_HEAD_DIM = 64
SSM_GROUPS = 4
SSM_STATE = 128
SSM_CONV = 4
SSM_CHUNK = 128
XA_HEADS = 4
FFN_HIDDEN = ((8 * D_MODEL + 767) // 768) * 256
ADAM_LR = 0.001
ADAM_B1 = 0.9
ADAM_B2 = 0.999
ADAM_EPS = 1e-08
ADAM_WD = 0.01
ADAM_STEP = 10

N_DEV = 8
LANES = 128
VMEM_LIMIT_BYTES = 56 * 1024 * 1024

AXES = ("x", "y", "c")
MESH = pl.DeviceIdType.MESH


class _Cfg:
    def __init__(self):
        self.d = D_MODEL
        self.sbw = SB_HEADS * SB_HEAD_DIM
        self.inner = SSM_INNER
        self.heads = SSM_INNER // SSM_HEAD_DIM
        self.epg = self.heads // SSM_GROUPS
        self.gn = SSM_GROUPS * SSM_STATE
        self.conv_dim = SSM_INNER + 2 * self.gn
        self.ffn = FFN_HIDDEN
        self.xa_dim = D_MODEL // XA_HEADS
        self.in_sizes = (self.sbw, self.sbw, self.sbw, self.inner, self.conv_dim, self.heads, self.d, self.d)
        self.in_width = sum(self.in_sizes)
        self.z0 = 0
        self.ga0 = self.inner
        self.gs0 = self.ga0 + self.d
        self.q0 = self.gs0 + self.d
        self.k0 = self.q0 + self.sbw
        self.v0 = self.k0 + self.sbw
        self.xbc0 = self.v0 + self.sbw
        self.dt0 = self.xbc0 + self.conv_dim
        self.proj_w = self.dt0 + LANES
        assert self.heads <= LANES


def _cparams(sem=None):
    return pltpu.CompilerParams(dimension_semantics=sem, vmem_limit_bytes=VMEM_LIMIT_BYTES)


def _tile(n, pref, mult):
    if n <= pref:
        return n
    t = (pref // mult) * mult
    while t >= mult:
        if n % t == 0:
            return t
        t -= mult
    return n


def _mm(a, b, *, ta=False, tb=False, out_dtype=F32, name, a_pre=(), b_pre=(), tm=512, tn=1152, tk=2048):
    ash = a.shape[len(a_pre):]
    bsh = b.shape[len(b_pre):]
    kk, m = (ash if ta else ash[::-1])
    if tb:
        n, k2 = bsh
    else:
        k2, n = bsh
    assert kk == k2, (name, a.shape, b.shape)
    tm = _tile(m, tm, LANES if ta else 16)
    tn = _tile(n, tn, LANES)
    tk = _tile(kk, tk, LANES if (tb or not ta) else 16)
    nk = kk // tk
    dims = (((0 if ta else 1,), (1 if tb else 0,)), ((), ()))
    npa, npb = len(a_pre), len(b_pre)

    def body(a_ref, b_ref, o_ref, *scratch):
        av = a_ref[...].astype(MXU_DTYPE)
        bv = b_ref[...].astype(MXU_DTYPE)
        part = lax.dot_general(av, bv, dims, preferred_element_type=F32)
        if nk == 1:
            o_ref[...] = part.astype(out_dtype)
        else:
            acc_ref, = scratch
            k = pl.program_id(2)

            @pl.when(k == 0)
            def _():
                acc_ref[...] = part

            @pl.when(k > 0)
            def _():
                acc_ref[...] += part

            @pl.when(k == nk - 1)
            def _():
                o_ref[...] = acc_ref[...].astype(out_dtype)

    if ta:
        a_spec = pl.BlockSpec((None,) * npa + (tk, tm), lambda i, j, k: a_pre + (k, i))
    else:
        a_spec = pl.BlockSpec((None,) * npa + (tm, tk), lambda i, j, k: a_pre + (i, k))
    if tb:
        b_spec = pl.BlockSpec((None,) * npb + (tn, tk), lambda i, j, k: b_pre + (j, k))
    else:
        b_spec = pl.BlockSpec((None,) * npb + (tk, tn), lambda i, j, k: b_pre + (k, j))
    return pl.pallas_call(
        body,
        out_shape=jax.ShapeDtypeStruct((m, n), out_dtype),
        grid=(m // tm, n // tn, nk),
        in_specs=[a_spec, b_spec],
        out_specs=pl.BlockSpec((tm, tn), lambda i, j, k: (i, j)),
        scratch_shapes=[] if nk == 1 else [pltpu.VMEM((tm, tn), F32)],
        compiler_params=_cparams(("parallel", "parallel", "arbitrary")),
        name=name,
    )(a, b)


def _rowwise(fn, name, rows, row_in, vec_in, row_out, acc_out=(), tr=256):
    tr = _tile(rows, tr, 16)
    n_in = len(row_in) + len(vec_in)
    n_ro = len(row_out)

    def body(*refs):
        ins = [r[...].astype(F32) for r in refs[:n_in]]
        outs = fn(*ins)
        if not isinstance(outs, (tuple, list)):
            outs = (outs,)
        out_refs = refs[n_in:]
        for o_ref, val in zip(out_refs[:n_ro], outs[:n_ro]):
            o_ref[...] = val.astype(o_ref.dtype)
        if acc_out:
            i = pl.program_id(0)
            for o_ref, val in zip(out_refs[n_ro:], outs[n_ro:]):
                @pl.when(i == 0)
                def _(o_ref=o_ref, val=val):
                    o_ref[...] = val

                @pl.when(i > 0)
                def _(o_ref=o_ref, val=val):
                    o_ref[...] += val

    in_specs = [pl.BlockSpec((tr, w), functools.partial(lambda i, cb: (i, cb), cb=cb)) for (_, w, cb) in row_in]
    in_specs += [pl.BlockSpec((None,) + v.shape[1:], functools.partial(lambda i, l: (l, 0, 0), l=l)) for (v, l) in vec_in]
    out_shape = [jax.ShapeDtypeStruct((rows, w), dt) for (w, dt) in row_out]
    out_shape += [jax.ShapeDtypeStruct(s, F32) for s in acc_out]
    out_specs = [pl.BlockSpec((tr, w), lambda i: (i, 0)) for (w, _) in row_out]
    out_specs += [pl.BlockSpec(s, lambda i: (0, 0)) for s in acc_out]
    res = pl.pallas_call(
        body,
        out_shape=out_shape,
        grid=(rows // tr,),
        in_specs=in_specs,
        out_specs=out_specs,
        compiler_params=_cparams(("arbitrary",) if acc_out else ("parallel",)),
        name=name,
    )(*[a for (a, _, _) in row_in], *[v for (v, _) in vec_in])
    return res


def _rms(x, g):
    r = lax.rsqrt(jnp.mean(x * x, axis=-1, keepdims=True) + RMS_EPS)
    return x * r * g


def _rms_bwd(x, g, dy):
    r = lax.rsqrt(jnp.mean(x * x, axis=-1, keepdims=True) + RMS_EPS)
    xh = x * r
    dxh = dy * g
    dx = r * (dxh - xh * jnp.mean(dxh * xh, axis=-1, keepdims=True))
    return dx, jnp.sum(dy * xh, axis=0, keepdims=True)


def _silu(x):
    return x * jax.nn.sigmoid(x)


def _silu_grad(x):
    s = jax.nn.sigmoid(x)
    return s * (1.0 + x * (1.0 - s))


def _softplus(x):
    return jnp.maximum(x, 0.0) + jnp.log1p(jnp.exp(-jnp.abs(x)))


def _full(a):
    return (a, a.shape[1], 0)


def _f_post_pre(x, ysub, g_post, g_pre):
    xn = x + _rms(ysub, g_post)
    return xn, _rms(xn, g_pre)


def _f_final(x, ysub, tgt, g_post):
    err = x + _rms(ysub, g_post) - tgt
    return err * (1.0 / D_MODEL), jnp.sum(err * err, axis=0, keepdims=True)


def _f_pre_post_bwd(xmid, dh, dxo, ysub, g_pre, g_post):
    d1, dg_pre = _rms_bwd(xmid, g_pre, dh)
    dxm = dxo + d1
    dys, dg_post = _rms_bwd(ysub, g_post, dxm)
    return dxm, dys, dg_pre, dg_post


def _f_pre_bwd(x, dh, dxo, g_pre):
    d1, dg_pre = _rms_bwd(x, g_pre, dh)
    return dxo + d1, dg_pre


def _f_gain_bwd(x, dy, g):
    return _rms_bwd(x, g, dy)[1]


def _group_norm_parts(u):
    gw = u.shape[1] // SSM_GROUPS
    parts = []
    for gi in range(SSM_GROUPS):
        ug = u[:, gi * gw:(gi + 1) * gw]
        r = lax.rsqrt(jnp.mean(ug * ug, axis=-1, keepdims=True) + RMS_EPS)
        parts.append((ug * r, r))
    return gw, parts


def _f_gate_norm(y, z, g):
    _, parts = _group_norm_parts(y * _silu(z))
    return jnp.concatenate([uh for uh, _ in parts], axis=1) * g


def _f_gate_norm_bwd(y, z, do, g):
    sz = _silu(z)
    gw, parts = _group_norm_parts(y * sz)
    dxh = do * g
    du = []
    for gi, (uh, r) in enumerate(parts):
        dg_ = dxh[:, gi * gw:(gi + 1) * gw]
        du.append(r * (dg_ - uh * jnp.mean(dg_ * uh, axis=-1, keepdims=True)))
    du = jnp.concatenate(du, axis=1)
    uh_all = jnp.concatenate([uh for uh, _ in parts], axis=1)
    return du * sz, du * y * _silu_grad(z), jnp.sum(do * uh_all, axis=0, keepdims=True)


def _f_merge(ga, gs, ba, bs):
    return jax.nn.sigmoid(ga) * ba + jax.nn.sigmoid(gs) * bs


def _f_merge_bwd(ga, gs, ba, bs, dm):
    sa, ss = jax.nn.sigmoid(ga), jax.nn.sigmoid(gs)
    dgg = jnp.concatenate([dm * ba * sa * (1.0 - sa), dm * bs * ss * (1.0 - ss)], axis=1)
    return dgg, dm * sa, dm * ss


def _f_swiglu(gate, up):
    return _silu(gate) * up


def _f_swiglu_bwd(gate, up, da):
    return jnp.concatenate([da * up * _silu_grad(gate), da * _silu(gate)], axis=1)


def _split_dot(x, u):
    hi = x.astype(BF16)
    lo = (x - hi.astype(F32)).astype(BF16)
    return jnp.dot(hi, u, preferred_element_type=F32) + jnp.dot(lo, u, preferred_element_type=F32)


def _sb_block(q, kj, diag, row, col):
    z = lax.dot_general(q, kj, (((1,), (1,)), ((), ())), preferred_element_type=F32)
    mask = jnp.logical_or(col < row, jnp.logical_not(diag))
    lm = jnp.where(mask, -_softplus(z), 0.0)
    return z, mask, lm


def _sb_fwd(proj, cfg, nb):
    blk = SB_BLOCK
    nq = SEQ // blk
    hd = SB_HEAD_DIM
    hp = LANES // hd
    ncb = cfg.sbw // LANES
    scale = hd ** -0.5
    qb, kb, vb = cfg.q0 // LANES, cfg.k0 // LANES, cfg.v0 // LANES

    def body(q_ref, k_ref, v_ref, o_ref):
        i = pl.program_id(2)
        row = lax.broadcasted_iota(jnp.int32, (blk, blk), 0)
        col = lax.broadcasted_iota(jnp.int32, (blk, blk), 1)
        u_incl = (row >= col).astype(BF16)
        for h in range(hp):
            cs_ = slice(h * hd, (h + 1) * hd)
            q = (q_ref[:, cs_] * scale).astype(MXU_DTYPE)

            def step(n, carry):
                acc, run = carry
                j = i - n
                rows = pl.ds(pl.multiple_of(j * blk, blk), blk)
                kj = k_ref[rows, cs_].astype(MXU_DTYPE)
                vj = v_ref[rows, cs_].astype(MXU_DTYPE)
                z, mask, lm = _sb_block(q, kj, n == 0, row, col)
                cs = _split_dot(lm, u_incl)
                w = jnp.where(mask, jnp.exp(z + cs + run), 0.0)
                acc = acc + jnp.dot(w.astype(MXU_DTYPE), vj, preferred_element_type=F32)
                return acc, run + cs[:, 0:1]

            acc, _ = lax.fori_loop(0, i + 1, step, (jnp.zeros((blk, hd), F32), jnp.zeros((blk, 1), F32)))
            o_ref[:, cs_] = acc

    return pl.pallas_call(
        body,
        out_shape=jax.ShapeDtypeStruct((nb * SEQ, cfg.sbw), F32),
        grid=(nb, ncb, nq),
        in_specs=[
            pl.BlockSpec((blk, LANES), lambda b, c, i: (b * nq + i, qb + c)),
            pl.BlockSpec((SEQ, LANES), lambda b, c, i: (b, kb + c)),
            pl.BlockSpec((SEQ, LANES), lambda b, c, i: (b, vb + c)),
        ],
        out_specs=pl.BlockSpec((blk, LANES), lambda b, c, i: (b * nq + i, c)),
        compiler_params=_cparams(("parallel", "parallel", "arbitrary")),
        name="sb_fwd",
    )(proj, proj, proj)


def _sb_bwd(proj, do_att, cfg, nb):
    blk = SB_BLOCK
    nq = SEQ // blk
    hd = SB_HEAD_DIM
    hp = LANES // hd
    ncb = cfg.sbw // LANES
    scale = hd ** -0.5
    qb, kb, vb = cfg.q0 // LANES, cfg.k0 // LANES, cfg.v0 // LANES

    def body(q_ref, k_ref, v_ref, do_ref, dq_ref, dk_ref, dv_ref, g_ref, z_ref):
        i = pl.program_id(2)

        @pl.when(i == 0)
        def _():
            dk_ref[...] = jnp.zeros_like(dk_ref)
            dv_ref[...] = jnp.zeros_like(dv_ref)

        row = lax.broadcasted_iota(jnp.int32, (blk, blk), 0)
        col = lax.broadcasted_iota(jnp.int32, (blk, blk), 1)
        u_rev = (row >= col).astype(BF16)
        u_fwd = (row <= col).astype(BF16)
        for h in range(hp):
            cs_ = slice(h * hd, (h + 1) * hd)
            q = (q_ref[:, cs_] * scale).astype(MXU_DTYPE)
            do_m = do_ref[:, cs_].astype(MXU_DTYPE)

            def sweep_left(n, run_l):
                j = i - n
                rows = pl.ds(pl.multiple_of(j * blk, blk), blk)
                kj = k_ref[rows, cs_].astype(MXU_DTYPE)
                vj = v_ref[rows, cs_].astype(MXU_DTYPE)
                z, mask, lm = _sb_block(q, kj, n == 0, row, col)
                cs = _split_dot(lm, u_rev)
                a = jnp.where(mask, jnp.exp(z + cs + run_l), 0.0)
                da = lax.dot_general(do_m, vj, (((1,), (1,)), ((), ())), preferred_element_type=F32)
                dv_ref[rows, cs_] += lax.dot_general(a.astype(MXU_DTYPE), do_m, (((0,), (0,)), ((), ())),
                                                     preferred_element_type=F32)
                g_ref[j] = a * da
                z_ref[j] = z
                return run_l + cs[:, 0:1]

            lax.fori_loop(0, i + 1, sweep_left, jnp.zeros((blk, 1), F32))

            def sweep_right(j, carry):
                dq, run_g = carry
                rows = pl.ds(pl.multiple_of(j * blk, blk), blk)
                kj = k_ref[rows, cs_].astype(MXU_DTYPE)
                g = g_ref[j]
                g_upto = _split_dot(g, u_fwd) + run_g
                mask = jnp.logical_or(col < row, j < i)
                dz = jnp.where(mask, g - jax.nn.sigmoid(z_ref[j]) * g_upto, 0.0)
                dz_m = dz.astype(MXU_DTYPE)
                dq = dq + jnp.dot(dz_m, kj, preferred_element_type=F32)
                dk_ref[rows, cs_] += lax.dot_general(dz_m, q, (((0,), (0,)), ((), ())), preferred_element_type=F32)
                return dq, run_g + jnp.sum(g, axis=1, keepdims=True)

            dq, _ = lax.fori_loop(0, i + 1, sweep_right, (jnp.zeros((blk, hd), F32), jnp.zeros((blk, 1), F32)))
            dq_ref[:, cs_] = (dq * scale).astype(dq_ref.dtype)

    kv_spec_out = pl.BlockSpec((SEQ, LANES), lambda b, c, i: (b, c))
    q_spec_out = pl.BlockSpec((blk, LANES), lambda b, c, i: (b * nq + i, c))
    return pl.pallas_call(
        body,
        out_shape=[
            jax.ShapeDtypeStruct((nb * SEQ, cfg.sbw), ACT_DTYPE),
            jax.ShapeDtypeStruct((nb * SEQ, cfg.sbw), F32),
            jax.ShapeDtypeStruct((nb * SEQ, cfg.sbw), F32),
        ],
        grid=(nb, ncb, nq),
        in_specs=[
            pl.BlockSpec((blk, LANES), lambda b, c, i: (b * nq + i, qb + c)),
            pl.BlockSpec((SEQ, LANES), lambda b, c, i: (b, kb + c)),
            pl.BlockSpec((SEQ, LANES), lambda b, c, i: (b, vb + c)),
            q_spec_out,
        ],
        out_specs=[q_spec_out, kv_spec_out, kv_spec_out],
        scratch_shapes=[pltpu.VMEM((nq, blk, blk), F32), pltpu.VMEM((nq, blk, blk), F32)],
        compiler_params=_cparams(("parallel", "parallel", "arbitrary")),
        name="sb_bwd",
    )(proj, proj, proj, do_att)


CONV_COLS = 256


def _conv_pre(x, w, b, t):
    kw = SSM_CONV
    shifted = []
    pre = b + w[kw - 1:kw, :] * x
    for k in range(kw - 1):
        d = kw - 1 - k
        xs = jnp.where(t >= d, pltpu.roll(x, d, 0), 0.0)
        shifted.append(xs)
        pre = pre + w[k:k + 1, :] * xs
    shifted.append(x)
    return pre, shifted


def _conv_fwd(proj, conv_w, conv_b, l, cfg, nb):
    cw = CONV_COLS
    ncb = cfg.conv_dim // cw
    xb = cfg.xbc0 // cw

    def body(x_ref, w_ref, b_ref, o_ref):
        x = x_ref[...]
        t = lax.broadcasted_iota(jnp.int32, x.shape, 0)
        pre, _ = _conv_pre(x, w_ref[...], b_ref[...], t)
        o_ref[...] = _silu(pre)

    return pl.pallas_call(
        body,
        out_shape=jax.ShapeDtypeStruct((nb * SEQ, cfg.conv_dim), F32),
        grid=(ncb, nb),
        in_specs=[
            pl.BlockSpec((SEQ, cw), lambda j, b: (b, xb + j)),
            pl.BlockSpec((None, SSM_CONV, cw), lambda j, b: (l, 0, j)),
            pl.BlockSpec((None, 1, cw), lambda j, b: (l, 0, j)),
        ],
        out_specs=pl.BlockSpec((SEQ, cw), lambda j, b: (b, j)),
        compiler_params=_cparams(("parallel", "parallel")),
        name="conv_fwd",
    )(proj, conv_w, conv_b)


def _conv_bwd(proj, dact, conv_w, conv_b, l, cfg, nb):
    cw = CONV_COLS
    ncb = cfg.conv_dim // cw
    xb = cfg.xbc0 // cw
    kw = SSM_CONV

    def body(x_ref, da_ref, w_ref, b_ref, dx_ref, dw_ref, db_ref):
        b_id = pl.program_id(1)
        x = x_ref[...]
        w = w_ref[...]
        t = lax.broadcasted_iota(jnp.int32, x.shape, 0)
        pre, shifted = _conv_pre(x, w, b_ref[...], t)
        dpre = da_ref[...] * _silu_grad(pre)
        dx = w[kw - 1:kw, :] * dpre
        for k in range(kw - 1):
            d = kw - 1 - k
            dx = dx + w[k:k + 1, :] * jnp.where(t < SEQ - d, pltpu.roll(dpre, SEQ - d, 0), 0.0)
        dx_ref[...] = dx.astype(dx_ref.dtype)
        dw = jnp.concatenate([jnp.sum(dpre * s, axis=0, keepdims=True) for s in shifted], axis=0)
        db = jnp.sum(dpre, axis=0, keepdims=True)

        @pl.when(b_id == 0)
        def _():
            dw_ref[...] = dw
            db_ref[...] = db

        @pl.when(b_id > 0)
        def _():
            dw_ref[...] += dw
            db_ref[...] += db

    return pl.pallas_call(
        body,
        out_shape=[
            jax.ShapeDtypeStruct((nb * SEQ, cfg.conv_dim), ACT_DTYPE),
            jax.ShapeDtypeStruct((kw, cfg.conv_dim), F32),
            jax.ShapeDtypeStruct((1, cfg.conv_dim), F32),
        ],
        grid=(ncb, nb),
        in_specs=[
            pl.BlockSpec((SEQ, cw), lambda j, b: (b, xb + j)),
            pl.BlockSpec((SEQ, cw), lambda j, b: (b, j)),
            pl.BlockSpec((None, kw, cw), lambda j, b: (l, 0, j)),
            pl.BlockSpec((None, 1, cw), lambda j, b: (l, 0, j)),
        ],
        out_specs=[
            pl.BlockSpec((SEQ, cw), lambda j, b: (b, j)),
            pl.BlockSpec((kw, cw), lambda j, b: (0, j)),
            pl.BlockSpec((1, cw), lambda j, b: (0, j)),
        ],
        compiler_params=_cparams(("parallel", "arbitrary")),
        name="conv_bwd",
    )(proj, dact, conv_w, conv_b)


def _ssd_common(dt_raw, dt_bias, a_log, tri):
    ln = SSM_CHUNK
    dt = _softplus(dt_raw + dt_bias)
    a = -jnp.exp(a_log)
    a_cs = jnp.dot(tri, dt * a, preferred_element_type=F32, precision=lax.Precision.HIGHEST)
    a_last = a_cs[ln - 1:ln, :]
    return dt, a, a_cs, a_cs.T, jnp.exp(a_cs), jnp.exp(a_last - a_cs), jnp.exp(a_last)


def _ssd_head(h, xs, dt, a_cs, a_t, cb, tril):
    p = SSM_HEAD_DIM
    x_h = xs[:, h * p:(h + 1) * p]
    xd = x_h * dt[:, h:h + 1]
    lmat = jnp.exp(jnp.where(tril, a_cs[:, h:h + 1] - a_t[h:h + 1, :], -jnp.inf))
    return x_h, xd, lmat


def _ssd_specs(cfg, nc, rev):
    ln = SSM_CHUNK
    cidx = (lambda c: nc - 1 - c) if rev else (lambda c: c)
    bmb = cfg.inner // cfg.gn
    return [
        pl.BlockSpec((ln, cfg.inner), lambda b, c: (b * nc + cidx(c), 0)),
        pl.BlockSpec((ln, cfg.gn), lambda b, c: (b * nc + cidx(c), bmb)),
        pl.BlockSpec((ln, cfg.gn), lambda b, c: (b * nc + cidx(c), bmb + 1)),
        pl.BlockSpec((ln, LANES), lambda b, c: (b * nc + cidx(c), cfg.dt0 // LANES)),
    ]


def _ssd_fwd(xbc, proj, dt_bias, a_log, d_skip, l, cfg, nb):
    ln, p, n = SSM_CHUNK, SSM_HEAD_DIM, SSM_STATE
    nc = SEQ // ln
    g_, e_ = SSM_GROUPS, cfg.epg
    assert cfg.inner % cfg.gn == 0

    def body(xs_ref, bm_ref, cm_ref, dtr_ref, bias_ref, alog_ref, dsk_ref, y_ref, st_ref, s_ref):
        c = pl.program_id(1)

        @pl.when(c == 0)
        def _():
            s_ref[...] = jnp.zeros_like(s_ref)

        st_ref[...] = s_ref[...]
        row = lax.broadcasted_iota(jnp.int32, (ln, ln), 0)
        col = lax.broadcasted_iota(jnp.int32, (ln, ln), 1)
        tril = row >= col
        dt, _, a_cs, a_t, e_a, dte, cd = _ssd_common(dtr_ref[...], bias_ref[...], alog_ref[...], tril.astype(F32))
        dsk = dsk_ref[...]
        xs = xs_ref[...]
        for g in range(g_):
            bm = bm_ref[:, g * n:(g + 1) * n].astype(MXU_DTYPE)
            cm = cm_ref[:, g * n:(g + 1) * n].astype(MXU_DTYPE)
            cb = lax.dot_general(cm, bm, (((1,), (1,)), ((), ())), preferred_element_type=F32)
            for e in range(e_):
                h = g * e_ + e
                x_h, xd, lmat = _ssd_head(h, xs, dt, a_cs, a_t, cb, tril)
                s_prev = s_ref[g * n:(g + 1) * n, e * p:(e + 1) * p]
                y = jnp.dot((cb * lmat).astype(MXU_DTYPE), xd.astype(MXU_DTYPE), preferred_element_type=F32)
                y = y + jnp.dot(cm, s_prev.astype(MXU_DTYPE), preferred_element_type=F32) * e_a[:, h:h + 1]
                y_ref[:, h * p:(h + 1) * p] = y + dsk[:, h:h + 1] * x_h
                upd = lax.dot_general(bm, (xd * dte[:, h:h + 1]).astype(MXU_DTYPE), (((0,), (0,)), ((), ())),
                                      preferred_element_type=F32)
                s_ref[g * n:(g + 1) * n, e * p:(e + 1) * p] = cd[:, h:h + 1] * s_prev + upd

    vec = lambda b, c: (l, 0, 0)
    return pl.pallas_call(
        body,
        out_shape=[
            jax.ShapeDtypeStruct((nb * SEQ, cfg.inner), F32),
            jax.ShapeDtypeStruct((nb * nc * g_ * n, e_ * p), F32),
        ],
        grid=(nb, nc),
        in_specs=_ssd_specs(cfg, nc, False) + [pl.BlockSpec((None, 1, LANES), vec)] * 3,
        out_specs=[
            pl.BlockSpec((ln, cfg.inner), lambda b, c: (b * nc + c, 0)),
            pl.BlockSpec((g_ * n, e_ * p), lambda b, c: (b * nc + c, 0)),
        ],
        scratch_shapes=[pltpu.VMEM((g_ * n, e_ * p), F32)],
        compiler_params=_cparams(("parallel", "arbitrary")),
        name="ssd_fwd",
    )(xbc, xbc, xbc, proj, dt_bias, a_log, d_skip)


def _ssd_bwd(xbc, proj, states, dy, dt_bias, a_log, d_skip, l, cfg, nb):
    ln, p, n = SSM_CHUNK, SSM_HEAD_DIM, SSM_STATE
    nc = SEQ // ln
    g_, e_ = SSM_GROUPS, cfg.epg
    c00 = (((0,), (0,)), ((), ()))
    c11 = (((1,), (1,)), ((), ()))

    def body(xs_ref, bm_ref, cm_ref, dtr_ref, st_ref, dy_ref, bias_ref, alog_ref, dsk_ref,
             dxbc_ref, ddt_ref, dvec_ref, ds_ref):
        first = jnp.logical_and(pl.program_id(0) == 0, pl.program_id(1) == 0)

        @pl.when(pl.program_id(1) == 0)
        def _():
            ds_ref[...] = jnp.zeros_like(ds_ref)

        row = lax.broadcasted_iota(jnp.int32, (ln, ln), 0)
        col = lax.broadcasted_iota(jnp.int32, (ln, ln), 1)
        tril = row >= col
        tri_f = tril.astype(F32)
        dtr = dtr_ref[...]
        bias = bias_ref[...]
        dt, a, a_cs, a_t, e_a, dte, cd = _ssd_common(dtr, bias, alog_ref[...], tri_f)
        dsk = dsk_ref[...]
        xs = xs_ref[...]
        lane = lax.broadcasted_iota(jnp.int32, (1, LANES), 1)
        sub = lax.broadcasted_iota(jnp.int32, (LANES, 1), 0)
        da_col = jnp.zeros((ln, LANES), F32)
        da_row_t = jnp.zeros((LANES, ln), F32)
        ddt = jnp.zeros((ln, LANES), F32)
        da_last = jnp.zeros((1, LANES), F32)
        ddsk = jnp.zeros((1, LANES), F32)
        for g in range(g_):
            bm = bm_ref[:, g * n:(g + 1) * n].astype(MXU_DTYPE)
            cm = cm_ref[:, g * n:(g + 1) * n].astype(MXU_DTYPE)
            cb = lax.dot_general(cm, bm, c11, preferred_element_type=F32)
            dbm = jnp.zeros((ln, n), F32)
            dcm = jnp.zeros((ln, n), F32)
            for e in range(e_):
                h = g * e_ + e
                hs = slice(h * p, (h + 1) * p)
                oh = (lane == h).astype(F32)
                x_h, xd, lmat = _ssd_head(h, xs, dt, a_cs, a_t, cb, tril)
                d_y = dy_ref[:, hs]
                s_prev = st_ref[g * n:(g + 1) * n, e * p:(e + 1) * p]
                d_s = ds_ref[g * n:(g + 1) * n, e * p:(e + 1) * p]
                dy_m = d_y.astype(MXU_DTYPE)
                xd_m = xd.astype(MXU_DTYPE)
                sp_m = s_prev.astype(MXU_DTYPE)
                ds_m = d_s.astype(MXU_DTYPE)
                e_a_h, dte_h, cd_h = e_a[:, h:h + 1], dte[:, h:h + 1], cd[:, h:h + 1]
                m_mat = (cb * lmat).astype(MXU_DTYPE)
                bds = jnp.dot(bm, ds_m, preferred_element_type=F32)
                d_xd = lax.dot_general(m_mat, dy_m, c00, preferred_element_type=F32) + dte_h * bds
                d_m = lax.dot_general(dy_m, xd_m, c11, preferred_element_type=F32)
                d_cb = (d_m * lmat).astype(MXU_DTYPE)
                w_mat = d_m * cb * lmat
                dy_e = (d_y * e_a_h).astype(MXU_DTYPE)
                xd_e = (xd * dte_h).astype(MXU_DTYPE)
                dcm = dcm + jnp.dot(d_cb, bm, preferred_element_type=F32)
                dcm = dcm + lax.dot_general(dy_e, sp_m, c11, preferred_element_type=F32)
                dbm = dbm + lax.dot_general(d_cb, cm, c00, preferred_element_type=F32)
                dbm = dbm + lax.dot_general(xd_e, ds_m, c11, preferred_element_type=F32)
                ds_ref[g * n:(g + 1) * n, e * p:(e + 1) * p] = (
                    cd_h * d_s + lax.dot_general(cm, dy_e, c00, preferred_element_type=F32))
                y_off = jnp.dot(cm, sp_m, preferred_element_type=F32) * e_a_h
                q_h = jnp.sum(bds * xd, axis=1, keepdims=True) * dte_h
                da_col_h = (jnp.sum(w_mat, axis=1, keepdims=True)
                            + jnp.sum(d_y * y_off, axis=1, keepdims=True) - q_h)
                da_col = da_col + da_col_h * oh
                da_row_t = da_row_t - (sub == h).astype(F32) * jnp.sum(w_mat, axis=0, keepdims=True)
                da_last = da_last + (jnp.sum(q_h, keepdims=True) + cd_h * jnp.sum(d_s * s_prev, keepdims=True)) * oh
                dxbc_ref[:, hs] = d_xd * dt[:, h:h + 1] + dsk[:, h:h + 1] * d_y
                ddt = ddt + jnp.sum(d_xd * x_h, axis=1, keepdims=True) * oh
                ddsk = ddsk + jnp.sum(d_y * x_h, keepdims=True) * oh
            dxbc_ref[:, cfg.inner + g * n:cfg.inner + (g + 1) * n] = dbm
            dxbc_ref[:, cfg.inner + cfg.gn + g * n:cfg.inner + cfg.gn + (g + 1) * n] = dcm
        d_acs = da_col + da_row_t.T + jnp.where(row[:, 0:1] == ln - 1, da_last, 0.0)
        da_dt = lax.dot_general(tri_f, d_acs, c00, preferred_element_type=F32, precision=lax.Precision.HIGHEST)
        ddt = ddt + da_dt * a
        ddt_raw = ddt * jax.nn.sigmoid(dtr + bias)
        ddt_ref[...] = ddt_raw.astype(ddt_ref.dtype)
        da_log = jnp.sum(da_dt * dt, axis=0, keepdims=True) * a
        dvec = jnp.concatenate([jnp.sum(ddt_raw, axis=0, keepdims=True), da_log, ddsk,
                                jnp.zeros((5, LANES), F32)], axis=0)

        @pl.when(first)
        def _():
            dvec_ref[...] = dvec

        @pl.when(jnp.logical_not(first))
        def _():
            dvec_ref[...] += dvec

    vec = lambda b, c: (l, 0, 0)
    rblk = lambda b, c: (b * nc + nc - 1 - c, 0)
    return pl.pallas_call(
        body,
        out_shape=[
            jax.ShapeDtypeStruct((nb * SEQ, cfg.conv_dim), F32),
            jax.ShapeDtypeStruct((nb * SEQ, LANES), ACT_DTYPE),
            jax.ShapeDtypeStruct((8, LANES), F32),
        ],
        grid=(nb, nc),
        in_specs=_ssd_specs(cfg, nc, True) + [
            pl.BlockSpec((g_ * n, e_ * p), rblk),
            pl.BlockSpec((ln, cfg.inner), rblk),
        ] + [pl.BlockSpec((None, 1, LANES), vec)] * 3,
        out_specs=[
            pl.BlockSpec((ln, cfg.conv_dim), rblk),
            pl.BlockSpec((ln, LANES), rblk),
            pl.BlockSpec((8, LANES), lambda b, c: (0, 0)),
        ],
        scratch_shapes=[pltpu.VMEM((g_ * n, e_ * p), F32)],
        compiler_params=_cparams(("arbitrary", "arbitrary")),
        name="ssd_bwd",
    )(xbc, xbc, xbc, proj, states, dy, dt_bias, a_log, d_skip)


XA_ROWS = 256


def _xa_probs(q_ref, kv_ref, h, dh):
    c11 = (((1,), (1,)), ((), ()))
    qh = q_ref[:, h * dh:(h + 1) * dh].astype(MXU_DTYPE)
    kh = kv_ref[:, h * dh:(h + 1) * dh].astype(MXU_DTYPE)
    vh = kv_ref[:, D_MODEL + h * dh:D_MODEL + (h + 1) * dh].astype(MXU_DTYPE)
    s = lax.dot_general(qh, kh, c11, preferred_element_type=F32) * (dh ** -0.5)
    s = s - jnp.max(s, axis=1, keepdims=True)
    pr = jnp.exp(s)
    return qh, kh, vh, pr / jnp.sum(pr, axis=1, keepdims=True)


def _xa_fwd(q, kv, cfg, nb):
    tq = _tile(SEQ, XA_ROWS, 16)
    nq = SEQ // tq
    dh = cfg.xa_dim

    def body(q_ref, kv_ref, o_ref):
        for h in range(XA_HEADS):
            _, _, vh, pr = _xa_probs(q_ref, kv_ref, h, dh)
            o_ref[:, h * dh:(h + 1) * dh] = jnp.dot(pr.astype(MXU_DTYPE), vh, preferred_element_type=F32).astype(o_ref.dtype)

    return pl.pallas_call(
        body,
        out_shape=jax.ShapeDtypeStruct((nb * SEQ, D_MODEL), ACT_DTYPE),
        grid=(nb, nq),
        in_specs=[
            pl.BlockSpec((tq, D_MODEL), lambda b, i: (b * nq + i, 0)),
            pl.BlockSpec((MEM_LEN, 2 * D_MODEL), lambda b, i: (b, 0)),
        ],
        out_specs=pl.BlockSpec((tq, D_MODEL), lambda b, i: (b * nq + i, 0)),
        compiler_params=_cparams(("parallel", "parallel")),
        name="xa_fwd",
    )(q, kv)


def _xa_bwd(q, kv, do, cfg, nb):
    tq = _tile(SEQ, XA_ROWS, 16)
    nq = SEQ // tq
    dh = cfg.xa_dim
    c00 = (((0,), (0,)), ((), ()))
    c11 = (((1,), (1,)), ((), ()))
    scale = dh ** -0.5

    def body(q_ref, kv_ref, do_ref, dq_ref, dkv_ref, acc_ref):
        i = pl.program_id(1)

        @pl.when(i == 0)
        def _():
            acc_ref[...] = jnp.zeros_like(acc_ref)

        for h in range(XA_HEADS):
            hs = slice(h * dh, (h + 1) * dh)
            vs = slice(D_MODEL + h * dh, D_MODEL + (h + 1) * dh)
            qh, kh, vh, pr = _xa_probs(q_ref, kv_ref, h, dh)
            do_h = do_ref[:, hs].astype(MXU_DTYPE)
            dp = lax.dot_general(do_h, vh, c11, preferred_element_type=F32)
            ds = (pr * (dp - jnp.sum(dp * pr, axis=1, keepdims=True))).astype(MXU_DTYPE)
            dq_ref[:, hs] = (jnp.dot(ds, kh, preferred_element_type=F32) * scale).astype(dq_ref.dtype)
            acc_ref[:, hs] += lax.dot_general(ds, qh, c00, preferred_element_type=F32) * scale
            acc_ref[:, vs] += lax.dot_general(pr.astype(MXU_DTYPE), do_h, c00, preferred_element_type=F32)

        @pl.when(i == nq - 1)
        def _():
            dkv_ref[...] = acc_ref[...].astype(dkv_ref.dtype)

    return pl.pallas_call(
        body,
        out_shape=[
            jax.ShapeDtypeStruct((nb * SEQ, D_MODEL), ACT_DTYPE),
            jax.ShapeDtypeStruct((nb * MEM_LEN, 2 * D_MODEL), ACT_DTYPE),
        ],
        grid=(nb, nq),
        in_specs=[
            pl.BlockSpec((tq, D_MODEL), lambda b, i: (b * nq + i, 0)),
            pl.BlockSpec((MEM_LEN, 2 * D_MODEL), lambda b, i: (b, 0)),
            pl.BlockSpec((tq, D_MODEL), lambda b, i: (b * nq + i, 0)),
        ],
        out_specs=[
            pl.BlockSpec((tq, D_MODEL), lambda b, i: (b * nq + i, 0)),
            pl.BlockSpec((MEM_LEN, 2 * D_MODEL), lambda b, i: (b, 0)),
        ],
        scratch_shapes=[pltpu.VMEM((MEM_LEN, 2 * D_MODEL), F32)],
        compiler_params=_cparams(("parallel", "arbitrary")),
        name="xa_bwd",
    )(q, kv, do)


def _adamw(parts, w, m, v, name, tr=128):
    n, r, c = parts.shape
    tr = _tile(r, tr, 16)

    def body(p_ref, w_ref, m_ref, v_ref, g_ref, d_ref, nm_ref, nv_ref):
        g = p_ref[0].astype(F32)
        for i in range(1, n):
            g = g + p_ref[i].astype(F32)
        m2 = ADAM_B1 * m_ref[...] + (1.0 - ADAM_B1) * g
        v2 = ADAM_B2 * v_ref[...] + (1.0 - ADAM_B2) * (g * g)
        m_hat = m2 / (1.0 - ADAM_B1 ** ADAM_STEP)
        v_hat = v2 / (1.0 - ADAM_B2 ** ADAM_STEP)
        g_ref[...] = g
        d_ref[...] = -ADAM_LR * (m_hat / (jnp.sqrt(v_hat) + ADAM_EPS) + ADAM_WD * w_ref[...])
        nm_ref[...] = m2
        nv_ref[...] = v2

    blk = pl.BlockSpec((tr, c), lambda i: (i, 0))
    return pl.pallas_call(
        body,
        out_shape=[jax.ShapeDtypeStruct((r, c), F32)] * 4,
        grid=(r // tr,),
        in_specs=[pl.BlockSpec((n, tr, c), lambda i: (0, i, 0)), blk, blk, blk],
        out_specs=[blk] * 4,
        compiler_params=_cparams(("parallel",)),
        name=name,
    )(parts, w, m, v)


def _flat_index(px, py, pc):
    return 4 * px + 2 * py + pc


def _all_gather(arrs, name):
    n = len(arrs)

    def body(*refs):
        ins, outs = refs[:n], refs[n:2 * n]
        send_sems, recv_sems, local_sems = refs[2 * n:]
        x, y, c = lax.axis_index("x"), lax.axis_index("y"), lax.axis_index("c")
        me, sibling = (x, y, c), (x, y, 1 - c)
        chips = [(1 - x, y), (x, 1 - y), (1 - x, 1 - y)]

        def copy(a, k, block, to, src=None):
            slot = outs[a].at[_flat_index(*block)]
            return pltpu.make_async_remote_copy(
                src_ref=slot if src is None else src, dst_ref=slot,
                send_sem=send_sems.at[a, k], recv_sem=recv_sems.at[a, k],
                device_id=to, device_id_type=MESH)

        mine = [pltpu.make_async_copy(ins[a], outs[a].at[_flat_index(*me)], local_sems.at[a]) for a in range(n)]
        for cp in mine:
            cp.start()
        first = []
        for a in range(n):
            first.append(copy(a, 0, me, sibling, src=ins[a]))
            first += [copy(a, 1 + j, me, (*chip, c), src=ins[a]) for j, chip in enumerate(chips)]
        for cp in first:
            cp.start()
        passed = []
        for j, chip in enumerate(chips):
            for a in range(n):
                copy(a, 1 + j, (*chip, c), me).wait_recv()
                fwd = copy(a, 4 + j, (*chip, c), sibling)
                fwd.start()
                passed.append(fwd)
        for a in range(n):
            copy(a, 0, sibling, me).wait_recv()
            for j, chip in enumerate(chips):
                copy(a, 4 + j, (*chip, 1 - c), me).wait_recv()
        for cp in first + passed:
            cp.wait_send()
        for cp in mine:
            cp.wait()

    any_spec = pl.BlockSpec(memory_space=pl.ANY)
    return pl.pallas_call(
        body,
        out_shape=[jax.ShapeDtypeStruct((N_DEV,) + a.shape, a.dtype) for a in arrs],
        in_specs=[any_spec] * n,
        out_specs=[any_spec] * n,
        scratch_shapes=[
            pltpu.SemaphoreType.DMA((n, N_DEV - 1)),
            pltpu.SemaphoreType.DMA((n, N_DEV - 1)),
            pltpu.SemaphoreType.DMA((n,)),
        ],
        name=name,
    )(*arrs)


def _scatter_blocks(arrs, name):
    n = len(arrs)

    def body(*refs):
        ins, outs = refs[:n], refs[n:2 * n]
        send_sems, recv_sems, local_sems = refs[2 * n:]
        x, y, c = lax.axis_index("x"), lax.axis_index("y"), lax.axis_index("c")
        me = _flat_index(x, y, c)

        def peer(k):
            return (1 - x if k & 4 else x, 1 - y if k & 2 else y, 1 - c if k & 1 else c)

        def copy(a, k):
            p = peer(k)
            return pltpu.make_async_remote_copy(
                src_ref=ins[a].at[_flat_index(*p)], dst_ref=outs[a].at[me],
                send_sem=send_sems.at[a, k - 1], recv_sem=recv_sems.at[a, k - 1],
                device_id=p, device_id_type=MESH)

        def landed(a, k):
            slot = outs[a].at[_flat_index(*peer(k))]
            return pltpu.make_async_remote_copy(
                src_ref=slot, dst_ref=slot, send_sem=send_sems.at[a, k - 1], recv_sem=recv_sems.at[a, k - 1],
                device_id=peer(k), device_id_type=MESH)

        mine = [pltpu.make_async_copy(ins[a].at[me], outs[a].at[me], local_sems.at[a]) for a in range(n)]
        for cp in mine:
            cp.start()
        sends = [copy(a, k) for a in range(n) for k in range(1, N_DEV)]
        for cp in sends:
            cp.start()
        for a in range(n):
            for k in range(1, N_DEV):
                landed(a, k).wait_recv()
        for cp in sends:
            cp.wait_send()
        for cp in mine:
            cp.wait()

    any_spec = pl.BlockSpec(memory_space=pl.ANY)
    return pl.pallas_call(
        body,
        out_shape=[jax.ShapeDtypeStruct(a.shape, a.dtype) for a in arrs],
        in_specs=[any_spec] * n,
        out_specs=[any_spec] * n,
        scratch_shapes=[
            pltpu.SemaphoreType.DMA((n, N_DEV - 1)),
            pltpu.SemaphoreType.DMA((n, N_DEV - 1)),
            pltpu.SemaphoreType.DMA((n,)),
        ],
        name=name,
    )(*arrs)


_BIG = ("w_in", "w_br_att", "w_br_ssm", "w_mix_out", "w_xq", "w_xkv", "w_xo", "w_gu", "w_down")
_COL_SHARDED = ("w_in", "w_xkv", "w_gu", "conv_w")
_SMALL = ("g_pre_mix", "conv_b", "dt_bias", "a_log", "d_skip", "g_ssm_norm", "g_post_mix", "g_pre_xa",
          "g_mem", "g_post_xa", "g_pre_ffn", "g_post_ffn")
_WEIGHTS = ("g_pre_mix", "w_in", "conv_w", "conv_b", "dt_bias", "a_log", "d_skip", "g_ssm_norm", "w_br_att",
            "w_br_ssm", "w_mix_out", "g_post_mix", "g_pre_xa", "g_mem", "w_xq", "w_xkv", "w_xo", "g_post_xa",
            "g_pre_ffn", "w_gu", "w_down", "g_post_ffn")
PACK_W = 8 * LANES


def _unshard(g, col):
    n, l, r, c = g.shape
    if col:
        return jnp.transpose(g, (1, 2, 0, 3)).reshape(l, r, n * c)
    return jnp.transpose(g, (1, 0, 2, 3)).reshape(l, n * r, c)


def _shard(w, col):
    l, r, c = w.shape
    if col:
        return jnp.transpose(w.reshape(l, r, N_DEV, c // N_DEV), (2, 0, 1, 3))
    return jnp.transpose(w.reshape(l, N_DEV, r // N_DEV, c), (1, 0, 2, 3))


def _permute_in(w, cfg):
    parts, off = [], 0
    for size in cfg.in_sizes:
        parts.append(w[..., off:off + size])
        off += size
    q, k, v, z, xbc, dt, ga, gs = parts
    pad = jnp.zeros(w.shape[:-1] + (LANES - cfg.heads,), w.dtype)
    return jnp.concatenate([z, ga, gs, q, k, v, xbc, dt, pad], axis=-1)


def _unpermute_in(w, cfg):
    c = cfg
    sl = lambda a, n: w[..., a:a + n]
    return jnp.concatenate([sl(c.q0, c.sbw), sl(c.k0, c.sbw), sl(c.v0, c.sbw), sl(c.z0, c.inner),
                            sl(c.xbc0, c.conv_dim), sl(c.dt0, c.heads), sl(c.ga0, c.d), sl(c.gs0, c.d)], axis=-1)


def _pack(arrs):
    flat = jnp.concatenate([a.reshape(-1).astype(F32) for a in arrs])
    rows = -(-flat.shape[0] // PACK_W)
    rows = -(-rows // 8) * 8
    return jnp.pad(flat, (0, rows * PACK_W - flat.shape[0])).reshape(rows, PACK_W)


def _unpack(p, shapes):
    flat = p.reshape(-1)
    out, off = [], 0
    for s in shapes:
        size = math.prod(s)
        out.append(flat[off:off + size].reshape(s))
        off += size
    return out


def _vec3(a, width=None):
    if width is not None and a.shape[1] < width:
        a = jnp.pad(a, ((0, 0), (0, width - a.shape[1])))
    return a[:, None, :]


def _forward_layer(l, xin, h1, memf, tgt, p, cfg, nb, last):
    t = xin.shape[0]
    d = cfg.d
    s = {"x_in": xin, "h1": h1}
    proj = _mm(h1, p["w_in"], b_pre=(l,), name="mm_proj")
    s["proj"] = proj
    s["o_att"] = _sb_fwd(proj, cfg, nb)
    s["xbc"] = _conv_fwd(proj, p["conv_w"], p["conv_b"], l, cfg, nb)
    s["y"], s["states"] = _ssd_fwd(s["xbc"], proj, p["dt_bias"], p["a_log"], p["d_skip"], l, cfg, nb)
    s["o_ssm"] = _rowwise(_f_gate_norm, "gate_norm_fwd", t, [_full(s["y"]), (proj, cfg.inner, 0)],
                          [(p["g_ssm_norm"], l)], [(cfg.inner, ACT_DTYPE)])[0]
    s["ba"] = _mm(s["o_att"], p["w_br_att"], b_pre=(l,), name="mm_br_att")
    s["bs"] = _mm(s["o_ssm"], p["w_br_ssm"], b_pre=(l,), name="mm_br_ssm")
    s["merged"] = _rowwise(_f_merge, "merge_fwd", t,
                           [(proj, d, cfg.ga0 // d), (proj, d, cfg.gs0 // d), _full(s["ba"]), _full(s["bs"])],
                           [], [(d, ACT_DTYPE)])[0]
    s["mo"] = _mm(s["merged"], p["w_mix_out"], b_pre=(l,), name="mm_mix_out")
    s["x1"], s["h2"] = _rowwise(_f_post_pre, "post_pre_mix", t, [_full(xin), _full(s["mo"])],
                                [(p["g_post_mix"], l), (p["g_pre_xa"], l)], [(d, F32), (d, ACT_DTYPE)])
    s["mem_n"] = _rowwise(_rms, "mem_norm", memf.shape[0], [_full(memf)], [(p["g_mem"], l)], [(d, ACT_DTYPE)])[0]
    s["q"] = _mm(s["h2"], p["w_xq"], b_pre=(l,), name="mm_xq")
    s["kv"] = _mm(s["mem_n"], p["w_xkv"], b_pre=(l,), name="mm_xkv")
    s["o_xa"] = _xa_fwd(s["q"], s["kv"], cfg, nb)
    s["xo"] = _mm(s["o_xa"], p["w_xo"], b_pre=(l,), name="mm_xo")
    s["x2"], s["h3"] = _rowwise(_f_post_pre, "post_pre_xa", t, [_full(s["x1"]), _full(s["xo"])],
                                [(p["g_post_xa"], l), (p["g_pre_ffn"], l)], [(d, F32), (d, ACT_DTYPE)])
    s["gu"] = _mm(s["h3"], p["w_gu"], b_pre=(l,), name="mm_gu")
    s["act"] = _rowwise(_f_swiglu, "swiglu_fwd", t, [(s["gu"], cfg.ffn, 0), (s["gu"], cfg.ffn, 1)], [],
                        [(cfg.ffn, ACT_DTYPE)])[0]
    s["dn"] = _mm(s["act"], p["w_down"], b_pre=(l,), name="mm_down")
    if last:
        nxt = _rowwise(_f_final, "final_loss", t, [_full(s["x2"]), _full(s["dn"]), _full(tgt)],
                       [(p["g_post_ffn"], l)], [(d, F32)], acc_out=[(1, d)])
    else:
        nxt = _rowwise(_f_post_pre, "post_pre_ffn", t, [_full(s["x2"]), _full(s["dn"])],
                       [(p["g_post_ffn"], l), (p["g_pre_mix"], l + 1)], [(d, F32), (d, ACT_DTYPE)])
    return s, nxt


def _backward_layer(l, s, dx, d_dn, memf, p, cfg, nb, prev_dn):
    t = dx.shape[0]
    d = cfg.d
    g = {}
    dact = _mm(d_dn, p["w_down"], tb=True, b_pre=(l,), out_dtype=ACT_DTYPE, name="mm_d_act", tn=1408)
    g["w_down"] = _mm(s["act"], d_dn, ta=True, name="mm_dw_down")
    dgu = _rowwise(_f_swiglu_bwd, "swiglu_bwd", t, [(s["gu"], cfg.ffn, 0), (s["gu"], cfg.ffn, 1), _full(dact)], [],
                   [(2 * cfg.ffn, ACT_DTYPE)])[0]
    dh3 = _mm(dgu, p["w_gu"], tb=True, b_pre=(l,), name="mm_d_h3", tk=1408)
    g["w_gu"] = _mm(s["h3"], dgu, ta=True, name="mm_dw_gu")
    dx2, d_xo, g["g_pre_ffn"], g["g_post_xa"] = _rowwise(
        _f_pre_post_bwd, "pre_post_bwd_ffn", t, [_full(s["x2"]), _full(dh3), _full(dx), _full(s["xo"])],
        [(p["g_pre_ffn"], l), (p["g_post_xa"], l)], [(d, F32), (d, ACT_DTYPE)], acc_out=[(1, d), (1, d)])
    do_xa = _mm(d_xo, p["w_xo"], tb=True, b_pre=(l,), out_dtype=ACT_DTYPE, name="mm_d_oxa")
    g["w_xo"] = _mm(s["o_xa"], d_xo, ta=True, name="mm_dw_xo")
    dq, dkv = _xa_bwd(s["q"], s["kv"], do_xa, cfg, nb)
    dh2 = _mm(dq, p["w_xq"], tb=True, b_pre=(l,), name="mm_d_h2")
    g["w_xq"] = _mm(s["h2"], dq, ta=True, name="mm_dw_xq")
    dmem_n = _mm(dkv, p["w_xkv"], tb=True, b_pre=(l,), name="mm_d_mem")
    g["w_xkv"] = _mm(s["mem_n"], dkv, ta=True, name="mm_dw_xkv")
    g["g_mem"] = _rowwise(_f_gain_bwd, "mem_norm_bwd", memf.shape[0], [_full(memf), _full(dmem_n)],
                          [(p["g_mem"], l)], [], acc_out=[(1, d)])[0]
    dx1, d_mo, g["g_pre_xa"], g["g_post_mix"] = _rowwise(
        _f_pre_post_bwd, "pre_post_bwd_xa", t, [_full(s["x1"]), _full(dh2), _full(dx2), _full(s["mo"])],
        [(p["g_pre_xa"], l), (p["g_post_mix"], l)], [(d, F32), (d, ACT_DTYPE)], acc_out=[(1, d), (1, d)])
    dmerged = _mm(d_mo, p["w_mix_out"], tb=True, b_pre=(l,), out_dtype=ACT_DTYPE, name="mm_d_merged")
    g["w_mix_out"] = _mm(s["merged"], d_mo, ta=True, name="mm_dw_mix_out")
    proj = s["proj"]
    dgg, dba, dbs = _rowwise(
        _f_merge_bwd, "merge_bwd", t,
        [(proj, d, cfg.ga0 // d), (proj, d, cfg.gs0 // d), _full(s["ba"]), _full(s["bs"]), _full(dmerged)], [],
        [(2 * d, ACT_DTYPE), (d, ACT_DTYPE), (d, ACT_DTYPE)])
    do_att = _mm(dba, p["w_br_att"], tb=True, b_pre=(l,), name="mm_d_oatt")
    g["w_br_att"] = _mm(s["o_att"], dba, ta=True, name="mm_dw_br_att")
    do_ssm = _mm(dbs, p["w_br_ssm"], tb=True, b_pre=(l,), out_dtype=ACT_DTYPE, name="mm_d_ossm")
    g["w_br_ssm"] = _mm(s["o_ssm"], dbs, ta=True, name="mm_dw_br_ssm")
    dy, dz, g["g_ssm_norm"] = _rowwise(
        _f_gate_norm_bwd, "gate_norm_bwd", t, [_full(s["y"]), (proj, cfg.inner, 0), _full(do_ssm)],
        [(p["g_ssm_norm"], l)], [(cfg.inner, F32), (cfg.inner, ACT_DTYPE)], acc_out=[(1, cfg.inner)])
    dxbc, ddt_raw, dvec = _ssd_bwd(s["xbc"], proj, s["states"], dy, p["dt_bias"], p["a_log"], p["d_skip"], l, cfg, nb)
    g["dt_bias"], g["a_log"], g["d_skip"] = (dvec[i:i + 1, :cfg.heads] for i in range(3))
    dxbc_raw, g["conv_w"], g["conv_b"] = _conv_bwd(proj, dxbc, p["conv_w"], p["conv_b"], l, cfg, nb)
    dq_sb, dk_sb, dv_sb = _sb_bwd(proj, do_att, cfg, nb)
    dproj = jnp.concatenate([dz, dgg, dq_sb, dk_sb.astype(ACT_DTYPE), dv_sb.astype(ACT_DTYPE), dxbc_raw, ddt_raw], axis=1)
    dh1 = _mm(dproj, p["w_in"], tb=True, b_pre=(l,), name="mm_d_h1", tk=1152)
    g["w_in"] = _mm(s["h1"], dproj, ta=True, name="mm_dw_in")
    if prev_dn is None:
        dx0, g["g_pre_mix"] = _rowwise(_f_pre_bwd, "pre_bwd_first", t, [_full(s["x_in"]), _full(dh1), _full(dx1)],
                                       [(p["g_pre_mix"], l)], [(d, F32)], acc_out=[(1, d)])
        return g, dx0, None, None
    dx0, d_dn_prev, g["g_pre_mix"], g_post_prev = _rowwise(
        _f_pre_post_bwd, "pre_post_bwd_mix", t, [_full(s["x_in"]), _full(dh1), _full(dx1), _full(prev_dn)],
        [(p["g_pre_mix"], l), (p["g_post_ffn"], l - 1)], [(d, F32), (d, ACT_DTYPE)], acc_out=[(1, d), (1, d)])
    return g, dx0, d_dn_prev, g_post_prev


def kernel(x, mem, g_pre_mix, w_in, conv_w, conv_b, dt_bias, a_log, d_skip, g_ssm_norm, w_br_att, w_br_ssm, w_mix_out, g_post_mix, g_pre_xa, g_mem, w_xq, w_xkv, w_xo, g_post_xa, g_pre_ffn, w_gu, w_down, g_post_ffn, loss_target, m_g_pre_mix, m_w_in, m_conv_w, m_conv_b, m_dt_bias, m_a_log, m_d_skip, m_g_ssm_norm, m_w_br_att, m_w_br_ssm, m_w_mix_out, m_g_post_mix, m_g_pre_xa, m_g_mem, m_w_xq, m_w_xkv, m_w_xo, m_g_post_xa, m_g_pre_ffn, m_w_gu, m_w_down, m_g_post_ffn, v_g_pre_mix, v_w_in, v_conv_w, v_conv_b, v_dt_bias, v_a_log, v_d_skip, v_g_ssm_norm, v_w_br_att, v_w_br_ssm, v_w_mix_out, v_g_post_mix, v_g_pre_xa, v_g_mem, v_w_xq, v_w_xkv, v_w_xo, v_g_post_xa, v_g_pre_ffn, v_w_gu, v_w_down, v_g_post_ffn):
    vals = dict(locals())
    cfg = _Cfg()
    nb = x.shape[0]
    t = nb * SEQ
    d = cfg.d
    depth = g_pre_mix.shape[0]

    wire = [vals[n].astype(WIRE_DTYPE) for n in _BIG] + [conv_w]
    gathered = _all_gather(wire, "ag_weights")
    p = {n: _unshard(gw, n in _COL_SHARDED) for n, gw in zip(_BIG + ("conv_w",), gathered)}
    p["w_in"] = _permute_in(p["w_in"], cfg)
    for n in _SMALL:
        p[n] = _vec3(vals[n], LANES if n in ("dt_bias", "a_log", "d_skip") else None)

    xf = x.reshape(t, d)
    memf = mem.reshape(nb * MEM_LEN, d)
    tgt = loss_target.reshape(t, d)
    h = _rowwise(_rms, "pre_norm_first", t, [_full(xf)], [(p["g_pre_mix"], 0)], [(d, ACT_DTYPE)])[0]
    saved = []
    xcur = xf
    for l in range(depth):
        s, nxt = _forward_layer(l, xcur, h, memf, tgt, p, cfg, nb, l == depth - 1)
        saved.append(s)
        if l < depth - 1:
            xcur, h = nxt
    dx, loss_row = nxt
    loss = lax.psum(0.5 * jnp.sum(loss_row) / d, AXES)

    top = saved[-1]
    d_dn, g_post_top = _rowwise(lambda ysub, dxo, gp: _rms_bwd(ysub, gp, dxo), "post_bwd_last", t,
                                [_full(top["dn"]), _full(dx)], [(p["g_post_ffn"], depth - 1)], [(d, ACT_DTYPE)],
                                acc_out=[(1, d)])
    grads = [None] * depth
    post_ffn = [None] * depth
    post_ffn[depth - 1] = g_post_top
    for l in reversed(range(depth)):
        prev_dn = saved[l - 1]["dn"] if l > 0 else None
        grads[l], dx, d_dn, g_post_prev = _backward_layer(l, saved[l], dx, d_dn, memf, p, cfg, nb, prev_dn)
        if l > 0:
            post_ffn[l - 1] = g_post_prev
    for l in range(depth):
        grads[l]["g_post_ffn"] = post_ffn[l]
    grad_x = dx.reshape(x.shape)
    stacked = {n: jnp.stack([grads[l][n] for l in range(depth)]) for n in _WEIGHTS}

    stacked["w_in"] = _unpermute_in(stacked["w_in"], cfg)
    blocks = [_shard(stacked[n], n in _COL_SHARDED).astype(WIRE_DTYPE) for n in _BIG]
    landed = _scatter_blocks(blocks, "scatter_grads")
    out = {}
    for n, parts in zip(_BIG, landed):
        shp = vals[n].shape
        flat = lambda a: a.reshape(shp[0] * shp[1], shp[2])
        res = _adamw(parts.reshape(N_DEV, shp[0] * shp[1], shp[2]), flat(vals[n]), flat(vals["m_" + n]),
                     flat(vals["v_" + n]), "adamw_" + n)
        out[n] = [r.reshape(shp) for r in res]

    small_shapes = [vals[n].shape for n in _SMALL]
    pack_g = _pack([stacked[n] for n in _SMALL])
    conv_g = stacked["conv_w"].reshape(depth * SSM_CONV, cfg.conv_dim)
    parts_small, parts_conv = _all_gather([pack_g, conv_g], "ag_small_grads")
    res = _adamw(parts_small, _pack([vals[n] for n in _SMALL]), _pack([vals["m_" + n] for n in _SMALL]),
                 _pack([vals["v_" + n] for n in _SMALL]), "adamw_small")
    unpacked = [_unpack(r, small_shapes) for r in res]
    for i, n in enumerate(_SMALL):
        out[n] = [unpacked[j][i] for j in range(4)]
    cs = conv_w.shape[2]
    me = _flat_index(lax.axis_index("x"), lax.axis_index("y"), lax.axis_index("c"))
    parts_conv = lax.dynamic_slice_in_dim(parts_conv, me * cs, cs, axis=2)
    flat = lambda a: a.reshape(depth * SSM_CONV, cs)
    res = _adamw(parts_conv, flat(conv_w), flat(m_conv_w), flat(v_conv_w), "adamw_conv_w")
    out["conv_w"] = [r.reshape(conv_w.shape) for r in res]

    return (loss, grad_x, *[out[n][0] for n in _WEIGHTS], *[out[n][1] for n in _WEIGHTS],
            *[out[n][2] for n in _WEIGHTS], *[out[n][3] for n in _WEIGHTS])
```

```python
import functools
import math

import jax
import jax.numpy as jnp
from jax import lax
from jax.experimental import pallas as pl
from jax.experimental.pallas import tpu as pltpu

F32 = jnp.float32
BF16 = jnp.bfloat16
MXU_DTYPE = BF16
ACT_DTYPE = BF16
WIRE_DTYPE = BF16

D_MODEL = 1024
SEQ = 2048
DEPTH = 4
MEM_LEN = 256
RMS_EPS = 1e-6
SB_HEADS = 16
SB_HEAD_DIM = 64
SB_BLOCK = 128
SSM_INNER = 2 * D_MODEL
SSM_HEAD_DIM = 64
SSM_GROUPS = 4
SSM_STATE = 128
SSM_CONV = 4
SSM_CHUNK = 128
XA_HEADS = 4
FFN_HIDDEN = ((8 * D_MODEL + 767) // 768) * 256
ADAM_LR = 0.001
ADAM_B1 = 0.9
ADAM_B2 = 0.999
ADAM_EPS = 1e-08
ADAM_WD = 0.01
ADAM_STEP = 10

N_DEV = 8
LANES = 128
VMEM_LIMIT_BYTES = 56 * 1024 * 1024

AXES = ("x", "y", "c")
MESH = pl.DeviceIdType.MESH


class _Cfg:
    def __init__(self):
        self.d = D_MODEL
        self.sbw = SB_HEADS * SB_HEAD_DIM
        self.inner = SSM_INNER
        self.heads = SSM_INNER // SSM_HEAD_DIM
        self.epg = self.heads // SSM_GROUPS
        self.gn = SSM_GROUPS * SSM_STATE
        self.conv_dim = SSM_INNER + 2 * self.gn
        self.ffn = FFN_HIDDEN
        self.xa_dim = D_MODEL // XA_HEADS
        self.in_sizes = (self.sbw, self.sbw, self.sbw, self.inner, self.conv_dim, self.heads, self.d, self.d)
        self.in_width = sum(self.in_sizes)
        self.z0 = 0
        self.ga0 = self.inner
        self.gs0 = self.ga0 + self.d
        self.q0 = self.gs0 + self.d
        self.k0 = self.q0 + self.sbw
        self.v0 = self.k0 + self.sbw
        self.xbc0 = self.v0 + self.sbw
        self.dt0 = self.xbc0 + self.conv_dim
        self.proj_w = self.dt0 + LANES
        assert self.heads <= LANES


def _cparams(sem=None):
    return pltpu.CompilerParams(dimension_semantics=sem, vmem_limit_bytes=VMEM_LIMIT_BYTES)


def _tile(n, pref, mult):
    if n <= pref:
        return n
    t = (pref // mult) * mult
    while t >= mult:
        if n % t == 0:
            return t
        t -= mult
    return n


def _mm(a, b, *, ta=False, tb=False, out_dtype=F32, name, a_pre=(), b_pre=(), tm=512, tn=1152, tk=2048):
    ash = a.shape[len(a_pre):]
    bsh = b.shape[len(b_pre):]
    kk, m = (ash if ta else ash[::-1])
    if tb:
        n, k2 = bsh
    else:
        k2, n = bsh
    assert kk == k2, (name, a.shape, b.shape)
    tm = _tile(m, tm, LANES if ta else 16)
    tn = _tile(n, tn, LANES)
    tk = _tile(kk, tk, LANES if (tb or not ta) else 16)
    nk = kk // tk
    dims = (((0 if ta else 1,), (1 if tb else 0,)), ((), ()))
    npa, npb = len(a_pre), len(b_pre)

    def body(a_ref, b_ref, o_ref, *scratch):
        av = a_ref[...].astype(MXU_DTYPE)
        bv = b_ref[...].astype(MXU_DTYPE)
        part = lax.dot_general(av, bv, dims, preferred_element_type=F32)
        if nk == 1:
            o_ref[...] = part.astype(out_dtype)
        else:
            acc_ref, = scratch
            k = pl.program_id(2)

            @pl.when(k == 0)
            def _():
                acc_ref[...] = part

            @pl.when(k > 0)
            def _():
                acc_ref[...] += part

            @pl.when(k == nk - 1)
            def _():
                o_ref[...] = acc_ref[...].astype(out_dtype)

    if ta:
        a_spec = pl.BlockSpec((None,) * npa + (tk, tm), lambda i, j, k: a_pre + (k, i))
    else:
        a_spec = pl.BlockSpec((None,) * npa + (tm, tk), lambda i, j, k: a_pre + (i, k))
    if tb:
        b_spec = pl.BlockSpec((None,) * npb + (tn, tk), lambda i, j, k: b_pre + (j, k))
    else:
        b_spec = pl.BlockSpec((None,) * npb + (tk, tn), lambda i, j, k: b_pre + (k, j))
    return pl.pallas_call(
        body,
        out_shape=jax.ShapeDtypeStruct((m, n), out_dtype),
        grid=(m // tm, n // tn, nk),
        in_specs=[a_spec, b_spec],
        out_specs=pl.BlockSpec((tm, tn), lambda i, j, k: (i, j)),
        scratch_shapes=[] if nk == 1 else [pltpu.VMEM((tm, tn), F32)],
        compiler_params=_cparams(("parallel", "parallel", "arbitrary")),
        name=name,
    )(a, b)


def _rowwise(fn, name, rows, row_in, vec_in, row_out, acc_out=(), tr=256):
    tr = _tile(rows, tr, 16)
    n_in = len(row_in) + len(vec_in)
    n_ro = len(row_out)

    def body(*refs):
        ins = [r[...].astype(F32) for r in refs[:n_in]]
        outs = fn(*ins)
        if not isinstance(outs, (tuple, list)):
            outs = (outs,)
        out_refs = refs[n_in:]
        for o_ref, val in zip(out_refs[:n_ro], outs[:n_ro]):
            o_ref[...] = val.astype(o_ref.dtype)
        if acc_out:
            i = pl.program_id(0)
            for o_ref, val in zip(out_refs[n_ro:], outs[n_ro:]):
                @pl.when(i == 0)
                def _(o_ref=o_ref, val=val):
                    o_ref[...] = val

                @pl.when(i > 0)
                def _(o_ref=o_ref, val=val):
                    o_ref[...] += val

    in_specs = [pl.BlockSpec((tr, w), functools.partial(lambda i, cb: (i, cb), cb=cb)) for (_, w, cb) in row_in]
    in_specs += [pl.BlockSpec((None,) + v.shape[1:], functools.partial(lambda i, l: (l, 0, 0), l=l)) for (v, l) in vec_in]
    out_shape = [jax.ShapeDtypeStruct((rows, w), dt) for (w, dt) in row_out]
    out_shape += [jax.ShapeDtypeStruct(s, F32) for s in acc_out]
    out_specs = [pl.BlockSpec((tr, w), lambda i: (i, 0)) for (w, _) in row_out]
    out_specs += [pl.BlockSpec(s, lambda i: (0, 0)) for s in acc_out]
    res = pl.pallas_call(
        body,
        out_shape=out_shape,
        grid=(rows // tr,),
        in_specs=in_specs,
        out_specs=out_specs,
        compiler_params=_cparams(("arbitrary",) if acc_out else ("parallel",)),
        name=name,
    )(*[a for (a, _, _) in row_in], *[v for (v, _) in vec_in])
    return res


def _rms(x, g):
    r = lax.rsqrt(jnp.mean(x * x, axis=-1, keepdims=True) + RMS_EPS)
    return x * r * g


def _rms_bwd(x, g, dy):
    r = lax.rsqrt(jnp.mean(x * x, axis=-1, keepdims=True) + RMS_EPS)
    xh = x * r
    dxh = dy * g
    dx = r * (dxh - xh * jnp.mean(dxh * xh, axis=-1, keepdims=True))
    return dx, jnp.sum(dy * xh, axis=0, keepdims=True)


def _silu(x):
    return x * jax.nn.sigmoid(x)


def _silu_grad(x):
    s = jax.nn.sigmoid(x)
    return s * (1.0 + x * (1.0 - s))


def _softplus(x):
    return jnp.maximum(x, 0.0) + jnp.log1p(jnp.exp(-jnp.abs(x)))


def _full(a):
    return (a, a.shape[1], 0)


def _f_post_pre(x, ysub, g_post, g_pre):
    xn = x + _rms(ysub, g_post)
    return xn, _rms(xn, g_pre)


def _f_final(x, ysub, tgt, g_post):
    err = x + _rms(ysub, g_post) - tgt
    return err * (1.0 / D_MODEL), jnp.sum(err * err, axis=0, keepdims=True)


def _f_pre_post_bwd(xmid, dh, dxo, ysub, g_pre, g_post):
    d1, dg_pre = _rms_bwd(xmid, g_pre, dh)
    dxm = dxo + d1
    dys, dg_post = _rms_bwd(ysub, g_post, dxm)
    return dxm, dys, dg_pre, dg_post


def _f_pre_bwd(x, dh, dxo, g_pre):
    d1, dg_pre = _rms_bwd(x, g_pre, dh)
    return dxo + d1, dg_pre


def _f_gain_bwd(x, dy, g):
    return _rms_bwd(x, g, dy)[1]


def _group_norm_parts(u):
    gw = u.shape[1] // SSM_GROUPS
    parts = []
    for gi in range(SSM_GROUPS):
        ug = u[:, gi * gw:(gi + 1) * gw]
        r = lax.rsqrt(jnp.mean(ug * ug, axis=-1, keepdims=True) + RMS_EPS)
        parts.append((ug * r, r))
    return gw, parts


def _f_gate_norm(y, z, g):
    _, parts = _group_norm_parts(y * _silu(z))
    return jnp.concatenate([uh for uh, _ in parts], axis=1) * g


def _f_gate_norm_bwd(y, z, do, g):
    sz = _silu(z)
    gw, parts = _group_norm_parts(y * sz)
    dxh = do * g
    du = []
    for gi, (uh, r) in enumerate(parts):
        dg_ = dxh[:, gi * gw:(gi + 1) * gw]
        du.append(r * (dg_ - uh * jnp.mean(dg_ * uh, axis=-1, keepdims=True)))
    du = jnp.concatenate(du, axis=1)
    uh_all = jnp.concatenate([uh for uh, _ in parts], axis=1)
    return du * sz, du * y * _silu_grad(z), jnp.sum(do * uh_all, axis=0, keepdims=True)


def _f_merge(ga, gs, ba, bs):
    return jax.nn.sigmoid(ga) * ba + jax.nn.sigmoid(gs) * bs


def _f_merge_bwd(ga, gs, ba, bs, dm):
    sa, ss = jax.nn.sigmoid(ga), jax.nn.sigmoid(gs)
    dgg = jnp.concatenate([dm * ba * sa * (1.0 - sa), dm * bs * ss * (1.0 - ss)], axis=1)
    return dgg, dm * sa, dm * ss


def _f_swiglu(gate, up):
    return _silu(gate) * up


def _f_swiglu_bwd(gate, up, da):
    return jnp.concatenate([da * up * _silu_grad(gate), da * _silu(gate)], axis=1)


def _split_dot(x, u):
    hi = x.astype(BF16)
    lo = (x - hi.astype(F32)).astype(BF16)
    return jnp.dot(hi, u, preferred_element_type=F32) + jnp.dot(lo, u, preferred_element_type=F32)


SB_ROWS = 512
SB_UNROLL = 4
C00 = (((0,), (0,)), ((), ()))
C11 = (((1,), (1,)), ((), ()))


def _sb_setup(q_ref, tq):
    hp = LANES // SB_HEAD_DIM
    lane = lax.broadcasted_iota(jnp.int32, (1, LANES), 1)
    heads = [jnp.logical_and(lane >= h * SB_HEAD_DIM, lane < (h + 1) * SB_HEAD_DIM) for h in range(hp)]
    qs = q_ref[...] * (SB_HEAD_DIM ** -0.5)
    q_h = [jnp.where(hm, qs, 0.0).astype(MXU_DTYPE) for hm in heads]
    row = lax.broadcasted_iota(jnp.int32, (tq, SB_BLOCK), 0)
    col = lax.broadcasted_iota(jnp.int32, (tq, SB_BLOCK), 1)
    sq_row = lax.broadcasted_iota(jnp.int32, (SB_BLOCK, SB_BLOCK), 0)
    sq_col = lax.broadcasted_iota(jnp.int32, (SB_BLOCK, SB_BLOCK), 1)
    return heads, q_h, col - row, sq_row, sq_col


def _sb_scores(q, kj, mask):
    z = lax.dot_general(q, kj, C11, preferred_element_type=F32)
    lm = -(jnp.maximum(z, 0.0) + jnp.log(1.0 + jnp.exp(-jnp.abs(z))))
    return z, jnp.where(mask, lm, 0.0)


def _sb_fwd(proj, cfg, nb):
    blk = SB_BLOCK
    tq = _tile(SEQ, SB_ROWS, blk)
    nq = SEQ // tq
    kpq = tq // blk
    unr = math.gcd(kpq, SB_UNROLL)
    hp = LANES // SB_HEAD_DIM
    ncb = cfg.sbw // LANES
    qb, kb, vb = cfg.q0 // LANES, cfg.k0 // LANES, cfg.v0 // LANES

    def body(q_ref, k_ref, v_ref, o_ref):
        i = pl.program_id(2)
        heads, q_h, cmr, sq_row, sq_col = _sb_setup(q_ref, tq)
        u_rev = (sq_row >= sq_col).astype(BF16)

        def step(n, carry):
            acc, runs = carry
            runs = list(runs)
            for jj in range(unr):
                j = (i + 1) * kpq - 1 - (n * unr + jj)
                rows = pl.ds(pl.multiple_of(j * blk, blk), blk)
                kj = k_ref[rows, :].astype(MXU_DTYPE)
                vj = v_ref[rows, :].astype(MXU_DTYPE)
                mask = cmr < i * tq - j * blk
                for h in range(hp):
                    z, lm = _sb_scores(q_h[h], kj, mask)
                    cs = _split_dot(lm, u_rev)
                    w = jnp.where(mask, jnp.exp(z + cs + runs[h]), 0.0)
                    acc = acc + jnp.dot(w.astype(MXU_DTYPE), jnp.where(heads[h], vj, 0), preferred_element_type=F32)
                    runs[h] = runs[h] + cs[:, 0:1]
            return acc, tuple(runs)

        init = (jnp.zeros((tq, LANES), F32), tuple(jnp.zeros((tq, 1), F32) for _ in range(hp)))
        acc, _ = lax.fori_loop(0, (i + 1) * (kpq // unr), step, init)
        o_ref[...] = acc

    return pl.pallas_call(
        body,
        out_shape=jax.ShapeDtypeStruct((nb * SEQ, cfg.sbw), F32),
        grid=(nb, ncb, nq),
        in_specs=[
            pl.BlockSpec((tq, LANES), lambda b, c, i: (b * nq + i, qb + c)),
            pl.BlockSpec((SEQ, LANES), lambda b, c, i: (b, kb + c)),
            pl.BlockSpec((SEQ, LANES), lambda b, c, i: (b, vb + c)),
        ],
        out_specs=pl.BlockSpec((tq, LANES), lambda b, c, i: (b * nq + i, c)),
        compiler_params=_cparams(("parallel", "parallel", "arbitrary")),
        name="sb_fwd",
    )(proj, proj, proj)


def _sb_bwd(proj, do_att, cfg, nb):
    blk = SB_BLOCK
    tq = _tile(SEQ, SB_ROWS, blk)
    nq = SEQ // tq
    kpq = tq // blk
    unr = math.gcd(kpq, SB_UNROLL)
    hp = LANES // SB_HEAD_DIM
    ncb = cfg.sbw // LANES
    scale = SB_HEAD_DIM ** -0.5
    qb, kb, vb = cfg.q0 // LANES, cfg.k0 // LANES, cfg.v0 // LANES

    def body(q_ref, k_ref, v_ref, do_ref, dq_ref, dk_ref, dv_ref, g_ref, z_ref):
        i = pl.program_id(2)

        @pl.when(i == 0)
        def _():
            dk_ref[...] = jnp.zeros_like(dk_ref)
            dv_ref[...] = jnp.zeros_like(dv_ref)

        heads, q_h, cmr, sq_row, sq_col = _sb_setup(q_ref, tq)
        u_rev = (sq_row >= sq_col).astype(BF16)
        u_fwd = (sq_row <= sq_col).astype(BF16)
        do = do_ref[...]
        do_h = [jnp.where(hm, do, 0.0).astype(MXU_DTYPE) for hm in heads]

        def sweep_left(n, runs):
            runs = list(runs)
            for jj in range(unr):
                j = (i + 1) * kpq - 1 - (n * unr + jj)
                rows = pl.ds(pl.multiple_of(j * blk, blk), blk)
                kj = k_ref[rows, :].astype(MXU_DTYPE)
                vj = v_ref[rows, :].astype(MXU_DTYPE)
                mask = cmr < i * tq - j * blk
                dv = jnp.zeros((blk, LANES), F32)
                for h in range(hp):
                    z, lm = _sb_scores(q_h[h], kj, mask)
                    cs = _split_dot(lm, u_rev)
                    a = jnp.where(mask, jnp.exp(z + cs + runs[h]), 0.0)
                    da = lax.dot_general(do_h[h], vj, C11, preferred_element_type=F32)
                    dv = dv + lax.dot_general(a.astype(MXU_DTYPE), do_h[h], C00, preferred_element_type=F32)
                    g_ref[h, j] = a * da
                    z_ref[h, j] = jax.nn.sigmoid(z)
                    runs[h] = runs[h] + cs[:, 0:1]
                dv_ref[rows, :] += dv
            return tuple(runs)

        trips = (i + 1) * (kpq // unr)
        lax.fori_loop(0, trips, sweep_left, tuple(jnp.zeros((tq, 1), F32) for _ in range(hp)))

        def sweep_right(n, carry):
            dq, runs = carry
            runs = list(runs)
            for jj in range(unr):
                j = n * unr + jj
                rows = pl.ds(pl.multiple_of(j * blk, blk), blk)
                kj = k_ref[rows, :].astype(MXU_DTYPE)
                mask = cmr < i * tq - j * blk
                dk = jnp.zeros((blk, LANES), F32)
                for h in range(hp):
                    g = g_ref[h, j]
                    g_upto = _split_dot(g, u_fwd) + runs[h]
                    dz = jnp.where(mask, g - z_ref[h, j] * g_upto, 0.0).astype(MXU_DTYPE)
                    dq = dq + jnp.dot(dz, jnp.where(heads[h], kj, 0), preferred_element_type=F32)
                    dk = dk + lax.dot_general(dz, q_h[h], C00, preferred_element_type=F32)
                    runs[h] = runs[h] + jnp.sum(g, axis=1, keepdims=True)
                dk_ref[rows, :] += dk
            return dq, tuple(runs)

        init = (jnp.zeros((tq, LANES), F32), tuple(jnp.zeros((tq, 1), F32) for _ in range(hp)))
        dq, _ = lax.fori_loop(0, trips, sweep_right, init)
        dq_ref[...] = (dq * scale).astype(dq_ref.dtype)

    kv_spec_out = pl.BlockSpec((SEQ, LANES), lambda b, c, i: (b, c))
    q_spec_out = pl.BlockSpec((tq, LANES), lambda b, c, i: (b * nq + i, c))
    return pl.pallas_call(
        body,
        out_shape=[
            jax.ShapeDtypeStruct((nb * SEQ, cfg.sbw), ACT_DTYPE),
            jax.ShapeDtypeStruct((nb * SEQ, cfg.sbw), F32),
            jax.ShapeDtypeStruct((nb * SEQ, cfg.sbw), F32),
        ],
        grid=(nb, ncb, nq),
        in_specs=[
            pl.BlockSpec((tq, LANES), lambda b, c, i: (b * nq + i, qb + c)),
            pl.BlockSpec((SEQ, LANES), lambda b, c, i: (b, kb + c)),
            pl.BlockSpec((SEQ, LANES), lambda b, c, i: (b, vb + c)),
            q_spec_out,
        ],
        out_specs=[q_spec_out, kv_spec_out, kv_spec_out],
        scratch_shapes=[pltpu.VMEM((hp, SEQ // blk, tq, blk), F32), pltpu.VMEM((hp, SEQ // blk, tq, blk), F32)],
        compiler_params=_cparams(("parallel", "parallel", "arbitrary")),
        name="sb_bwd",
    )(proj, proj, proj, do_att)


CONV_COLS = 256


def _conv_pre(x, w, b, t):
    kw = SSM_CONV
    shifted = []
    pre = b + w[kw - 1:kw, :] * x
    for k in range(kw - 1):
        d = kw - 1 - k
        xs = jnp.where(t >= d, pltpu.roll(x, d, 0), 0.0)
        shifted.append(xs)
        pre = pre + w[k:k + 1, :] * xs
    shifted.append(x)
    return pre, shifted


def _conv_fwd(proj, conv_w, conv_b, l, cfg, nb):
    cw = CONV_COLS
    ncb = cfg.conv_dim // cw
    xb = cfg.xbc0 // cw

    def body(x_ref, w_ref, b_ref, o_ref):
        x = x_ref[...]
        t = lax.broadcasted_iota(jnp.int32, x.shape, 0)
        pre, _ = _conv_pre(x, w_ref[...], b_ref[...], t)
        o_ref[...] = _silu(pre)

    return pl.pallas_call(
        body,
        out_shape=jax.ShapeDtypeStruct((nb * SEQ, cfg.conv_dim), F32),
        grid=(ncb, nb),
        in_specs=[
            pl.BlockSpec((SEQ, cw), lambda j, b: (b, xb + j)),
            pl.BlockSpec((None, SSM_CONV, cw), lambda j, b: (l, 0, j)),
            pl.BlockSpec((None, 1, cw), lambda j, b: (l, 0, j)),
        ],
        out_specs=pl.BlockSpec((SEQ, cw), lambda j, b: (b, j)),
        compiler_params=_cparams(("parallel", "parallel")),
        name="conv_fwd",
    )(proj, conv_w, conv_b)


def _conv_bwd(proj, dact, conv_w, conv_b, l, cfg, nb):
    cw = CONV_COLS
    ncb = cfg.conv_dim // cw
    xb = cfg.xbc0 // cw
    kw = SSM_CONV

    def body(x_ref, da_ref, w_ref, b_ref, dx_ref, dw_ref, db_ref):
        b_id = pl.program_id(1)
        x = x_ref[...]
        w = w_ref[...]
        t = lax.broadcasted_iota(jnp.int32, x.shape, 0)
        pre, shifted = _conv_pre(x, w, b_ref[...], t)
        dpre = da_ref[...] * _silu_grad(pre)
        dx = w[kw - 1:kw, :] * dpre
        for k in range(kw - 1):
            d = kw - 1 - k
            dx = dx + w[k:k + 1, :] * jnp.where(t < SEQ - d, pltpu.roll(dpre, SEQ - d, 0), 0.0)
        dx_ref[...] = dx.astype(dx_ref.dtype)
        dw = jnp.concatenate([jnp.sum(dpre * s, axis=0, keepdims=True) for s in shifted], axis=0)
        db = jnp.sum(dpre, axis=0, keepdims=True)

        @pl.when(b_id == 0)
        def _():
            dw_ref[...] = dw
            db_ref[...] = db

        @pl.when(b_id > 0)
        def _():
            dw_ref[...] += dw
            db_ref[...] += db

    return pl.pallas_call(
        body,
        out_shape=[
            jax.ShapeDtypeStruct((nb * SEQ, cfg.conv_dim), ACT_DTYPE),
            jax.ShapeDtypeStruct((kw, cfg.conv_dim), F32),
            jax.ShapeDtypeStruct((1, cfg.conv_dim), F32),
        ],
        grid=(ncb, nb),
        in_specs=[
            pl.BlockSpec((SEQ, cw), lambda j, b: (b, xb + j)),
            pl.BlockSpec((SEQ, cw), lambda j, b: (b, j)),
            pl.BlockSpec((None, kw, cw), lambda j, b: (l, 0, j)),
            pl.BlockSpec((None, 1, cw), lambda j, b: (l, 0, j)),
        ],
        out_specs=[
            pl.BlockSpec((SEQ, cw), lambda j, b: (b, j)),
            pl.BlockSpec((kw, cw), lambda j, b: (0, j)),
            pl.BlockSpec((1, cw), lambda j, b: (0, j)),
        ],
        compiler_params=_cparams(("parallel", "arbitrary")),
        name="conv_bwd",
    )(proj, dact, conv_w, conv_b)


def _ssd_common(dt_raw, dt_bias, a_log, tri):
    ln = SSM_CHUNK
    dt = _softplus(dt_raw + dt_bias)
    a = -jnp.exp(a_log)
    a_cs = jnp.dot(tri, dt * a, preferred_element_type=F32, precision=lax.Precision.HIGHEST)
    a_last = a_cs[ln - 1:ln, :]
    return dt, a, a_cs, a_cs.T, jnp.exp(a_cs), jnp.exp(a_last - a_cs), jnp.exp(a_last)


def _ssd_head(h, xs, dt, a_cs, a_t, cb, tril):
    p = SSM_HEAD_DIM
    x_h = xs[:, h * p:(h + 1) * p]
    xd = x_h * dt[:, h:h + 1]
    lmat = jnp.exp(jnp.where(tril, a_cs[:, h:h + 1] - a_t[h:h + 1, :], -jnp.inf))
    return x_h, xd, lmat


def _ssd_specs(cfg, nc, rev):
    ln = SSM_CHUNK
    cidx = (lambda c: nc - 1 - c) if rev else (lambda c: c)
    bmb = cfg.inner // cfg.gn
    return [
        pl.BlockSpec((ln, cfg.inner), lambda b, c: (b * nc + cidx(c), 0)),
        pl.BlockSpec((ln, cfg.gn), lambda b, c: (b * nc + cidx(c), bmb)),
        pl.BlockSpec((ln, cfg.gn), lambda b, c: (b * nc + cidx(c), bmb + 1)),
        pl.BlockSpec((ln, LANES), lambda b, c: (b * nc + cidx(c), cfg.dt0 // LANES)),
    ]


def _ssd_fwd(xbc, proj, dt_bias, a_log, d_skip, l, cfg, nb):
    ln, p, n = SSM_CHUNK, SSM_HEAD_DIM, SSM_STATE
    nc = SEQ // ln
    g_, e_ = SSM_GROUPS, cfg.epg
    assert cfg.inner % cfg.gn == 0

    def body(xs_ref, bm_ref, cm_ref, dtr_ref, bias_ref, alog_ref, dsk_ref, y_ref, st_ref, s_ref):
        c = pl.program_id(1)

        @pl.when(c == 0)
        def _():
            s_ref[...] = jnp.zeros_like(s_ref)

        st_ref[...] = s_ref[...]
        row = lax.broadcasted_iota(jnp.int32, (ln, ln), 0)
        col = lax.broadcasted_iota(jnp.int32, (ln, ln), 1)
        tril = row >= col
        dt, _, a_cs, a_t, e_a, dte, cd = _ssd_common(dtr_ref[...], bias_ref[...], alog_ref[...], tril.astype(F32))
        dsk = dsk_ref[...]
        xs = xs_ref[...]
        for g in range(g_):
            bm = bm_ref[:, g * n:(g + 1) * n].astype(MXU_DTYPE)
            cm = cm_ref[:, g * n:(g + 1) * n].astype(MXU_DTYPE)
            cb = lax.dot_general(cm, bm, (((1,), (1,)), ((), ())), preferred_element_type=F32)
            for e in range(e_):
                h = g * e_ + e
                x_h, xd, lmat = _ssd_head(h, xs, dt, a_cs, a_t, cb, tril)
                s_prev = s_ref[g * n:(g + 1) * n, e * p:(e + 1) * p]
                y = jnp.dot((cb * lmat).astype(MXU_DTYPE), xd.astype(MXU_DTYPE), preferred_element_type=F32)
                y = y + jnp.dot(cm, s_prev.astype(MXU_DTYPE), preferred_element_type=F32) * e_a[:, h:h + 1]
                y_ref[:, h * p:(h + 1) * p] = y + dsk[:, h:h + 1] * x_h
                upd = lax.dot_general(bm, (xd * dte[:, h:h + 1]).astype(MXU_DTYPE), (((0,), (0,)), ((), ())),
                                      preferred_element_type=F32)
                s_ref[g * n:(g + 1) * n, e * p:(e + 1) * p] = cd[:, h:h + 1] * s_prev + upd

    vec = lambda b, c: (l, 0, 0)
    return pl.pallas_call(
        body,
        out_shape=[
            jax.ShapeDtypeStruct((nb * SEQ, cfg.inner), F32),
            jax.ShapeDtypeStruct((nb * nc * g_ * n, e_ * p), F32),
        ],
        grid=(nb, nc),
        in_specs=_ssd_specs(cfg, nc, False) + [pl.BlockSpec((None, 1, LANES), vec)] * 3,
        out_specs=[
            pl.BlockSpec((ln, cfg.inner), lambda b, c: (b * nc + c, 0)),
            pl.BlockSpec((g_ * n, e_ * p), lambda b, c: (b * nc + c, 0)),
        ],
        scratch_shapes=[pltpu.VMEM((g_ * n, e_ * p), F32)],
        compiler_params=_cparams(("parallel", "arbitrary")),
        name="ssd_fwd",
    )(xbc, xbc, xbc, proj, dt_bias, a_log, d_skip)


def _ssd_bwd(xbc, proj, states, dy, dt_bias, a_log, d_skip, l, cfg, nb):
    ln, p, n = SSM_CHUNK, SSM_HEAD_DIM, SSM_STATE
    nc = SEQ // ln
    g_, e_ = SSM_GROUPS, cfg.epg
    c00 = (((0,), (0,)), ((), ()))
    c11 = (((1,), (1,)), ((), ()))

    def body(xs_ref, bm_ref, cm_ref, dtr_ref, st_ref, dy_ref, bias_ref, alog_ref, dsk_ref,
             dxbc_ref, ddt_ref, dvec_ref, ds_ref):
        first = jnp.logical_and(pl.program_id(0) == 0, pl.program_id(1) == 0)

        @pl.when(pl.program_id(1) == 0)
        def _():
            ds_ref[...] = jnp.zeros_like(ds_ref)

        row = lax.broadcasted_iota(jnp.int32, (ln, ln), 0)
        col = lax.broadcasted_iota(jnp.int32, (ln, ln), 1)
        tril = row >= col
        tri_f = tril.astype(F32)
        dtr = dtr_ref[...]
        bias = bias_ref[...]
        dt, a, a_cs, a_t, e_a, dte, cd = _ssd_common(dtr, bias, alog_ref[...], tri_f)
        dsk = dsk_ref[...]
        xs = xs_ref[...]
        lane = lax.broadcasted_iota(jnp.int32, (1, LANES), 1)
        sub = lax.broadcasted_iota(jnp.int32, (LANES, 1), 0)
        da_col = jnp.zeros((ln, LANES), F32)
        da_row_t = jnp.zeros((LANES, ln), F32)
        ddt = jnp.zeros((ln, LANES), F32)
        da_last = jnp.zeros((1, LANES), F32)
        ddsk = jnp.zeros((1, LANES), F32)
        for g in range(g_):
            bm = bm_ref[:, g * n:(g + 1) * n].astype(MXU_DTYPE)
            cm = cm_ref[:, g * n:(g + 1) * n].astype(MXU_DTYPE)
            cb = lax.dot_general(cm, bm, c11, preferred_element_type=F32)
            dbm = jnp.zeros((ln, n), F32)
            dcm = jnp.zeros((ln, n), F32)
            for e in range(e_):
                h = g * e_ + e
                hs = slice(h * p, (h + 1) * p)
                oh = (lane == h).astype(F32)
                x_h, xd, lmat = _ssd_head(h, xs, dt, a_cs, a_t, cb, tril)
                d_y = dy_ref[:, hs]
                s_prev = st_ref[g * n:(g + 1) * n, e * p:(e + 1) * p]
                d_s = ds_ref[g * n:(g + 1) * n, e * p:(e + 1) * p]
                dy_m = d_y.astype(MXU_DTYPE)
                xd_m = xd.astype(MXU_DTYPE)
                sp_m = s_prev.astype(MXU_DTYPE)
                ds_m = d_s.astype(MXU_DTYPE)
                e_a_h, dte_h, cd_h = e_a[:, h:h + 1], dte[:, h:h + 1], cd[:, h:h + 1]
                m_mat = (cb * lmat).astype(MXU_DTYPE)
                bds = jnp.dot(bm, ds_m, preferred_element_type=F32)
                d_xd = lax.dot_general(m_mat, dy_m, c00, preferred_element_type=F32) + dte_h * bds
                d_m = lax.dot_general(dy_m, xd_m, c11, preferred_element_type=F32)
                d_cb = (d_m * lmat).astype(MXU_DTYPE)
                w_mat = d_m * cb * lmat
                dy_e = (d_y * e_a_h).astype(MXU_DTYPE)
                xd_e = (xd * dte_h).astype(MXU_DTYPE)
                dcm = dcm + jnp.dot(d_cb, bm, preferred_element_type=F32)
                dcm = dcm + lax.dot_general(dy_e, sp_m, c11, preferred_element_type=F32)
                dbm = dbm + lax.dot_general(d_cb, cm, c00, preferred_element_type=F32)
                dbm = dbm + lax.dot_general(xd_e, ds_m, c11, preferred_element_type=F32)
                ds_ref[g * n:(g + 1) * n, e * p:(e + 1) * p] = (
                    cd_h * d_s + lax.dot_general(cm, dy_e, c00, preferred_element_type=F32))
                y_off = jnp.dot(cm, sp_m, preferred_element_type=F32) * e_a_h
                q_h = jnp.sum(bds * xd, axis=1, keepdims=True) * dte_h
                da_col_h = (jnp.sum(w_mat, axis=1, keepdims=True)
                            + jnp.sum(d_y * y_off, axis=1, keepdims=True) - q_h)
                da_col = da_col + da_col_h * oh
                da_row_t = da_row_t - (sub == h).astype(F32) * jnp.sum(w_mat, axis=0, keepdims=True)
                da_last = da_last + (jnp.sum(q_h, keepdims=True) + cd_h * jnp.sum(d_s * s_prev, keepdims=True)) * oh
                dxbc_ref[:, hs] = d_xd * dt[:, h:h + 1] + dsk[:, h:h + 1] * d_y
                ddt = ddt + jnp.sum(d_xd * x_h, axis=1, keepdims=True) * oh
                ddsk = ddsk + jnp.sum(d_y * x_h, keepdims=True) * oh
            dxbc_ref[:, cfg.inner + g * n:cfg.inner + (g + 1) * n] = dbm
            dxbc_ref[:, cfg.inner + cfg.gn + g * n:cfg.inner + cfg.gn + (g + 1) * n] = dcm
        d_acs = da_col + da_row_t.T + jnp.where(row[:, 0:1] == ln - 1, da_last, 0.0)
        da_dt = lax.dot_general(tri_f, d_acs, c00, preferred_element_type=F32, precision=lax.Precision.HIGHEST)
        ddt = ddt + da_dt * a
        ddt_raw = ddt * jax.nn.sigmoid(dtr + bias)
        ddt_ref[...] = ddt_raw.astype(ddt_ref.dtype)
        da_log = jnp.sum(da_dt * dt, axis=0, keepdims=True) * a
        dvec = jnp.concatenate([jnp.sum(ddt_raw, axis=0, keepdims=True), da_log, ddsk,
                                jnp.zeros((5, LANES), F32)], axis=0)

        @pl.when(first)
        def _():
            dvec_ref[...] = dvec

        @pl.when(jnp.logical_not(first))
        def _():
            dvec_ref[...] += dvec

    vec = lambda b, c: (l, 0, 0)
    rblk = lambda b, c: (b * nc + nc - 1 - c, 0)
    return pl.pallas_call(
        body,
        out_shape=[
            jax.ShapeDtypeStruct((nb * SEQ, cfg.conv_dim), F32),
            jax.ShapeDtypeStruct((nb * SEQ, LANES), ACT_DTYPE),
            jax.ShapeDtypeStruct((8, LANES), F32),
        ],
        grid=(nb, nc),
        in_specs=_ssd_specs(cfg, nc, True) + [
            pl.BlockSpec((g_ * n, e_ * p), rblk),
            pl.BlockSpec((ln, cfg.inner), rblk),
        ] + [pl.BlockSpec((None, 1, LANES), vec)] * 3,
        out_specs=[
            pl.BlockSpec((ln, cfg.conv_dim), rblk),
            pl.BlockSpec((ln, LANES), rblk),
            pl.BlockSpec((8, LANES), lambda b, c: (0, 0)),
        ],
        scratch_shapes=[pltpu.VMEM((g_ * n, e_ * p), F32)],
        compiler_params=_cparams(("arbitrary", "arbitrary")),
        name="ssd_bwd",
    )(xbc, xbc, xbc, proj, states, dy, dt_bias, a_log, d_skip)


XA_ROWS = 256


def _xa_probs(q_ref, kv_ref, h, dh):
    c11 = (((1,), (1,)), ((), ()))
    qh = q_ref[:, h * dh:(h + 1) * dh].astype(MXU_DTYPE)
    kh = kv_ref[:, h * dh:(h + 1) * dh].astype(MXU_DTYPE)
    vh = kv_ref[:, D_MODEL + h * dh:D_MODEL + (h + 1) * dh].astype(MXU_DTYPE)
    s = lax.dot_general(qh, kh, c11, preferred_element_type=F32) * (dh ** -0.5)
    s = s - jnp.max(s, axis=1, keepdims=True)
    pr = jnp.exp(s)
    return qh, kh, vh, pr / jnp.sum(pr, axis=1, keepdims=True)


def _xa_fwd(q, kv, cfg, nb):
    tq = _tile(SEQ, XA_ROWS, 16)
    nq = SEQ // tq
    dh = cfg.xa_dim

    def body(q_ref, kv_ref, o_ref):
        for h in range(XA_HEADS):
            _, _, vh, pr = _xa_probs(q_ref, kv_ref, h, dh)
            o_ref[:, h * dh:(h + 1) * dh] = jnp.dot(pr.astype(MXU_DTYPE), vh, preferred_element_type=F32).astype(o_ref.dtype)

    return pl.pallas_call(
        body,
        out_shape=jax.ShapeDtypeStruct((nb * SEQ, D_MODEL), ACT_DTYPE),
        grid=(nb, nq),
        in_specs=[
            pl.BlockSpec((tq, D_MODEL), lambda b, i: (b * nq + i, 0)),
            pl.BlockSpec((MEM_LEN, 2 * D_MODEL), lambda b, i: (b, 0)),
        ],
        out_specs=pl.BlockSpec((tq, D_MODEL), lambda b, i: (b * nq + i, 0)),
        compiler_params=_cparams(("parallel", "parallel")),
        name="xa_fwd",
    )(q, kv)


def _xa_bwd(q, kv, do, cfg, nb):
    tq = _tile(SEQ, XA_ROWS, 16)
    nq = SEQ // tq
    dh = cfg.xa_dim
    c00 = (((0,), (0,)), ((), ()))
    c11 = (((1,), (1,)), ((), ()))
    scale = dh ** -0.5

    def body(q_ref, kv_ref, do_ref, dq_ref, dkv_ref, acc_ref):
        i = pl.program_id(1)

        @pl.when(i == 0)
        def _():
            acc_ref[...] = jnp.zeros_like(acc_ref)

        for h in range(XA_HEADS):
            hs = slice(h * dh, (h + 1) * dh)
            vs = slice(D_MODEL + h * dh, D_MODEL + (h + 1) * dh)
            qh, kh, vh, pr = _xa_probs(q_ref, kv_ref, h, dh)
            do_h = do_ref[:, hs].astype(MXU_DTYPE)
            dp = lax.dot_general(do_h, vh, c11, preferred_element_type=F32)
            ds = (pr * (dp - jnp.sum(dp * pr, axis=1, keepdims=True))).astype(MXU_DTYPE)
            dq_ref[:, hs] = (jnp.dot(ds, kh, preferred_element_type=F32) * scale).astype(dq_ref.dtype)
            acc_ref[:, hs] += lax.dot_general(ds, qh, c00, preferred_element_type=F32) * scale
            acc_ref[:, vs] += lax.dot_general(pr.astype(MXU_DTYPE), do_h, c00, preferred_element_type=F32)

        @pl.when(i == nq - 1)
        def _():
            dkv_ref[...] = acc_ref[...].astype(dkv_ref.dtype)

    return pl.pallas_call(
        body,
        out_shape=[
            jax.ShapeDtypeStruct((nb * SEQ, D_MODEL), ACT_DTYPE),
            jax.ShapeDtypeStruct((nb * MEM_LEN, 2 * D_MODEL), ACT_DTYPE),
        ],
        grid=(nb, nq),
        in_specs=[
            pl.BlockSpec((tq, D_MODEL), lambda b, i: (b * nq + i, 0)),
            pl.BlockSpec((MEM_LEN, 2 * D_MODEL), lambda b, i: (b, 0)),
            pl.BlockSpec((tq, D_MODEL), lambda b, i: (b * nq + i, 0)),
        ],
        out_specs=[
            pl.BlockSpec((tq, D_MODEL), lambda b, i: (b * nq + i, 0)),
            pl.BlockSpec((MEM_LEN, 2 * D_MODEL), lambda b, i: (b, 0)),
        ],
        scratch_shapes=[pltpu.VMEM((MEM_LEN, 2 * D_MODEL), F32)],
        compiler_params=_cparams(("parallel", "arbitrary")),
        name="xa_bwd",
    )(q, kv, do)


def _adamw(parts, w, m, v, name, tr=128):
    n, r, c = parts.shape
    tr = _tile(r, tr, 16)

    def body(p_ref, w_ref, m_ref, v_ref, g_ref, d_ref, nm_ref, nv_ref):
        g = p_ref[0].astype(F32)
        for i in range(1, n):
            g = g + p_ref[i].astype(F32)
        m2 = ADAM_B1 * m_ref[...] + (1.0 - ADAM_B1) * g
        v2 = ADAM_B2 * v_ref[...] + (1.0 - ADAM_B2) * (g * g)
        m_hat = m2 / (1.0 - ADAM_B1 ** ADAM_STEP)
        v_hat = v2 / (1.0 - ADAM_B2 ** ADAM_STEP)
        g_ref[...] = g
        d_ref[...] = -ADAM_LR * (m_hat / (jnp.sqrt(v_hat) + ADAM_EPS) + ADAM_WD * w_ref[...])
        nm_ref[...] = m2
        nv_ref[...] = v2

    blk = pl.BlockSpec((tr, c), lambda i: (i, 0))
    return pl.pallas_call(
        body,
        out_shape=[jax.ShapeDtypeStruct((r, c), F32)] * 4,
        grid=(r // tr,),
        in_specs=[pl.BlockSpec((n, tr, c), lambda i: (0, i, 0)), blk, blk, blk],
        out_specs=[blk] * 4,
        compiler_params=_cparams(("parallel",)),
        name=name,
    )(parts, w, m, v)


def _flat_index(px, py, pc):
    return 4 * px + 2 * py + pc


def _all_gather(arrs, name):
    n = len(arrs)

    def body(*refs):
        ins, outs = refs[:n], refs[n:2 * n]
        send_sems, recv_sems, local_sems = refs[2 * n:]
        x, y, c = lax.axis_index("x"), lax.axis_index("y"), lax.axis_index("c")
        me, sibling = (x, y, c), (x, y, 1 - c)
        chips = [(1 - x, y), (x, 1 - y), (1 - x, 1 - y)]

        def copy(a, k, block, to, src=None):
            slot = outs[a].at[_flat_index(*block)]
            return pltpu.make_async_remote_copy(
                src_ref=slot if src is None else src, dst_ref=slot,
                send_sem=send_sems.at[a, k], recv_sem=recv_sems.at[a, k],
                device_id=to, device_id_type=MESH)

        mine = [pltpu.make_async_copy(ins[a], outs[a].at[_flat_index(*me)], local_sems.at[a]) for a in range(n)]
        for cp in mine:
            cp.start()
        first = []
        for a in range(n):
            first.append(copy(a, 0, me, sibling, src=ins[a]))
            first += [copy(a, 1 + j, me, (*chip, c), src=ins[a]) for j, chip in enumerate(chips)]
        for cp in first:
            cp.start()
        passed = []
        for j, chip in enumerate(chips):
            for a in range(n):
                copy(a, 1 + j, (*chip, c), me).wait_recv()
                fwd = copy(a, 4 + j, (*chip, c), sibling)
                fwd.start()
                passed.append(fwd)
        for a in range(n):
            copy(a, 0, sibling, me).wait_recv()
            for j, chip in enumerate(chips):
                copy(a, 4 + j, (*chip, 1 - c), me).wait_recv()
        for cp in first + passed:
            cp.wait_send()
        for cp in mine:
            cp.wait()

    any_spec = pl.BlockSpec(memory_space=pl.ANY)
    return pl.pallas_call(
        body,
        out_shape=[jax.ShapeDtypeStruct((N_DEV,) + a.shape, a.dtype) for a in arrs],
        in_specs=[any_spec] * n,
        out_specs=[any_spec] * n,
        scratch_shapes=[
            pltpu.SemaphoreType.DMA((n, N_DEV - 1)),
            pltpu.SemaphoreType.DMA((n, N_DEV - 1)),
            pltpu.SemaphoreType.DMA((n,)),
        ],
        name=name,
    )(*arrs)


def _scatter_blocks(arrs, name):
    n = len(arrs)

    def body(*refs):
        ins, outs = refs[:n], refs[n:2 * n]
        send_sems, recv_sems, local_sems = refs[2 * n:]
        x, y, c = lax.axis_index("x"), lax.axis_index("y"), lax.axis_index("c")
        me = _flat_index(x, y, c)

        def peer(k):
            return (1 - x if k & 4 else x, 1 - y if k & 2 else y, 1 - c if k & 1 else c)

        def copy(a, k):
            p = peer(k)
            return pltpu.make_async_remote_copy(
                src_ref=ins[a].at[_flat_index(*p)], dst_ref=outs[a].at[me],
                send_sem=send_sems.at[a, k - 1], recv_sem=recv_sems.at[a, k - 1],
                device_id=p, device_id_type=MESH)

        def landed(a, k):
            slot = outs[a].at[_flat_index(*peer(k))]
            return pltpu.make_async_remote_copy(
                src_ref=slot, dst_ref=slot, send_sem=send_sems.at[a, k - 1], recv_sem=recv_sems.at[a, k - 1],
                device_id=peer(k), device_id_type=MESH)

        mine = [pltpu.make_async_copy(ins[a].at[me], outs[a].at[me], local_sems.at[a]) for a in range(n)]
        for cp in mine:
            cp.start()
        sends = [copy(a, k) for a in range(n) for k in range(1, N_DEV)]
        for cp in sends:
            cp.start()
        for a in range(n):
            for k in range(1, N_DEV):
                landed(a, k).wait_recv()
        for cp in sends:
            cp.wait_send()
        for cp in mine:
            cp.wait()

    any_spec = pl.BlockSpec(memory_space=pl.ANY)
    return pl.pallas_call(
        body,
        out_shape=[jax.ShapeDtypeStruct(a.shape, a.dtype) for a in arrs],
        in_specs=[any_spec] * n,
        out_specs=[any_spec] * n,
        scratch_shapes=[
            pltpu.SemaphoreType.DMA((n, N_DEV - 1)),
            pltpu.SemaphoreType.DMA((n, N_DEV - 1)),
            pltpu.SemaphoreType.DMA((n,)),
        ],
        name=name,
    )(*arrs)


_BIG = ("w_in", "w_br_att", "w_br_ssm", "w_mix_out", "w_xq", "w_xkv", "w_xo", "w_gu", "w_down")
_COL_SHARDED = ("w_in", "w_xkv", "w_gu", "conv_w")
_SMALL = ("g_pre_mix", "conv_b", "dt_bias", "a_log", "d_skip", "g_ssm_norm", "g_post_mix", "g_pre_xa",
          "g_mem", "g_post_xa", "g_pre_ffn", "g_post_ffn")
_WEIGHTS = ("g_pre_mix", "w_in", "conv_w", "conv_b", "dt_bias", "a_log", "d_skip", "g_ssm_norm", "w_br_att",
            "w_br_ssm", "w_mix_out", "g_post_mix", "g_pre_xa", "g_mem", "w_xq", "w_xkv", "w_xo", "g_post_xa",
            "g_pre_ffn", "w_gu", "w_down", "g_post_ffn")
PACK_W = 8 * LANES


def _unshard(g, col):
    n, l, r, c = g.shape
    if col:
        return jnp.transpose(g, (1, 2, 0, 3)).reshape(l, r, n * c)
    return jnp.transpose(g, (1, 0, 2, 3)).reshape(l, n * r, c)


def _shard(w, col):
    l, r, c = w.shape
    if col:
        return jnp.transpose(w.reshape(l, r, N_DEV, c // N_DEV), (2, 0, 1, 3))
    return jnp.transpose(w.reshape(l, N_DEV, r // N_DEV, c), (1, 0, 2, 3))


def _permute_in(w, cfg):
    parts, off = [], 0
    for size in cfg.in_sizes:
        parts.append(w[..., off:off + size])
        off += size
    q, k, v, z, xbc, dt, ga, gs = parts
    pad = jnp.zeros(w.shape[:-1] + (LANES - cfg.heads,), w.dtype)
    return jnp.concatenate([z, ga, gs, q, k, v, xbc, dt, pad], axis=-1)


def _unpermute_in(w, cfg):
    c = cfg
    sl = lambda a, n: w[..., a:a + n]
    return jnp.concatenate([sl(c.q0, c.sbw), sl(c.k0, c.sbw), sl(c.v0, c.sbw), sl(c.z0, c.inner),
                            sl(c.xbc0, c.conv_dim), sl(c.dt0, c.heads), sl(c.ga0, c.d), sl(c.gs0, c.d)], axis=-1)


def _pack(arrs):
    flat = jnp.concatenate([a.reshape(-1).astype(F32) for a in arrs])
    rows = -(-flat.shape[0] // PACK_W)
    rows = -(-rows // 8) * 8
    return jnp.pad(flat, (0, rows * PACK_W - flat.shape[0])).reshape(rows, PACK_W)


def _unpack(p, shapes):
    flat = p.reshape(-1)
    out, off = [], 0
    for s in shapes:
        size = math.prod(s)
        out.append(flat[off:off + size].reshape(s))
        off += size
    return out


def _vec3(a, width=None):
    if width is not None and a.shape[1] < width:
        a = jnp.pad(a, ((0, 0), (0, width - a.shape[1])))
    return a[:, None, :]


def _forward_layer(l, xin, h1, memf, tgt, p, cfg, nb, last):
    t = xin.shape[0]
    d = cfg.d
    s = {"x_in": xin, "h1": h1}
    proj = _mm(h1, p["w_in"], b_pre=(l,), name="mm_proj")
    s["proj"] = proj
    s["o_att"] = _sb_fwd(proj, cfg, nb)
    s["xbc"] = _conv_fwd(proj, p["conv_w"], p["conv_b"], l, cfg, nb)
    s["y"], s["states"] = _ssd_fwd(s["xbc"], proj, p["dt_bias"], p["a_log"], p["d_skip"], l, cfg, nb)
    s["o_ssm"] = _rowwise(_f_gate_norm, "gate_norm_fwd", t, [_full(s["y"]), (proj, cfg.inner, 0)],
                          [(p["g_ssm_norm"], l)], [(cfg.inner, ACT_DTYPE)])[0]
    s["ba"] = _mm(s["o_att"], p["w_br_att"], b_pre=(l,), name="mm_br_att")
    s["bs"] = _mm(s["o_ssm"], p["w_br_ssm"], b_pre=(l,), name="mm_br_ssm")
    s["merged"] = _rowwise(_f_merge, "merge_fwd", t,
                           [(proj, d, cfg.ga0 // d), (proj, d, cfg.gs0 // d), _full(s["ba"]), _full(s["bs"])],
                           [], [(d, ACT_DTYPE)])[0]
    s["mo"] = _mm(s["merged"], p["w_mix_out"], b_pre=(l,), name="mm_mix_out")
    s["x1"], s["h2"] = _rowwise(_f_post_pre, "post_pre_mix", t, [_full(xin), _full(s["mo"])],
                                [(p["g_post_mix"], l), (p["g_pre_xa"], l)], [(d, F32), (d, ACT_DTYPE)])
    s["mem_n"] = _rowwise(_rms, "mem_norm", memf.shape[0], [_full(memf)], [(p["g_mem"], l)], [(d, ACT_DTYPE)])[0]
    s["q"] = _mm(s["h2"], p["w_xq"], b_pre=(l,), name="mm_xq")
    s["kv"] = _mm(s["mem_n"], p["w_xkv"], b_pre=(l,), name="mm_xkv")
    s["o_xa"] = _xa_fwd(s["q"], s["kv"], cfg, nb)
    s["xo"] = _mm(s["o_xa"], p["w_xo"], b_pre=(l,), name="mm_xo")
    s["x2"], s["h3"] = _rowwise(_f_post_pre, "post_pre_xa", t, [_full(s["x1"]), _full(s["xo"])],
                                [(p["g_post_xa"], l), (p["g_pre_ffn"], l)], [(d, F32), (d, ACT_DTYPE)])
    s["gu"] = _mm(s["h3"], p["w_gu"], b_pre=(l,), name="mm_gu")
    s["act"] = _rowwise(_f_swiglu, "swiglu_fwd", t, [(s["gu"], cfg.ffn, 0), (s["gu"], cfg.ffn, 1)], [],
                        [(cfg.ffn, ACT_DTYPE)])[0]
    s["dn"] = _mm(s["act"], p["w_down"], b_pre=(l,), name="mm_down")
    if last:
        nxt = _rowwise(_f_final, "final_loss", t, [_full(s["x2"]), _full(s["dn"]), _full(tgt)],
                       [(p["g_post_ffn"], l)], [(d, F32)], acc_out=[(1, d)])
    else:
        nxt = _rowwise(_f_post_pre, "post_pre_ffn", t, [_full(s["x2"]), _full(s["dn"])],
                       [(p["g_post_ffn"], l), (p["g_pre_mix"], l + 1)], [(d, F32), (d, ACT_DTYPE)])
    return s, nxt


def _backward_layer(l, s, dx, d_dn, memf, p, cfg, nb, prev_dn):
    t = dx.shape[0]
    d = cfg.d
    g = {}
    dact = _mm(d_dn, p["w_down"], tb=True, b_pre=(l,), out_dtype=ACT_DTYPE, name="mm_d_act", tn=1408)
    g["w_down"] = _mm(s["act"], d_dn, ta=True, name="mm_dw_down")
    dgu = _rowwise(_f_swiglu_bwd, "swiglu_bwd", t, [(s["gu"], cfg.ffn, 0), (s["gu"], cfg.ffn, 1), _full(dact)], [],
                   [(2 * cfg.ffn, ACT_DTYPE)])[0]
    dh3 = _mm(dgu, p["w_gu"], tb=True, b_pre=(l,), name="mm_d_h3", tk=1408)
    g["w_gu"] = _mm(s["h3"], dgu, ta=True, name="mm_dw_gu")
    dx2, d_xo, g["g_pre_ffn"], g["g_post_xa"] = _rowwise(
        _f_pre_post_bwd, "pre_post_bwd_ffn", t, [_full(s["x2"]), _full(dh3), _full(dx), _full(s["xo"])],
        [(p["g_pre_ffn"], l), (p["g_post_xa"], l)], [(d, F32), (d, ACT_DTYPE)], acc_out=[(1, d), (1, d)])
    do_xa = _mm(d_xo, p["w_xo"], tb=True, b_pre=(l,), out_dtype=ACT_DTYPE, name="mm_d_oxa")
    g["w_xo"] = _mm(s["o_xa"], d_xo, ta=True, name="mm_dw_xo")
    dq, dkv = _xa_bwd(s["q"], s["kv"], do_xa, cfg, nb)
    dh2 = _mm(dq, p["w_xq"], tb=True, b_pre=(l,), name="mm_d_h2")
    g["w_xq"] = _mm(s["h2"], dq, ta=True, name="mm_dw_xq")
    dmem_n = _mm(dkv, p["w_xkv"], tb=True, b_pre=(l,), name="mm_d_mem")
    g["w_xkv"] = _mm(s["mem_n"], dkv, ta=True, name="mm_dw_xkv")
    g["g_mem"] = _rowwise(_f_gain_bwd, "mem_norm_bwd", memf.shape[0], [_full(memf), _full(dmem_n)],
                          [(p["g_mem"], l)], [], acc_out=[(1, d)])[0]
    dx1, d_mo, g["g_pre_xa"], g["g_post_mix"] = _rowwise(
        _f_pre_post_bwd, "pre_post_bwd_xa", t, [_full(s["x1"]), _full(dh2), _full(dx2), _full(s["mo"])],
        [(p["g_pre_xa"], l), (p["g_post_mix"], l)], [(d, F32), (d, ACT_DTYPE)], acc_out=[(1, d), (1, d)])
    dmerged = _mm(d_mo, p["w_mix_out"], tb=True, b_pre=(l,), out_dtype=ACT_DTYPE, name="mm_d_merged")
    g["w_mix_out"] = _mm(s["merged"], d_mo, ta=True, name="mm_dw_mix_out")
    proj = s["proj"]
    dgg, dba, dbs = _rowwise(
        _f_merge_bwd, "merge_bwd", t,
        [(proj, d, cfg.ga0 // d), (proj, d, cfg.gs0 // d), _full(s["ba"]), _full(s["bs"]), _full(dmerged)], [],
        [(2 * d, ACT_DTYPE), (d, ACT_DTYPE), (d, ACT_DTYPE)])
    do_att = _mm(dba, p["w_br_att"], tb=True, b_pre=(l,), name="mm_d_oatt")
    g["w_br_att"] = _mm(s["o_att"], dba, ta=True, name="mm_dw_br_att")
    do_ssm = _mm(dbs, p["w_br_ssm"], tb=True, b_pre=(l,), out_dtype=ACT_DTYPE, name="mm_d_ossm")
    g["w_br_ssm"] = _mm(s["o_ssm"], dbs, ta=True, name="mm_dw_br_ssm")
    dy, dz, g["g_ssm_norm"] = _rowwise(
        _f_gate_norm_bwd, "gate_norm_bwd", t, [_full(s["y"]), (proj, cfg.inner, 0), _full(do_ssm)],
        [(p["g_ssm_norm"], l)], [(cfg.inner, F32), (cfg.inner, ACT_DTYPE)], acc_out=[(1, cfg.inner)])
    dxbc, ddt_raw, dvec = _ssd_bwd(s["xbc"], proj, s["states"], dy, p["dt_bias"], p["a_log"], p["d_skip"], l, cfg, nb)
    g["dt_bias"], g["a_log"], g["d_skip"] = (dvec[i:i + 1, :cfg.heads] for i in range(3))
    dxbc_raw, g["conv_w"], g["conv_b"] = _conv_bwd(proj, dxbc, p["conv_w"], p["conv_b"], l, cfg, nb)
    dq_sb, dk_sb, dv_sb = _sb_bwd(proj, do_att, cfg, nb)
    dproj = jnp.concatenate([dz, dgg, dq_sb, dk_sb.astype(ACT_DTYPE), dv_sb.astype(ACT_DTYPE), dxbc_raw, ddt_raw], axis=1)
    dh1 = _mm(dproj, p["w_in"], tb=True, b_pre=(l,), name="mm_d_h1", tk=1152)
    g["w_in"] = _mm(s["h1"], dproj, ta=True, name="mm_dw_in")
    if prev_dn is None:
        dx0, g["g_pre_mix"] = _rowwise(_f_pre_bwd, "pre_bwd_first", t, [_full(s["x_in"]), _full(dh1), _full(dx1)],
                                       [(p["g_pre_mix"], l)], [(d, F32)], acc_out=[(1, d)])
        return g, dx0, None, None
    dx0, d_dn_prev, g["g_pre_mix"], g_post_prev = _rowwise(
        _f_pre_post_bwd, "pre_post_bwd_mix", t, [_full(s["x_in"]), _full(dh1), _full(dx1), _full(prev_dn)],
        [(p["g_pre_mix"], l), (p["g_post_ffn"], l - 1)], [(d, F32), (d, ACT_DTYPE)], acc_out=[(1, d), (1, d)])
    return g, dx0, d_dn_prev, g_post_prev


def kernel(x, mem, g_pre_mix, w_in, conv_w, conv_b, dt_bias, a_log, d_skip, g_ssm_norm, w_br_att, w_br_ssm, w_mix_out, g_post_mix, g_pre_xa, g_mem, w_xq, w_xkv, w_xo, g_post_xa, g_pre_ffn, w_gu, w_down, g_post_ffn, loss_target, m_g_pre_mix, m_w_in, m_conv_w, m_conv_b, m_dt_bias, m_a_log, m_d_skip, m_g_ssm_norm, m_w_br_att, m_w_br_ssm, m_w_mix_out, m_g_post_mix, m_g_pre_xa, m_g_mem, m_w_xq, m_w_xkv, m_w_xo, m_g_post_xa, m_g_pre_ffn, m_w_gu, m_w_down, m_g_post_ffn, v_g_pre_mix, v_w_in, v_conv_w, v_conv_b, v_dt_bias, v_a_log, v_d_skip, v_g_ssm_norm, v_w_br_att, v_w_br_ssm, v_w_mix_out, v_g_post_mix, v_g_pre_xa, v_g_mem, v_w_xq, v_w_xkv, v_w_xo, v_g_post_xa, v_g_pre_ffn, v_w_gu, v_w_down, v_g_post_ffn):
    vals = dict(locals())
    cfg = _Cfg()
    nb = x.shape[0]
    t = nb * SEQ
    d = cfg.d
    depth = g_pre_mix.shape[0]

    wire = [vals[n].astype(WIRE_DTYPE) for n in _BIG] + [conv_w]
    gathered = _all_gather(wire, "ag_weights")
    p = {n: _unshard(gw, n in _COL_SHARDED) for n, gw in zip(_BIG + ("conv_w",), gathered)}
    p["w_in"] = _permute_in(p["w_in"], cfg)
    for n in _SMALL:
        p[n] = _vec3(vals[n], LANES if n in ("dt_bias", "a_log", "d_skip") else None)

    xf = x.reshape(t, d)
    memf = mem.reshape(nb * MEM_LEN, d)
    tgt = loss_target.reshape(t, d)
    h = _rowwise(_rms, "pre_norm_first", t, [_full(xf)], [(p["g_pre_mix"], 0)], [(d, ACT_DTYPE)])[0]
    saved = []
    xcur = xf
    for l in range(depth):
        s, nxt = _forward_layer(l, xcur, h, memf, tgt, p, cfg, nb, l == depth - 1)
        saved.append(s)
        if l < depth - 1:
            xcur, h = nxt
    dx, loss_row = nxt
    loss = lax.psum(0.5 * jnp.sum(loss_row) / d, AXES)

    top = saved[-1]
    d_dn, g_post_top = _rowwise(lambda ysub, dxo, gp: _rms_bwd(ysub, gp, dxo), "post_bwd_last", t,
                                [_full(top["dn"]), _full(dx)], [(p["g_post_ffn"], depth - 1)], [(d, ACT_DTYPE)],
                                acc_out=[(1, d)])
    grads = [None] * depth
    post_ffn = [None] * depth
    post_ffn[depth - 1] = g_post_top
    for l in reversed(range(depth)):
        prev_dn = saved[l - 1]["dn"] if l > 0 else None
        grads[l], dx, d_dn, g_post_prev = _backward_layer(l, saved[l], dx, d_dn, memf, p, cfg, nb, prev_dn)
        if l > 0:
            post_ffn[l - 1] = g_post_prev
    for l in range(depth):
        grads[l]["g_post_ffn"] = post_ffn[l]
    grad_x = dx.reshape(x.shape)
    stacked = {n: jnp.stack([grads[l][n] for l in range(depth)]) for n in _WEIGHTS}

    stacked["w_in"] = _unpermute_in(stacked["w_in"], cfg)
    blocks = [_shard(stacked[n], n in _COL_SHARDED).astype(WIRE_DTYPE) for n in _BIG]
    landed = _scatter_blocks(blocks, "scatter_grads")
    out = {}
    for n, parts in zip(_BIG, landed):
        shp = vals[n].shape
        flat = lambda a: a.reshape(shp[0] * shp[1], shp[2])
        res = _adamw(parts.reshape(N_DEV, shp[0] * shp[1], shp[2]), flat(vals[n]), flat(vals["m_" + n]),
                     flat(vals["v_" + n]), "adamw_" + n)
        out[n] = [r.reshape(shp) for r in res]

    small_shapes = [vals[n].shape for n in _SMALL]
    pack_g = _pack([stacked[n] for n in _SMALL])
    conv_g = stacked["conv_w"].reshape(depth * SSM_CONV, cfg.conv_dim)
    parts_small, parts_conv = _all_gather([pack_g, conv_g], "ag_small_grads")
    res = _adamw(parts_small, _pack([vals[n] for n in _SMALL]), _pack([vals["m_" + n] for n in _SMALL]),
                 _pack([vals["v_" + n] for n in _SMALL]), "adamw_small")
    unpacked = [_unpack(r, small_shapes) for r in res]
    for i, n in enumerate(_SMALL):
        out[n] = [unpacked[j][i] for j in range(4)]
    cs = conv_w.shape[2]
    me = _flat_index(lax.axis_index("x"), lax.axis_index("y"), lax.axis_index("c"))
    parts_conv = lax.dynamic_slice_in_dim(parts_conv, me * cs, cs, axis=2)
    flat = lambda a: a.reshape(depth * SSM_CONV, cs)
    res = _adamw(parts_conv, flat(conv_w), flat(m_conv_w), flat(v_conv_w), "adamw_conv_w")
    out["conv_w"] = [r.reshape(conv_w.shape) for r in res]

    return (loss, grad_x, *[out[n][0] for n in _WEIGHTS], *[out[n][1] for n in _WEIGHTS],
            *[out[n][2] for n in _WEIGHTS], *[out[n][3] for n in _WEIGHTS])
```

```python
import functools
import math

import jax
import jax.numpy as jnp
from jax import lax
from jax.experimental import pallas as pl
from jax.experimental.pallas import tpu as pltpu

F32 = jnp.float32
BF16 = jnp.bfloat16
MXU_DTYPE = BF16
ACT_DTYPE = BF16
WIRE_DTYPE = BF16

D_MODEL = 1024
SEQ = 2048
DEPTH = 4
MEM_LEN = 256
RMS_EPS = 1e-6
SB_HEADS = 16
SB_HEAD_DIM = 64
SB_BLOCK = 128
SSM_INNER = 2 * D_MODEL
SSM_HEAD_DIM = 64
SSM_GROUPS = 4
SSM_STATE = 128
SSM_CONV = 4
SSM_CHUNK = 128
XA_HEADS = 4
FFN_HIDDEN = ((8 * D_MODEL + 767) // 768) * 256
ADAM_LR = 0.001
ADAM_B1 = 0.9
ADAM_B2 = 0.999
ADAM_EPS = 1e-08
ADAM_WD = 0.01
ADAM_STEP = 10

N_DEV = 8
LANES = 128
VMEM_LIMIT_BYTES = 56 * 1024 * 1024

AXES = ("x", "y", "c")
MESH = pl.DeviceIdType.MESH


class _Cfg:
    def __init__(self):
        self.d = D_MODEL
        self.sbw = SB_HEADS * SB_HEAD_DIM
        self.inner = SSM_INNER
        self.heads = SSM_INNER // SSM_HEAD_DIM
        self.epg = self.heads // SSM_GROUPS
        self.gn = SSM_GROUPS * SSM_STATE
        self.conv_dim = SSM_INNER + 2 * self.gn
        self.ffn = FFN_HIDDEN
        self.xa_dim = D_MODEL // XA_HEADS
        self.in_sizes = (self.sbw, self.sbw, self.sbw, self.inner, self.conv_dim, self.heads, self.d, self.d)
        self.in_width = sum(self.in_sizes)
        self.z0 = 0
        self.ga0 = self.inner
        self.gs0 = self.ga0 + self.d
        self.q0 = self.gs0 + self.d
        self.k0 = self.q0 + self.sbw
        self.v0 = self.k0 + self.sbw
        self.xbc0 = self.v0 + self.sbw
        self.dt0 = self.xbc0 + self.conv_dim
        self.proj_w = self.dt0 + LANES
        assert self.heads <= LANES


def _cparams(sem=None):
    return pltpu.CompilerParams(dimension_semantics=sem, vmem_limit_bytes=VMEM_LIMIT_BYTES)


def _tile(n, pref, mult):
    if n <= pref:
        return n
    t = (pref // mult) * mult
    while t >= mult:
        if n % t == 0:
            return t
        t -= mult
    return n


def _mm(a, b, *, ta=False, tb=False, out_dtype=F32, name, a_pre=(), b_pre=(), tm=512, tn=1152, tk=2048):
    ash = a.shape[len(a_pre):]
    bsh = b.shape[len(b_pre):]
    kk, m = (ash if ta else ash[::-1])
    if tb:
        n, k2 = bsh
    else:
        k2, n = bsh
    assert kk == k2, (name, a.shape, b.shape)
    tm = _tile(m, tm, LANES if ta else 16)
    tn = _tile(n, tn, LANES)
    tk = _tile(kk, tk, LANES if (tb or not ta) else 16)
    nk = kk // tk
    dims = (((0 if ta else 1,), (1 if tb else 0,)), ((), ()))
    npa, npb = len(a_pre), len(b_pre)

    def body(a_ref, b_ref, o_ref, *scratch):
        av = a_ref[...].astype(MXU_DTYPE)
        bv = b_ref[...].astype(MXU_DTYPE)
        part = lax.dot_general(av, bv, dims, preferred_element_type=F32)
        if nk == 1:
            o_ref[...] = part.astype(out_dtype)
        else:
            acc_ref, = scratch
            k = pl.program_id(2)

            @pl.when(k == 0)
            def _():
                acc_ref[...] = part

            @pl.when(k > 0)
            def _():
                acc_ref[...] += part

            @pl.when(k == nk - 1)
            def _():
                o_ref[...] = acc_ref[...].astype(out_dtype)

    if ta:
        a_spec = pl.BlockSpec((None,) * npa + (tk, tm), lambda i, j, k: a_pre + (k, i))
    else:
        a_spec = pl.BlockSpec((None,) * npa + (tm, tk), lambda i, j, k: a_pre + (i, k))
    if tb:
        b_spec = pl.BlockSpec((None,) * npb + (tn, tk), lambda i, j, k: b_pre + (j, k))
    else:
        b_spec = pl.BlockSpec((None,) * npb + (tk, tn), lambda i, j, k: b_pre + (k, j))
    return pl.pallas_call(
        body,
        out_shape=jax.ShapeDtypeStruct((m, n), out_dtype),
        grid=(m // tm, n // tn, nk),
        in_specs=[a_spec, b_spec],
        out_specs=pl.BlockSpec((tm, tn), lambda i, j, k: (i, j)),
        scratch_shapes=[] if nk == 1 else [pltpu.VMEM((tm, tn), F32)],
        compiler_params=_cparams(("parallel", "parallel", "arbitrary")),
        name=name,
    )(a, b)


def _rowwise(fn, name, rows, row_in, vec_in, row_out, acc_out=(), tr=256):
    tr = _tile(rows, tr, 16)
    n_in = len(row_in) + len(vec_in)
    n_ro = len(row_out)

    def body(*refs):
        ins = [r[...].astype(F32) for r in refs[:n_in]]
        outs = fn(*ins)
        if not isinstance(outs, (tuple, list)):
            outs = (outs,)
        out_refs = refs[n_in:]
        for o_ref, val in zip(out_refs[:n_ro], outs[:n_ro]):
            o_ref[...] = val.astype(o_ref.dtype)
        if acc_out:
            i = pl.program_id(0)
            for o_ref, val in zip(out_refs[n_ro:], outs[n_ro:]):
                @pl.when(i == 0)
                def _(o_ref=o_ref, val=val):
                    o_ref[...] = val

                @pl.when(i > 0)
                def _(o_ref=o_ref, val=val):
                    o_ref[...] += val

    in_specs = [pl.BlockSpec((tr, w), functools.partial(lambda i, cb: (i, cb), cb=cb)) for (_, w, cb) in row_in]
    in_specs += [pl.BlockSpec((None,) + v.shape[1:], functools.partial(lambda i, l: (l, 0, 0), l=l)) for (v, l) in vec_in]
    out_shape = [jax.ShapeDtypeStruct((rows, w), dt) for (w, dt) in row_out]
    out_shape += [jax.ShapeDtypeStruct(s, F32) for s in acc_out]
    out_specs = [pl.BlockSpec((tr, w), lambda i: (i, 0)) for (w, _) in row_out]
    out_specs += [pl.BlockSpec(s, lambda i: (0, 0)) for s in acc_out]
    res = pl.pallas_call(
        body,
        out_shape=out_shape,
        grid=(rows // tr,),
        in_specs=in_specs,
        out_specs=out_specs,
        compiler_params=_cparams(("arbitrary",) if acc_out else ("parallel",)),
        name=name,
    )(*[a for (a, _, _) in row_in], *[v for (v, _) in vec_in])
    return res


def _rms(x, g):
    r = lax.rsqrt(jnp.mean(x * x, axis=-1, keepdims=True) + RMS_EPS)
    return x * r * g


def _rms_bwd(x, g, dy):
    r = lax.rsqrt(jnp.mean(x * x, axis=-1, keepdims=True) + RMS_EPS)
    xh = x * r
    dxh = dy * g
    dx = r * (dxh - xh * jnp.mean(dxh * xh, axis=-1, keepdims=True))
    return dx, jnp.sum(dy * xh, axis=0, keepdims=True)


def _silu(x):
    return x * jax.nn.sigmoid(x)


def _silu_grad(x):
    s = jax.nn.sigmoid(x)
    return s * (1.0 + x * (1.0 - s))


def _softplus(x):
    return jnp.maximum(x, 0.0) + jnp.log1p(jnp.exp(-jnp.abs(x)))


def _full(a):
    return (a, a.shape[1], 0)


def _f_post_pre(x, ysub, g_post, g_pre):
    xn = x + _rms(ysub, g_post)
    return xn, _rms(xn, g_pre)


def _f_final(x, ysub, tgt, g_post):
    err = x + _rms(ysub, g_post) - tgt
    return err * (1.0 / D_MODEL), jnp.sum(err * err, axis=0, keepdims=True)


def _f_pre_post_bwd(xmid, dh, dxo, ysub, g_pre, g_post):
    d1, dg_pre = _rms_bwd(xmid, g_pre, dh)
    dxm = dxo + d1
    dys, dg_post = _rms_bwd(ysub, g_post, dxm)
    return dxm, dys, dg_pre, dg_post


def _f_pre_bwd(x, dh, dxo, g_pre):
    d1, dg_pre = _rms_bwd(x, g_pre, dh)
    return dxo + d1, dg_pre


def _f_gain_bwd(x, dy, g):
    return _rms_bwd(x, g, dy)[1]


def _group_norm_parts(u):
    gw = u.shape[1] // SSM_GROUPS
    parts = []
    for gi in range(SSM_GROUPS):
        ug = u[:, gi * gw:(gi + 1) * gw]
        r = lax.rsqrt(jnp.mean(ug * ug, axis=-1, keepdims=True) + RMS_EPS)
        parts.append((ug * r, r))
    return gw, parts


def _f_gate_norm(y, z, g):
    _, parts = _group_norm_parts(y * _silu(z))
    return jnp.concatenate([uh for uh, _ in parts], axis=1) * g


def _f_gate_norm_bwd(y, z, do, g):
    sz = _silu(z)
    gw, parts = _group_norm_parts(y * sz)
    dxh = do * g
    du = []
    for gi, (uh, r) in enumerate(parts):
        dg_ = dxh[:, gi * gw:(gi + 1) * gw]
        du.append(r * (dg_ - uh * jnp.mean(dg_ * uh, axis=-1, keepdims=True)))
    du = jnp.concatenate(du, axis=1)
    uh_all = jnp.concatenate([uh for uh, _ in parts], axis=1)
    return du * sz, du * y * _silu_grad(z), jnp.sum(do * uh_all, axis=0, keepdims=True)


def _f_merge(ga, gs, ba, bs):
    return jax.nn.sigmoid(ga) * ba + jax.nn.sigmoid(gs) * bs


def _f_merge_bwd(ga, gs, ba, bs, dm):
    sa, ss = jax.nn.sigmoid(ga), jax.nn.sigmoid(gs)
    dgg = jnp.concatenate([dm * ba * sa * (1.0 - sa), dm * bs * ss * (1.0 - ss)], axis=1)
    return dgg, dm * sa, dm * ss


def _f_swiglu(gate, up):
    return _silu(gate) * up


def _f_swiglu_bwd(gate, up, da):
    return jnp.concatenate([da * up * _silu_grad(gate), da * _silu(gate)], axis=1)


def _split_dot(x, u):
    hi = x.astype(BF16)
    lo = (x - hi.astype(F32)).astype(BF16)
    return jnp.dot(hi, u, preferred_element_type=F32) + jnp.dot(lo, u, preferred_element_type=F32)


SB_ROWS = 512
SB_UNROLL = 4
C00 = (((0,), (0,)), ((), ()))
C11 = (((1,), (1,)), ((), ()))


def _sb_setup(q_ref, tq):
    hp = LANES // SB_HEAD_DIM
    lane = lax.broadcasted_iota(jnp.int32, (1, LANES), 1)
    heads = [jnp.logical_and(lane >= h * SB_HEAD_DIM, lane < (h + 1) * SB_HEAD_DIM) for h in range(hp)]
    qs = q_ref[...] * (SB_HEAD_DIM ** -0.5)
    q_h = [jnp.where(hm, qs, 0.0).astype(MXU_DTYPE) for hm in heads]
    row = lax.broadcasted_iota(jnp.int32, (tq, SB_BLOCK), 0)
    col = lax.broadcasted_iota(jnp.int32, (tq, SB_BLOCK), 1)
    sq_row = lax.broadcasted_iota(jnp.int32, (SB_BLOCK, SB_BLOCK), 0)
    sq_col = lax.broadcasted_iota(jnp.int32, (SB_BLOCK, SB_BLOCK), 1)
    return heads, q_h, col - row, sq_row, sq_col


def _sb_scores(q, kj, mask):
    z = lax.dot_general(q, kj, C11, preferred_element_type=F32)
    lm = -(jnp.maximum(z, 0.0) + jnp.log(1.0 + jnp.exp(-jnp.abs(z))))
    return z, jnp.where(mask, lm, 0.0)


def _grid_step3(nb, ncb, nq):
    return (pl.program_id(0) * ncb + pl.program_id(1)) * nq + pl.program_id(2), nb * ncb * nq


def _sb_fwd(proj, cfg, nb, gather=()):
    blk = SB_BLOCK
    tq = _tile(SEQ, SB_ROWS, blk)
    nq = SEQ // tq
    kpq = tq // blk
    unr = math.gcd(kpq, SB_UNROLL)
    hp = LANES // SB_HEAD_DIM
    ncb = cfg.sbw // LANES
    qb, kb, vb = cfg.q0 // LANES, cfg.k0 // LANES, cfg.v0 // LANES
    ng = len(gather)

    def body(q_ref, k_ref, v_ref, *rest):
        o_ref = rest[ng]
        if ng:
            step_id, n_steps = _grid_step3(nb, ncb, nq)
            start, forward, finish = _gather_phases(rest[:ng], rest[ng + 1:2 * ng + 1], *rest[2 * ng + 1:])
            pl.when(step_id == 0)(start)
            pl.when(step_id == n_steps // 2)(forward)
        _sb_fwd_block(q_ref, k_ref, v_ref, o_ref)
        if ng:
            pl.when(step_id == n_steps - 1)(finish)

    def _sb_fwd_block(q_ref, k_ref, v_ref, o_ref):
        i = pl.program_id(2)
        heads, q_h, cmr, sq_row, sq_col = _sb_setup(q_ref, tq)
        u_rev = (sq_row >= sq_col).astype(BF16)

        def step(n, carry):
            acc, runs = carry
            runs = list(runs)
            for jj in range(unr):
                j = (i + 1) * kpq - 1 - (n * unr + jj)
                rows = pl.ds(pl.multiple_of(j * blk, blk), blk)
                kj = k_ref[rows, :].astype(MXU_DTYPE)
                vj = v_ref[rows, :].astype(MXU_DTYPE)
                mask = cmr < i * tq - j * blk
                for h in range(hp):
                    z, lm = _sb_scores(q_h[h], kj, mask)
                    cs = _split_dot(lm, u_rev)
                    w = jnp.where(mask, jnp.exp(z + cs + runs[h]), 0.0)
                    acc = acc + jnp.dot(w.astype(MXU_DTYPE), jnp.where(heads[h], vj, 0), preferred_element_type=F32)
                    runs[h] = runs[h] + cs[:, 0:1]
            return acc, tuple(runs)

        init = (jnp.zeros((tq, LANES), F32), tuple(jnp.zeros((tq, 1), F32) for _ in range(hp)))
        acc, _ = lax.fori_loop(0, (i + 1) * (kpq // unr), step, init)
        o_ref[...] = acc

    res = pl.pallas_call(
        body,
        out_shape=[jax.ShapeDtypeStruct((nb * SEQ, cfg.sbw), F32)] + _gather_shapes(gather),
        grid=(nb, ncb, nq),
        in_specs=[
            pl.BlockSpec((tq, LANES), lambda b, c, i: (b * nq + i, qb + c)),
            pl.BlockSpec((SEQ, LANES), lambda b, c, i: (b, kb + c)),
            pl.BlockSpec((SEQ, LANES), lambda b, c, i: (b, vb + c)),
        ] + [ANY_SPEC] * ng,
        out_specs=[pl.BlockSpec((tq, LANES), lambda b, c, i: (b * nq + i, c))] + [ANY_SPEC] * ng,
        scratch_shapes=_comm_sems(ng) if ng else [],
        compiler_params=_cparams(("arbitrary",) * 3 if ng else ("parallel", "parallel", "arbitrary")),
        name="sb_fwd_gather" if ng else "sb_fwd",
    )(proj, proj, proj, *gather)
    return res[0], res[1:]


def _sb_bwd(proj, do_att, cfg, nb, scatter=()):
    blk = SB_BLOCK
    tq = _tile(SEQ, SB_ROWS, blk)
    nq = SEQ // tq
    kpq = tq // blk
    unr = math.gcd(kpq, SB_UNROLL)
    hp = LANES // SB_HEAD_DIM
    ncb = cfg.sbw // LANES
    scale = SB_HEAD_DIM ** -0.5
    qb, kb, vb = cfg.q0 // LANES, cfg.k0 // LANES, cfg.v0 // LANES
    ns = len(scatter)

    def body(q_ref, k_ref, v_ref, do_ref, *rest):
        dq_ref, dk_ref, dv_ref = rest[ns:ns + 3]
        g_ref, z_ref = rest[2 * ns + 3:2 * ns + 5]
        if ns:
            step_id, n_steps = _grid_step3(nb, ncb, nq)
            start, finish = _scatter_phases(rest[:ns], rest[ns + 3:2 * ns + 3], *rest[2 * ns + 5:])
            pl.when(step_id == 0)(start)
        _sb_bwd_block(q_ref, k_ref, v_ref, do_ref, dq_ref, dk_ref, dv_ref, g_ref, z_ref)
        if ns:
            pl.when(step_id == n_steps - 1)(finish)

    def _sb_bwd_block(q_ref, k_ref, v_ref, do_ref, dq_ref, dk_ref, dv_ref, g_ref, z_ref):
        i = pl.program_id(2)

        @pl.when(i == 0)
        def _():
            dk_ref[...] = jnp.zeros_like(dk_ref)
            dv_ref[...] = jnp.zeros_like(dv_ref)

        heads, q_h, cmr, sq_row, sq_col = _sb_setup(q_ref, tq)
        u_rev = (sq_row >= sq_col).astype(BF16)
        u_fwd = (sq_row <= sq_col).astype(BF16)
        do = do_ref[...]
        do_h = [jnp.where(hm, do, 0.0).astype(MXU_DTYPE) for hm in heads]

        def sweep_left(n, runs):
            runs = list(runs)
            for jj in range(unr):
                j = (i + 1) * kpq - 1 - (n * unr + jj)
                rows = pl.ds(pl.multiple_of(j * blk, blk), blk)
                kj = k_ref[rows, :].astype(MXU_DTYPE)
                vj = v_ref[rows, :].astype(MXU_DTYPE)
                mask = cmr < i * tq - j * blk
                dv = jnp.zeros((blk, LANES), F32)
                for h in range(hp):
                    z, lm = _sb_scores(q_h[h], kj, mask)
                    cs = _split_dot(lm, u_rev)
                    a = jnp.where(mask, jnp.exp(z + cs + runs[h]), 0.0)
                    da = lax.dot_general(do_h[h], vj, C11, preferred_element_type=F32)
                    dv = dv + lax.dot_general(a.astype(MXU_DTYPE), do_h[h], C00, preferred_element_type=F32)
                    g_ref[h, j] = a * da
                    z_ref[h, j] = jax.nn.sigmoid(z)
                    runs[h] = runs[h] + cs[:, 0:1]
                dv_ref[rows, :] += dv
            return tuple(runs)

        trips = (i + 1) * (kpq // unr)
        lax.fori_loop(0, trips, sweep_left, tuple(jnp.zeros((tq, 1), F32) for _ in range(hp)))

        def sweep_right(n, carry):
            dq, runs = carry
            runs = list(runs)
            for jj in range(unr):
                j = n * unr + jj
                rows = pl.ds(pl.multiple_of(j * blk, blk), blk)
                kj = k_ref[rows, :].astype(MXU_DTYPE)
                mask = cmr < i * tq - j * blk
                dk = jnp.zeros((blk, LANES), F32)
                for h in range(hp):
                    g = g_ref[h, j]
                    g_upto = _split_dot(g, u_fwd) + runs[h]
                    dz = jnp.where(mask, g - z_ref[h, j] * g_upto, 0.0).astype(MXU_DTYPE)
                    dq = dq + jnp.dot(dz, jnp.where(heads[h], kj, 0), preferred_element_type=F32)
                    dk = dk + lax.dot_general(dz, q_h[h], C00, preferred_element_type=F32)
                    runs[h] = runs[h] + jnp.sum(g, axis=1, keepdims=True)
                dk_ref[rows, :] += dk
            return dq, tuple(runs)

        init = (jnp.zeros((tq, LANES), F32), tuple(jnp.zeros((tq, 1), F32) for _ in range(hp)))
        dq, _ = lax.fori_loop(0, trips, sweep_right, init)
        dq_ref[...] = (dq * scale).astype(dq_ref.dtype)

    kv_spec_out = pl.BlockSpec((SEQ, LANES), lambda b, c, i: (b, c))
    q_spec_out = pl.BlockSpec((tq, LANES), lambda b, c, i: (b * nq + i, c))
    res = pl.pallas_call(
        body,
        out_shape=[
            jax.ShapeDtypeStruct((nb * SEQ, cfg.sbw), ACT_DTYPE),
            jax.ShapeDtypeStruct((nb * SEQ, cfg.sbw), F32),
            jax.ShapeDtypeStruct((nb * SEQ, cfg.sbw), F32),
        ] + [jax.ShapeDtypeStruct(a.shape, a.dtype) for a in scatter],
        grid=(nb, ncb, nq),
        in_specs=[
            pl.BlockSpec((tq, LANES), lambda b, c, i: (b * nq + i, qb + c)),
            pl.BlockSpec((SEQ, LANES), lambda b, c, i: (b, kb + c)),
            pl.BlockSpec((SEQ, LANES), lambda b, c, i: (b, vb + c)),
            q_spec_out,
        ] + [ANY_SPEC] * ns,
        out_specs=[q_spec_out, kv_spec_out, kv_spec_out] + [ANY_SPEC] * ns,
        scratch_shapes=[pltpu.VMEM((hp, SEQ // blk, tq, blk), F32), pltpu.VMEM((hp, SEQ // blk, tq, blk), F32)]
        + (_comm_sems(ns) if ns else []),
        compiler_params=_cparams(("arbitrary",) * 3 if ns else ("parallel", "parallel", "arbitrary")),
        name="sb_bwd_scatter" if ns else "sb_bwd",
    )(proj, proj, proj, do_att, *scatter)
    return res[0], res[1], res[2], res[3:]


CONV_COLS = 256


def _conv_pre(x, w, b, t):
    kw = SSM_CONV
    shifted = []
    pre = b + w[kw - 1:kw, :] * x
    for k in range(kw - 1):
        d = kw - 1 - k
        xs = jnp.where(t >= d, pltpu.roll(x, d, 0), 0.0)
        shifted.append(xs)
        pre = pre + w[k:k + 1, :] * xs
    shifted.append(x)
    return pre, shifted


def _conv_fwd(proj, conv_w, conv_b, l, cfg, nb):
    cw = CONV_COLS
    ncb = cfg.conv_dim // cw
    xb = cfg.xbc0 // cw

    def body(x_ref, w_ref, b_ref, o_ref):
        x = x_ref[...]
        t = lax.broadcasted_iota(jnp.int32, x.shape, 0)
        pre, _ = _conv_pre(x, w_ref[...], b_ref[...], t)
        o_ref[...] = _silu(pre)

    return pl.pallas_call(
        body,
        out_shape=jax.ShapeDtypeStruct((nb * SEQ, cfg.conv_dim), F32),
        grid=(ncb, nb),
        in_specs=[
            pl.BlockSpec((SEQ, cw), lambda j, b: (b, xb + j)),
            pl.BlockSpec((None, SSM_CONV, cw), lambda j, b: (0, 0, j)),
            pl.BlockSpec((None, 1, cw), lambda j, b: (l, 0, j)),
        ],
        out_specs=pl.BlockSpec((SEQ, cw), lambda j, b: (b, j)),
        compiler_params=_cparams(("parallel", "parallel")),
        name="conv_fwd",
    )(proj, conv_w, conv_b)


def _conv_bwd(proj, dact, conv_w, conv_b, l, cfg, nb):
    cw = CONV_COLS
    ncb = cfg.conv_dim // cw
    xb = cfg.xbc0 // cw
    kw = SSM_CONV

    def body(x_ref, da_ref, w_ref, b_ref, dx_ref, dw_ref, db_ref):
        b_id = pl.program_id(1)
        x = x_ref[...]
        w = w_ref[...]
        t = lax.broadcasted_iota(jnp.int32, x.shape, 0)
        pre, shifted = _conv_pre(x, w, b_ref[...], t)
        dpre = da_ref[...] * _silu_grad(pre)
        dx = w[kw - 1:kw, :] * dpre
        for k in range(kw - 1):
            d = kw - 1 - k
            dx = dx + w[k:k + 1, :] * jnp.where(t < SEQ - d, pltpu.roll(dpre, SEQ - d, 0), 0.0)
        dx_ref[...] = dx.astype(dx_ref.dtype)
        dw = jnp.concatenate([jnp.sum(dpre * s, axis=0, keepdims=True) for s in shifted], axis=0)
        db = jnp.sum(dpre, axis=0, keepdims=True)

        @pl.when(b_id == 0)
        def _():
            dw_ref[...] = dw
            db_ref[...] = db

        @pl.when(b_id > 0)
        def _():
            dw_ref[...] += dw
            db_ref[...] += db

    return pl.pallas_call(
        body,
        out_shape=[
            jax.ShapeDtypeStruct((nb * SEQ, cfg.conv_dim), ACT_DTYPE),
            jax.ShapeDtypeStruct((kw, cfg.conv_dim), F32),
            jax.ShapeDtypeStruct((1, cfg.conv_dim), F32),
        ],
        grid=(ncb, nb),
        in_specs=[
            pl.BlockSpec((SEQ, cw), lambda j, b: (b, xb + j)),
            pl.BlockSpec((SEQ, cw), lambda j, b: (b, j)),
            pl.BlockSpec((None, kw, cw), lambda j, b: (0, 0, j)),
            pl.BlockSpec((None, 1, cw), lambda j, b: (l, 0, j)),
        ],
        out_specs=[
            pl.BlockSpec((SEQ, cw), lambda j, b: (b, j)),
            pl.BlockSpec((kw, cw), lambda j, b: (0, j)),
            pl.BlockSpec((1, cw), lambda j, b: (0, j)),
        ],
        compiler_params=_cparams(("parallel", "arbitrary")),
        name="conv_bwd",
    )(proj, dact, conv_w, conv_b)


def _ssd_common(dt_raw, dt_bias, a_log, tri):
    ln = SSM_CHUNK
    dt = _softplus(dt_raw + dt_bias)
    a = -jnp.exp(a_log)
    a_cs = jnp.dot(tri, dt * a, preferred_element_type=F32, precision=lax.Precision.HIGHEST)
    a_last = a_cs[ln - 1:ln, :]
    return dt, a, a_cs, a_cs.T, jnp.exp(a_cs), jnp.exp(a_last - a_cs), jnp.exp(a_last)


def _ssd_head(h, xs, dt, a_cs, a_t, cb, tril):
    p = SSM_HEAD_DIM
    x_h = xs[:, h * p:(h + 1) * p]
    xd = x_h * dt[:, h:h + 1]
    lmat = jnp.exp(jnp.where(tril, a_cs[:, h:h + 1] - a_t[h:h + 1, :], -jnp.inf))
    return x_h, xd, lmat


def _ssd_specs(cfg, nc, rev):
    ln = SSM_CHUNK
    cidx = (lambda c: nc - 1 - c) if rev else (lambda c: c)
    bmb = cfg.inner // cfg.gn
    return [
        pl.BlockSpec((ln, cfg.inner), lambda b, c: (b * nc + cidx(c), 0)),
        pl.BlockSpec((ln, cfg.gn), lambda b, c: (b * nc + cidx(c), bmb)),
        pl.BlockSpec((ln, cfg.gn), lambda b, c: (b * nc + cidx(c), bmb + 1)),
        pl.BlockSpec((ln, LANES), lambda b, c: (b * nc + cidx(c), cfg.dt0 // LANES)),
    ]


def _ssd_fwd(xbc, proj, dt_bias, a_log, d_skip, l, cfg, nb):
    ln, p, n = SSM_CHUNK, SSM_HEAD_DIM, SSM_STATE
    nc = SEQ // ln
    g_, e_ = SSM_GROUPS, cfg.epg
    assert cfg.inner % cfg.gn == 0

    def body(xs_ref, bm_ref, cm_ref, dtr_ref, bias_ref, alog_ref, dsk_ref, y_ref, st_ref, s_ref):
        c = pl.program_id(1)

        @pl.when(c == 0)
        def _():
            s_ref[...] = jnp.zeros_like(s_ref)

        st_ref[...] = s_ref[...]
        row = lax.broadcasted_iota(jnp.int32, (ln, ln), 0)
        col = lax.broadcasted_iota(jnp.int32, (ln, ln), 1)
        tril = row >= col
        dt, _, a_cs, a_t, e_a, dte, cd = _ssd_common(dtr_ref[...], bias_ref[...], alog_ref[...], tril.astype(F32))
        dsk = dsk_ref[...]
        xs = xs_ref[...]
        for g in range(g_):
            bm = bm_ref[:, g * n:(g + 1) * n].astype(MXU_DTYPE)
            cm = cm_ref[:, g * n:(g + 1) * n].astype(MXU_DTYPE)
            cb = lax.dot_general(cm, bm, (((1,), (1,)), ((), ())), preferred_element_type=F32)
            for e in range(e_):
                h = g * e_ + e
                x_h, xd, lmat = _ssd_head(h, xs, dt, a_cs, a_t, cb, tril)
                s_prev = s_ref[g * n:(g + 1) * n, e * p:(e + 1) * p]
                y = jnp.dot((cb * lmat).astype(MXU_DTYPE), xd.astype(MXU_DTYPE), preferred_element_type=F32)
                y = y + jnp.dot(cm, s_prev.astype(MXU_DTYPE), preferred_element_type=F32) * e_a[:, h:h + 1]
                y_ref[:, h * p:(h + 1) * p] = y + dsk[:, h:h + 1] * x_h
                upd = lax.dot_general(bm, (xd * dte[:, h:h + 1]).astype(MXU_DTYPE), (((0,), (0,)), ((), ())),
                                      preferred_element_type=F32)
                s_ref[g * n:(g + 1) * n, e * p:(e + 1) * p] = cd[:, h:h + 1] * s_prev + upd

    vec = lambda b, c: (l, 0, 0)
    return pl.pallas_call(
        body,
        out_shape=[
            jax.ShapeDtypeStruct((nb * SEQ, cfg.inner), F32),
            jax.ShapeDtypeStruct((nb * nc * g_ * n, e_ * p), F32),
        ],
        grid=(nb, nc),
        in_specs=_ssd_specs(cfg, nc, False) + [pl.BlockSpec((None, 1, LANES), vec)] * 3,
        out_specs=[
            pl.BlockSpec((ln, cfg.inner), lambda b, c: (b * nc + c, 0)),
            pl.BlockSpec((g_ * n, e_ * p), lambda b, c: (b * nc + c, 0)),
        ],
        scratch_shapes=[pltpu.VMEM((g_ * n, e_ * p), F32)],
        compiler_params=_cparams(("parallel", "arbitrary")),
        name="ssd_fwd",
    )(xbc, xbc, xbc, proj, dt_bias, a_log, d_skip)


def _ssd_bwd(xbc, proj, states, dy, dt_bias, a_log, d_skip, l, cfg, nb):
    ln, p, n = SSM_CHUNK, SSM_HEAD_DIM, SSM_STATE
    nc = SEQ // ln
    g_, e_ = SSM_GROUPS, cfg.epg
    c00 = (((0,), (0,)), ((), ()))
    c11 = (((1,), (1,)), ((), ()))

    def body(xs_ref, bm_ref, cm_ref, dtr_ref, st_ref, dy_ref, bias_ref, alog_ref, dsk_ref,
             dxbc_ref, ddt_ref, dvec_ref, ds_ref):
        first = jnp.logical_and(pl.program_id(0) == 0, pl.program_id(1) == 0)

        @pl.when(pl.program_id(1) == 0)
        def _():
            ds_ref[...] = jnp.zeros_like(ds_ref)

        row = lax.broadcasted_iota(jnp.int32, (ln, ln), 0)
        col = lax.broadcasted_iota(jnp.int32, (ln, ln), 1)
        tril = row >= col
        tri_f = tril.astype(F32)
        dtr = dtr_ref[...]
        bias = bias_ref[...]
        dt, a, a_cs, a_t, e_a, dte, cd = _ssd_common(dtr, bias, alog_ref[...], tri_f)
        dsk = dsk_ref[...]
        xs = xs_ref[...]
        lane = lax.broadcasted_iota(jnp.int32, (1, LANES), 1)
        sub = lax.broadcasted_iota(jnp.int32, (LANES, 1), 0)
        da_col = jnp.zeros((ln, LANES), F32)
        da_row_t = jnp.zeros((LANES, ln), F32)
        ddt = jnp.zeros((ln, LANES), F32)
        da_last = jnp.zeros((1, LANES), F32)
        ddsk = jnp.zeros((1, LANES), F32)
        for g in range(g_):
            bm = bm_ref[:, g * n:(g + 1) * n].astype(MXU_DTYPE)
            cm = cm_ref[:, g * n:(g + 1) * n].astype(MXU_DTYPE)
            cb = lax.dot_general(cm, bm, c11, preferred_element_type=F32)
            dbm = jnp.zeros((ln, n), F32)
            dcm = jnp.zeros((ln, n), F32)
            for e in range(e_):
                h = g * e_ + e
                hs = slice(h * p, (h + 1) * p)
                oh = (lane == h).astype(F32)
                x_h, xd, lmat = _ssd_head(h, xs, dt, a_cs, a_t, cb, tril)
                d_y = dy_ref[:, hs]
                s_prev = st_ref[g * n:(g + 1) * n, e * p:(e + 1) * p]
                d_s = ds_ref[g * n:(g + 1) * n, e * p:(e + 1) * p]
                dy_m = d_y.astype(MXU_DTYPE)
                xd_m = xd.astype(MXU_DTYPE)
                sp_m = s_prev.astype(MXU_DTYPE)
                ds_m = d_s.astype(MXU_DTYPE)
                e_a_h, dte_h, cd_h = e_a[:, h:h + 1], dte[:, h:h + 1], cd[:, h:h + 1]
                m_mat = (cb * lmat).astype(MXU_DTYPE)
                bds = jnp.dot(bm, ds_m, preferred_element_type=F32)
                d_xd = lax.dot_general(m_mat, dy_m, c00, preferred_element_type=F32) + dte_h * bds
                d_m = lax.dot_general(dy_m, xd_m, c11, preferred_element_type=F32)
                d_cb = (d_m * lmat).astype(MXU_DTYPE)
                w_mat = d_m * cb * lmat
                dy_e = (d_y * e_a_h).astype(MXU_DTYPE)
                xd_e = (xd * dte_h).astype(MXU_DTYPE)
                dcm = dcm + jnp.dot(d_cb, bm, preferred_element_type=F32)
                dcm = dcm + lax.dot_general(dy_e, sp_m, c11, preferred_element_type=F32)
                dbm = dbm + lax.dot_general(d_cb, cm, c00, preferred_element_type=F32)
                dbm = dbm + lax.dot_general(xd_e, ds_m, c11, preferred_element_type=F32)
                ds_ref[g * n:(g + 1) * n, e * p:(e + 1) * p] = (
                    cd_h * d_s + lax.dot_general(cm, dy_e, c00, preferred_element_type=F32))
                y_off = jnp.dot(cm, sp_m, preferred_element_type=F32) * e_a_h
                q_h = jnp.sum(bds * xd, axis=1, keepdims=True) * dte_h
                da_col_h = (jnp.sum(w_mat, axis=1, keepdims=True)
                            + jnp.sum(d_y * y_off, axis=1, keepdims=True) - q_h)
                da_col = da_col + da_col_h * oh
                da_row_t = da_row_t - (sub == h).astype(F32) * jnp.sum(w_mat, axis=0, keepdims=True)
                da_last = da_last + (jnp.sum(q_h, keepdims=True) + cd_h * jnp.sum(d_s * s_prev, keepdims=True)) * oh
                dxbc_ref[:, hs] = d_xd * dt[:, h:h + 1] + dsk[:, h:h + 1] * d_y
                ddt = ddt + jnp.sum(d_xd * x_h, axis=1, keepdims=True) * oh
                ddsk = ddsk + jnp.sum(d_y * x_h, keepdims=True) * oh
            dxbc_ref[:, cfg.inner + g * n:cfg.inner + (g + 1) * n] = dbm
            dxbc_ref[:, cfg.inner + cfg.gn + g * n:cfg.inner + cfg.gn + (g + 1) * n] = dcm
        d_acs = da_col + da_row_t.T + jnp.where(row[:, 0:1] == ln - 1, da_last, 0.0)
        da_dt = lax.dot_general(tri_f, d_acs, c00, preferred_element_type=F32, precision=lax.Precision.HIGHEST)
        ddt = ddt + da_dt * a
        ddt_raw = ddt * jax.nn.sigmoid(dtr + bias)
        ddt_ref[...] = ddt_raw.astype(ddt_ref.dtype)
        da_log = jnp.sum(da_dt * dt, axis=0, keepdims=True) * a
        dvec = jnp.concatenate([jnp.sum(ddt_raw, axis=0, keepdims=True), da_log, ddsk,
                                jnp.zeros((5, LANES), F32)], axis=0)

        @pl.when(first)
        def _():
            dvec_ref[...] = dvec

        @pl.when(jnp.logical_not(first))
        def _():
            dvec_ref[...] += dvec

    vec = lambda b, c: (l, 0, 0)
    rblk = lambda b, c: (b * nc + nc - 1 - c, 0)
    return pl.pallas_call(
        body,
        out_shape=[
            jax.ShapeDtypeStruct((nb * SEQ, cfg.conv_dim), F32),
            jax.ShapeDtypeStruct((nb * SEQ, LANES), ACT_DTYPE),
            jax.ShapeDtypeStruct((8, LANES), F32),
        ],
        grid=(nb, nc),
        in_specs=_ssd_specs(cfg, nc, True) + [
            pl.BlockSpec((g_ * n, e_ * p), rblk),
            pl.BlockSpec((ln, cfg.inner), rblk),
        ] + [pl.BlockSpec((None, 1, LANES), vec)] * 3,
        out_specs=[
            pl.BlockSpec((ln, cfg.conv_dim), rblk),
            pl.BlockSpec((ln, LANES), rblk),
            pl.BlockSpec((8, LANES), lambda b, c: (0, 0)),
        ],
        scratch_shapes=[pltpu.VMEM((g_ * n, e_ * p), F32)],
        compiler_params=_cparams(("arbitrary", "arbitrary")),
        name="ssd_bwd",
    )(xbc, xbc, xbc, proj, states, dy, dt_bias, a_log, d_skip)


XA_ROWS = 256


def _xa_probs(q_ref, kv_ref, h, dh):
    c11 = (((1,), (1,)), ((), ()))
    qh = q_ref[:, h * dh:(h + 1) * dh].astype(MXU_DTYPE)
    kh = kv_ref[:, h * dh:(h + 1) * dh].astype(MXU_DTYPE)
    vh = kv_ref[:, D_MODEL + h * dh:D_MODEL + (h + 1) * dh].astype(MXU_DTYPE)
    s = lax.dot_general(qh, kh, c11, preferred_element_type=F32) * (dh ** -0.5)
    s = s - jnp.max(s, axis=1, keepdims=True)
    pr = jnp.exp(s)
    return qh, kh, vh, pr / jnp.sum(pr, axis=1, keepdims=True)


def _xa_fwd(q, kv, cfg, nb):
    tq = _tile(SEQ, XA_ROWS, 16)
    nq = SEQ // tq
    dh = cfg.xa_dim

    def body(q_ref, kv_ref, o_ref):
        for h in range(XA_HEADS):
            _, _, vh, pr = _xa_probs(q_ref, kv_ref, h, dh)
            o_ref[:, h * dh:(h + 1) * dh] = jnp.dot(pr.astype(MXU_DTYPE), vh, preferred_element_type=F32).astype(o_ref.dtype)

    return pl.pallas_call(
        body,
        out_shape=jax.ShapeDtypeStruct((nb * SEQ, D_MODEL), ACT_DTYPE),
        grid=(nb, nq),
        in_specs=[
            pl.BlockSpec((tq, D_MODEL), lambda b, i: (b * nq + i, 0)),
            pl.BlockSpec((MEM_LEN, 2 * D_MODEL), lambda b, i: (b, 0)),
        ],
        out_specs=pl.BlockSpec((tq, D_MODEL), lambda b, i: (b * nq + i, 0)),
        compiler_params=_cparams(("parallel", "parallel")),
        name="xa_fwd",
    )(q, kv)


def _xa_bwd(q, kv, do, cfg, nb):
    tq = _tile(SEQ, XA_ROWS, 16)
    nq = SEQ // tq
    dh = cfg.xa_dim
    c00 = (((0,), (0,)), ((), ()))
    c11 = (((1,), (1,)), ((), ()))
    scale = dh ** -0.5

    def body(q_ref, kv_ref, do_ref, dq_ref, dkv_ref, acc_ref):
        i = pl.program_id(1)

        @pl.when(i == 0)
        def _():
            acc_ref[...] = jnp.zeros_like(acc_ref)

        for h in range(XA_HEADS):
            hs = slice(h * dh, (h + 1) * dh)
            vs = slice(D_MODEL + h * dh, D_MODEL + (h + 1) * dh)
            qh, kh, vh, pr = _xa_probs(q_ref, kv_ref, h, dh)
            do_h = do_ref[:, hs].astype(MXU_DTYPE)
            dp = lax.dot_general(do_h, vh, c11, preferred_element_type=F32)
            ds = (pr * (dp - jnp.sum(dp * pr, axis=1, keepdims=True))).astype(MXU_DTYPE)
            dq_ref[:, hs] = (jnp.dot(ds, kh, preferred_element_type=F32) * scale).astype(dq_ref.dtype)
            acc_ref[:, hs] += lax.dot_general(ds, qh, c00, preferred_element_type=F32) * scale
            acc_ref[:, vs] += lax.dot_general(pr.astype(MXU_DTYPE), do_h, c00, preferred_element_type=F32)

        @pl.when(i == nq - 1)
        def _():
            dkv_ref[...] = acc_ref[...].astype(dkv_ref.dtype)

    return pl.pallas_call(
        body,
        out_shape=[
            jax.ShapeDtypeStruct((nb * SEQ, D_MODEL), ACT_DTYPE),
            jax.ShapeDtypeStruct((nb * MEM_LEN, 2 * D_MODEL), ACT_DTYPE),
        ],
        grid=(nb, nq),
        in_specs=[
            pl.BlockSpec((tq, D_MODEL), lambda b, i: (b * nq + i, 0)),
            pl.BlockSpec((MEM_LEN, 2 * D_MODEL), lambda b, i: (b, 0)),
            pl.BlockSpec((tq, D_MODEL), lambda b, i: (b * nq + i, 0)),
        ],
        out_specs=[
            pl.BlockSpec((tq, D_MODEL), lambda b, i: (b * nq + i, 0)),
            pl.BlockSpec((MEM_LEN, 2 * D_MODEL), lambda b, i: (b, 0)),
        ],
        scratch_shapes=[pltpu.VMEM((MEM_LEN, 2 * D_MODEL), F32)],
        compiler_params=_cparams(("parallel", "arbitrary")),
        name="xa_bwd",
    )(q, kv, do)


def _adamw(parts, w, m, v, name, tr=128, pre=()):
    n, r, c = parts.shape
    tr = _tile(r, tr, 16)

    def body(p_ref, w_ref, m_ref, v_ref, g_ref, d_ref, nm_ref, nv_ref):
        g = p_ref[0].astype(F32)
        for i in range(1, n):
            g = g + p_ref[i].astype(F32)
        m2 = ADAM_B1 * m_ref[...] + (1.0 - ADAM_B1) * g
        v2 = ADAM_B2 * v_ref[...] + (1.0 - ADAM_B2) * (g * g)
        m_hat = m2 / (1.0 - ADAM_B1 ** ADAM_STEP)
        v_hat = v2 / (1.0 - ADAM_B2 ** ADAM_STEP)
        g_ref[...] = g
        d_ref[...] = -ADAM_LR * (m_hat / (jnp.sqrt(v_hat) + ADAM_EPS) + ADAM_WD * w_ref[...])
        nm_ref[...] = m2
        nv_ref[...] = v2

    blk = pl.BlockSpec((tr, c), lambda i: (i, 0))
    wblk = pl.BlockSpec((None,) * len(pre) + (tr, c), lambda i: pre + (i, 0))
    return pl.pallas_call(
        body,
        out_shape=[jax.ShapeDtypeStruct((r, c), F32)] * 4,
        grid=(r // tr,),
        in_specs=[pl.BlockSpec((n, tr, c), lambda i: (0, i, 0)), wblk, wblk, wblk],
        out_specs=[blk] * 4,
        compiler_params=_cparams(("parallel",)),
        name=name,
    )(parts, w, m, v)


def _flat_index(px, py, pc):
    return 4 * px + 2 * py + pc


def _all_gather(arrs, name):
    n = len(arrs)

    def body(*refs):
        start, forward, finish = _gather_phases(refs[:n], refs[n:2 * n], *refs[2 * n:])
        start()
        forward()
        finish()

    return pl.pallas_call(
        body,
        out_shape=_gather_shapes(arrs),
        in_specs=[ANY_SPEC] * n,
        out_specs=[ANY_SPEC] * n,
        scratch_shapes=_comm_sems(n),
        name=name,
    )(*arrs)


ANY_SPEC = pl.BlockSpec(memory_space=pl.ANY)


def _comm_sems(n):
    return [pltpu.SemaphoreType.DMA((n, N_DEV - 1)), pltpu.SemaphoreType.DMA((n, N_DEV - 1)),
            pltpu.SemaphoreType.DMA((n,))]


def _gather_shapes(arrs):
    return [jax.ShapeDtypeStruct((N_DEV,) + a.shape, a.dtype) for a in arrs]


def _gather_phases(ins, outs, send_sems, recv_sems, local_sems):
    n = len(ins)
    x, y, c = lax.axis_index("x"), lax.axis_index("y"), lax.axis_index("c")
    me, sibling = (x, y, c), (x, y, 1 - c)
    chips = [(1 - x, y), (x, 1 - y), (1 - x, 1 - y)]

    def copy(a, k, block, to, src=None):
        slot = outs[a].at[_flat_index(*block)]
        return pltpu.make_async_remote_copy(
            src_ref=slot if src is None else src, dst_ref=slot,
            send_sem=send_sems.at[a, k], recv_sem=recv_sems.at[a, k],
            device_id=to, device_id_type=MESH)

    def mine(a):
        return pltpu.make_async_copy(ins[a], outs[a].at[_flat_index(*me)], local_sems.at[a])

    def first(a):
        return [copy(a, 0, me, sibling, src=ins[a])] + [
            copy(a, 1 + j, me, (*chip, c), src=ins[a]) for j, chip in enumerate(chips)]

    def start():
        for a in range(n):
            mine(a).start()
            for cp in first(a):
                cp.start()

    def forward():
        for j, chip in enumerate(chips):
            for a in range(n):
                copy(a, 1 + j, (*chip, c), me).wait_recv()
                copy(a, 4 + j, (*chip, c), sibling).start()

    def finish():
        for a in range(n):
            copy(a, 0, sibling, me).wait_recv()
            for j, chip in enumerate(chips):
                copy(a, 4 + j, (*chip, 1 - c), me).wait_recv()
        for a in range(n):
            for cp in first(a):
                cp.wait_send()
            for j, chip in enumerate(chips):
                copy(a, 4 + j, (*chip, c), sibling).wait_send()
            mine(a).wait()

    return start, forward, finish


def _scatter_blocks(arrs, name):
    n = len(arrs)

    def body(*refs):
        start, finish = _scatter_phases(refs[:n], refs[n:2 * n], *refs[2 * n:])
        start()
        finish()

    return pl.pallas_call(
        body,
        out_shape=[jax.ShapeDtypeStruct(a.shape, a.dtype) for a in arrs],
        in_specs=[ANY_SPEC] * n,
        out_specs=[ANY_SPEC] * n,
        scratch_shapes=_comm_sems(n),
        name=name,
    )(*arrs)


def _scatter_phases(ins, outs, send_sems, recv_sems, local_sems):
    n = len(ins)
    x, y, c = lax.axis_index("x"), lax.axis_index("y"), lax.axis_index("c")
    me = _flat_index(x, y, c)

    def peer(k):
        return (1 - x if k & 4 else x, 1 - y if k & 2 else y, 1 - c if k & 1 else c)

    def copy(a, k):
        p = peer(k)
        return pltpu.make_async_remote_copy(
            src_ref=ins[a].at[_flat_index(*p)], dst_ref=outs[a].at[me],
            send_sem=send_sems.at[a, k - 1], recv_sem=recv_sems.at[a, k - 1],
            device_id=p, device_id_type=MESH)

    def landed(a, k):
        slot = outs[a].at[_flat_index(*peer(k))]
        return pltpu.make_async_remote_copy(
            src_ref=slot, dst_ref=slot, send_sem=send_sems.at[a, k - 1], recv_sem=recv_sems.at[a, k - 1],
            device_id=peer(k), device_id_type=MESH)

    def mine(a):
        return pltpu.make_async_copy(ins[a].at[me], outs[a].at[me], local_sems.at[a])

    def start():
        for a in range(n):
            mine(a).start()
            for k in range(1, N_DEV):
                copy(a, k).start()

    def finish():
        for a in range(n):
            for k in range(1, N_DEV):
                landed(a, k).wait_recv()
        for a in range(n):
            for k in range(1, N_DEV):
                copy(a, k).wait_send()
            mine(a).wait()

    return start, finish


_BIG = ("w_in", "w_br_att", "w_br_ssm", "w_mix_out", "w_xq", "w_xkv", "w_xo", "w_gu", "w_down")
_COL_SHARDED = ("w_in", "w_xkv", "w_gu", "conv_w")
_SMALL = ("g_pre_mix", "conv_b", "dt_bias", "a_log", "d_skip", "g_ssm_norm", "g_post_mix", "g_pre_xa",
          "g_mem", "g_post_xa", "g_pre_ffn", "g_post_ffn")
_WEIGHTS = ("g_pre_mix", "w_in", "conv_w", "conv_b", "dt_bias", "a_log", "d_skip", "g_ssm_norm", "w_br_att",
            "w_br_ssm", "w_mix_out", "g_post_mix", "g_pre_xa", "g_mem", "w_xq", "w_xkv", "w_xo", "g_post_xa",
            "g_pre_ffn", "w_gu", "w_down", "g_post_ffn")
PACK_W = 8 * LANES


def _unshard(g, col):
    n, r, c = g.shape
    if col:
        return jnp.transpose(g, (1, 0, 2)).reshape(r, n * c)
    return g.reshape(n * r, c)


def _shard(w, col):
    r, c = w.shape
    if col:
        return jnp.transpose(w.reshape(r, N_DEV, c // N_DEV), (1, 0, 2))
    return w.reshape(N_DEV, r // N_DEV, c)


def _permute_in(w, cfg):
    parts, off = [], 0
    for size in cfg.in_sizes:
        parts.append(w[..., off:off + size])
        off += size
    q, k, v, z, xbc, dt, ga, gs = parts
    pad = jnp.zeros(w.shape[:-1] + (LANES - cfg.heads,), w.dtype)
    return jnp.concatenate([z, ga, gs, q, k, v, xbc, dt, pad], axis=-1)


def _unpermute_in(w, cfg):
    c = cfg
    sl = lambda a, n: w[..., a:a + n]
    return jnp.concatenate([sl(c.q0, c.sbw), sl(c.k0, c.sbw), sl(c.v0, c.sbw), sl(c.z0, c.inner),
                            sl(c.xbc0, c.conv_dim), sl(c.dt0, c.heads), sl(c.ga0, c.d), sl(c.gs0, c.d)], axis=-1)


def _pack(arrs):
    flat = jnp.concatenate([a.reshape(-1).astype(F32) for a in arrs])
    rows = -(-flat.shape[0] // PACK_W)
    rows = -(-rows // 8) * 8
    return jnp.pad(flat, (0, rows * PACK_W - flat.shape[0])).reshape(rows, PACK_W)


def _unpack(p, shapes):
    flat = p.reshape(-1)
    out, off = [], 0
    for s in shapes:
        size = math.prod(s)
        out.append(flat[off:off + size].reshape(s))
        off += size
    return out


def _vec3(a, width=None):
    if width is not None and a.shape[1] < width:
        a = jnp.pad(a, ((0, 0), (0, width - a.shape[1])))
    return a[:, None, :]


def _forward_layer(l, xin, h1, memf, tgt, w, p, cfg, nb, last, gather):
    t = xin.shape[0]
    d = cfg.d
    s = {"x_in": xin, "h1": h1}
    proj = _mm(h1, w["w_in"], name="mm_proj")
    s["proj"] = proj
    s["o_att"], gathered = _sb_fwd(proj, cfg, nb, gather)
    s["xbc"] = _conv_fwd(proj, w["conv_w"], p["conv_b"], l, cfg, nb)
    s["y"], s["states"] = _ssd_fwd(s["xbc"], proj, p["dt_bias"], p["a_log"], p["d_skip"], l, cfg, nb)
    s["o_ssm"] = _rowwise(_f_gate_norm, "gate_norm_fwd", t, [_full(s["y"]), (proj, cfg.inner, 0)],
                          [(p["g_ssm_norm"], l)], [(cfg.inner, ACT_DTYPE)])[0]
    s["ba"] = _mm(s["o_att"], w["w_br_att"], name="mm_br_att")
    s["bs"] = _mm(s["o_ssm"], w["w_br_ssm"], name="mm_br_ssm")
    s["merged"] = _rowwise(_f_merge, "merge_fwd", t,
                           [(proj, d, cfg.ga0 // d), (proj, d, cfg.gs0 // d), _full(s["ba"]), _full(s["bs"])],
                           [], [(d, ACT_DTYPE)])[0]
    s["mo"] = _mm(s["merged"], w["w_mix_out"], name="mm_mix_out")
    s["x1"], s["h2"] = _rowwise(_f_post_pre, "post_pre_mix", t, [_full(xin), _full(s["mo"])],
                                [(p["g_post_mix"], l), (p["g_pre_xa"], l)], [(d, F32), (d, ACT_DTYPE)])
    s["mem_n"] = _rowwise(_rms, "mem_norm", memf.shape[0], [_full(memf)], [(p["g_mem"], l)], [(d, ACT_DTYPE)])[0]
    s["q"] = _mm(s["h2"], w["w_xq"], name="mm_xq")
    s["kv"] = _mm(s["mem_n"], w["w_xkv"], name="mm_xkv")
    s["o_xa"] = _xa_fwd(s["q"], s["kv"], cfg, nb)
    s["xo"] = _mm(s["o_xa"], w["w_xo"], name="mm_xo")
    s["x2"], s["h3"] = _rowwise(_f_post_pre, "post_pre_xa", t, [_full(s["x1"]), _full(s["xo"])],
                                [(p["g_post_xa"], l), (p["g_pre_ffn"], l)], [(d, F32), (d, ACT_DTYPE)])
    s["gu"] = _mm(s["h3"], w["w_gu"], name="mm_gu")
    s["act"] = _rowwise(_f_swiglu, "swiglu_fwd", t, [(s["gu"], cfg.ffn, 0), (s["gu"], cfg.ffn, 1)], [],
                        [(cfg.ffn, ACT_DTYPE)])[0]
    s["dn"] = _mm(s["act"], w["w_down"], name="mm_down")
    if last:
        nxt = _rowwise(_f_final, "final_loss", t, [_full(s["x2"]), _full(s["dn"]), _full(tgt)],
                       [(p["g_post_ffn"], l)], [(d, F32)], acc_out=[(1, d)])
    else:
        nxt = _rowwise(_f_post_pre, "post_pre_ffn", t, [_full(s["x2"]), _full(s["dn"])],
                       [(p["g_post_ffn"], l), (p["g_pre_mix"], l + 1)], [(d, F32), (d, ACT_DTYPE)])
    return s, nxt, gathered


def _backward_layer(l, s, dx, d_dn, memf, w, p, cfg, nb, prev_dn, scatter):
    t = dx.shape[0]
    d = cfg.d
    g = {}
    dact = _mm(d_dn, w["w_down"], tb=True, out_dtype=ACT_DTYPE, name="mm_d_act", tn=1408)
    g["w_down"] = _mm(s["act"], d_dn, ta=True, name="mm_dw_down")
    dgu = _rowwise(_f_swiglu_bwd, "swiglu_bwd", t, [(s["gu"], cfg.ffn, 0), (s["gu"], cfg.ffn, 1), _full(dact)], [],
                   [(2 * cfg.ffn, ACT_DTYPE)])[0]
    dh3 = _mm(dgu, w["w_gu"], tb=True, name="mm_d_h3", tk=1408)
    g["w_gu"] = _mm(s["h3"], dgu, ta=True, name="mm_dw_gu")
    dx2, d_xo, g["g_pre_ffn"], g["g_post_xa"] = _rowwise(
        _f_pre_post_bwd, "pre_post_bwd_ffn", t, [_full(s["x2"]), _full(dh3), _full(dx), _full(s["xo"])],
        [(p["g_pre_ffn"], l), (p["g_post_xa"], l)], [(d, F32), (d, ACT_DTYPE)], acc_out=[(1, d), (1, d)])
    do_xa = _mm(d_xo, w["w_xo"], tb=True, out_dtype=ACT_DTYPE, name="mm_d_oxa")
    g["w_xo"] = _mm(s["o_xa"], d_xo, ta=True, name="mm_dw_xo")
    dq, dkv = _xa_bwd(s["q"], s["kv"], do_xa, cfg, nb)
    dh2 = _mm(dq, w["w_xq"], tb=True, name="mm_d_h2")
    g["w_xq"] = _mm(s["h2"], dq, ta=True, name="mm_dw_xq")
    dmem_n = _mm(dkv, w["w_xkv"], tb=True, name="mm_d_mem")
    g["w_xkv"] = _mm(s["mem_n"], dkv, ta=True, name="mm_dw_xkv")
    g["g_mem"] = _rowwise(_f_gain_bwd, "mem_norm_bwd", memf.shape[0], [_full(memf), _full(dmem_n)],
                          [(p["g_mem"], l)], [], acc_out=[(1, d)])[0]
    dx1, d_mo, g["g_pre_xa"], g["g_post_mix"] = _rowwise(
        _f_pre_post_bwd, "pre_post_bwd_xa", t, [_full(s["x1"]), _full(dh2), _full(dx2), _full(s["mo"])],
        [(p["g_pre_xa"], l), (p["g_post_mix"], l)], [(d, F32), (d, ACT_DTYPE)], acc_out=[(1, d), (1, d)])
    dmerged = _mm(d_mo, w["w_mix_out"], tb=True, out_dtype=ACT_DTYPE, name="mm_d_merged")
    g["w_mix_out"] = _mm(s["merged"], d_mo, ta=True, name="mm_dw_mix_out")
    proj = s["proj"]
    dgg, dba, dbs = _rowwise(
        _f_merge_bwd, "merge_bwd", t,
        [(proj, d, cfg.ga0 // d), (proj, d, cfg.gs0 // d), _full(s["ba"]), _full(s["bs"]), _full(dmerged)], [],
        [(2 * d, ACT_DTYPE), (d, ACT_DTYPE), (d, ACT_DTYPE)])
    do_att = _mm(dba, w["w_br_att"], tb=True, name="mm_d_oatt")
    g["w_br_att"] = _mm(s["o_att"], dba, ta=True, name="mm_dw_br_att")
    do_ssm = _mm(dbs, w["w_br_ssm"], tb=True, out_dtype=ACT_DTYPE, name="mm_d_ossm")
    g["w_br_ssm"] = _mm(s["o_ssm"], dbs, ta=True, name="mm_dw_br_ssm")
    dy, dz, g["g_ssm_norm"] = _rowwise(
        _f_gate_norm_bwd, "gate_norm_bwd", t, [_full(s["y"]), (proj, cfg.inner, 0), _full(do_ssm)],
        [(p["g_ssm_norm"], l)], [(cfg.inner, F32), (cfg.inner, ACT_DTYPE)], acc_out=[(1, cfg.inner)])
    dxbc, ddt_raw, dvec = _ssd_bwd(s["xbc"], proj, s["states"], dy, p["dt_bias"], p["a_log"], p["d_skip"], l, cfg, nb)
    g["dt_bias"], g["a_log"], g["d_skip"] = (dvec[i:i + 1, :cfg.heads] for i in range(3))
    dxbc_raw, g["conv_w"], g["conv_b"] = _conv_bwd(proj, dxbc, w["conv_w"], p["conv_b"], l, cfg, nb)
    dq_sb, dk_sb, dv_sb, landed = _sb_bwd(proj, do_att, cfg, nb, scatter)
    dproj = jnp.concatenate([dz, dgg, dq_sb, dk_sb.astype(ACT_DTYPE), dv_sb.astype(ACT_DTYPE), dxbc_raw, ddt_raw], axis=1)
    dh1 = _mm(dproj, w["w_in"], tb=True, name="mm_d_h1", tk=1152)
    g["w_in"] = _mm(s["h1"], dproj, ta=True, name="mm_dw_in")
    if prev_dn is None:
        dx0, g["g_pre_mix"] = _rowwise(_f_pre_bwd, "pre_bwd_first", t, [_full(s["x_in"]), _full(dh1), _full(dx1)],
                                       [(p["g_pre_mix"], l)], [(d, F32)], acc_out=[(1, d)])
        return g, dx0, None, None, landed
    dx0, d_dn_prev, g["g_pre_mix"], g_post_prev = _rowwise(
        _f_pre_post_bwd, "pre_post_bwd_mix", t, [_full(s["x_in"]), _full(dh1), _full(dx1), _full(prev_dn)],
        [(p["g_pre_mix"], l), (p["g_post_ffn"], l - 1)], [(d, F32), (d, ACT_DTYPE)], acc_out=[(1, d), (1, d)])
    return g, dx0, d_dn_prev, g_post_prev, landed


def kernel(x, mem, g_pre_mix, w_in, conv_w, conv_b, dt_bias, a_log, d_skip, g_ssm_norm, w_br_att, w_br_ssm, w_mix_out, g_post_mix, g_pre_xa, g_mem, w_xq, w_xkv, w_xo, g_post_xa, g_pre_ffn, w_gu, w_down, g_post_ffn, loss_target, m_g_pre_mix, m_w_in, m_conv_w, m_conv_b, m_dt_bias, m_a_log, m_d_skip, m_g_ssm_norm, m_w_br_att, m_w_br_ssm, m_w_mix_out, m_g_post_mix, m_g_pre_xa, m_g_mem, m_w_xq, m_w_xkv, m_w_xo, m_g_post_xa, m_g_pre_ffn, m_w_gu, m_w_down, m_g_post_ffn, v_g_pre_mix, v_w_in, v_conv_w, v_conv_b, v_dt_bias, v_a_log, v_d_skip, v_g_ssm_norm, v_w_br_att, v_w_br_ssm, v_w_mix_out, v_g_post_mix, v_g_pre_xa, v_g_mem, v_w_xq, v_w_xkv, v_w_xo, v_g_post_xa, v_g_pre_ffn, v_w_gu, v_w_down, v_g_post_ffn):
    vals = dict(locals())
    cfg = _Cfg()
    nb = x.shape[0]
    t = nb * SEQ
    d = cfg.d
    depth = g_pre_mix.shape[0]

    def wire(l):
        return [vals[n][l].astype(WIRE_DTYPE) for n in _BIG] + [conv_w[l]]

    def layer_weights(gathered):
        w = {n: _unshard(gw, n in _COL_SHARDED) for n, gw in zip(_BIG + ("conv_w",), gathered)}
        w["w_in"] = _permute_in(w["w_in"], cfg)
        w["conv_w"] = w["conv_w"][None]
        return w

    p = {n: _vec3(vals[n], LANES if n in ("dt_bias", "a_log", "d_skip") else None) for n in _SMALL}
    weights = [None] * depth
    weights[0] = layer_weights(_all_gather(wire(0), "ag_weights_first"))

    xf = x.reshape(t, d)
    memf = mem.reshape(nb * MEM_LEN, d)
    tgt = loss_target.reshape(t, d)
    h = _rowwise(_rms, "pre_norm_first", t, [_full(xf)], [(p["g_pre_mix"], 0)], [(d, ACT_DTYPE)])[0]
    saved = []
    xcur = xf
    for l in range(depth):
        last = l == depth - 1
        s, nxt, gathered = _forward_layer(l, xcur, h, memf, tgt, weights[l], p, cfg, nb, last,
                                          () if last else wire(l + 1))
        saved.append(s)
        if not last:
            weights[l + 1] = layer_weights(gathered)
            xcur, h = nxt
    dx, loss_row = nxt
    loss = lax.psum(0.5 * jnp.sum(loss_row) / d, AXES)

    def wire_grads(g):
        g = dict(g, w_in=_unpermute_in(g["w_in"], cfg))
        return [_shard(g[n], n in _COL_SHARDED).astype(WIRE_DTYPE) for n in _BIG]

    top = saved[-1]
    d_dn, g_post_top = _rowwise(lambda ysub, dxo, gp: _rms_bwd(ysub, gp, dxo), "post_bwd_last", t,
                                [_full(top["dn"]), _full(dx)], [(p["g_post_ffn"], depth - 1)], [(d, ACT_DTYPE)],
                                acc_out=[(1, d)])
    grads = [None] * depth
    landed = [None] * depth
    post_ffn = [None] * depth
    post_ffn[depth - 1] = g_post_top
    pending = ()
    for l in reversed(range(depth)):
        prev_dn = saved[l - 1]["dn"] if l > 0 else None
        grads[l], dx, d_dn, g_post_prev, got = _backward_layer(l, saved[l], dx, d_dn, memf, weights[l], p, cfg, nb,
                                                               prev_dn, pending)
        if l < depth - 1:
            landed[l + 1] = got
        pending = wire_grads(grads[l])
        if l > 0:
            post_ffn[l - 1] = g_post_prev
    landed[0] = _scatter_blocks(pending, "scatter_grads_last")
    for l in range(depth):
        grads[l]["g_post_ffn"] = post_ffn[l]
    grad_x = dx.reshape(x.shape)
    stacked = {n: jnp.stack([grads[l][n] for l in range(depth)]) for n in _SMALL + ("conv_w",)}

    out = {}
    for i, n in enumerate(_BIG):
        per_layer = [_adamw(landed[l][i], vals[n], vals["m_" + n], vals["v_" + n], "adamw_" + n, pre=(l,))
                     for l in range(depth)]
        out[n] = [jnp.stack([per_layer[l][j] for l in range(depth)]) for j in range(4)]

    small_shapes = [vals[n].shape for n in _SMALL]
    pack_g = _pack([stacked[n] for n in _SMALL])
    conv_g = stacked["conv_w"].reshape(depth * SSM_CONV, cfg.conv_dim)
    parts_small, parts_conv = _all_gather([pack_g, conv_g], "ag_small_grads")
    res = _adamw(parts_small, _pack([vals[n] for n in _SMALL]), _pack([vals["m_" + n] for n in _SMALL]),
                 _pack([vals["v_" + n] for n in _SMALL]), "adamw_small")
    unpacked = [_unpack(r, small_shapes) for r in res]
    for i, n in enumerate(_SMALL):
        out[n] = [unpacked[j][i] for j in range(4)]
    cs = conv_w.shape[2]
    me = _flat_index(lax.axis_index("x"), lax.axis_index("y"), lax.axis_index("c"))
    parts_conv = lax.dynamic_slice_in_dim(parts_conv, me * cs, cs, axis=2)
    flat = lambda a: a.reshape(depth * SSM_CONV, cs)
    res = _adamw(parts_conv, flat(conv_w), flat(m_conv_w), flat(v_conv_w), "adamw_conv_w")
    out["conv_w"] = [r.reshape(conv_w.shape) for r in res]

    return (loss, grad_x, *[out[n][0] for n in _WEIGHTS], *[out[n][1] for n in _WEIGHTS],
            *[out[n][2] for n in _WEIGHTS], *[out[n][3] for n in _WEIGHTS])
```

```python
import functools
import math

import jax
import jax.numpy as jnp
from jax import lax
from jax.experimental import pallas as pl
from jax.experimental.pallas import tpu as pltpu

F32 = jnp.float32
BF16 = jnp.bfloat16
MXU_DTYPE = BF16
ACT_DTYPE = BF16
WIRE_DTYPE = BF16

D_MODEL = 1024
SEQ = 2048
DEPTH = 4
MEM_LEN = 256
RMS_EPS = 1e-6
SB_HEADS = 16
SB_HEAD_DIM = 64
SB_BLOCK = 128
SSM_INNER = 2 * D_MODEL
SSM_HEAD_DIM = 64
SSM_GROUPS = 4
SSM_STATE = 128
SSM_CONV = 4
SSM_CHUNK = 128
XA_HEADS = 4
FFN_HIDDEN = ((8 * D_MODEL + 767) // 768) * 256
ADAM_LR = 0.001
ADAM_B1 = 0.9
ADAM_B2 = 0.999
ADAM_EPS = 1e-08
ADAM_WD = 0.01
ADAM_STEP = 10

N_DEV = 8
LANES = 128
VMEM_LIMIT_BYTES = 56 * 1024 * 1024

AXES = ("x", "y", "c")
MESH = pl.DeviceIdType.MESH


class _Cfg:
    def __init__(self):
        self.d = D_MODEL
        self.sbw = SB_HEADS * SB_HEAD_DIM
        self.inner = SSM_INNER
        self.heads = SSM_INNER // SSM_HEAD_DIM
        self.epg = self.heads // SSM_GROUPS
        self.gn = SSM_GROUPS * SSM_STATE
        self.conv_dim = SSM_INNER + 2 * self.gn
        self.ffn = FFN_HIDDEN
        self.xa_dim = D_MODEL // XA_HEADS
        self.in_sizes = (self.sbw, self.sbw, self.sbw, self.inner, self.conv_dim, self.heads, self.d, self.d)
        self.in_width = sum(self.in_sizes)
        self.z0 = 0
        self.ga0 = self.inner
        self.gs0 = self.ga0 + self.d
        self.q0 = self.gs0 + self.d
        self.k0 = self.q0 + self.sbw
        self.v0 = self.k0 + self.sbw
        self.xbc0 = self.v0 + self.sbw
        self.dt0 = self.xbc0 + self.conv_dim
        self.proj_w = self.dt0 + LANES
        assert self.heads <= LANES


def _cparams(sem=None):
    return pltpu.CompilerParams(dimension_semantics=sem, vmem_limit_bytes=VMEM_LIMIT_BYTES)


def _tile(n, pref, mult):
    if n <= pref:
        return n
    t = (pref // mult) * mult
    while t >= mult:
        if n % t == 0:
            return t
        t -= mult
    return n


def _mm(a, b, *, ta=False, tb=False, out_dtype=F32, name, a_pre=(), b_pre=(), tm=512, tn=1152, tk=2048):
    ash = a.shape[len(a_pre):]
    bsh = b.shape[len(b_pre):]
    kk, m = (ash if ta else ash[::-1])
    if tb:
        n, k2 = bsh
    else:
        k2, n = bsh
    assert kk == k2, (name, a.shape, b.shape)
    tm = _tile(m, tm, LANES if ta else 16)
    tn = _tile(n, tn, LANES)
    tk = _tile(kk, tk, LANES if (tb or not ta) else 16)
    nk = kk // tk
    dims = (((0 if ta else 1,), (1 if tb else 0,)), ((), ()))
    npa, npb = len(a_pre), len(b_pre)

    def body(a_ref, b_ref, o_ref, *scratch):
        av = a_ref[...].astype(MXU_DTYPE)
        bv = b_ref[...].astype(MXU_DTYPE)
        part = lax.dot_general(av, bv, dims, preferred_element_type=F32)
        if nk == 1:
            o_ref[...] = part.astype(out_dtype)
        else:
            acc_ref, = scratch
            k = pl.program_id(2)

            @pl.when(k == 0)
            def _():
                acc_ref[...] = part

            @pl.when(k > 0)
            def _():
                acc_ref[...] += part

            @pl.when(k == nk - 1)
            def _():
                o_ref[...] = acc_ref[...].astype(out_dtype)

    if ta:
        a_spec = pl.BlockSpec((None,) * npa + (tk, tm), lambda i, j, k: a_pre + (k, i))
    else:
        a_spec = pl.BlockSpec((None,) * npa + (tm, tk), lambda i, j, k: a_pre + (i, k))
    if tb:
        b_spec = pl.BlockSpec((None,) * npb + (tn, tk), lambda i, j, k: b_pre + (j, k))
    else:
        b_spec = pl.BlockSpec((None,) * npb + (tk, tn), lambda i, j, k: b_pre + (k, j))
    return pl.pallas_call(
        body,
        out_shape=jax.ShapeDtypeStruct((m, n), out_dtype),
        grid=(m // tm, n // tn, nk),
        in_specs=[a_spec, b_spec],
        out_specs=pl.BlockSpec((tm, tn), lambda i, j, k: (i, j)),
        scratch_shapes=[] if nk == 1 else [pltpu.VMEM((tm, tn), F32)],
        compiler_params=_cparams(("parallel", "parallel", "arbitrary")),
        name=name,
    )(a, b)


def _rowwise(fn, name, rows, row_in, vec_in, row_out, acc_out=(), tr=256):
    tr = _tile(rows, tr, 16)
    n_in = len(row_in) + len(vec_in)
    n_ro = len(row_out)

    def body(*refs):
        ins = [r[...].astype(F32) for r in refs[:n_in]]
        outs = fn(*ins)
        if not isinstance(outs, (tuple, list)):
            outs = (outs,)
        out_refs = refs[n_in:]
        for o_ref, val in zip(out_refs[:n_ro], outs[:n_ro]):
            o_ref[...] = val.astype(o_ref.dtype)
        if acc_out:
            i = pl.program_id(0)
            for o_ref, val in zip(out_refs[n_ro:], outs[n_ro:]):
                @pl.when(i == 0)
                def _(o_ref=o_ref, val=val):
                    o_ref[...] = val

                @pl.when(i > 0)
                def _(o_ref=o_ref, val=val):
                    o_ref[...] += val

    in_specs = [pl.BlockSpec((tr, w), functools.partial(lambda i, cb: (i, cb), cb=cb)) for (_, w, cb) in row_in]
    in_specs += [pl.BlockSpec((None,) + v.shape[1:], functools.partial(lambda i, l: (l, 0, 0), l=l)) for (v, l) in vec_in]
    out_shape = [jax.ShapeDtypeStruct((rows, w), dt) for (w, dt) in row_out]
    out_shape += [jax.ShapeDtypeStruct(s, F32) for s in acc_out]
    out_specs = [pl.BlockSpec((tr, w), lambda i: (i, 0)) for (w, _) in row_out]
    out_specs += [pl.BlockSpec(s, lambda i: (0, 0)) for s in acc_out]
    res = pl.pallas_call(
        body,
        out_shape=out_shape,
        grid=(rows // tr,),
        in_specs=in_specs,
        out_specs=out_specs,
        compiler_params=_cparams(("arbitrary",) if acc_out else ("parallel",)),
        name=name,
    )(*[a for (a, _, _) in row_in], *[v for (v, _) in vec_in])
    return res


def _rms(x, g):
    r = lax.rsqrt(jnp.mean(x * x, axis=-1, keepdims=True) + RMS_EPS)
    return x * r * g


def _rms_bwd(x, g, dy):
    r = lax.rsqrt(jnp.mean(x * x, axis=-1, keepdims=True) + RMS_EPS)
    xh = x * r
    dxh = dy * g
    dx = r * (dxh - xh * jnp.mean(dxh * xh, axis=-1, keepdims=True))
    return dx, jnp.sum(dy * xh, axis=0, keepdims=True)


def _silu(x):
    return x * jax.nn.sigmoid(x)


def _silu_grad(x):
    s = jax.nn.sigmoid(x)
    return s * (1.0 + x * (1.0 - s))


def _softplus(x):
    return jnp.maximum(x, 0.0) + jnp.log1p(jnp.exp(-jnp.abs(x)))


def _full(a):
    return (a, a.shape[1], 0)


def _f_post_pre(x, ysub, g_post, g_pre):
    xn = x + _rms(ysub, g_post)
    return xn, _rms(xn, g_pre)


def _f_final(x, ysub, tgt, g_post):
    err = x + _rms(ysub, g_post) - tgt
    return err * (1.0 / D_MODEL), jnp.sum(err * err, axis=0, keepdims=True)


def _f_pre_post_bwd(xmid, dh, dxo, ysub, g_pre, g_post):
    d1, dg_pre = _rms_bwd(xmid, g_pre, dh)
    dxm = dxo + d1
    dys, dg_post = _rms_bwd(ysub, g_post, dxm)
    return dxm, dys, dg_pre, dg_post


def _f_pre_bwd(x, dh, dxo, g_pre):
    d1, dg_pre = _rms_bwd(x, g_pre, dh)
    return dxo + d1, dg_pre


def _f_gain_bwd(x, dy, g):
    return _rms_bwd(x, g, dy)[1]


def _group_norm_parts(u):
    gw = u.shape[1] // SSM_GROUPS
    parts = []
    for gi in range(SSM_GROUPS):
        ug = u[:, gi * gw:(gi + 1) * gw]
        r = lax.rsqrt(jnp.mean(ug * ug, axis=-1, keepdims=True) + RMS_EPS)
        parts.append((ug * r, r))
    return gw, parts


def _f_gate_norm(y, z, g):
    _, parts = _group_norm_parts(y * _silu(z))
    return jnp.concatenate([uh for uh, _ in parts], axis=1) * g


def _f_gate_norm_bwd(y, z, do, g):
    sz = _silu(z)
    gw, parts = _group_norm_parts(y * sz)
    dxh = do * g
    du = []
    for gi, (uh, r) in enumerate(parts):
        dg_ = dxh[:, gi * gw:(gi + 1) * gw]
        du.append(r * (dg_ - uh * jnp.mean(dg_ * uh, axis=-1, keepdims=True)))
    du = jnp.concatenate(du, axis=1)
    uh_all = jnp.concatenate([uh for uh, _ in parts], axis=1)
    return du * sz, du * y * _silu_grad(z), jnp.sum(do * uh_all, axis=0, keepdims=True)


def _f_merge(ga, gs, ba, bs):
    return jax.nn.sigmoid(ga) * ba + jax.nn.sigmoid(gs) * bs


def _f_merge_bwd(ga, gs, ba, bs, dm):
    sa, ss = jax.nn.sigmoid(ga), jax.nn.sigmoid(gs)
    dgg = jnp.concatenate([dm * ba * sa * (1.0 - sa), dm * bs * ss * (1.0 - ss)], axis=1)
    return dgg, dm * sa, dm * ss


def _f_swiglu(gate, up):
    return _silu(gate) * up


def _f_swiglu_bwd(gate, up, da):
    return jnp.concatenate([da * up * _silu_grad(gate), da * _silu(gate)], axis=1)


def _split_dot(x, u):
    hi = x.astype(BF16)
    lo = (x - hi.astype(F32)).astype(BF16)
    return jnp.dot(hi, u, preferred_element_type=F32) + jnp.dot(lo, u, preferred_element_type=F32)


SB_ROWS = 512
SB_UNROLL = 4
C00 = (((0,), (0,)), ((), ()))
C11 = (((1,), (1,)), ((), ()))


def _sb_setup(q_ref, tq):
    hp = LANES // SB_HEAD_DIM
    lane = lax.broadcasted_iota(jnp.int32, (1, LANES), 1)
    heads = [jnp.logical_and(lane >= h * SB_HEAD_DIM, lane < (h + 1) * SB_HEAD_DIM) for h in range(hp)]
    qs = q_ref[...] * (SB_HEAD_DIM ** -0.5)
    q_h = [jnp.where(hm, qs, 0.0).astype(MXU_DTYPE) for hm in heads]
    row = lax.broadcasted_iota(jnp.int32, (tq, SB_BLOCK), 0)
    col = lax.broadcasted_iota(jnp.int32, (tq, SB_BLOCK), 1)
    sq_row = lax.broadcasted_iota(jnp.int32, (SB_BLOCK, SB_BLOCK), 0)
    sq_col = lax.broadcasted_iota(jnp.int32, (SB_BLOCK, SB_BLOCK), 1)
    return heads, q_h, col - row, sq_row, sq_col


def _sb_scores(q, kj, mask):
    z = lax.dot_general(q, kj, C11, preferred_element_type=F32)
    lm = -(jnp.maximum(z, 0.0) + jnp.log(1.0 + jnp.exp(-jnp.abs(z))))
    return z, jnp.where(mask, lm, 0.0)


def _grid_step3(nb, ncb, nq):
    return (pl.program_id(0) * ncb + pl.program_id(1)) * nq + pl.program_id(2), nb * ncb * nq


def _sb_fwd(proj, cfg, nb, gather=()):
    blk = SB_BLOCK
    tq = _tile(SEQ, SB_ROWS, blk)
    nq = SEQ // tq
    kpq = tq // blk
    unr = math.gcd(kpq, SB_UNROLL)
    hp = LANES // SB_HEAD_DIM
    ncb = cfg.sbw // LANES
    qb, kb, vb = cfg.q0 // LANES, cfg.k0 // LANES, cfg.v0 // LANES
    ng = len(gather)

    def body(q_ref, k_ref, v_ref, *rest):
        o_ref = rest[ng]
        if ng:
            step_id, n_steps = _grid_step3(nb, ncb, nq)
            start, forward, finish = _gather_phases(rest[:ng], rest[ng + 1:2 * ng + 1], *rest[2 * ng + 1:])
            pl.when(step_id == 0)(start)
            pl.when(step_id == n_steps // 2)(forward)
        _sb_fwd_block(q_ref, k_ref, v_ref, o_ref)
        if ng:
            pl.when(step_id == n_steps - 1)(finish)

    def _sb_fwd_block(q_ref, k_ref, v_ref, o_ref):
        i = pl.program_id(2)
        heads, q_h, cmr, sq_row, sq_col = _sb_setup(q_ref, tq)
        u_rev = (sq_row >= sq_col).astype(BF16)

        def step(n, carry):
            acc, runs = carry
            runs = list(runs)
            for jj in range(unr):
                j = (i + 1) * kpq - 1 - (n * unr + jj)
                rows = pl.ds(pl.multiple_of(j * blk, blk), blk)
                kj = k_ref[rows, :].astype(MXU_DTYPE)
                vj = v_ref[rows, :].astype(MXU_DTYPE)
                mask = cmr < i * tq - j * blk
                for h in range(hp):
                    z, lm = _sb_scores(q_h[h], kj, mask)
                    cs = _split_dot(lm, u_rev)
                    w = jnp.where(mask, jnp.exp(z + cs + runs[h]), 0.0)
                    acc = acc + jnp.dot(w.astype(MXU_DTYPE), jnp.where(heads[h], vj, 0), preferred_element_type=F32)
                    runs[h] = runs[h] + cs[:, 0:1]
            return acc, tuple(runs)

        init = (jnp.zeros((tq, LANES), F32), tuple(jnp.zeros((tq, 1), F32) for _ in range(hp)))
        acc, _ = lax.fori_loop(0, (i + 1) * (kpq // unr), step, init)
        o_ref[...] = acc

    res = pl.pallas_call(
        body,
        out_shape=[jax.ShapeDtypeStruct((nb * SEQ, cfg.sbw), F32)] + _gather_shapes(gather),
        grid=(nb, ncb, nq),
        in_specs=[
            pl.BlockSpec((tq, LANES), lambda b, c, i: (b * nq + i, qb + c)),
            pl.BlockSpec((SEQ, LANES), lambda b, c, i: (b, kb + c)),
            pl.BlockSpec((SEQ, LANES), lambda b, c, i: (b, vb + c)),
        ] + [ANY_SPEC] * ng,
        out_specs=[pl.BlockSpec((tq, LANES), lambda b, c, i: (b * nq + i, c))] + [ANY_SPEC] * ng,
        scratch_shapes=_comm_sems(ng) if ng else [],
        compiler_params=_cparams(("arbitrary",) * 3 if ng else ("parallel", "parallel", "arbitrary")),
        name="sb_fwd_gather" if ng else "sb_fwd",
    )(proj, proj, proj, *gather)
    return res[0], res[1:]


def _sb_bwd(proj, do_att, cfg, nb, scatter=()):
    blk = SB_BLOCK
    tq = _tile(SEQ, SB_ROWS, blk)
    nq = SEQ // tq
    kpq = tq // blk
    unr = math.gcd(kpq, SB_UNROLL)
    hp = LANES // SB_HEAD_DIM
    ncb = cfg.sbw // LANES
    scale = SB_HEAD_DIM ** -0.5
    qb, kb, vb = cfg.q0 // LANES, cfg.k0 // LANES, cfg.v0 // LANES
    ns = len(scatter)

    def body(q_ref, k_ref, v_ref, do_ref, *rest):
        dq_ref, dk_ref, dv_ref = rest[ns:ns + 3]
        g_ref, z_ref = rest[2 * ns + 3:2 * ns + 5]
        if ns:
            step_id, n_steps = _grid_step3(nb, ncb, nq)
            start, finish = _scatter_phases(rest[:ns], rest[ns + 3:2 * ns + 3], *rest[2 * ns + 5:])
            pl.when(step_id == 0)(start)
        _sb_bwd_block(q_ref, k_ref, v_ref, do_ref, dq_ref, dk_ref, dv_ref, g_ref, z_ref)
        if ns:
            pl.when(step_id == n_steps - 1)(finish)

    def _sb_bwd_block(q_ref, k_ref, v_ref, do_ref, dq_ref, dk_ref, dv_ref, g_ref, z_ref):
        i = pl.program_id(2)

        @pl.when(i == 0)
        def _():
            dk_ref[...] = jnp.zeros_like(dk_ref)
            dv_ref[...] = jnp.zeros_like(dv_ref)

        heads, q_h, cmr, sq_row, sq_col = _sb_setup(q_ref, tq)
        u_rev = (sq_row >= sq_col).astype(BF16)
        u_fwd = (sq_row <= sq_col).astype(BF16)
        do = do_ref[...]
        do_h = [jnp.where(hm, do, 0.0).astype(MXU_DTYPE) for hm in heads]

        def sweep_left(n, runs):
            runs = list(runs)
            for jj in range(unr):
                j = (i + 1) * kpq - 1 - (n * unr + jj)
                rows = pl.ds(pl.multiple_of(j * blk, blk), blk)
                kj = k_ref[rows, :].astype(MXU_DTYPE)
                vj = v_ref[rows, :].astype(MXU_DTYPE)
                mask = cmr < i * tq - j * blk
                dv = jnp.zeros((blk, LANES), F32)
                for h in range(hp):
                    z, lm = _sb_scores(q_h[h], kj, mask)
                    cs = _split_dot(lm, u_rev)
                    a = jnp.where(mask, jnp.exp(z + cs + runs[h]), 0.0)
                    da = lax.dot_general(do_h[h], vj, C11, preferred_element_type=F32)
                    dv = dv + lax.dot_general(a.astype(MXU_DTYPE), do_h[h], C00, preferred_element_type=F32)
                    g_ref[h, j] = a * da
                    z_ref[h, j] = jax.nn.sigmoid(z)
                    runs[h] = runs[h] + cs[:, 0:1]
                dv_ref[rows, :] += dv
            return tuple(runs)

        trips = (i + 1) * (kpq // unr)
        lax.fori_loop(0, trips, sweep_left, tuple(jnp.zeros((tq, 1), F32) for _ in range(hp)))

        def sweep_right(n, carry):
            dq, runs = carry
            runs = list(runs)
            for jj in range(unr):
                j = n * unr + jj
                rows = pl.ds(pl.multiple_of(j * blk, blk), blk)
                kj = k_ref[rows, :].astype(MXU_DTYPE)
                mask = cmr < i * tq - j * blk
                dk = jnp.zeros((blk, LANES), F32)
                for h in range(hp):
                    g = g_ref[h, j]
                    g_upto = _split_dot(g, u_fwd) + runs[h]
                    dz = jnp.where(mask, g - z_ref[h, j] * g_upto, 0.0).astype(MXU_DTYPE)
                    dq = dq + jnp.dot(dz, jnp.where(heads[h], kj, 0), preferred_element_type=F32)
                    dk = dk + lax.dot_general(dz, q_h[h], C00, preferred_element_type=F32)
                    runs[h] = runs[h] + jnp.sum(g, axis=1, keepdims=True)
                dk_ref[rows, :] += dk
            return dq, tuple(runs)

        init = (jnp.zeros((tq, LANES), F32), tuple(jnp.zeros((tq, 1), F32) for _ in range(hp)))
        dq, _ = lax.fori_loop(0, trips, sweep_right, init)
        dq_ref[...] = (dq * scale).astype(dq_ref.dtype)

    kv_spec_out = pl.BlockSpec((SEQ, LANES), lambda b, c, i: (b, c))
    q_spec_out = pl.BlockSpec((tq, LANES), lambda b, c, i: (b * nq + i, c))
    res = pl.pallas_call(
        body,
        out_shape=[
            jax.ShapeDtypeStruct((nb * SEQ, cfg.sbw), ACT_DTYPE),
            jax.ShapeDtypeStruct((nb * SEQ, cfg.sbw), F32),
            jax.ShapeDtypeStruct((nb * SEQ, cfg.sbw), F32),
        ] + [jax.ShapeDtypeStruct(a.shape, a.dtype) for a in scatter],
        grid=(nb, ncb, nq),
        in_specs=[
            pl.BlockSpec((tq, LANES), lambda b, c, i: (b * nq + i, qb + c)),
            pl.BlockSpec((SEQ, LANES), lambda b, c, i: (b, kb + c)),
            pl.BlockSpec((SEQ, LANES), lambda b, c, i: (b, vb + c)),
            q_spec_out,
        ] + [ANY_SPEC] * ns,
        out_specs=[q_spec_out, kv_spec_out, kv_spec_out] + [ANY_SPEC] * ns,
        scratch_shapes=[pltpu.VMEM((hp, SEQ // blk, tq, blk), F32), pltpu.VMEM((hp, SEQ // blk, tq, blk), F32)]
        + (_comm_sems(ns) if ns else []),
        compiler_params=_cparams(("arbitrary",) * 3 if ns else ("parallel", "parallel", "arbitrary")),
        name="sb_bwd_scatter" if ns else "sb_bwd",
    )(proj, proj, proj, do_att, *scatter)
    return res[0], res[1], res[2], res[3:]


CONV_COLS = 256


def _conv_pre(x, w, b, t):
    kw = SSM_CONV
    shifted = []
    pre = b + w[kw - 1:kw, :] * x
    for k in range(kw - 1):
        d = kw - 1 - k
        xs = jnp.where(t >= d, pltpu.roll(x, d, 0), 0.0)
        shifted.append(xs)
        pre = pre + w[k:k + 1, :] * xs
    shifted.append(x)
    return pre, shifted


def _conv_fwd(proj, conv_w, conv_b, l, cfg, nb):
    cw = CONV_COLS
    ncb = cfg.conv_dim // cw
    xb = cfg.xbc0 // cw

    def body(x_ref, w_ref, b_ref, o_ref):
        x = x_ref[...]
        t = lax.broadcasted_iota(jnp.int32, x.shape, 0)
        pre, _ = _conv_pre(x, w_ref[...], b_ref[...], t)
        o_ref[...] = _silu(pre)

    return pl.pallas_call(
        body,
        out_shape=jax.ShapeDtypeStruct((nb * SEQ, cfg.conv_dim), F32),
        grid=(ncb, nb),
        in_specs=[
            pl.BlockSpec((SEQ, cw), lambda j, b: (b, xb + j)),
            pl.BlockSpec((None, SSM_CONV, cw), lambda j, b: (0, 0, j)),
            pl.BlockSpec((None, 1, cw), lambda j, b: (l, 0, j)),
        ],
        out_specs=pl.BlockSpec((SEQ, cw), lambda j, b: (b, j)),
        compiler_params=_cparams(("parallel", "parallel")),
        name="conv_fwd",
    )(proj, conv_w, conv_b)


def _conv_bwd(proj, dact, conv_w, conv_b, l, cfg, nb):
    cw = CONV_COLS
    ncb = cfg.conv_dim // cw
    xb = cfg.xbc0 // cw
    kw = SSM_CONV

    def body(x_ref, da_ref, w_ref, b_ref, dx_ref, dw_ref, db_ref):
        b_id = pl.program_id(1)
        x = x_ref[...]
        w = w_ref[...]
        t = lax.broadcasted_iota(jnp.int32, x.shape, 0)
        pre, shifted = _conv_pre(x, w, b_ref[...], t)
        dpre = da_ref[...] * _silu_grad(pre)
        dx = w[kw - 1:kw, :] * dpre
        for k in range(kw - 1):
            d = kw - 1 - k
            dx = dx + w[k:k + 1, :] * jnp.where(t < SEQ - d, pltpu.roll(dpre, SEQ - d, 0), 0.0)
        dx_ref[...] = dx.astype(dx_ref.dtype)
        dw = jnp.concatenate([jnp.sum(dpre * s, axis=0, keepdims=True) for s in shifted], axis=0)
        db = jnp.sum(dpre, axis=0, keepdims=True)

        @pl.when(b_id == 0)
        def _():
            dw_ref[...] = dw
            db_ref[...] = db

        @pl.when(b_id > 0)
        def _():
            dw_ref[...] += dw
            db_ref[...] += db

    return pl.pallas_call(
        body,
        out_shape=[
            jax.ShapeDtypeStruct((nb * SEQ, cfg.conv_dim), ACT_DTYPE),
            jax.ShapeDtypeStruct((kw, cfg.conv_dim), F32),
            jax.ShapeDtypeStruct((1, cfg.conv_dim), F32),
        ],
        grid=(ncb, nb),
        in_specs=[
            pl.BlockSpec((SEQ, cw), lambda j, b: (b, xb + j)),
            pl.BlockSpec((SEQ, cw), lambda j, b: (b, j)),
            pl.BlockSpec((None, kw, cw), lambda j, b: (0, 0, j)),
            pl.BlockSpec((None, 1, cw), lambda j, b: (l, 0, j)),
        ],
        out_specs=[
            pl.BlockSpec((SEQ, cw), lambda j, b: (b, j)),
            pl.BlockSpec((kw, cw), lambda j, b: (0, j)),
            pl.BlockSpec((1, cw), lambda j, b: (0, j)),
        ],
        compiler_params=_cparams(("parallel", "arbitrary")),
        name="conv_bwd",
    )(proj, dact, conv_w, conv_b)


def _ssd_common(dt_raw, dt_bias, a_log, tri):
    ln = SSM_CHUNK
    dt = _softplus(dt_raw + dt_bias)
    a = -jnp.exp(a_log)
    a_cs = jnp.dot(tri, dt * a, preferred_element_type=F32, precision=lax.Precision.HIGHEST)
    a_last = a_cs[ln - 1:ln, :]
    return dt, a, a_cs, a_cs.T, jnp.exp(a_cs), jnp.exp(a_last - a_cs), jnp.exp(a_last)


def _ssd_head_v1(h, xs, dt, a_cs, a_t, cb, tril):
    p = SSM_HEAD_DIM
    x_h = xs[:, h * p:(h + 1) * p]
    xd = x_h * dt[:, h:h + 1]
    lmat = jnp.exp(jnp.where(tril, a_cs[:, h:h + 1] - a_t[h:h + 1, :], -jnp.inf))
    return x_h, xd, lmat


def _ssd_specs(cfg, nc, rev):
    ln = SSM_CHUNK
    cidx = (lambda c: nc - 1 - c) if rev else (lambda c: c)
    bmb = cfg.inner // cfg.gn
    return [
        pl.BlockSpec((ln, cfg.inner), lambda b, c: (b * nc + cidx(c), 0)),
        pl.BlockSpec((ln, cfg.gn), lambda b, c: (b * nc + cidx(c), bmb)),
        pl.BlockSpec((ln, cfg.gn), lambda b, c: (b * nc + cidx(c), bmb + 1)),
        pl.BlockSpec((ln, LANES), lambda b, c: (b * nc + cidx(c), cfg.dt0 // LANES)),
    ]


def _ssd_fwd_v1(xbc, proj, dt_bias, a_log, d_skip, l, cfg, nb):
    ln, p, n = SSM_CHUNK, SSM_HEAD_DIM, SSM_STATE
    nc = SEQ // ln
    g_, e_ = SSM_GROUPS, cfg.epg
    assert cfg.inner % cfg.gn == 0

    def body(xs_ref, bm_ref, cm_ref, dtr_ref, bias_ref, alog_ref, dsk_ref, y_ref, st_ref, s_ref):
        c = pl.program_id(1)

        @pl.when(c == 0)
        def _():
            s_ref[...] = jnp.zeros_like(s_ref)

        st_ref[...] = s_ref[...]
        row = lax.broadcasted_iota(jnp.int32, (ln, ln), 0)
        col = lax.broadcasted_iota(jnp.int32, (ln, ln), 1)
        tril = row >= col
        dt, _, a_cs, a_t, e_a, dte, cd = _ssd_common(dtr_ref[...], bias_ref[...], alog_ref[...], tril.astype(F32))
        dsk = dsk_ref[...]
        xs = xs_ref[...]
        for g in range(g_):
            bm = bm_ref[:, g * n:(g + 1) * n].astype(MXU_DTYPE)
            cm = cm_ref[:, g * n:(g + 1) * n].astype(MXU_DTYPE)
            cb = lax.dot_general(cm, bm, (((1,), (1,)), ((), ())), preferred_element_type=F32)
            for e in range(e_):
                h = g * e_ + e
                x_h, xd, lmat = _ssd_head_v1(h, xs, dt, a_cs, a_t, cb, tril)
                s_prev = s_ref[g * n:(g + 1) * n, e * p:(e + 1) * p]
                y = jnp.dot((cb * lmat).astype(MXU_DTYPE), xd.astype(MXU_DTYPE), preferred_element_type=F32)
                y = y + jnp.dot(cm, s_prev.astype(MXU_DTYPE), preferred_element_type=F32) * e_a[:, h:h + 1]
                y_ref[:, h * p:(h + 1) * p] = y + dsk[:, h:h + 1] * x_h
                upd = lax.dot_general(bm, (xd * dte[:, h:h + 1]).astype(MXU_DTYPE), (((0,), (0,)), ((), ())),
                                      preferred_element_type=F32)
                s_ref[g * n:(g + 1) * n, e * p:(e + 1) * p] = cd[:, h:h + 1] * s_prev + upd

    vec = lambda b, c: (l, 0, 0)
    return pl.pallas_call(
        body,
        out_shape=[
            jax.ShapeDtypeStruct((nb * SEQ, cfg.inner), F32),
            jax.ShapeDtypeStruct((nb * nc * g_ * n, e_ * p), F32),
        ],
        grid=(nb, nc),
        in_specs=_ssd_specs(cfg, nc, False) + [pl.BlockSpec((None, 1, LANES), vec)] * 3,
        out_specs=[
            pl.BlockSpec((ln, cfg.inner), lambda b, c: (b * nc + c, 0)),
            pl.BlockSpec((g_ * n, e_ * p), lambda b, c: (b * nc + c, 0)),
        ],
        scratch_shapes=[pltpu.VMEM((g_ * n, e_ * p), F32)],
        compiler_params=_cparams(("parallel", "arbitrary")),
        name="ssd_fwd",
    )(xbc, xbc, xbc, proj, dt_bias, a_log, d_skip)


def _ssd_bwd_v1(xbc, proj, states, dy, dt_bias, a_log, d_skip, l, cfg, nb):
    ln, p, n = SSM_CHUNK, SSM_HEAD_DIM, SSM_STATE
    nc = SEQ // ln
    g_, e_ = SSM_GROUPS, cfg.epg
    c00 = (((0,), (0,)), ((), ()))
    c11 = (((1,), (1,)), ((), ()))

    def body(xs_ref, bm_ref, cm_ref, dtr_ref, st_ref, dy_ref, bias_ref, alog_ref, dsk_ref,
             dxbc_ref, ddt_ref, dvec_ref, ds_ref):
        first = jnp.logical_and(pl.program_id(0) == 0, pl.program_id(1) == 0)

        @pl.when(pl.program_id(1) == 0)
        def _():
            ds_ref[...] = jnp.zeros_like(ds_ref)

        row = lax.broadcasted_iota(jnp.int32, (ln, ln), 0)
        col = lax.broadcasted_iota(jnp.int32, (ln, ln), 1)
        tril = row >= col
        tri_f = tril.astype(F32)
        dtr = dtr_ref[...]
        bias = bias_ref[...]
        dt, a, a_cs, a_t, e_a, dte, cd = _ssd_common(dtr, bias, alog_ref[...], tri_f)
        dsk = dsk_ref[...]
        xs = xs_ref[...]
        lane = lax.broadcasted_iota(jnp.int32, (1, LANES), 1)
        sub = lax.broadcasted_iota(jnp.int32, (LANES, 1), 0)
        da_col = jnp.zeros((ln, LANES), F32)
        da_row_t = jnp.zeros((LANES, ln), F32)
        ddt = jnp.zeros((ln, LANES), F32)
        da_last = jnp.zeros((1, LANES), F32)
        ddsk = jnp.zeros((1, LANES), F32)
        for g in range(g_):
            bm = bm_ref[:, g * n:(g + 1) * n].astype(MXU_DTYPE)
            cm = cm_ref[:, g * n:(g + 1) * n].astype(MXU_DTYPE)
            cb = lax.dot_general(cm, bm, c11, preferred_element_type=F32)
            dbm = jnp.zeros((ln, n), F32)
            dcm = jnp.zeros((ln, n), F32)
            for e in range(e_):
                h = g * e_ + e
                hs = slice(h * p, (h + 1) * p)
                oh = (lane == h).astype(F32)
                x_h, xd, lmat = _ssd_head_v1(h, xs, dt, a_cs, a_t, cb, tril)
                d_y = dy_ref[:, hs]
                s_prev = st_ref[g * n:(g + 1) * n, e * p:(e + 1) * p]
                d_s = ds_ref[g * n:(g + 1) * n, e * p:(e + 1) * p]
                dy_m = d_y.astype(MXU_DTYPE)
                xd_m = xd.astype(MXU_DTYPE)
                sp_m = s_prev.astype(MXU_DTYPE)
                ds_m = d_s.astype(MXU_DTYPE)
                e_a_h, dte_h, cd_h = e_a[:, h:h + 1], dte[:, h:h + 1], cd[:, h:h + 1]
                m_mat = (cb * lmat).astype(MXU_DTYPE)
                bds = jnp.dot(bm, ds_m, preferred_element_type=F32)
                d_xd = lax.dot_general(m_mat, dy_m, c00, preferred_element_type=F32) + dte_h * bds
                d_m = lax.dot_general(dy_m, xd_m, c11, preferred_element_type=F32)
                d_cb = (d_m * lmat).astype(MXU_DTYPE)
                w_mat = d_m * cb * lmat
                dy_e = (d_y * e_a_h).astype(MXU_DTYPE)
                xd_e = (xd * dte_h).astype(MXU_DTYPE)
                dcm = dcm + jnp.dot(d_cb, bm, preferred_element_type=F32)
                dcm = dcm + lax.dot_general(dy_e, sp_m, c11, preferred_element_type=F32)
                dbm = dbm + lax.dot_general(d_cb, cm, c00, preferred_element_type=F32)
                dbm = dbm + lax.dot_general(xd_e, ds_m, c11, preferred_element_type=F32)
                ds_ref[g * n:(g + 1) * n, e * p:(e + 1) * p] = (
                    cd_h * d_s + lax.dot_general(cm, dy_e, c00, preferred_element_type=F32))
                y_off = jnp.dot(cm, sp_m, preferred_element_type=F32) * e_a_h
                q_h = jnp.sum(bds * xd, axis=1, keepdims=True) * dte_h
                da_col_h = (jnp.sum(w_mat, axis=1, keepdims=True)
                            + jnp.sum(d_y * y_off, axis=1, keepdims=True) - q_h)
                da_col = da_col + da_col_h * oh
                da_row_t = da_row_t - (sub == h).astype(F32) * jnp.sum(w_mat, axis=0, keepdims=True)
                da_last = da_last + (jnp.sum(q_h, keepdims=True) + cd_h * jnp.sum(d_s * s_prev, keepdims=True)) * oh
                dxbc_ref[:, hs] = d_xd * dt[:, h:h + 1] + dsk[:, h:h + 1] * d_y
                ddt = ddt + jnp.sum(d_xd * x_h, axis=1, keepdims=True) * oh
                ddsk = ddsk + jnp.sum(d_y * x_h, keepdims=True) * oh
            dxbc_ref[:, cfg.inner + g * n:cfg.inner + (g + 1) * n] = dbm
            dxbc_ref[:, cfg.inner + cfg.gn + g * n:cfg.inner + cfg.gn + (g + 1) * n] = dcm
        d_acs = da_col + da_row_t.T + jnp.where(row[:, 0:1] == ln - 1, da_last, 0.0)
        da_dt = lax.dot_general(tri_f, d_acs, c00, preferred_element_type=F32, precision=lax.Precision.HIGHEST)
        ddt = ddt + da_dt * a
        ddt_raw = ddt * jax.nn.sigmoid(dtr + bias)
        ddt_ref[...] = ddt_raw.astype(ddt_ref.dtype)
        da_log = jnp.sum(da_dt * dt, axis=0, keepdims=True) * a
        dvec = jnp.concatenate([jnp.sum(ddt_raw, axis=0, keepdims=True), da_log, ddsk,
                                jnp.zeros((5, LANES), F32)], axis=0)

        @pl.when(first)
        def _():
            dvec_ref[...] = dvec

        @pl.when(jnp.logical_not(first))
        def _():
            dvec_ref[...] += dvec

    vec = lambda b, c: (l, 0, 0)
    rblk = lambda b, c: (b * nc + nc - 1 - c, 0)
    return pl.pallas_call(
        body,
        out_shape=[
            jax.ShapeDtypeStruct((nb * SEQ, cfg.conv_dim), F32),
            jax.ShapeDtypeStruct((nb * SEQ, LANES), ACT_DTYPE),
            jax.ShapeDtypeStruct((8, LANES), F32),
        ],
        grid=(nb, nc),
        in_specs=_ssd_specs(cfg, nc, True) + [
            pl.BlockSpec((g_ * n, e_ * p), rblk),
            pl.BlockSpec((ln, cfg.inner), rblk),
        ] + [pl.BlockSpec((None, 1, LANES), vec)] * 3,
        out_specs=[
            pl.BlockSpec((ln, cfg.conv_dim), rblk),
            pl.BlockSpec((ln, LANES), rblk),
            pl.BlockSpec((8, LANES), lambda b, c: (0, 0)),
        ],
        scratch_shapes=[pltpu.VMEM((g_ * n, e_ * p), F32)],
        compiler_params=_cparams(("arbitrary", "arbitrary")),
        name="ssd_bwd",
    )(xbc, xbc, xbc, proj, states, dy, dt_bias, a_log, d_skip)


def _split3_dot(x, u):
    hi = x.astype(BF16)
    r1 = x - hi.astype(F32)
    mid = r1.astype(BF16)
    lo = (r1 - mid.astype(F32)).astype(BF16)
    dot = lambda a: jnp.dot(a, u, preferred_element_type=F32)
    return dot(hi) + dot(mid) + dot(lo)


def _ssd_consts(cfg):
    p = SSM_HEAD_DIM
    hrow = jnp.arange(LANES)[:, None]
    spread = (jnp.arange(cfg.inner)[None, :] // p == hrow).astype(BF16)
    spread_tile = (jnp.arange(cfg.heads * LANES)[None, :] // LANES == hrow).astype(BF16)
    return spread, spread_tile, spread.T


def _ssd_chunk_terms(dtr, bias, alog, spread, spread_tile):
    ln = SSM_CHUNK
    row = lax.broadcasted_iota(jnp.int32, (ln, ln), 0)
    col = lax.broadcasted_iota(jnp.int32, (ln, ln), 1)
    tril = row >= col
    dt, a, a_cs, a_t, e_a, dte, cd = _ssd_common(dtr, bias, alog, tril.astype(F32))
    ex = lambda v: _split_dot(v, spread)
    cd_x = ex(jnp.broadcast_to(cd, (8, LANES)))[0:1]
    colb = _split3_dot(a_cs, spread_tile)
    return dict(tril=tril, row=row, col=col, dt=dt, a=a, a_cs=a_cs, a_t=a_t, e_a=e_a, dte=dte, cd=cd,
                dt_x=ex(dt), ea_x=ex(e_a), dte_x=ex(dte), cd_x=cd_x, colb=colb)


def _head_masks():
    lane = lax.broadcasted_iota(jnp.int32, (1, LANES), 1)
    hpt = LANES // SSM_HEAD_DIM
    return [jnp.logical_and(lane >= i * SSM_HEAD_DIM, lane < (i + 1) * SSM_HEAD_DIM) for i in range(hpt)]


def _ssd_fwd(xbc, proj, dt_bias, a_log, d_skip_x, l, cfg, nb):
    ln, p, n = SSM_CHUNK, SSM_HEAD_DIM, SSM_STATE
    nc = SEQ // ln
    g_, e_ = SSM_GROUPS, cfg.epg
    gw = e_ * p
    hpt = LANES // p
    assert cfg.inner % cfg.gn == 0 and gw % LANES == 0
    spread, spread_tile, _ = _ssd_consts(cfg)

    def body(xs_ref, bm_ref, cm_ref, dtr_ref, bias_ref, alog_ref, dskx_ref, sp_ref, spt_ref, y_ref, st_ref, s_ref):
        c = pl.program_id(1)

        @pl.when(c == 0)
        def _():
            s_ref[...] = jnp.zeros_like(s_ref)

        st_ref[...] = s_ref[...]
        t = _ssd_chunk_terms(dtr_ref[...], bias_ref[...], alog_ref[...], sp_ref[...], spt_ref[...])
        hm = _head_masks()
        xs = xs_ref[...]
        xd = xs * t["dt_x"]
        xde = (xd * t["dte_x"]).astype(MXU_DTYPE)
        for g in range(g_):
            gc = slice(g * gw, (g + 1) * gw)
            bm = bm_ref[:, g * n:(g + 1) * n].astype(MXU_DTYPE)
            cm = cm_ref[:, g * n:(g + 1) * n].astype(MXU_DTYPE)
            cb = lax.dot_general(cm, bm, C11, preferred_element_type=F32)
            sg = s_ref[g * n:(g + 1) * n, :]
            y_off = jnp.dot(cm, sg.astype(MXU_DTYPE), preferred_element_type=F32) * t["ea_x"][:, gc]
            s_ref[g * n:(g + 1) * n, :] = t["cd_x"][:, gc] * sg + lax.dot_general(
                bm, xde[:, gc], C00, preferred_element_type=F32)
            for k in range(gw // LANES):
                lanes = slice(g * gw + k * LANES, g * gw + (k + 1) * LANES)
                xp = xd[:, lanes]
                acc = y_off[:, k * LANES:(k + 1) * LANES] + dskx_ref[:, lanes] * xs[:, lanes]
                for i in range(hpt):
                    h = (g * gw + k * LANES) // p + i
                    lmat = jnp.exp(jnp.where(t["tril"], t["colb"][:, h * LANES:(h + 1) * LANES] - t["a_t"][h:h + 1, :],
                                             -jnp.inf))
                    acc = acc + jnp.dot((cb * lmat).astype(MXU_DTYPE),
                                        jnp.where(hm[i], xp, 0.0).astype(MXU_DTYPE), preferred_element_type=F32)
                y_ref[:, lanes] = acc

    vec = lambda b, c: (l, 0, 0)
    whole = lambda a: pl.BlockSpec(a.shape, lambda b, c: (0, 0))
    return pl.pallas_call(
        body,
        out_shape=[
            jax.ShapeDtypeStruct((nb * SEQ, cfg.inner), F32),
            jax.ShapeDtypeStruct((nb * nc * g_ * n, gw), F32),
        ],
        grid=(nb, nc),
        in_specs=_ssd_specs(cfg, nc, False) + [pl.BlockSpec((None, 1, LANES), vec)] * 2
        + [pl.BlockSpec((None, 1, cfg.inner), vec), whole(spread), whole(spread_tile)],
        out_specs=[
            pl.BlockSpec((ln, cfg.inner), lambda b, c: (b * nc + c, 0)),
            pl.BlockSpec((g_ * n, gw), lambda b, c: (b * nc + c, 0)),
        ],
        scratch_shapes=[pltpu.VMEM((g_ * n, gw), F32)],
        compiler_params=_cparams(("parallel", "arbitrary")),
        name="ssd_fwd",
    )(xbc, xbc, xbc, proj, dt_bias, a_log, d_skip_x, spread, spread_tile)


def _ssd_bwd(xbc, proj, states, dy, dt_bias, a_log, d_skip_x, l, cfg, nb):
    ln, p, n = SSM_CHUNK, SSM_HEAD_DIM, SSM_STATE
    nc = SEQ // ln
    g_, e_ = SSM_GROUPS, cfg.epg
    gw = e_ * p
    hpt = LANES // p
    spread, spread_tile, gather_t = _ssd_consts(cfg)

    def body(xs_ref, bm_ref, cm_ref, dtr_ref, st_ref, dy_ref, bias_ref, alog_ref, dskx_ref, sp_ref, spt_ref, gt_ref,
             dxbc_ref, ddt_ref, dvec_ref, ds_ref, r1_ref, r2_ref, r4_ref, ss_ref):
        first = jnp.logical_and(pl.program_id(0) == 0, pl.program_id(1) == 0)

        @pl.when(pl.program_id(1) == 0)
        def _():
            ds_ref[...] = jnp.zeros_like(ds_ref)

        dtr = dtr_ref[...]
        bias = bias_ref[...]
        t = _ssd_chunk_terms(dtr, bias, alog_ref[...], sp_ref[...], spt_ref[...])
        tril = t["tril"]
        triu = t["row"] <= t["col"]
        hm = _head_masks()
        lane = lax.broadcasted_iota(jnp.int32, (1, LANES), 1)
        sub = lax.broadcasted_iota(jnp.int32, (LANES, 1), 0)
        xs = xs_ref[...]
        dyv = dy_ref[...]
        xd = xs * t["dt_x"]
        xde = xd * t["dte_x"]
        xde_m = xde.astype(MXU_DTYPE)
        dye_m = (dyv * t["ea_x"]).astype(MXU_DTYPE)
        da_col = jnp.zeros((ln, LANES), F32)
        da_row_t = jnp.zeros((LANES, ln), F32)
        ss_ref[...] = jnp.zeros_like(ss_ref)
        for g in range(g_):
            gc = slice(g * gw, (g + 1) * gw)
            gr = slice(g * n, (g + 1) * n)
            bm = bm_ref[:, gr].astype(MXU_DTYPE)
            cm = cm_ref[:, gr].astype(MXU_DTYPE)
            cb = lax.dot_general(cm, bm, C11, preferred_element_type=F32)
            cb_t = lax.dot_general(bm, cm, C11, preferred_element_type=F32)
            sp = st_ref[gr, :]
            dsg = ds_ref[gr, :]
            sp_m = sp.astype(MXU_DTYPE)
            dsg_m = dsg.astype(MXU_DTYPE)
            bds = jnp.dot(bm, dsg_m, preferred_element_type=F32)
            y_off = jnp.dot(cm, sp_m, preferred_element_type=F32) * t["ea_x"][:, gc]
            dcm = lax.dot_general(dye_m[:, gc], sp_m, C11, preferred_element_type=F32)
            dbm = lax.dot_general(xde_m[:, gc], dsg_m, C11, preferred_element_type=F32)
            ds_ref[gr, :] = t["cd_x"][:, gc] * dsg + lax.dot_general(cm, dye_m[:, gc], C00, preferred_element_type=F32)
            r4 = bds * xde[:, gc]
            r4_ref[:, gc] = r4
            r1_ref[:, gc] = dyv[:, gc] * y_off - r4
            ss_ref[0:1, gc] = jnp.sum(dsg * sp, axis=0, keepdims=True)
            dcb = jnp.zeros((ln, ln), F32)
            for k in range(gw // LANES):
                lanes = slice(g * gw + k * LANES, g * gw + (k + 1) * LANES)
                xp = xd[:, lanes]
                xp_m = xp.astype(MXU_DTYPE)
                dyp = dyv[:, lanes]
                dxp = t["dte_x"][:, lanes] * bds[:, k * LANES:(k + 1) * LANES]
                for i in range(hpt):
                    h = (g * gw + k * LANES) // p + i
                    diff = t["colb"][:, h * LANES:(h + 1) * LANES] - t["a_t"][h:h + 1, :]
                    lmat = jnp.exp(jnp.where(tril, diff, -jnp.inf))
                    lmat_t = jnp.exp(jnp.where(triu, -diff, -jnp.inf))
                    dy_h = jnp.where(hm[i], dyp, 0.0).astype(MXU_DTYPE)
                    d_ml = lax.dot_general(dy_h, xp_m, C11, preferred_element_type=F32) * lmat
                    dcb = dcb + d_ml
                    w_mat = d_ml * cb
                    dxp = dxp + jnp.dot((cb_t * lmat_t).astype(MXU_DTYPE), dy_h, preferred_element_type=F32)
                    da_col = da_col + jnp.sum(w_mat, axis=1, keepdims=True) * (lane == h).astype(F32)
                    da_row_t = da_row_t - (sub == h).astype(F32) * jnp.sum(w_mat, axis=0, keepdims=True)
                dxbc_ref[:, lanes] = dxp * t["dt_x"][:, lanes] + dskx_ref[:, lanes] * dyp
                r2_ref[:, lanes] = dxp * xs[:, lanes]
            dcb_m = dcb.astype(MXU_DTYPE)
            dxbc_ref[:, cfg.inner + g * n:cfg.inner + (g + 1) * n] = dbm + lax.dot_general(
                dcb_m, cm, C00, preferred_element_type=F32)
            dxbc_ref[:, cfg.inner + cfg.gn + g * n:cfg.inner + cfg.gn + (g + 1) * n] = dcm + jnp.dot(
                dcb_m, bm, preferred_element_type=F32)
        gt = gt_ref[...]
        rd = lambda v: _split_dot(v, gt)
        red4 = rd(r4_ref[...])
        da_last = jnp.sum(red4, axis=0, keepdims=True) + t["cd"] * rd(ss_ref[...])[0:1]
        d_acs = da_col + rd(r1_ref[...]) + da_row_t.T + jnp.where(t["row"][:, 0:1] == ln - 1, da_last, 0.0)
        da_dt = lax.dot_general(tril.astype(F32), d_acs, C00, preferred_element_type=F32,
                                precision=lax.Precision.HIGHEST)
        ddt = rd(r2_ref[...]) + da_dt * t["a"]
        ddt_raw = ddt * jax.nn.sigmoid(dtr + bias)
        ddt_ref[...] = ddt_raw.astype(ddt_ref.dtype)
        da_log = jnp.sum(da_dt * t["dt"], axis=0, keepdims=True) * t["a"]
        ddsk = jnp.sum(rd(dyv * xs), axis=0, keepdims=True)
        dvec = jnp.concatenate([jnp.sum(ddt_raw, axis=0, keepdims=True), da_log, ddsk,
                                jnp.zeros((5, LANES), F32)], axis=0)

        @pl.when(first)
        def _():
            dvec_ref[...] = dvec

        @pl.when(jnp.logical_not(first))
        def _():
            dvec_ref[...] += dvec

    vec = lambda b, c: (l, 0, 0)
    rblk = lambda b, c: (b * nc + nc - 1 - c, 0)
    whole = lambda a: pl.BlockSpec(a.shape, lambda b, c: (0, 0))
    return pl.pallas_call(
        body,
        out_shape=[
            jax.ShapeDtypeStruct((nb * SEQ, cfg.conv_dim), F32),
            jax.ShapeDtypeStruct((nb * SEQ, LANES), ACT_DTYPE),
            jax.ShapeDtypeStruct((8, LANES), F32),
        ],
        grid=(nb, nc),
        in_specs=_ssd_specs(cfg, nc, True) + [
            pl.BlockSpec((g_ * n, gw), rblk),
            pl.BlockSpec((ln, cfg.inner), rblk),
        ] + [pl.BlockSpec((None, 1, LANES), vec)] * 2 + [pl.BlockSpec((None, 1, cfg.inner), vec),
                                                           whole(spread), whole(spread_tile), whole(gather_t)],
        out_specs=[
            pl.BlockSpec((ln, cfg.conv_dim), rblk),
            pl.BlockSpec((ln, LANES), rblk),
            pl.BlockSpec((8, LANES), lambda b, c: (0, 0)),
        ],
        scratch_shapes=[pltpu.VMEM((g_ * n, gw), F32)] + [pltpu.VMEM((ln, cfg.inner), F32)] * 3
        + [pltpu.VMEM((8, cfg.inner), F32)],
        compiler_params=_cparams(("arbitrary", "arbitrary")),
        name="ssd_bwd",
    )(xbc, xbc, xbc, proj, states, dy, dt_bias, a_log, d_skip_x, spread, spread_tile, gather_t)


XA_ROWS = 256


def _xa_probs(q_ref, kv_ref, h, dh):
    c11 = (((1,), (1,)), ((), ()))
    qh = q_ref[:, h * dh:(h + 1) * dh].astype(MXU_DTYPE)
    kh = kv_ref[:, h * dh:(h + 1) * dh].astype(MXU_DTYPE)
    vh = kv_ref[:, D_MODEL + h * dh:D_MODEL + (h + 1) * dh].astype(MXU_DTYPE)
    s = lax.dot_general(qh, kh, c11, preferred_element_type=F32) * (dh ** -0.5)
    s = s - jnp.max(s, axis=1, keepdims=True)
    pr = jnp.exp(s)
    return qh, kh, vh, pr / jnp.sum(pr, axis=1, keepdims=True)


def _xa_fwd(q, kv, cfg, nb):
    tq = _tile(SEQ, XA_ROWS, 16)
    nq = SEQ // tq
    dh = cfg.xa_dim

    def body(q_ref, kv_ref, o_ref):
        for h in range(XA_HEADS):
            _, _, vh, pr = _xa_probs(q_ref, kv_ref, h, dh)
            o_ref[:, h * dh:(h + 1) * dh] = jnp.dot(pr.astype(MXU_DTYPE), vh, preferred_element_type=F32).astype(o_ref.dtype)

    return pl.pallas_call(
        body,
        out_shape=jax.ShapeDtypeStruct((nb * SEQ, D_MODEL), ACT_DTYPE),
        grid=(nb, nq),
        in_specs=[
            pl.BlockSpec((tq, D_MODEL), lambda b, i: (b * nq + i, 0)),
            pl.BlockSpec((MEM_LEN, 2 * D_MODEL), lambda b, i: (b, 0)),
        ],
        out_specs=pl.BlockSpec((tq, D_MODEL), lambda b, i: (b * nq + i, 0)),
        compiler_params=_cparams(("parallel", "parallel")),
        name="xa_fwd",
    )(q, kv)


def _xa_bwd(q, kv, do, cfg, nb):
    tq = _tile(SEQ, XA_ROWS, 16)
    nq = SEQ // tq
    dh = cfg.xa_dim
    c00 = (((0,), (0,)), ((), ()))
    c11 = (((1,), (1,)), ((), ()))
    scale = dh ** -0.5

    def body(q_ref, kv_ref, do_ref, dq_ref, dkv_ref, acc_ref):
        i = pl.program_id(1)

        @pl.when(i == 0)
        def _():
            acc_ref[...] = jnp.zeros_like(acc_ref)

        for h in range(XA_HEADS):
            hs = slice(h * dh, (h + 1) * dh)
            vs = slice(D_MODEL + h * dh, D_MODEL + (h + 1) * dh)
            qh, kh, vh, pr = _xa_probs(q_ref, kv_ref, h, dh)
            do_h = do_ref[:, hs].astype(MXU_DTYPE)
            dp = lax.dot_general(do_h, vh, c11, preferred_element_type=F32)
            ds = (pr * (dp - jnp.sum(dp * pr, axis=1, keepdims=True))).astype(MXU_DTYPE)
            dq_ref[:, hs] = (jnp.dot(ds, kh, preferred_element_type=F32) * scale).astype(dq_ref.dtype)
            acc_ref[:, hs] += lax.dot_general(ds, qh, c00, preferred_element_type=F32) * scale
            acc_ref[:, vs] += lax.dot_general(pr.astype(MXU_DTYPE), do_h, c00, preferred_element_type=F32)

        @pl.when(i == nq - 1)
        def _():
            dkv_ref[...] = acc_ref[...].astype(dkv_ref.dtype)

    return pl.pallas_call(
        body,
        out_shape=[
            jax.ShapeDtypeStruct((nb * SEQ, D_MODEL), ACT_DTYPE),
            jax.ShapeDtypeStruct((nb * MEM_LEN, 2 * D_MODEL), ACT_DTYPE),
        ],
        grid=(nb, nq),
        in_specs=[
            pl.BlockSpec((tq, D_MODEL), lambda b, i: (b * nq + i, 0)),
            pl.BlockSpec((MEM_LEN, 2 * D_MODEL), lambda b, i: (b, 0)),
            pl.BlockSpec((tq, D_MODEL), lambda b, i: (b * nq + i, 0)),
        ],
        out_specs=[
            pl.BlockSpec((tq, D_MODEL), lambda b, i: (b * nq + i, 0)),
            pl.BlockSpec((MEM_LEN, 2 * D_MODEL), lambda b, i: (b, 0)),
        ],
        scratch_shapes=[pltpu.VMEM((MEM_LEN, 2 * D_MODEL), F32)],
        compiler_params=_cparams(("parallel", "arbitrary")),
        name="xa_bwd",
    )(q, kv, do)


def _adamw(parts, w, m, v, name, tr=128, pre=()):
    n, r, c = parts.shape
    tr = _tile(r, tr, 16)

    def body(p_ref, w_ref, m_ref, v_ref, g_ref, d_ref, nm_ref, nv_ref):
        g = p_ref[0].astype(F32)
        for i in range(1, n):
            g = g + p_ref[i].astype(F32)
        m2 = ADAM_B1 * m_ref[...] + (1.0 - ADAM_B1) * g
        v2 = ADAM_B2 * v_ref[...] + (1.0 - ADAM_B2) * (g * g)
        m_hat = m2 / (1.0 - ADAM_B1 ** ADAM_STEP)
        v_hat = v2 / (1.0 - ADAM_B2 ** ADAM_STEP)
        g_ref[...] = g
        d_ref[...] = -ADAM_LR * (m_hat / (jnp.sqrt(v_hat) + ADAM_EPS) + ADAM_WD * w_ref[...])
        nm_ref[...] = m2
        nv_ref[...] = v2

    blk = pl.BlockSpec((tr, c), lambda i: (i, 0))
    wblk = pl.BlockSpec((None,) * len(pre) + (tr, c), lambda i: pre + (i, 0))
    return pl.pallas_call(
        body,
        out_shape=[jax.ShapeDtypeStruct((r, c), F32)] * 4,
        grid=(r // tr,),
        in_specs=[pl.BlockSpec((n, tr, c), lambda i: (0, i, 0)), wblk, wblk, wblk],
        out_specs=[blk] * 4,
        compiler_params=_cparams(("parallel",)),
        name=name,
    )(parts, w, m, v)


def _flat_index(px, py, pc):
    return 4 * px + 2 * py + pc


def _all_gather(arrs, name):
    n = len(arrs)

    def body(*refs):
        start, forward, finish = _gather_phases(refs[:n], refs[n:2 * n], *refs[2 * n:])
        start()
        forward()
        finish()

    return pl.pallas_call(
        body,
        out_shape=_gather_shapes(arrs),
        in_specs=[ANY_SPEC] * n,
        out_specs=[ANY_SPEC] * n,
        scratch_shapes=_comm_sems(n),
        name=name,
    )(*arrs)


ANY_SPEC = pl.BlockSpec(memory_space=pl.ANY)


def _comm_sems(n):
    return [pltpu.SemaphoreType.DMA((n, N_DEV - 1)), pltpu.SemaphoreType.DMA((n, N_DEV - 1)),
            pltpu.SemaphoreType.DMA((n,))]


def _gather_shapes(arrs):
    return [jax.ShapeDtypeStruct((N_DEV,) + a.shape, a.dtype) for a in arrs]


def _gather_phases(ins, outs, send_sems, recv_sems, local_sems):
    n = len(ins)
    x, y, c = lax.axis_index("x"), lax.axis_index("y"), lax.axis_index("c")
    me, sibling = (x, y, c), (x, y, 1 - c)
    chips = [(1 - x, y), (x, 1 - y), (1 - x, 1 - y)]

    def copy(a, k, block, to, src=None):
        slot = outs[a].at[_flat_index(*block)]
        return pltpu.make_async_remote_copy(
            src_ref=slot if src is None else src, dst_ref=slot,
            send_sem=send_sems.at[a, k], recv_sem=recv_sems.at[a, k],
            device_id=to, device_id_type=MESH)

    def mine(a):
        return pltpu.make_async_copy(ins[a], outs[a].at[_flat_index(*me)], local_sems.at[a])

    def first(a):
        return [copy(a, 0, me, sibling, src=ins[a])] + [
            copy(a, 1 + j, me, (*chip, c), src=ins[a]) for j, chip in enumerate(chips)]

    def start():
        for a in range(n):
            mine(a).start()
            for cp in first(a):
                cp.start()

    def forward():
        for j, chip in enumerate(chips):
            for a in range(n):
                copy(a, 1 + j, (*chip, c), me).wait_recv()
                copy(a, 4 + j, (*chip, c), sibling).start()

    def finish():
        for a in range(n):
            copy(a, 0, sibling, me).wait_recv()
            for j, chip in enumerate(chips):
                copy(a, 4 + j, (*chip, 1 - c), me).wait_recv()
        for a in range(n):
            for cp in first(a):
                cp.wait_send()
            for j, chip in enumerate(chips):
                copy(a, 4 + j, (*chip, c), sibling).wait_send()
            mine(a).wait()

    return start, forward, finish


def _scatter_blocks(arrs, name):
    n = len(arrs)

    def body(*refs):
        start, finish = _scatter_phases(refs[:n], refs[n:2 * n], *refs[2 * n:])
        start()
        finish()

    return pl.pallas_call(
        body,
        out_shape=[jax.ShapeDtypeStruct(a.shape, a.dtype) for a in arrs],
        in_specs=[ANY_SPEC] * n,
        out_specs=[ANY_SPEC] * n,
        scratch_shapes=_comm_sems(n),
        name=name,
    )(*arrs)


def _scatter_phases(ins, outs, send_sems, recv_sems, local_sems):
    n = len(ins)
    x, y, c = lax.axis_index("x"), lax.axis_index("y"), lax.axis_index("c")
    me = _flat_index(x, y, c)

    def peer(k):
        return (1 - x if k & 4 else x, 1 - y if k & 2 else y, 1 - c if k & 1 else c)

    def copy(a, k):
        p = peer(k)
        return pltpu.make_async_remote_copy(
            src_ref=ins[a].at[_flat_index(*p)], dst_ref=outs[a].at[me],
            send_sem=send_sems.at[a, k - 1], recv_sem=recv_sems.at[a, k - 1],
            device_id=p, device_id_type=MESH)

    def landed(a, k):
        slot = outs[a].at[_flat_index(*peer(k))]
        return pltpu.make_async_remote_copy(
            src_ref=slot, dst_ref=slot, send_sem=send_sems.at[a, k - 1], recv_sem=recv_sems.at[a, k - 1],
            device_id=peer(k), device_id_type=MESH)

    def mine(a):
        return pltpu.make_async_copy(ins[a].at[me], outs[a].at[me], local_sems.at[a])

    def start():
        for a in range(n):
            mine(a).start()
            for k in range(1, N_DEV):
                copy(a, k).start()

    def finish():
        for a in range(n):
            for k in range(1, N_DEV):
                landed(a, k).wait_recv()
        for a in range(n):
            for k in range(1, N_DEV):
                copy(a, k).wait_send()
            mine(a).wait()

    return start, finish


_BIG = ("w_in", "w_br_att", "w_br_ssm", "w_mix_out", "w_xq", "w_xkv", "w_xo", "w_gu", "w_down")
_COL_SHARDED = ("w_in", "w_xkv", "w_gu", "conv_w")
_SMALL = ("g_pre_mix", "conv_b", "dt_bias", "a_log", "d_skip", "g_ssm_norm", "g_post_mix", "g_pre_xa",
          "g_mem", "g_post_xa", "g_pre_ffn", "g_post_ffn")
_WEIGHTS = ("g_pre_mix", "w_in", "conv_w", "conv_b", "dt_bias", "a_log", "d_skip", "g_ssm_norm", "w_br_att",
            "w_br_ssm", "w_mix_out", "g_post_mix", "g_pre_xa", "g_mem", "w_xq", "w_xkv", "w_xo", "g_post_xa",
            "g_pre_ffn", "w_gu", "w_down", "g_post_ffn")
PACK_W = 8 * LANES


def _unshard(g, col):
    n, r, c = g.shape
    if col:
        return jnp.transpose(g, (1, 0, 2)).reshape(r, n * c)
    return g.reshape(n * r, c)


def _shard(w, col):
    r, c = w.shape
    if col:
        return jnp.transpose(w.reshape(r, N_DEV, c // N_DEV), (1, 0, 2))
    return w.reshape(N_DEV, r // N_DEV, c)


def _permute_in(w, cfg):
    parts, off = [], 0
    for size in cfg.in_sizes:
        parts.append(w[..., off:off + size])
        off += size
    q, k, v, z, xbc, dt, ga, gs = parts
    pad = jnp.zeros(w.shape[:-1] + (LANES - cfg.heads,), w.dtype)
    return jnp.concatenate([z, ga, gs, q, k, v, xbc, dt, pad], axis=-1)


def _unpermute_in(w, cfg):
    c = cfg
    sl = lambda a, n: w[..., a:a + n]
    return jnp.concatenate([sl(c.q0, c.sbw), sl(c.k0, c.sbw), sl(c.v0, c.sbw), sl(c.z0, c.inner),
                            sl(c.xbc0, c.conv_dim), sl(c.dt0, c.heads), sl(c.ga0, c.d), sl(c.gs0, c.d)], axis=-1)


def _pack(arrs):
    flat = jnp.concatenate([a.reshape(-1).astype(F32) for a in arrs])
    rows = -(-flat.shape[0] // PACK_W)
    rows = -(-rows // 8) * 8
    return jnp.pad(flat, (0, rows * PACK_W - flat.shape[0])).reshape(rows, PACK_W)


def _unpack(p, shapes):
    flat = p.reshape(-1)
    out, off = [], 0
    for s in shapes:
        size = math.prod(s)
        out.append(flat[off:off + size].reshape(s))
        off += size
    return out


def _vec3(a, width=None):
    if width is not None and a.shape[1] < width:
        a = jnp.pad(a, ((0, 0), (0, width - a.shape[1])))
    return a[:, None, :]


def _forward_layer(l, xin, h1, memf, tgt, w, p, cfg, nb, last, gather):
    t = xin.shape[0]
    d = cfg.d
    s = {"x_in": xin, "h1": h1}
    proj = _mm(h1, w["w_in"], name="mm_proj")
    s["proj"] = proj
    s["o_att"], gathered = _sb_fwd(proj, cfg, nb, gather)
    s["xbc"] = _conv_fwd(proj, w["conv_w"], p["conv_b"], l, cfg, nb)
    s["y"], s["states"] = _ssd_fwd(s["xbc"], proj, p["dt_bias"], p["a_log"], p["d_skip_x"], l, cfg, nb)
    s["o_ssm"] = _rowwise(_f_gate_norm, "gate_norm_fwd", t, [_full(s["y"]), (proj, cfg.inner, 0)],
                          [(p["g_ssm_norm"], l)], [(cfg.inner, ACT_DTYPE)])[0]
    s["ba"] = _mm(s["o_att"], w["w_br_att"], name="mm_br_att")
    s["bs"] = _mm(s["o_ssm"], w["w_br_ssm"], name="mm_br_ssm")
    s["merged"] = _rowwise(_f_merge, "merge_fwd", t,
                           [(proj, d, cfg.ga0 // d), (proj, d, cfg.gs0 // d), _full(s["ba"]), _full(s["bs"])],
                           [], [(d, ACT_DTYPE)])[0]
    s["mo"] = _mm(s["merged"], w["w_mix_out"], name="mm_mix_out")
    s["x1"], s["h2"] = _rowwise(_f_post_pre, "post_pre_mix", t, [_full(xin), _full(s["mo"])],
                                [(p["g_post_mix"], l), (p["g_pre_xa"], l)], [(d, F32), (d, ACT_DTYPE)])
    s["mem_n"] = _rowwise(_rms, "mem_norm", memf.shape[0], [_full(memf)], [(p["g_mem"], l)], [(d, ACT_DTYPE)])[0]
    s["q"] = _mm(s["h2"], w["w_xq"], name="mm_xq")
    s["kv"] = _mm(s["mem_n"], w["w_xkv"], name="mm_xkv")
    s["o_xa"] = _xa_fwd(s["q"], s["kv"], cfg, nb)
    s["xo"] = _mm(s["o_xa"], w["w_xo"], name="mm_xo")
    s["x2"], s["h3"] = _rowwise(_f_post_pre, "post_pre_xa", t, [_full(s["x1"]), _full(s["xo"])],
                                [(p["g_post_xa"], l), (p["g_pre_ffn"], l)], [(d, F32), (d, ACT_DTYPE)])
    s["gu"] = _mm(s["h3"], w["w_gu"], name="mm_gu")
    s["act"] = _rowwise(_f_swiglu, "swiglu_fwd", t, [(s["gu"], cfg.ffn, 0), (s["gu"], cfg.ffn, 1)], [],
                        [(cfg.ffn, ACT_DTYPE)])[0]
    s["dn"] = _mm(s["act"], w["w_down"], name="mm_down")
    if last:
        nxt = _rowwise(_f_final, "final_loss", t, [_full(s["x2"]), _full(s["dn"]), _full(tgt)],
                       [(p["g_post_ffn"], l)], [(d, F32)], acc_out=[(1, d)])
    else:
        nxt = _rowwise(_f_post_pre, "post_pre_ffn", t, [_full(s["x2"]), _full(s["dn"])],
                       [(p["g_post_ffn"], l), (p["g_pre_mix"], l + 1)], [(d, F32), (d, ACT_DTYPE)])
    return s, nxt, gathered


def _backward_layer(l, s, dx, d_dn, memf, w, p, cfg, nb, prev_dn, scatter):
    t = dx.shape[0]
    d = cfg.d
    g = {}
    dact = _mm(d_dn, w["w_down"], tb=True, out_dtype=ACT_DTYPE, name="mm_d_act", tn=1408)
    g["w_down"] = _mm(s["act"], d_dn, ta=True, name="mm_dw_down")
    dgu = _rowwise(_f_swiglu_bwd, "swiglu_bwd", t, [(s["gu"], cfg.ffn, 0), (s["gu"], cfg.ffn, 1), _full(dact)], [],
                   [(2 * cfg.ffn, ACT_DTYPE)])[0]
    dh3 = _mm(dgu, w["w_gu"], tb=True, name="mm_d_h3", tk=1408)
    g["w_gu"] = _mm(s["h3"], dgu, ta=True, name="mm_dw_gu")
    dx2, d_xo, g["g_pre_ffn"], g["g_post_xa"] = _rowwise(
        _f_pre_post_bwd, "pre_post_bwd_ffn", t, [_full(s["x2"]), _full(dh3), _full(dx), _full(s["xo"])],
        [(p["g_pre_ffn"], l), (p["g_post_xa"], l)], [(d, F32), (d, ACT_DTYPE)], acc_out=[(1, d), (1, d)])
    do_xa = _mm(d_xo, w["w_xo"], tb=True, out_dtype=ACT_DTYPE, name="mm_d_oxa")
    g["w_xo"] = _mm(s["o_xa"], d_xo, ta=True, name="mm_dw_xo")
    dq, dkv = _xa_bwd(s["q"], s["kv"], do_xa, cfg, nb)
    dh2 = _mm(dq, w["w_xq"], tb=True, name="mm_d_h2")
    g["w_xq"] = _mm(s["h2"], dq, ta=True, name="mm_dw_xq")
    dmem_n = _mm(dkv, w["w_xkv"], tb=True, name="mm_d_mem")
    g["w_xkv"] = _mm(s["mem_n"], dkv, ta=True, name="mm_dw_xkv")
    g["g_mem"] = _rowwise(_f_gain_bwd, "mem_norm_bwd", memf.shape[0], [_full(memf), _full(dmem_n)],
                          [(p["g_mem"], l)], [], acc_out=[(1, d)])[0]
    dx1, d_mo, g["g_pre_xa"], g["g_post_mix"] = _rowwise(
        _f_pre_post_bwd, "pre_post_bwd_xa", t, [_full(s["x1"]), _full(dh2), _full(dx2), _full(s["mo"])],
        [(p["g_pre_xa"], l), (p["g_post_mix"], l)], [(d, F32), (d, ACT_DTYPE)], acc_out=[(1, d), (1, d)])
    dmerged = _mm(d_mo, w["w_mix_out"], tb=True, out_dtype=ACT_DTYPE, name="mm_d_merged")
    g["w_mix_out"] = _mm(s["merged"], d_mo, ta=True, name="mm_dw_mix_out")
    proj = s["proj"]
    dgg, dba, dbs = _rowwise(
        _f_merge_bwd, "merge_bwd", t,
        [(proj, d, cfg.ga0 // d), (proj, d, cfg.gs0 // d), _full(s["ba"]), _full(s["bs"]), _full(dmerged)], [],
        [(2 * d, ACT_DTYPE), (d, ACT_DTYPE), (d, ACT_DTYPE)])
    do_att = _mm(dba, w["w_br_att"], tb=True, name="mm_d_oatt")
    g["w_br_att"] = _mm(s["o_att"], dba, ta=True, name="mm_dw_br_att")
    do_ssm = _mm(dbs, w["w_br_ssm"], tb=True, out_dtype=ACT_DTYPE, name="mm_d_ossm")
    g["w_br_ssm"] = _mm(s["o_ssm"], dbs, ta=True, name="mm_dw_br_ssm")
    dy, dz, g["g_ssm_norm"] = _rowwise(
        _f_gate_norm_bwd, "gate_norm_bwd", t, [_full(s["y"]), (proj, cfg.inner, 0), _full(do_ssm)],
        [(p["g_ssm_norm"], l)], [(cfg.inner, F32), (cfg.inner, ACT_DTYPE)], acc_out=[(1, cfg.inner)])
    dxbc, ddt_raw, dvec = _ssd_bwd(s["xbc"], proj, s["states"], dy, p["dt_bias"], p["a_log"], p["d_skip_x"], l, cfg, nb)
    g["dt_bias"], g["a_log"], g["d_skip"] = (dvec[i:i + 1, :cfg.heads] for i in range(3))
    dxbc_raw, g["conv_w"], g["conv_b"] = _conv_bwd(proj, dxbc, w["conv_w"], p["conv_b"], l, cfg, nb)
    dq_sb, dk_sb, dv_sb, landed = _sb_bwd(proj, do_att, cfg, nb, scatter)
    dproj = jnp.concatenate([dz, dgg, dq_sb, dk_sb.astype(ACT_DTYPE), dv_sb.astype(ACT_DTYPE), dxbc_raw, ddt_raw], axis=1)
    dh1 = _mm(dproj, w["w_in"], tb=True, name="mm_d_h1", tk=1152)
    g["w_in"] = _mm(s["h1"], dproj, ta=True, name="mm_dw_in")
    if prev_dn is None:
        dx0, g["g_pre_mix"] = _rowwise(_f_pre_bwd, "pre_bwd_first", t, [_full(s["x_in"]), _full(dh1), _full(dx1)],
                                       [(p["g_pre_mix"], l)], [(d, F32)], acc_out=[(1, d)])
        return g, dx0, None, None, landed
    dx0, d_dn_prev, g["g_pre_mix"], g_post_prev = _rowwise(
        _f_pre_post_bwd, "pre_post_bwd_mix", t, [_full(s["x_in"]), _full(dh1), _full(dx1), _full(prev_dn)],
        [(p["g_pre_mix"], l), (p["g_post_ffn"], l - 1)], [(d, F32), (d, ACT_DTYPE)], acc_out=[(1, d), (1, d)])
    return g, dx0, d_dn_prev, g_post_prev, landed


def kernel(x, mem, g_pre_mix, w_in, conv_w, conv_b, dt_bias, a_log, d_skip, g_ssm_norm, w_br_att, w_br_ssm, w_mix_out, g_post_mix, g_pre_xa, g_mem, w_xq, w_xkv, w_xo, g_post_xa, g_pre_ffn, w_gu, w_down, g_post_ffn, loss_target, m_g_pre_mix, m_w_in, m_conv_w, m_conv_b, m_dt_bias, m_a_log, m_d_skip, m_g_ssm_norm, m_w_br_att, m_w_br_ssm, m_w_mix_out, m_g_post_mix, m_g_pre_xa, m_g_mem, m_w_xq, m_w_xkv, m_w_xo, m_g_post_xa, m_g_pre_ffn, m_w_gu, m_w_down, m_g_post_ffn, v_g_pre_mix, v_w_in, v_conv_w, v_conv_b, v_dt_bias, v_a_log, v_d_skip, v_g_ssm_norm, v_w_br_att, v_w_br_ssm, v_w_mix_out, v_g_post_mix, v_g_pre_xa, v_g_mem, v_w_xq, v_w_xkv, v_w_xo, v_g_post_xa, v_g_pre_ffn, v_w_gu, v_w_down, v_g_post_ffn):
    vals = dict(locals())
    cfg = _Cfg()
    nb = x.shape[0]
    t = nb * SEQ
    d = cfg.d
    depth = g_pre_mix.shape[0]

    def wire(l):
        return [vals[n][l].astype(WIRE_DTYPE) for n in _BIG] + [conv_w[l]]

    def layer_weights(gathered):
        w = {n: _unshard(gw, n in _COL_SHARDED) for n, gw in zip(_BIG + ("conv_w",), gathered)}
        w["w_in"] = _permute_in(w["w_in"], cfg)
        w["conv_w"] = w["conv_w"][None]
        return w

    p = {n: _vec3(vals[n], LANES if n in ("dt_bias", "a_log", "d_skip") else None) for n in _SMALL}
    p["d_skip_x"] = _vec3(jnp.repeat(d_skip, SSM_HEAD_DIM, axis=1))
    weights = [None] * depth
    weights[0] = layer_weights(_all_gather(wire(0), "ag_weights_first"))

    xf = x.reshape(t, d)
    memf = mem.reshape(nb * MEM_LEN, d)
    tgt = loss_target.reshape(t, d)
    h = _rowwise(_rms, "pre_norm_first", t, [_full(xf)], [(p["g_pre_mix"], 0)], [(d, ACT_DTYPE)])[0]
    saved = []
    xcur = xf
    for l in range(depth):
        last = l == depth - 1
        s, nxt, gathered = _forward_layer(l, xcur, h, memf, tgt, weights[l], p, cfg, nb, last,
                                          () if last else wire(l + 1))
        saved.append(s)
        if not last:
            weights[l + 1] = layer_weights(gathered)
            xcur, h = nxt
    dx, loss_row = nxt
    loss = lax.psum(0.5 * jnp.sum(loss_row) / d, AXES)

    def wire_grads(g):
        g = dict(g, w_in=_unpermute_in(g["w_in"], cfg))
        return [_shard(g[n], n in _COL_SHARDED).astype(WIRE_DTYPE) for n in _BIG]

    top = saved[-1]
    d_dn, g_post_top = _rowwise(lambda ysub, dxo, gp: _rms_bwd(ysub, gp, dxo), "post_bwd_last", t,
                                [_full(top["dn"]), _full(dx)], [(p["g_post_ffn"], depth - 1)], [(d, ACT_DTYPE)],
                                acc_out=[(1, d)])
    grads = [None] * depth
    landed = [None] * depth
    post_ffn = [None] * depth
    post_ffn[depth - 1] = g_post_top
    pending = ()
    for l in reversed(range(depth)):
        prev_dn = saved[l - 1]["dn"] if l > 0 else None
        grads[l], dx, d_dn, g_post_prev, got = _backward_layer(l, saved[l], dx, d_dn, memf, weights[l], p, cfg, nb,
                                                               prev_dn, pending)
        if l < depth - 1:
            landed[l + 1] = got
        pending = wire_grads(grads[l])
        if l > 0:
            post_ffn[l - 1] = g_post_prev
    landed[0] = _scatter_blocks(pending, "scatter_grads_last")
    for l in range(depth):
        grads[l]["g_post_ffn"] = post_ffn[l]
    grad_x = dx.reshape(x.shape)
    stacked = {n: jnp.stack([grads[l][n] for l in range(depth)]) for n in _SMALL + ("conv_w",)}

    out = {}
    for i, n in enumerate(_BIG):
        per_layer = [_adamw(landed[l][i], vals[n], vals["m_" + n], vals["v_" + n], "adamw_" + n, pre=(l,))
                     for l in range(depth)]
        out[n] = [jnp.stack([per_layer[l][j] for l in range(depth)]) for j in range(4)]

    small_shapes = [vals[n].shape for n in _SMALL]
    pack_g = _pack([stacked[n] for n in _SMALL])
    conv_g = stacked["conv_w"].reshape(depth * SSM_CONV, cfg.conv_dim)
    parts_small, parts_conv = _all_gather([pack_g, conv_g], "ag_small_grads")
    res = _adamw(parts_small, _pack([vals[n] for n in _SMALL]), _pack([vals["m_" + n] for n in _SMALL]),
                 _pack([vals["v_" + n] for n in _SMALL]), "adamw_small")
    unpacked = [_unpack(r, small_shapes) for r in res]
    for i, n in enumerate(_SMALL):
        out[n] = [unpacked[j][i] for j in range(4)]
    cs = conv_w.shape[2]
    me = _flat_index(lax.axis_index("x"), lax.axis_index("y"), lax.axis_index("c"))
    parts_conv = lax.dynamic_slice_in_dim(parts_conv, me * cs, cs, axis=2)
    flat = lambda a: a.reshape(depth * SSM_CONV, cs)
    res = _adamw(parts_conv, flat(conv_w), flat(m_conv_w), flat(v_conv_w), "adamw_conv_w")
    out["conv_w"] = [r.reshape(conv_w.shape) for r in res]

    return (loss, grad_x, *[out[n][0] for n in _WEIGHTS], *[out[n][1] for n in _WEIGHTS],
            *[out[n][2] for n in _WEIGHTS], *[out[n][3] for n in _WEIGHTS])
```

```python
import functools
import math

import jax
import jax.numpy as jnp
from jax import lax
from jax.experimental import pallas as pl
from jax.experimental.pallas import tpu as pltpu

F32 = jnp.float32
BF16 = jnp.bfloat16
MXU_DTYPE = BF16
ACT_DTYPE = BF16
WIRE_DTYPE = BF16

D_MODEL = 1024
SEQ = 2048
DEPTH = 4
MEM_LEN = 256
RMS_EPS = 1e-6
SB_HEADS = 16
SB_HEAD_DIM = 64
SB_BLOCK = 128
SSM_INNER = 2 * D_MODEL
SSM_HEAD_DIM = 64
SSM_GROUPS = 4
SSM_STATE = 128
SSM_CONV = 4
SSM_CHUNK = 128
XA_HEADS = 4
FFN_HIDDEN = ((8 * D_MODEL + 767) // 768) * 256
ADAM_LR = 0.001
ADAM_B1 = 0.9
ADAM_B2 = 0.999
ADAM_EPS = 1e-08
ADAM_WD = 0.01
ADAM_STEP = 10

N_DEV = 8
LANES = 128
VMEM_LIMIT_BYTES = 56 * 1024 * 1024

AXES = ("x", "y", "c")
MESH = pl.DeviceIdType.MESH


class _Cfg:
    def __init__(self):
        self.d = D_MODEL
        self.sbw = SB_HEADS * SB_HEAD_DIM
        self.inner = SSM_INNER
        self.heads = SSM_INNER // SSM_HEAD_DIM
        self.epg = self.heads // SSM_GROUPS
        self.gn = SSM_GROUPS * SSM_STATE
        self.conv_dim = SSM_INNER + 2 * self.gn
        self.ffn = FFN_HIDDEN
        self.xa_dim = D_MODEL // XA_HEADS
        self.in_sizes = (self.sbw, self.sbw, self.sbw, self.inner, self.conv_dim, self.heads, self.d, self.d)
        self.in_width = sum(self.in_sizes)
        self.z0 = 0
        self.ga0 = self.inner
        self.gs0 = self.ga0 + self.d
        self.q0 = self.gs0 + self.d
        self.k0 = self.q0 + self.sbw
        self.v0 = self.k0 + self.sbw
        self.xbc0 = self.v0 + self.sbw
        self.dt0 = self.xbc0 + self.conv_dim
        self.proj_w = self.dt0 + LANES
        assert self.heads <= LANES


def _cparams(sem=None):
    return pltpu.CompilerParams(dimension_semantics=sem, vmem_limit_bytes=VMEM_LIMIT_BYTES)


def _tile(n, pref, mult):
    if n <= pref:
        return n
    t = (pref // mult) * mult
    while t >= mult:
        if n % t == 0:
            return t
        t -= mult
    return n


MM_VMEM_BUDGET = 40 * 1024 * 1024


def _mm(a, b, *, ta=False, tb=False, out_dtype=F32, name, tm=1024, tn=1152, tk=2048):
    kk, m = (a.shape if ta else a.shape[::-1])
    if tb:
        n, k2 = b.shape
    else:
        k2, n = b.shape
    assert kk == k2, (name, a.shape, b.shape)
    size = lambda dt: jnp.dtype(dt).itemsize
    tn = _tile(n, tn, LANES)
    tk = _tile(kk, tk, LANES if (tb or not ta) else 16)
    nk = kk // tk
    while True:
        tm_ = _tile(m, tm, LANES if ta else 16)
        need = (2 * (tm_ * tk * size(a.dtype) + tk * tn * size(b.dtype) + tm_ * tn * size(out_dtype))
                + (tm_ * tk + tk * tn) * size(MXU_DTYPE) + tm_ * tn * 4 * (2 if nk > 1 else 1))
        if need <= MM_VMEM_BUDGET or tm <= 128:
            break
        tm //= 2
    tm = tm_
    gi, gj = m // tm, n // tn
    j_outer = nk == 1 and b.size * size(b.dtype) * (gi - 1) > a.size * size(a.dtype) * (gj - 1)
    dims = (((0 if ta else 1,), (1 if tb else 0,)), ((), ()))

    def ij(g0, g1):
        return (g1, g0) if j_outer else (g0, g1)

    def body(a_ref, b_ref, o_ref, *scratch):
        av = a_ref[...].astype(MXU_DTYPE)
        bv = b_ref[...].astype(MXU_DTYPE)
        part = lax.dot_general(av, bv, dims, preferred_element_type=F32)
        if nk == 1:
            o_ref[...] = part.astype(out_dtype)
        else:
            acc_ref, = scratch
            k = pl.program_id(2)

            @pl.when(k == 0)
            def _():
                acc_ref[...] = part

            @pl.when(k > 0)
            def _():
                acc_ref[...] += part

            @pl.when(k == nk - 1)
            def _():
                o_ref[...] = acc_ref[...].astype(out_dtype)

    if ta:
        a_spec = pl.BlockSpec((tk, tm), lambda g0, g1, k: (k, ij(g0, g1)[0]))
    else:
        a_spec = pl.BlockSpec((tm, tk), lambda g0, g1, k: (ij(g0, g1)[0], k))
    if tb:
        b_spec = pl.BlockSpec((tn, tk), lambda g0, g1, k: (ij(g0, g1)[1], k))
    else:
        b_spec = pl.BlockSpec((tk, tn), lambda g0, g1, k: (k, ij(g0, g1)[1]))
    return pl.pallas_call(
        body,
        out_shape=jax.ShapeDtypeStruct((m, n), out_dtype),
        grid=(gj, gi, nk) if j_outer else (gi, gj, nk),
        in_specs=[a_spec, b_spec],
        out_specs=pl.BlockSpec((tm, tn), lambda g0, g1, k: ij(g0, g1)),
        scratch_shapes=[] if nk == 1 else [pltpu.VMEM((tm, tn), F32)],
        compiler_params=_cparams(("parallel", "parallel", "arbitrary")),
        name=name,
    )(a, b)


def _rowwise(fn, name, rows, row_in, vec_in, row_out, acc_out=(), tr=256):
    tr = _tile(rows, tr, 16)
    n_in = len(row_in) + len(vec_in)
    n_ro = len(row_out)

    def body(*refs):
        ins = [r[...].astype(F32) for r in refs[:n_in]]
        outs = fn(*ins)
        if not isinstance(outs, (tuple, list)):
            outs = (outs,)
        out_refs = refs[n_in:]
        for o_ref, val in zip(out_refs[:n_ro], outs[:n_ro]):
            o_ref[...] = val.astype(o_ref.dtype)
        if acc_out:
            i = pl.program_id(0)
            for o_ref, val in zip(out_refs[n_ro:], outs[n_ro:]):
                @pl.when(i == 0)
                def _(o_ref=o_ref, val=val):
                    o_ref[...] = val

                @pl.when(i > 0)
                def _(o_ref=o_ref, val=val):
                    o_ref[...] += val

    in_specs = [pl.BlockSpec((tr, w), functools.partial(lambda i, cb: (i, cb), cb=cb)) for (_, w, cb) in row_in]
    in_specs += [pl.BlockSpec((None,) + v.shape[1:], functools.partial(lambda i, l: (l, 0, 0), l=l)) for (v, l) in vec_in]
    out_shape = [jax.ShapeDtypeStruct((rows, w), dt) for (w, dt) in row_out]
    out_shape += [jax.ShapeDtypeStruct(s, F32) for s in acc_out]
    out_specs = [pl.BlockSpec((tr, w), lambda i: (i, 0)) for (w, _) in row_out]
    out_specs += [pl.BlockSpec(s, lambda i: (0, 0)) for s in acc_out]
    res = pl.pallas_call(
        body,
        out_shape=out_shape,
        grid=(rows // tr,),
        in_specs=in_specs,
        out_specs=out_specs,
        compiler_params=_cparams(("arbitrary",) if acc_out else ("parallel",)),
        name=name,
    )(*[a for (a, _, _) in row_in], *[v for (v, _) in vec_in])
    return res


def _rms(x, g):
    r = lax.rsqrt(jnp.mean(x * x, axis=-1, keepdims=True) + RMS_EPS)
    return x * r * g


def _rms_bwd(x, g, dy):
    r = lax.rsqrt(jnp.mean(x * x, axis=-1, keepdims=True) + RMS_EPS)
    xh = x * r
    dxh = dy * g
    dx = r * (dxh - xh * jnp.mean(dxh * xh, axis=-1, keepdims=True))
    return dx, jnp.sum(dy * xh, axis=0, keepdims=True)


def _silu(x):
    return x * jax.nn.sigmoid(x)


def _silu_grad(x):
    s = jax.nn.sigmoid(x)
    return s * (1.0 + x * (1.0 - s))


def _softplus(x):
    return jnp.maximum(x, 0.0) + jnp.log1p(jnp.exp(-jnp.abs(x)))


def _full(a):
    return (a, a.shape[1], 0)


def _f_post_pre(x, ysub, g_post, g_pre):
    xn = x + _rms(ysub, g_post)
    return xn, _rms(xn, g_pre)


def _f_final(x, ysub, tgt, g_post):
    err = x + _rms(ysub, g_post) - tgt
    return err * (1.0 / D_MODEL), jnp.sum(err * err, axis=0, keepdims=True)


def _f_pre_post_bwd(xmid, dh, dxo, ysub, g_pre, g_post):
    d1, dg_pre = _rms_bwd(xmid, g_pre, dh)
    dxm = dxo + d1
    dys, dg_post = _rms_bwd(ysub, g_post, dxm)
    return dxm, dys, dg_pre, dg_post


def _f_pre_bwd(x, dh, dxo, g_pre):
    d1, dg_pre = _rms_bwd(x, g_pre, dh)
    return dxo + d1, dg_pre


def _f_gain_bwd(x, dy, g):
    return _rms_bwd(x, g, dy)[1]


def _group_norm_parts(u):
    gw = u.shape[1] // SSM_GROUPS
    parts = []
    for gi in range(SSM_GROUPS):
        ug = u[:, gi * gw:(gi + 1) * gw]
        r = lax.rsqrt(jnp.mean(ug * ug, axis=-1, keepdims=True) + RMS_EPS)
        parts.append((ug * r, r))
    return gw, parts


def _f_gate_norm(y, z, g):
    _, parts = _group_norm_parts(y * _silu(z))
    return jnp.concatenate([uh for uh, _ in parts], axis=1) * g


def _f_gate_norm_bwd(y, z, do, g):
    sz = _silu(z)
    gw, parts = _group_norm_parts(y * sz)
    dxh = do * g
    du = []
    for gi, (uh, r) in enumerate(parts):
        dg_ = dxh[:, gi * gw:(gi + 1) * gw]
        du.append(r * (dg_ - uh * jnp.mean(dg_ * uh, axis=-1, keepdims=True)))
    du = jnp.concatenate(du, axis=1)
    uh_all = jnp.concatenate([uh for uh, _ in parts], axis=1)
    return du * sz, du * y * _silu_grad(z), jnp.sum(do * uh_all, axis=0, keepdims=True)


def _f_merge(ga, gs, ba, bs):
    return jax.nn.sigmoid(ga) * ba + jax.nn.sigmoid(gs) * bs


def _f_merge_bwd(ga, gs, ba, bs, dm):
    sa, ss = jax.nn.sigmoid(ga), jax.nn.sigmoid(gs)
    dgg = jnp.concatenate([dm * ba * sa * (1.0 - sa), dm * bs * ss * (1.0 - ss)], axis=1)
    return dgg, dm * sa, dm * ss


def _f_swiglu(gate, up):
    return _silu(gate) * up


def _f_swiglu_bwd(gate, up, da):
    return jnp.concatenate([da * up * _silu_grad(gate), da * _silu(gate)], axis=1)


def _split_dot(x, u):
    hi = x.astype(BF16)
    lo = (x - hi.astype(F32)).astype(BF16)
    return jnp.dot(hi, u, preferred_element_type=F32) + jnp.dot(lo, u, preferred_element_type=F32)


SB_ROWS = 512
SB_UNROLL = 4
C00 = (((0,), (0,)), ((), ()))
C11 = (((1,), (1,)), ((), ()))


def _sb_setup(q_ref, tq):
    hp = LANES // SB_HEAD_DIM
    lane = lax.broadcasted_iota(jnp.int32, (1, LANES), 1)
    heads = [jnp.logical_and(lane >= h * SB_HEAD_DIM, lane < (h + 1) * SB_HEAD_DIM) for h in range(hp)]
    qs = q_ref[...] * (SB_HEAD_DIM ** -0.5)
    q_h = [jnp.where(hm, qs, 0.0).astype(MXU_DTYPE) for hm in heads]
    row = lax.broadcasted_iota(jnp.int32, (tq, SB_BLOCK), 0)
    col = lax.broadcasted_iota(jnp.int32, (tq, SB_BLOCK), 1)
    sq_row = lax.broadcasted_iota(jnp.int32, (SB_BLOCK, SB_BLOCK), 0)
    sq_col = lax.broadcasted_iota(jnp.int32, (SB_BLOCK, SB_BLOCK), 1)
    return heads, q_h, col - row, sq_row, sq_col


def _sb_scores(q, kj, mask):
    z = lax.dot_general(q, kj, C11, preferred_element_type=F32)
    lm = -(jnp.maximum(z, 0.0) + jnp.log(1.0 + jnp.exp(-jnp.abs(z))))
    return z, jnp.where(mask, lm, 0.0)


def _grid_step3(nb, ncb, nq):
    return (pl.program_id(0) * ncb + pl.program_id(1)) * nq + pl.program_id(2), nb * ncb * nq


def _sb_fwd(proj, cfg, nb, gather=()):
    blk = SB_BLOCK
    tq = _tile(SEQ, SB_ROWS, blk)
    nq = SEQ // tq
    kpq = tq // blk
    unr = math.gcd(kpq, SB_UNROLL)
    hp = LANES // SB_HEAD_DIM
    ncb = cfg.sbw // LANES
    qb, kb, vb = cfg.q0 // LANES, cfg.k0 // LANES, cfg.v0 // LANES
    ng = len(gather)

    def body(q_ref, k_ref, v_ref, *rest):
        o_ref = rest[ng]
        if ng:
            step_id, n_steps = _grid_step3(nb, ncb, nq)
            start, forward, finish = _gather_phases(rest[:ng], rest[ng + 1:2 * ng + 1], *rest[2 * ng + 1:])
            pl.when(step_id == 0)(start)
            pl.when(step_id == n_steps // 2)(forward)
        _sb_fwd_block(q_ref, k_ref, v_ref, o_ref)
        if ng:
            pl.when(step_id == n_steps - 1)(finish)

    def _sb_fwd_block(q_ref, k_ref, v_ref, o_ref):
        i = pl.program_id(2)
        heads, q_h, cmr, sq_row, sq_col = _sb_setup(q_ref, tq)
        u_rev = (sq_row >= sq_col).astype(BF16)

        def step(n, carry):
            acc, runs = carry
            runs = list(runs)
            for jj in range(unr):
                j = (i + 1) * kpq - 1 - (n * unr + jj)
                rows = pl.ds(pl.multiple_of(j * blk, blk), blk)
                kj = k_ref[rows, :].astype(MXU_DTYPE)
                vj = v_ref[rows, :].astype(MXU_DTYPE)
                mask = cmr < i * tq - j * blk
                for h in range(hp):
                    z, lm = _sb_scores(q_h[h], kj, mask)
                    cs = _split_dot(lm, u_rev)
                    w = jnp.where(mask, jnp.exp(z + cs + runs[h]), 0.0)
                    acc = acc + jnp.dot(w.astype(MXU_DTYPE), jnp.where(heads[h], vj, 0), preferred_element_type=F32)
                    runs[h] = runs[h] + cs[:, 0:1]
            return acc, tuple(runs)

        init = (jnp.zeros((tq, LANES), F32), tuple(jnp.zeros((tq, 1), F32) for _ in range(hp)))
        acc, _ = lax.fori_loop(0, (i + 1) * (kpq // unr), step, init)
        o_ref[...] = acc

    res = pl.pallas_call(
        body,
        out_shape=[jax.ShapeDtypeStruct((nb * SEQ, cfg.sbw), F32)] + _gather_shapes(gather),
        grid=(nb, ncb, nq),
        in_specs=[
            pl.BlockSpec((tq, LANES), lambda b, c, i: (b * nq + i, qb + c)),
            pl.BlockSpec((SEQ, LANES), lambda b, c, i: (b, kb + c)),
            pl.BlockSpec((SEQ, LANES), lambda b, c, i: (b, vb + c)),
        ] + [ANY_SPEC] * ng,
        out_specs=[pl.BlockSpec((tq, LANES), lambda b, c, i: (b * nq + i, c))] + [ANY_SPEC] * ng,
        scratch_shapes=_comm_sems(ng) if ng else [],
        compiler_params=_cparams(("arbitrary",) * 3 if ng else ("parallel", "parallel", "arbitrary")),
        name="sb_fwd_gather" if ng else "sb_fwd",
    )(proj, proj, proj, *gather)
    return res[0], res[1:]


def _sb_bwd(proj, do_att, cfg, nb, scatter=()):
    blk = SB_BLOCK
    tq = _tile(SEQ, SB_ROWS, blk)
    nq = SEQ // tq
    kpq = tq // blk
    unr = math.gcd(kpq, SB_UNROLL)
    hp = LANES // SB_HEAD_DIM
    ncb = cfg.sbw // LANES
    scale = SB_HEAD_DIM ** -0.5
    qb, kb, vb = cfg.q0 // LANES, cfg.k0 // LANES, cfg.v0 // LANES
    ns = len(scatter)

    def body(q_ref, k_ref, v_ref, do_ref, *rest):
        dq_ref, dk_ref, dv_ref = rest[ns:ns + 3]
        g_ref, z_ref = rest[2 * ns + 3:2 * ns + 5]
        if ns:
            step_id, n_steps = _grid_step3(nb, ncb, nq)
            start, finish = _scatter_phases(rest[:ns], rest[ns + 3:2 * ns + 3], *rest[2 * ns + 5:])
            pl.when(step_id == 0)(start)
        _sb_bwd_block(q_ref, k_ref, v_ref, do_ref, dq_ref, dk_ref, dv_ref, g_ref, z_ref)
        if ns:
            pl.when(step_id == n_steps - 1)(finish)

    def _sb_bwd_block(q_ref, k_ref, v_ref, do_ref, dq_ref, dk_ref, dv_ref, g_ref, z_ref):
        i = pl.program_id(2)

        @pl.when(i == 0)
        def _():
            dk_ref[...] = jnp.zeros_like(dk_ref)
            dv_ref[...] = jnp.zeros_like(dv_ref)

        heads, q_h, cmr, sq_row, sq_col = _sb_setup(q_ref, tq)
        u_rev = (sq_row >= sq_col).astype(BF16)
        u_fwd = (sq_row <= sq_col).astype(BF16)
        do = do_ref[...]
        do_h = [jnp.where(hm, do, 0.0).astype(MXU_DTYPE) for hm in heads]

        def sweep_left(n, runs):
            runs = list(runs)
            for jj in range(unr):
                j = (i + 1) * kpq - 1 - (n * unr + jj)
                rows = pl.ds(pl.multiple_of(j * blk, blk), blk)
                kj = k_ref[rows, :].astype(MXU_DTYPE)
                vj = v_ref[rows, :].astype(MXU_DTYPE)
                mask = cmr < i * tq - j * blk
                dv = jnp.zeros((blk, LANES), F32)
                for h in range(hp):
                    z, lm = _sb_scores(q_h[h], kj, mask)
                    cs = _split_dot(lm, u_rev)
                    a = jnp.where(mask, jnp.exp(z + cs + runs[h]), 0.0)
                    da = lax.dot_general(do_h[h], vj, C11, preferred_element_type=F32)
                    dv = dv + lax.dot_general(a.astype(MXU_DTYPE), do_h[h], C00, preferred_element_type=F32)
                    g_ref[h, j] = a * da
                    z_ref[h, j] = jax.nn.sigmoid(z)
                    runs[h] = runs[h] + cs[:, 0:1]
                dv_ref[rows, :] += dv
            return tuple(runs)

        trips = (i + 1) * (kpq // unr)
        lax.fori_loop(0, trips, sweep_left, tuple(jnp.zeros((tq, 1), F32) for _ in range(hp)))

        def sweep_right(n, carry):
            dq, runs = carry
            runs = list(runs)
            for jj in range(unr):
                j = n * unr + jj
                rows = pl.ds(pl.multiple_of(j * blk, blk), blk)
                kj = k_ref[rows, :].astype(MXU_DTYPE)
                mask = cmr < i * tq - j * blk
                dk = jnp.zeros((blk, LANES), F32)
                for h in range(hp):
                    g = g_ref[h, j]
                    g_upto = _split_dot(g, u_fwd) + runs[h]
                    dz = jnp.where(mask, g - z_ref[h, j] * g_upto, 0.0).astype(MXU_DTYPE)
                    dq = dq + jnp.dot(dz, jnp.where(heads[h], kj, 0), preferred_element_type=F32)
                    dk = dk + lax.dot_general(dz, q_h[h], C00, preferred_element_type=F32)
                    runs[h] = runs[h] + jnp.sum(g, axis=1, keepdims=True)
                dk_ref[rows, :] += dk
            return dq, tuple(runs)

        init = (jnp.zeros((tq, LANES), F32), tuple(jnp.zeros((tq, 1), F32) for _ in range(hp)))
        dq, _ = lax.fori_loop(0, trips, sweep_right, init)
        dq_ref[...] = (dq * scale).astype(dq_ref.dtype)

    kv_spec_out = pl.BlockSpec((SEQ, LANES), lambda b, c, i: (b, c))
    q_spec_out = pl.BlockSpec((tq, LANES), lambda b, c, i: (b * nq + i, c))
    res = pl.pallas_call(
        body,
        out_shape=[
            jax.ShapeDtypeStruct((nb * SEQ, cfg.sbw), ACT_DTYPE),
            jax.ShapeDtypeStruct((nb * SEQ, cfg.sbw), F32),
            jax.ShapeDtypeStruct((nb * SEQ, cfg.sbw), F32),
        ] + [jax.ShapeDtypeStruct(a.shape, a.dtype) for a in scatter],
        grid=(nb, ncb, nq),
        in_specs=[
            pl.BlockSpec((tq, LANES), lambda b, c, i: (b * nq + i, qb + c)),
            pl.BlockSpec((SEQ, LANES), lambda b, c, i: (b, kb + c)),
            pl.BlockSpec((SEQ, LANES), lambda b, c, i: (b, vb + c)),
            q_spec_out,
        ] + [ANY_SPEC] * ns,
        out_specs=[q_spec_out, kv_spec_out, kv_spec_out] + [ANY_SPEC] * ns,
        scratch_shapes=[pltpu.VMEM((hp, SEQ // blk, tq, blk), F32), pltpu.VMEM((hp, SEQ // blk, tq, blk), F32)]
        + (_comm_sems(ns) if ns else []),
        compiler_params=_cparams(("arbitrary",) * 3 if ns else ("parallel", "parallel", "arbitrary")),
        name="sb_bwd_scatter" if ns else "sb_bwd",
    )(proj, proj, proj, do_att, *scatter)
    return res[0], res[1], res[2], res[3:]


CONV_COLS = 256


def _conv_pre(x, w, b, t):
    kw = SSM_CONV
    shifted = []
    pre = b + w[kw - 1:kw, :] * x
    for k in range(kw - 1):
        d = kw - 1 - k
        xs = jnp.where(t >= d, pltpu.roll(x, d, 0), 0.0)
        shifted.append(xs)
        pre = pre + w[k:k + 1, :] * xs
    shifted.append(x)
    return pre, shifted


def _conv_fwd(proj, conv_w, conv_b, l, cfg, nb):
    cw = CONV_COLS
    ncb = cfg.conv_dim // cw
    xb = cfg.xbc0 // cw

    def body(x_ref, w_ref, b_ref, o_ref):
        x = x_ref[...]
        t = lax.broadcasted_iota(jnp.int32, x.shape, 0)
        pre, _ = _conv_pre(x, w_ref[...], b_ref[...], t)
        o_ref[...] = _silu(pre)

    return pl.pallas_call(
        body,
        out_shape=jax.ShapeDtypeStruct((nb * SEQ, cfg.conv_dim), F32),
        grid=(ncb, nb),
        in_specs=[
            pl.BlockSpec((SEQ, cw), lambda j, b: (b, xb + j)),
            pl.BlockSpec((None, SSM_CONV, cw), lambda j, b: (0, 0, j)),
            pl.BlockSpec((None, 1, cw), lambda j, b: (l, 0, j)),
        ],
        out_specs=pl.BlockSpec((SEQ, cw), lambda j, b: (b, j)),
        compiler_params=_cparams(("parallel", "parallel")),
        name="conv_fwd",
    )(proj, conv_w, conv_b)


def _conv_bwd(proj, dact, conv_w, conv_b, l, cfg, nb):
    cw = CONV_COLS
    ncb = cfg.conv_dim // cw
    xb = cfg.xbc0 // cw
    kw = SSM_CONV

    def body(x_ref, da_ref, w_ref, b_ref, dx_ref, dw_ref, db_ref):
        b_id = pl.program_id(1)
        x = x_ref[...]
        w = w_ref[...]
        t = lax.broadcasted_iota(jnp.int32, x.shape, 0)
        pre, shifted = _conv_pre(x, w, b_ref[...], t)
        dpre = da_ref[...] * _silu_grad(pre)
        dx = w[kw - 1:kw, :] * dpre
        for k in range(kw - 1):
            d = kw - 1 - k
            dx = dx + w[k:k + 1, :] * jnp.where(t < SEQ - d, pltpu.roll(dpre, SEQ - d, 0), 0.0)
        dx_ref[...] = dx.astype(dx_ref.dtype)
        dw = jnp.concatenate([jnp.sum(dpre * s, axis=0, keepdims=True) for s in shifted], axis=0)
        db = jnp.sum(dpre, axis=0, keepdims=True)

        @pl.when(b_id == 0)
        def _():
            dw_ref[...] = dw
            db_ref[...] = db

        @pl.when(b_id > 0)
        def _():
            dw_ref[...] += dw
            db_ref[...] += db

    return pl.pallas_call(
        body,
        out_shape=[
            jax.ShapeDtypeStruct((nb * SEQ, cfg.conv_dim), ACT_DTYPE),
            jax.ShapeDtypeStruct((kw, cfg.conv_dim), F32),
            jax.ShapeDtypeStruct((1, cfg.conv_dim), F32),
        ],
        grid=(ncb, nb),
        in_specs=[
            pl.BlockSpec((SEQ, cw), lambda j, b: (b, xb + j)),
            pl.BlockSpec((SEQ, cw), lambda j, b: (b, j)),
            pl.BlockSpec((None, kw, cw), lambda j, b: (0, 0, j)),
            pl.BlockSpec((None, 1, cw), lambda j, b: (l, 0, j)),
        ],
        out_specs=[
            pl.BlockSpec((SEQ, cw), lambda j, b: (b, j)),
            pl.BlockSpec((kw, cw), lambda j, b: (0, j)),
            pl.BlockSpec((1, cw), lambda j, b: (0, j)),
        ],
        compiler_params=_cparams(("parallel", "arbitrary")),
        name="conv_bwd",
    )(proj, dact, conv_w, conv_b)


def _ssd_common(dt_raw, dt_bias, a_log, tri):
    ln = SSM_CHUNK
    dt = _softplus(dt_raw + dt_bias)
    a = -jnp.exp(a_log)
    a_cs = jnp.dot(tri, dt * a, preferred_element_type=F32, precision=lax.Precision.HIGHEST)
    a_last = a_cs[ln - 1:ln, :]
    return dt, a, a_cs, a_cs.T, jnp.exp(a_cs), jnp.exp(a_last - a_cs), jnp.exp(a_last)


def _ssd_head_v1(h, xs, dt, a_cs, a_t, cb, tril):
    p = SSM_HEAD_DIM
    x_h = xs[:, h * p:(h + 1) * p]
    xd = x_h * dt[:, h:h + 1]
    lmat = jnp.exp(jnp.where(tril, a_cs[:, h:h + 1] - a_t[h:h + 1, :], -jnp.inf))
    return x_h, xd, lmat


def _ssd_specs(cfg, nc, rev):
    ln = SSM_CHUNK
    cidx = (lambda c: nc - 1 - c) if rev else (lambda c: c)
    bmb = cfg.inner // cfg.gn
    return [
        pl.BlockSpec((ln, cfg.inner), lambda b, c: (b * nc + cidx(c), 0)),
        pl.BlockSpec((ln, cfg.gn), lambda b, c: (b * nc + cidx(c), bmb)),
        pl.BlockSpec((ln, cfg.gn), lambda b, c: (b * nc + cidx(c), bmb + 1)),
        pl.BlockSpec((ln, LANES), lambda b, c: (b * nc + cidx(c), cfg.dt0 // LANES)),
    ]


def _ssd_fwd_v1(xbc, proj, dt_bias, a_log, d_skip, l, cfg, nb):
    ln, p, n = SSM_CHUNK, SSM_HEAD_DIM, SSM_STATE
    nc = SEQ // ln
    g_, e_ = SSM_GROUPS, cfg.epg
    assert cfg.inner % cfg.gn == 0

    def body(xs_ref, bm_ref, cm_ref, dtr_ref, bias_ref, alog_ref, dsk_ref, y_ref, st_ref, s_ref):
        c = pl.program_id(1)

        @pl.when(c == 0)
        def _():
            s_ref[...] = jnp.zeros_like(s_ref)

        st_ref[...] = s_ref[...]
        row = lax.broadcasted_iota(jnp.int32, (ln, ln), 0)
        col = lax.broadcasted_iota(jnp.int32, (ln, ln), 1)
        tril = row >= col
        dt, _, a_cs, a_t, e_a, dte, cd = _ssd_common(dtr_ref[...], bias_ref[...], alog_ref[...], tril.astype(F32))
        dsk = dsk_ref[...]
        xs = xs_ref[...]
        for g in range(g_):
            bm = bm_ref[:, g * n:(g + 1) * n].astype(MXU_DTYPE)
            cm = cm_ref[:, g * n:(g + 1) * n].astype(MXU_DTYPE)
            cb = lax.dot_general(cm, bm, (((1,), (1,)), ((), ())), preferred_element_type=F32)
            for e in range(e_):
                h = g * e_ + e
                x_h, xd, lmat = _ssd_head_v1(h, xs, dt, a_cs, a_t, cb, tril)
                s_prev = s_ref[g * n:(g + 1) * n, e * p:(e + 1) * p]
                y = jnp.dot((cb * lmat).astype(MXU_DTYPE), xd.astype(MXU_DTYPE), preferred_element_type=F32)
                y = y + jnp.dot(cm, s_prev.astype(MXU_DTYPE), preferred_element_type=F32) * e_a[:, h:h + 1]
                y_ref[:, h * p:(h + 1) * p] = y + dsk[:, h:h + 1] * x_h
                upd = lax.dot_general(bm, (xd * dte[:, h:h + 1]).astype(MXU_DTYPE), (((0,), (0,)), ((), ())),
                                      preferred_element_type=F32)
                s_ref[g * n:(g + 1) * n, e * p:(e + 1) * p] = cd[:, h:h + 1] * s_prev + upd

    vec = lambda b, c: (l, 0, 0)
    return pl.pallas_call(
        body,
        out_shape=[
            jax.ShapeDtypeStruct((nb * SEQ, cfg.inner), F32),
            jax.ShapeDtypeStruct((nb * nc * g_ * n, e_ * p), F32),
        ],
        grid=(nb, nc),
        in_specs=_ssd_specs(cfg, nc, False) + [pl.BlockSpec((None, 1, LANES), vec)] * 3,
        out_specs=[
            pl.BlockSpec((ln, cfg.inner), lambda b, c: (b * nc + c, 0)),
            pl.BlockSpec((g_ * n, e_ * p), lambda b, c: (b * nc + c, 0)),
        ],
        scratch_shapes=[pltpu.VMEM((g_ * n, e_ * p), F32)],
        compiler_params=_cparams(("parallel", "arbitrary")),
        name="ssd_fwd",
    )(xbc, xbc, xbc, proj, dt_bias, a_log, d_skip)


def _ssd_bwd_v1(xbc, proj, states, dy, dt_bias, a_log, d_skip, l, cfg, nb):
    ln, p, n = SSM_CHUNK, SSM_HEAD_DIM, SSM_STATE
    nc = SEQ // ln
    g_, e_ = SSM_GROUPS, cfg.epg
    c00 = (((0,), (0,)), ((), ()))
    c11 = (((1,), (1,)), ((), ()))

    def body(xs_ref, bm_ref, cm_ref, dtr_ref, st_ref, dy_ref, bias_ref, alog_ref, dsk_ref,
             dxbc_ref, ddt_ref, dvec_ref, ds_ref):
        first = jnp.logical_and(pl.program_id(0) == 0, pl.program_id(1) == 0)

        @pl.when(pl.program_id(1) == 0)
        def _():
            ds_ref[...] = jnp.zeros_like(ds_ref)

        row = lax.broadcasted_iota(jnp.int32, (ln, ln), 0)
        col = lax.broadcasted_iota(jnp.int32, (ln, ln), 1)
        tril = row >= col
        tri_f = tril.astype(F32)
        dtr = dtr_ref[...]
        bias = bias_ref[...]
        dt, a, a_cs, a_t, e_a, dte, cd = _ssd_common(dtr, bias, alog_ref[...], tri_f)
        dsk = dsk_ref[...]
        xs = xs_ref[...]
        lane = lax.broadcasted_iota(jnp.int32, (1, LANES), 1)
        sub = lax.broadcasted_iota(jnp.int32, (LANES, 1), 0)
        da_col = jnp.zeros((ln, LANES), F32)
        da_row_t = jnp.zeros((LANES, ln), F32)
        ddt = jnp.zeros((ln, LANES), F32)
        da_last = jnp.zeros((1, LANES), F32)
        ddsk = jnp.zeros((1, LANES), F32)
        for g in range(g_):
            bm = bm_ref[:, g * n:(g + 1) * n].astype(MXU_DTYPE)
            cm = cm_ref[:, g * n:(g + 1) * n].astype(MXU_DTYPE)
            cb = lax.dot_general(cm, bm, c11, preferred_element_type=F32)
            dbm = jnp.zeros((ln, n), F32)
            dcm = jnp.zeros((ln, n), F32)
            for e in range(e_):
                h = g * e_ + e
                hs = slice(h * p, (h + 1) * p)
                oh = (lane == h).astype(F32)
                x_h, xd, lmat = _ssd_head_v1(h, xs, dt, a_cs, a_t, cb, tril)
                d_y = dy_ref[:, hs]
                s_prev = st_ref[g * n:(g + 1) * n, e * p:(e + 1) * p]
                d_s = ds_ref[g * n:(g + 1) * n, e * p:(e + 1) * p]
                dy_m = d_y.astype(MXU_DTYPE)
                xd_m = xd.astype(MXU_DTYPE)
                sp_m = s_prev.astype(MXU_DTYPE)
                ds_m = d_s.astype(MXU_DTYPE)
                e_a_h, dte_h, cd_h = e_a[:, h:h + 1], dte[:, h:h + 1], cd[:, h:h + 1]
                m_mat = (cb * lmat).astype(MXU_DTYPE)
                bds = jnp.dot(bm, ds_m, preferred_element_type=F32)
                d_xd = lax.dot_general(m_mat, dy_m, c00, preferred_element_type=F32) + dte_h * bds
                d_m = lax.dot_general(dy_m, xd_m, c11, preferred_element_type=F32)
                d_cb = (d_m * lmat).astype(MXU_DTYPE)
                w_mat = d_m * cb * lmat
                dy_e = (d_y * e_a_h).astype(MXU_DTYPE)
                xd_e = (xd * dte_h).astype(MXU_DTYPE)
                dcm = dcm + jnp.dot(d_cb, bm, preferred_element_type=F32)
                dcm = dcm + lax.dot_general(dy_e, sp_m, c11, preferred_element_type=F32)
                dbm = dbm + lax.dot_general(d_cb, cm, c00, preferred_element_type=F32)
                dbm = dbm + lax.dot_general(xd_e, ds_m, c11, preferred_element_type=F32)
                ds_ref[g * n:(g + 1) * n, e * p:(e + 1) * p] = (
                    cd_h * d_s + lax.dot_general(cm, dy_e, c00, preferred_element_type=F32))
                y_off = jnp.dot(cm, sp_m, preferred_element_type=F32) * e_a_h
                q_h = jnp.sum(bds * xd, axis=1, keepdims=True) * dte_h
                da_col_h = (jnp.sum(w_mat, axis=1, keepdims=True)
                            + jnp.sum(d_y * y_off, axis=1, keepdims=True) - q_h)
                da_col = da_col + da_col_h * oh
                da_row_t = da_row_t - (sub == h).astype(F32) * jnp.sum(w_mat, axis=0, keepdims=True)
                da_last = da_last + (jnp.sum(q_h, keepdims=True) + cd_h * jnp.sum(d_s * s_prev, keepdims=True)) * oh
                dxbc_ref[:, hs] = d_xd * dt[:, h:h + 1] + dsk[:, h:h + 1] * d_y
                ddt = ddt + jnp.sum(d_xd * x_h, axis=1, keepdims=True) * oh
                ddsk = ddsk + jnp.sum(d_y * x_h, keepdims=True) * oh
            dxbc_ref[:, cfg.inner + g * n:cfg.inner + (g + 1) * n] = dbm
            dxbc_ref[:, cfg.inner + cfg.gn + g * n:cfg.inner + cfg.gn + (g + 1) * n] = dcm
        d_acs = da_col + da_row_t.T + jnp.where(row[:, 0:1] == ln - 1, da_last, 0.0)
        da_dt = lax.dot_general(tri_f, d_acs, c00, preferred_element_type=F32, precision=lax.Precision.HIGHEST)
        ddt = ddt + da_dt * a
        ddt_raw = ddt * jax.nn.sigmoid(dtr + bias)
        ddt_ref[...] = ddt_raw.astype(ddt_ref.dtype)
        da_log = jnp.sum(da_dt * dt, axis=0, keepdims=True) * a
        dvec = jnp.concatenate([jnp.sum(ddt_raw, axis=0, keepdims=True), da_log, ddsk,
                                jnp.zeros((5, LANES), F32)], axis=0)

        @pl.when(first)
        def _():
            dvec_ref[...] = dvec

        @pl.when(jnp.logical_not(first))
        def _():
            dvec_ref[...] += dvec

    vec = lambda b, c: (l, 0, 0)
    rblk = lambda b, c: (b * nc + nc - 1 - c, 0)
    return pl.pallas_call(
        body,
        out_shape=[
            jax.ShapeDtypeStruct((nb * SEQ, cfg.conv_dim), F32),
            jax.ShapeDtypeStruct((nb * SEQ, LANES), ACT_DTYPE),
            jax.ShapeDtypeStruct((8, LANES), F32),
        ],
        grid=(nb, nc),
        in_specs=_ssd_specs(cfg, nc, True) + [
            pl.BlockSpec((g_ * n, e_ * p), rblk),
            pl.BlockSpec((ln, cfg.inner), rblk),
        ] + [pl.BlockSpec((None, 1, LANES), vec)] * 3,
        out_specs=[
            pl.BlockSpec((ln, cfg.conv_dim), rblk),
            pl.BlockSpec((ln, LANES), rblk),
            pl.BlockSpec((8, LANES), lambda b, c: (0, 0)),
        ],
        scratch_shapes=[pltpu.VMEM((g_ * n, e_ * p), F32)],
        compiler_params=_cparams(("arbitrary", "arbitrary")),
        name="ssd_bwd",
    )(xbc, xbc, xbc, proj, states, dy, dt_bias, a_log, d_skip)


def _split3_dot(x, u):
    hi = x.astype(BF16)
    r1 = x - hi.astype(F32)
    mid = r1.astype(BF16)
    lo = (r1 - mid.astype(F32)).astype(BF16)
    dot = lambda a: jnp.dot(a, u, preferred_element_type=F32)
    return dot(hi) + dot(mid) + dot(lo)


def _ssd_consts(cfg):
    p = SSM_HEAD_DIM
    hrow = jnp.arange(LANES)[:, None]
    spread = (jnp.arange(cfg.inner)[None, :] // p == hrow).astype(BF16)
    spread_tile = (jnp.arange(cfg.heads * LANES)[None, :] // LANES == hrow).astype(BF16)
    return spread, spread_tile, spread.T


def _ssd_chunk_terms(dtr, bias, alog, spread, spread_tile):
    ln = SSM_CHUNK
    row = lax.broadcasted_iota(jnp.int32, (ln, ln), 0)
    col = lax.broadcasted_iota(jnp.int32, (ln, ln), 1)
    tril = row >= col
    dt, a, a_cs, a_t, e_a, dte, cd = _ssd_common(dtr, bias, alog, tril.astype(F32))
    ex = lambda v: _split_dot(v, spread)
    cd_x = ex(jnp.broadcast_to(cd, (8, LANES)))[0:1]
    colb = _split3_dot(a_cs, spread_tile)
    return dict(tril=tril, row=row, col=col, dt=dt, a=a, a_cs=a_cs, a_t=a_t, e_a=e_a, dte=dte, cd=cd,
                dt_x=ex(dt), ea_x=ex(e_a), dte_x=ex(dte), cd_x=cd_x, colb=colb)


def _head_masks():
    lane = lax.broadcasted_iota(jnp.int32, (1, LANES), 1)
    hpt = LANES // SSM_HEAD_DIM
    return [jnp.logical_and(lane >= i * SSM_HEAD_DIM, lane < (i + 1) * SSM_HEAD_DIM) for i in range(hpt)]


def _ssd_fwd(xbc, proj, dt_bias, a_log, d_skip_x, l, cfg, nb):
    ln, p, n = SSM_CHUNK, SSM_HEAD_DIM, SSM_STATE
    nc = SEQ // ln
    g_, e_ = SSM_GROUPS, cfg.epg
    gw = e_ * p
    hpt = LANES // p
    assert cfg.inner % cfg.gn == 0 and gw % LANES == 0
    spread, spread_tile, _ = _ssd_consts(cfg)

    def body(xs_ref, bm_ref, cm_ref, dtr_ref, bias_ref, alog_ref, dskx_ref, sp_ref, spt_ref, y_ref, st_ref, s_ref):
        c = pl.program_id(1)

        @pl.when(c == 0)
        def _():
            s_ref[...] = jnp.zeros_like(s_ref)

        st_ref[...] = s_ref[...]
        t = _ssd_chunk_terms(dtr_ref[...], bias_ref[...], alog_ref[...], sp_ref[...], spt_ref[...])
        hm = _head_masks()
        xs = xs_ref[...]
        xd = xs * t["dt_x"]
        xde = (xd * t["dte_x"]).astype(MXU_DTYPE)
        for g in range(g_):
            gc = slice(g * gw, (g + 1) * gw)
            bm = bm_ref[:, g * n:(g + 1) * n].astype(MXU_DTYPE)
            cm = cm_ref[:, g * n:(g + 1) * n].astype(MXU_DTYPE)
            cb = lax.dot_general(cm, bm, C11, preferred_element_type=F32)
            sg = s_ref[g * n:(g + 1) * n, :]
            y_off = jnp.dot(cm, sg.astype(MXU_DTYPE), preferred_element_type=F32) * t["ea_x"][:, gc]
            s_ref[g * n:(g + 1) * n, :] = t["cd_x"][:, gc] * sg + lax.dot_general(
                bm, xde[:, gc], C00, preferred_element_type=F32)
            for k in range(gw // LANES):
                lanes = slice(g * gw + k * LANES, g * gw + (k + 1) * LANES)
                xp = xd[:, lanes]
                acc = y_off[:, k * LANES:(k + 1) * LANES] + dskx_ref[:, lanes] * xs[:, lanes]
                for i in range(hpt):
                    h = (g * gw + k * LANES) // p + i
                    lmat = jnp.exp(jnp.where(t["tril"], t["colb"][:, h * LANES:(h + 1) * LANES] - t["a_t"][h:h + 1, :],
                                             -jnp.inf))
                    acc = acc + jnp.dot((cb * lmat).astype(MXU_DTYPE),
                                        jnp.where(hm[i], xp, 0.0).astype(MXU_DTYPE), preferred_element_type=F32)
                y_ref[:, lanes] = acc

    vec = lambda b, c: (l, 0, 0)
    whole = lambda a: pl.BlockSpec(a.shape, lambda b, c: (0, 0))
    return pl.pallas_call(
        body,
        out_shape=[
            jax.ShapeDtypeStruct((nb * SEQ, cfg.inner), F32),
            jax.ShapeDtypeStruct((nb * nc * g_ * n, gw), F32),
        ],
        grid=(nb, nc),
        in_specs=_ssd_specs(cfg, nc, False) + [pl.BlockSpec((None, 1, LANES), vec)] * 2
        + [pl.BlockSpec((None, 1, cfg.inner), vec), whole(spread), whole(spread_tile)],
        out_specs=[
            pl.BlockSpec((ln, cfg.inner), lambda b, c: (b * nc + c, 0)),
            pl.BlockSpec((g_ * n, gw), lambda b, c: (b * nc + c, 0)),
        ],
        scratch_shapes=[pltpu.VMEM((g_ * n, gw), F32)],
        compiler_params=_cparams(("parallel", "arbitrary")),
        name="ssd_fwd",
    )(xbc, xbc, xbc, proj, dt_bias, a_log, d_skip_x, spread, spread_tile)


def _ssd_bwd(xbc, proj, states, dy, dt_bias, a_log, d_skip_x, l, cfg, nb):
    ln, p, n = SSM_CHUNK, SSM_HEAD_DIM, SSM_STATE
    nc = SEQ // ln
    g_, e_ = SSM_GROUPS, cfg.epg
    gw = e_ * p
    hpt = LANES // p
    spread, spread_tile, gather_t = _ssd_consts(cfg)

    def body(xs_ref, bm_ref, cm_ref, dtr_ref, st_ref, dy_ref, bias_ref, alog_ref, dskx_ref, sp_ref, spt_ref, gt_ref,
             dxbc_ref, ddt_ref, dvec_ref, ds_ref, r1_ref, r2_ref, r4_ref, ss_ref):
        first = jnp.logical_and(pl.program_id(0) == 0, pl.program_id(1) == 0)

        @pl.when(pl.program_id(1) == 0)
        def _():
            ds_ref[...] = jnp.zeros_like(ds_ref)

        dtr = dtr_ref[...]
        bias = bias_ref[...]
        t = _ssd_chunk_terms(dtr, bias, alog_ref[...], sp_ref[...], spt_ref[...])
        tril = t["tril"]
        triu = t["row"] <= t["col"]
        hm = _head_masks()
        lane = lax.broadcasted_iota(jnp.int32, (1, LANES), 1)
        sub = lax.broadcasted_iota(jnp.int32, (LANES, 1), 0)
        xs = xs_ref[...]
        dyv = dy_ref[...]
        xd = xs * t["dt_x"]
        xde = xd * t["dte_x"]
        xde_m = xde.astype(MXU_DTYPE)
        dye_m = (dyv * t["ea_x"]).astype(MXU_DTYPE)
        da_col = jnp.zeros((ln, LANES), F32)
        da_row_t = jnp.zeros((LANES, ln), F32)
        ss_ref[...] = jnp.zeros_like(ss_ref)
        for g in range(g_):
            gc = slice(g * gw, (g + 1) * gw)
            gr = slice(g * n, (g + 1) * n)
            bm = bm_ref[:, gr].astype(MXU_DTYPE)
            cm = cm_ref[:, gr].astype(MXU_DTYPE)
            cb = lax.dot_general(cm, bm, C11, preferred_element_type=F32)
            cb_t = lax.dot_general(bm, cm, C11, preferred_element_type=F32)
            sp = st_ref[gr, :]
            dsg = ds_ref[gr, :]
            sp_m = sp.astype(MXU_DTYPE)
            dsg_m = dsg.astype(MXU_DTYPE)
            bds = jnp.dot(bm, dsg_m, preferred_element_type=F32)
            y_off = jnp.dot(cm, sp_m, preferred_element_type=F32) * t["ea_x"][:, gc]
            dcm = lax.dot_general(dye_m[:, gc], sp_m, C11, preferred_element_type=F32)
            dbm = lax.dot_general(xde_m[:, gc], dsg_m, C11, preferred_element_type=F32)
            ds_ref[gr, :] = t["cd_x"][:, gc] * dsg + lax.dot_general(cm, dye_m[:, gc], C00, preferred_element_type=F32)
            r4 = bds * xde[:, gc]
            r4_ref[:, gc] = r4
            r1_ref[:, gc] = dyv[:, gc] * y_off - r4
            ss_ref[0:1, gc] = jnp.sum(dsg * sp, axis=0, keepdims=True)
            dcb = jnp.zeros((ln, ln), F32)
            for k in range(gw // LANES):
                lanes = slice(g * gw + k * LANES, g * gw + (k + 1) * LANES)
                xp = xd[:, lanes]
                xp_m = xp.astype(MXU_DTYPE)
                dyp = dyv[:, lanes]
                dxp = t["dte_x"][:, lanes] * bds[:, k * LANES:(k + 1) * LANES]
                for i in range(hpt):
                    h = (g * gw + k * LANES) // p + i
                    diff = t["colb"][:, h * LANES:(h + 1) * LANES] - t["a_t"][h:h + 1, :]
                    lmat = jnp.exp(jnp.where(tril, diff, -jnp.inf))
                    lmat_t = jnp.exp(jnp.where(triu, -diff, -jnp.inf))
                    dy_h = jnp.where(hm[i], dyp, 0.0).astype(MXU_DTYPE)
                    d_ml = lax.dot_general(dy_h, xp_m, C11, preferred_element_type=F32) * lmat
                    dcb = dcb + d_ml
                    w_mat = d_ml * cb
                    dxp = dxp + jnp.dot((cb_t * lmat_t).astype(MXU_DTYPE), dy_h, preferred_element_type=F32)
                    da_col = da_col + jnp.sum(w_mat, axis=1, keepdims=True) * (lane == h).astype(F32)
                    da_row_t = da_row_t - (sub == h).astype(F32) * jnp.sum(w_mat, axis=0, keepdims=True)
                dxbc_ref[:, lanes] = dxp * t["dt_x"][:, lanes] + dskx_ref[:, lanes] * dyp
                r2_ref[:, lanes] = dxp * xs[:, lanes]
            dcb_m = dcb.astype(MXU_DTYPE)
            dxbc_ref[:, cfg.inner + g * n:cfg.inner + (g + 1) * n] = dbm + lax.dot_general(
                dcb_m, cm, C00, preferred_element_type=F32)
            dxbc_ref[:, cfg.inner + cfg.gn + g * n:cfg.inner + cfg.gn + (g + 1) * n] = dcm + jnp.dot(
                dcb_m, bm, preferred_element_type=F32)
        gt = gt_ref[...]
        rd = lambda v: _split_dot(v, gt)
        red4 = rd(r4_ref[...])
        da_last = jnp.sum(red4, axis=0, keepdims=True) + t["cd"] * rd(ss_ref[...])[0:1]
        d_acs = da_col + rd(r1_ref[...]) + da_row_t.T + jnp.where(t["row"][:, 0:1] == ln - 1, da_last, 0.0)
        da_dt = lax.dot_general(tril.astype(F32), d_acs, C00, preferred_element_type=F32,
                                precision=lax.Precision.HIGHEST)
        ddt = rd(r2_ref[...]) + da_dt * t["a"]
        ddt_raw = ddt * jax.nn.sigmoid(dtr + bias)
        ddt_ref[...] = ddt_raw.astype(ddt_ref.dtype)
        da_log = jnp.sum(da_dt * t["dt"], axis=0, keepdims=True) * t["a"]
        ddsk = jnp.sum(rd(dyv * xs), axis=0, keepdims=True)
        dvec = jnp.concatenate([jnp.sum(ddt_raw, axis=0, keepdims=True), da_log, ddsk,
                                jnp.zeros((5, LANES), F32)], axis=0)

        @pl.when(first)
        def _():
            dvec_ref[...] = dvec

        @pl.when(jnp.logical_not(first))
        def _():
            dvec_ref[...] += dvec

    vec = lambda b, c: (l, 0, 0)
    rblk = lambda b, c: (b * nc + nc - 1 - c, 0)
    whole = lambda a: pl.BlockSpec(a.shape, lambda b, c: (0, 0))
    return pl.pallas_call(
        body,
        out_shape=[
            jax.ShapeDtypeStruct((nb * SEQ, cfg.conv_dim), F32),
            jax.ShapeDtypeStruct((nb * SEQ, LANES), ACT_DTYPE),
            jax.ShapeDtypeStruct((8, LANES), F32),
        ],
        grid=(nb, nc),
        in_specs=_ssd_specs(cfg, nc, True) + [
            pl.BlockSpec((g_ * n, gw), rblk),
            pl.BlockSpec((ln, cfg.inner), rblk),
        ] + [pl.BlockSpec((None, 1, LANES), vec)] * 2 + [pl.BlockSpec((None, 1, cfg.inner), vec),
                                                           whole(spread), whole(spread_tile), whole(gather_t)],
        out_specs=[
            pl.BlockSpec((ln, cfg.conv_dim), rblk),
            pl.BlockSpec((ln, LANES), rblk),
            pl.BlockSpec((8, LANES), lambda b, c: (0, 0)),
        ],
        scratch_shapes=[pltpu.VMEM((g_ * n, gw), F32)] + [pltpu.VMEM((ln, cfg.inner), F32)] * 3
        + [pltpu.VMEM((8, cfg.inner), F32)],
        compiler_params=_cparams(("arbitrary", "arbitrary")),
        name="ssd_bwd",
    )(xbc, xbc, xbc, proj, states, dy, dt_bias, a_log, d_skip_x, spread, spread_tile, gather_t)


XA_ROWS = 256


def _xa_probs(q_ref, kv_ref, h, dh):
    c11 = (((1,), (1,)), ((), ()))
    qh = q_ref[:, h * dh:(h + 1) * dh].astype(MXU_DTYPE)
    kh = kv_ref[:, h * dh:(h + 1) * dh].astype(MXU_DTYPE)
    vh = kv_ref[:, D_MODEL + h * dh:D_MODEL + (h + 1) * dh].astype(MXU_DTYPE)
    s = lax.dot_general(qh, kh, c11, preferred_element_type=F32) * (dh ** -0.5)
    s = s - jnp.max(s, axis=1, keepdims=True)
    pr = jnp.exp(s)
    return qh, kh, vh, pr / jnp.sum(pr, axis=1, keepdims=True)


def _xa_fwd(q, kv, cfg, nb):
    tq = _tile(SEQ, XA_ROWS, 16)
    nq = SEQ // tq
    dh = cfg.xa_dim

    def body(q_ref, kv_ref, o_ref):
        for h in range(XA_HEADS):
            _, _, vh, pr = _xa_probs(q_ref, kv_ref, h, dh)
            o_ref[:, h * dh:(h + 1) * dh] = jnp.dot(pr.astype(MXU_DTYPE), vh, preferred_element_type=F32).astype(o_ref.dtype)

    return pl.pallas_call(
        body,
        out_shape=jax.ShapeDtypeStruct((nb * SEQ, D_MODEL), ACT_DTYPE),
        grid=(nb, nq),
        in_specs=[
            pl.BlockSpec((tq, D_MODEL), lambda b, i: (b * nq + i, 0)),
            pl.BlockSpec((MEM_LEN, 2 * D_MODEL), lambda b, i: (b, 0)),
        ],
        out_specs=pl.BlockSpec((tq, D_MODEL), lambda b, i: (b * nq + i, 0)),
        compiler_params=_cparams(("parallel", "parallel")),
        name="xa_fwd",
    )(q, kv)


def _xa_bwd(q, kv, do, cfg, nb):
    tq = _tile(SEQ, XA_ROWS, 16)
    nq = SEQ // tq
    dh = cfg.xa_dim
    c00 = (((0,), (0,)), ((), ()))
    c11 = (((1,), (1,)), ((), ()))
    scale = dh ** -0.5

    def body(q_ref, kv_ref, do_ref, dq_ref, dkv_ref, acc_ref):
        i = pl.program_id(1)

        @pl.when(i == 0)
        def _():
            acc_ref[...] = jnp.zeros_like(acc_ref)

        for h in range(XA_HEADS):
            hs = slice(h * dh, (h + 1) * dh)
            vs = slice(D_MODEL + h * dh, D_MODEL + (h + 1) * dh)
            qh, kh, vh, pr = _xa_probs(q_ref, kv_ref, h, dh)
            do_h = do_ref[:, hs].astype(MXU_DTYPE)
            dp = lax.dot_general(do_h, vh, c11, preferred_element_type=F32)
            ds = (pr * (dp - jnp.sum(dp * pr, axis=1, keepdims=True))).astype(MXU_DTYPE)
            dq_ref[:, hs] = (jnp.dot(ds, kh, preferred_element_type=F32) * scale).astype(dq_ref.dtype)
            acc_ref[:, hs] += lax.dot_general(ds, qh, c00, preferred_element_type=F32) * scale
            acc_ref[:, vs] += lax.dot_general(pr.astype(MXU_DTYPE), do_h, c00, preferred_element_type=F32)

        @pl.when(i == nq - 1)
        def _():
            dkv_ref[...] = acc_ref[...].astype(dkv_ref.dtype)

    return pl.pallas_call(
        body,
        out_shape=[
            jax.ShapeDtypeStruct((nb * SEQ, D_MODEL), ACT_DTYPE),
            jax.ShapeDtypeStruct((nb * MEM_LEN, 2 * D_MODEL), ACT_DTYPE),
        ],
        grid=(nb, nq),
        in_specs=[
            pl.BlockSpec((tq, D_MODEL), lambda b, i: (b * nq + i, 0)),
            pl.BlockSpec((MEM_LEN, 2 * D_MODEL), lambda b, i: (b, 0)),
            pl.BlockSpec((tq, D_MODEL), lambda b, i: (b * nq + i, 0)),
        ],
        out_specs=[
            pl.BlockSpec((tq, D_MODEL), lambda b, i: (b * nq + i, 0)),
            pl.BlockSpec((MEM_LEN, 2 * D_MODEL), lambda b, i: (b, 0)),
        ],
        scratch_shapes=[pltpu.VMEM((MEM_LEN, 2 * D_MODEL), F32)],
        compiler_params=_cparams(("parallel", "arbitrary")),
        name="xa_bwd",
    )(q, kv, do)


def _adamw(parts, w, m, v, name, tr=128):
    n, nl, r, c = parts.shape
    tr = _tile(r, tr, 16)

    def body(p_ref, w_ref, m_ref, v_ref, g_ref, d_ref, nm_ref, nv_ref):
        g = p_ref[0].astype(F32)
        for i in range(1, n):
            g = g + p_ref[i].astype(F32)
        m2 = ADAM_B1 * m_ref[...] + (1.0 - ADAM_B1) * g
        v2 = ADAM_B2 * v_ref[...] + (1.0 - ADAM_B2) * (g * g)
        m_hat = m2 / (1.0 - ADAM_B1 ** ADAM_STEP)
        v_hat = v2 / (1.0 - ADAM_B2 ** ADAM_STEP)
        g_ref[...] = g
        d_ref[...] = -ADAM_LR * (m_hat / (jnp.sqrt(v_hat) + ADAM_EPS) + ADAM_WD * w_ref[...])
        nm_ref[...] = m2
        nv_ref[...] = v2

    blk = pl.BlockSpec((None, tr, c), lambda l, i: (l, i, 0))
    return pl.pallas_call(
        body,
        out_shape=[jax.ShapeDtypeStruct((nl, r, c), F32)] * 4,
        grid=(nl, r // tr),
        in_specs=[pl.BlockSpec((n, None, tr, c), lambda l, i: (0, l, i, 0)), blk, blk, blk],
        out_specs=[blk] * 4,
        compiler_params=_cparams(("parallel", "parallel")),
        name=name,
    )(parts, w, m, v)


def _flat_index(px, py, pc):
    return 4 * px + 2 * py + pc


def _all_gather(arrs, name):
    n = len(arrs)

    def body(*refs):
        start, forward, finish = _gather_phases(refs[:n], refs[n:2 * n], *refs[2 * n:])
        start()
        forward()
        finish()

    return pl.pallas_call(
        body,
        out_shape=_gather_shapes(arrs),
        in_specs=[ANY_SPEC] * n,
        out_specs=[ANY_SPEC] * n,
        scratch_shapes=_comm_sems(n),
        name=name,
    )(*arrs)


ANY_SPEC = pl.BlockSpec(memory_space=pl.ANY)


def _comm_sems(n):
    return [pltpu.SemaphoreType.DMA((n, N_DEV - 1)), pltpu.SemaphoreType.DMA((n, N_DEV - 1)),
            pltpu.SemaphoreType.DMA((n,))]


def _gather_shapes(arrs):
    return [jax.ShapeDtypeStruct((N_DEV,) + a.shape, a.dtype) for a in arrs]


def _gather_phases(ins, outs, send_sems, recv_sems, local_sems):
    n = len(ins)
    x, y, c = lax.axis_index("x"), lax.axis_index("y"), lax.axis_index("c")
    me, sibling = (x, y, c), (x, y, 1 - c)
    chips = [(1 - x, y), (x, 1 - y), (1 - x, 1 - y)]

    def copy(a, k, block, to, src=None):
        slot = outs[a].at[_flat_index(*block)]
        return pltpu.make_async_remote_copy(
            src_ref=slot if src is None else src, dst_ref=slot,
            send_sem=send_sems.at[a, k], recv_sem=recv_sems.at[a, k],
            device_id=to, device_id_type=MESH)

    def mine(a):
        return pltpu.make_async_copy(ins[a], outs[a].at[_flat_index(*me)], local_sems.at[a])

    def first(a):
        return [copy(a, 0, me, sibling, src=ins[a])] + [
            copy(a, 1 + j, me, (*chip, c), src=ins[a]) for j, chip in enumerate(chips)]

    def start():
        for a in range(n):
            mine(a).start()
            for cp in first(a):
                cp.start()

    def forward():
        for j, chip in enumerate(chips):
            for a in range(n):
                copy(a, 1 + j, (*chip, c), me).wait_recv()
                copy(a, 4 + j, (*chip, c), sibling).start()

    def finish():
        for a in range(n):
            copy(a, 0, sibling, me).wait_recv()
            for j, chip in enumerate(chips):
                copy(a, 4 + j, (*chip, 1 - c), me).wait_recv()
        for a in range(n):
            for cp in first(a):
                cp.wait_send()
            for j, chip in enumerate(chips):
                copy(a, 4 + j, (*chip, c), sibling).wait_send()
            mine(a).wait()

    return start, forward, finish


def _scatter_blocks(arrs, name):
    n = len(arrs)

    def body(*refs):
        start, finish = _scatter_phases(refs[:n], refs[n:2 * n], *refs[2 * n:])
        start()
        finish()

    return pl.pallas_call(
        body,
        out_shape=[jax.ShapeDtypeStruct(a.shape, a.dtype) for a in arrs],
        in_specs=[ANY_SPEC] * n,
        out_specs=[ANY_SPEC] * n,
        scratch_shapes=_comm_sems(n),
        name=name,
    )(*arrs)


def _scatter_phases(ins, outs, send_sems, recv_sems, local_sems):
    n = len(ins)
    x, y, c = lax.axis_index("x"), lax.axis_index("y"), lax.axis_index("c")
    me = _flat_index(x, y, c)

    def peer(k):
        return (1 - x if k & 4 else x, 1 - y if k & 2 else y, 1 - c if k & 1 else c)

    def copy(a, k):
        p = peer(k)
        return pltpu.make_async_remote_copy(
            src_ref=ins[a].at[_flat_index(*p)], dst_ref=outs[a].at[me],
            send_sem=send_sems.at[a, k - 1], recv_sem=recv_sems.at[a, k - 1],
            device_id=p, device_id_type=MESH)

    def landed(a, k):
        slot = outs[a].at[_flat_index(*peer(k))]
        return pltpu.make_async_remote_copy(
            src_ref=slot, dst_ref=slot, send_sem=send_sems.at[a, k - 1], recv_sem=recv_sems.at[a, k - 1],
            device_id=peer(k), device_id_type=MESH)

    def mine(a):
        return pltpu.make_async_copy(ins[a].at[me], outs[a].at[me], local_sems.at[a])

    def start():
        for a in range(n):
            mine(a).start()
            for k in range(1, N_DEV):
                copy(a, k).start()

    def finish():
        for a in range(n):
            for k in range(1, N_DEV):
                landed(a, k).wait_recv()
        for a in range(n):
            for k in range(1, N_DEV):
                copy(a, k).wait_send()
            mine(a).wait()

    return start, finish


_BIG = ("w_in", "w_br_att", "w_br_ssm", "w_mix_out", "w_xq", "w_xkv", "w_xo", "w_gu", "w_down")
_COL_SHARDED = ("w_in", "w_xkv", "w_gu", "conv_w")
_SMALL = ("g_pre_mix", "conv_b", "dt_bias", "a_log", "d_skip", "g_ssm_norm", "g_post_mix", "g_pre_xa",
          "g_mem", "g_post_xa", "g_pre_ffn", "g_post_ffn")
_WEIGHTS = ("g_pre_mix", "w_in", "conv_w", "conv_b", "dt_bias", "a_log", "d_skip", "g_ssm_norm", "w_br_att",
            "w_br_ssm", "w_mix_out", "g_post_mix", "g_pre_xa", "g_mem", "w_xq", "w_xkv", "w_xo", "g_post_xa",
            "g_pre_ffn", "w_gu", "w_down", "g_post_ffn")
PACK_W = 8 * LANES


def _unshard(g, col):
    n, r, c = g.shape
    if col:
        return jnp.transpose(g, (1, 0, 2)).reshape(r, n * c)
    return g.reshape(n * r, c)


def _shard(w, col):
    r, c = w.shape
    if col:
        return jnp.transpose(w.reshape(r, N_DEV, c // N_DEV), (1, 0, 2))
    return w.reshape(N_DEV, r // N_DEV, c)


def _permute_in(w, cfg):
    parts, off = [], 0
    for size in cfg.in_sizes:
        parts.append(w[..., off:off + size])
        off += size
    q, k, v, z, xbc, dt, ga, gs = parts
    pad = jnp.zeros(w.shape[:-1] + (LANES - cfg.heads,), w.dtype)
    return jnp.concatenate([z, ga, gs, q, k, v, xbc, dt, pad], axis=-1)


def _unpermute_in(w, cfg):
    c = cfg
    sl = lambda a, n: w[..., a:a + n]
    return jnp.concatenate([sl(c.q0, c.sbw), sl(c.k0, c.sbw), sl(c.v0, c.sbw), sl(c.z0, c.inner),
                            sl(c.xbc0, c.conv_dim), sl(c.dt0, c.heads), sl(c.ga0, c.d), sl(c.gs0, c.d)], axis=-1)


def _pack(arrs):
    flat = jnp.concatenate([a.reshape(-1).astype(F32) for a in arrs])
    rows = -(-flat.shape[0] // PACK_W)
    rows = -(-rows // 8) * 8
    return jnp.pad(flat, (0, rows * PACK_W - flat.shape[0])).reshape(rows, PACK_W)


def _unpack(p, shapes):
    flat = p.reshape(-1)
    out, off = [], 0
    for s in shapes:
        size = math.prod(s)
        out.append(flat[off:off + size].reshape(s))
        off += size
    return out


def _vec3(a, width=None):
    if width is not None and a.shape[1] < width:
        a = jnp.pad(a, ((0, 0), (0, width - a.shape[1])))
    return a[:, None, :]


def _forward_layer(l, xin, h1, memf, tgt, w, p, cfg, nb, last, gather):
    t = xin.shape[0]
    d = cfg.d
    s = {"x_in": xin, "h1": h1}
    proj = _mm(h1, w["w_in"], name="mm_proj")
    s["proj"] = proj
    s["o_att"], gathered = _sb_fwd(proj, cfg, nb, gather)
    s["xbc"] = _conv_fwd(proj, w["conv_w"], p["conv_b"], l, cfg, nb)
    s["y"], s["states"] = _ssd_fwd(s["xbc"], proj, p["dt_bias"], p["a_log"], p["d_skip_x"], l, cfg, nb)
    s["o_ssm"] = _rowwise(_f_gate_norm, "gate_norm_fwd", t, [_full(s["y"]), (proj, cfg.inner, 0)],
                          [(p["g_ssm_norm"], l)], [(cfg.inner, ACT_DTYPE)])[0]
    s["ba"] = _mm(s["o_att"], w["w_br_att"], name="mm_br_att")
    s["bs"] = _mm(s["o_ssm"], w["w_br_ssm"], name="mm_br_ssm")
    s["merged"] = _rowwise(_f_merge, "merge_fwd", t,
                           [(proj, d, cfg.ga0 // d), (proj, d, cfg.gs0 // d), _full(s["ba"]), _full(s["bs"])],
                           [], [(d, ACT_DTYPE)])[0]
    s["mo"] = _mm(s["merged"], w["w_mix_out"], name="mm_mix_out")
    s["x1"], s["h2"] = _rowwise(_f_post_pre, "post_pre_mix", t, [_full(xin), _full(s["mo"])],
                                [(p["g_post_mix"], l), (p["g_pre_xa"], l)], [(d, F32), (d, ACT_DTYPE)])
    s["mem_n"] = _rowwise(_rms, "mem_norm", memf.shape[0], [_full(memf)], [(p["g_mem"], l)], [(d, ACT_DTYPE)])[0]
    s["q"] = _mm(s["h2"], w["w_xq"], name="mm_xq")
    s["kv"] = _mm(s["mem_n"], w["w_xkv"], name="mm_xkv")
    s["o_xa"] = _xa_fwd(s["q"], s["kv"], cfg, nb)
    s["xo"] = _mm(s["o_xa"], w["w_xo"], name="mm_xo")
    s["x2"], s["h3"] = _rowwise(_f_post_pre, "post_pre_xa", t, [_full(s["x1"]), _full(s["xo"])],
                                [(p["g_post_xa"], l), (p["g_pre_ffn"], l)], [(d, F32), (d, ACT_DTYPE)])
    s["gu"] = _mm(s["h3"], w["w_gu"], name="mm_gu")
    s["act"] = _rowwise(_f_swiglu, "swiglu_fwd", t, [(s["gu"], cfg.ffn, 0), (s["gu"], cfg.ffn, 1)], [],
                        [(cfg.ffn, ACT_DTYPE)])[0]
    s["dn"] = _mm(s["act"], w["w_down"], name="mm_down")
    if last:
        nxt = _rowwise(_f_final, "final_loss", t, [_full(s["x2"]), _full(s["dn"]), _full(tgt)],
                       [(p["g_post_ffn"], l)], [(d, F32)], acc_out=[(1, d)])
    else:
        nxt = _rowwise(_f_post_pre, "post_pre_ffn", t, [_full(s["x2"]), _full(s["dn"])],
                       [(p["g_post_ffn"], l), (p["g_pre_mix"], l + 1)], [(d, F32), (d, ACT_DTYPE)])
    return s, nxt, gathered


def _backward_layer(l, s, dx, d_dn, memf, w, p, cfg, nb, prev_dn, scatter):
    t = dx.shape[0]
    d = cfg.d
    g = {}
    dact = _mm(d_dn, w["w_down"], tb=True, out_dtype=ACT_DTYPE, name="mm_d_act", tn=1408)
    g["w_down"] = _mm(s["act"], d_dn, ta=True, name="mm_dw_down")
    dgu = _rowwise(_f_swiglu_bwd, "swiglu_bwd", t, [(s["gu"], cfg.ffn, 0), (s["gu"], cfg.ffn, 1), _full(dact)], [],
                   [(2 * cfg.ffn, ACT_DTYPE)])[0]
    dh3 = _mm(dgu, w["w_gu"], tb=True, name="mm_d_h3", tk=1408)
    g["w_gu"] = _mm(s["h3"], dgu, ta=True, name="mm_dw_gu")
    dx2, d_xo, g["g_pre_ffn"], g["g_post_xa"] = _rowwise(
        _f_pre_post_bwd, "pre_post_bwd_ffn", t, [_full(s["x2"]), _full(dh3), _full(dx), _full(s["xo"])],
        [(p["g_pre_ffn"], l), (p["g_post_xa"], l)], [(d, F32), (d, ACT_DTYPE)], acc_out=[(1, d), (1, d)])
    do_xa = _mm(d_xo, w["w_xo"], tb=True, out_dtype=ACT_DTYPE, name="mm_d_oxa")
    g["w_xo"] = _mm(s["o_xa"], d_xo, ta=True, name="mm_dw_xo")
    dq, dkv = _xa_bwd(s["q"], s["kv"], do_xa, cfg, nb)
    dh2 = _mm(dq, w["w_xq"], tb=True, name="mm_d_h2")
    g["w_xq"] = _mm(s["h2"], dq, ta=True, name="mm_dw_xq")
    dmem_n = _mm(dkv, w["w_xkv"], tb=True, name="mm_d_mem")
    g["w_xkv"] = _mm(s["mem_n"], dkv, ta=True, name="mm_dw_xkv")
    g["g_mem"] = _rowwise(_f_gain_bwd, "mem_norm_bwd", memf.shape[0], [_full(memf), _full(dmem_n)],
                          [(p["g_mem"], l)], [], acc_out=[(1, d)])[0]
    dx1, d_mo, g["g_pre_xa"], g["g_post_mix"] = _rowwise(
        _f_pre_post_bwd, "pre_post_bwd_xa", t, [_full(s["x1"]), _full(dh2), _full(dx2), _full(s["mo"])],
        [(p["g_pre_xa"], l), (p["g_post_mix"], l)], [(d, F32), (d, ACT_DTYPE)], acc_out=[(1, d), (1, d)])
    dmerged = _mm(d_mo, w["w_mix_out"], tb=True, out_dtype=ACT_DTYPE, name="mm_d_merged")
    g["w_mix_out"] = _mm(s["merged"], d_mo, ta=True, name="mm_dw_mix_out")
    proj = s["proj"]
    dgg, dba, dbs = _rowwise(
        _f_merge_bwd, "merge_bwd", t,
        [(proj, d, cfg.ga0 // d), (proj, d, cfg.gs0 // d), _full(s["ba"]), _full(s["bs"]), _full(dmerged)], [],
        [(2 * d, ACT_DTYPE), (d, ACT_DTYPE), (d, ACT_DTYPE)])
    do_att = _mm(dba, w["w_br_att"], tb=True, name="mm_d_oatt")
    g["w_br_att"] = _mm(s["o_att"], dba, ta=True, name="mm_dw_br_att")
    do_ssm = _mm(dbs, w["w_br_ssm"], tb=True, out_dtype=ACT_DTYPE, name="mm_d_ossm")
    g["w_br_ssm"] = _mm(s["o_ssm"], dbs, ta=True, name="mm_dw_br_ssm")
    dy, dz, g["g_ssm_norm"] = _rowwise(
        _f_gate_norm_bwd, "gate_norm_bwd", t, [_full(s["y"]), (proj, cfg.inner, 0), _full(do_ssm)],
        [(p["g_ssm_norm"], l)], [(cfg.inner, F32), (cfg.inner, ACT_DTYPE)], acc_out=[(1, cfg.inner)])
    dxbc, ddt_raw, dvec = _ssd_bwd(s["xbc"], proj, s["states"], dy, p["dt_bias"], p["a_log"], p["d_skip_x"], l, cfg, nb)
    g["dt_bias"], g["a_log"], g["d_skip"] = (dvec[i:i + 1, :cfg.heads] for i in range(3))
    dxbc_raw, g["conv_w"], g["conv_b"] = _conv_bwd(proj, dxbc, w["conv_w"], p["conv_b"], l, cfg, nb)
    own = [_shard(g[n], n in _COL_SHARDED).astype(WIRE_DTYPE) for n in _BIG if n != "w_in"]
    dq_sb, dk_sb, dv_sb, landed = _sb_bwd(proj, do_att, cfg, nb, list(scatter) + own)
    landed = (landed[:len(scatter)], landed[len(scatter):])
    dproj = jnp.concatenate([dz, dgg, dq_sb, dk_sb.astype(ACT_DTYPE), dv_sb.astype(ACT_DTYPE), dxbc_raw, ddt_raw], axis=1)
    dh1 = _mm(dproj, w["w_in"], tb=True, name="mm_d_h1", tk=1152)
    g["w_in"] = _mm(s["h1"], dproj, ta=True, name="mm_dw_in")
    if prev_dn is None:
        dx0, g["g_pre_mix"] = _rowwise(_f_pre_bwd, "pre_bwd_first", t, [_full(s["x_in"]), _full(dh1), _full(dx1)],
                                       [(p["g_pre_mix"], l)], [(d, F32)], acc_out=[(1, d)])
        return g, dx0, None, None, landed
    dx0, d_dn_prev, g["g_pre_mix"], g_post_prev = _rowwise(
        _f_pre_post_bwd, "pre_post_bwd_mix", t, [_full(s["x_in"]), _full(dh1), _full(dx1), _full(prev_dn)],
        [(p["g_pre_mix"], l), (p["g_post_ffn"], l - 1)], [(d, F32), (d, ACT_DTYPE)], acc_out=[(1, d), (1, d)])
    return g, dx0, d_dn_prev, g_post_prev, landed


def kernel(x, mem, g_pre_mix, w_in, conv_w, conv_b, dt_bias, a_log, d_skip, g_ssm_norm, w_br_att, w_br_ssm, w_mix_out, g_post_mix, g_pre_xa, g_mem, w_xq, w_xkv, w_xo, g_post_xa, g_pre_ffn, w_gu, w_down, g_post_ffn, loss_target, m_g_pre_mix, m_w_in, m_conv_w, m_conv_b, m_dt_bias, m_a_log, m_d_skip, m_g_ssm_norm, m_w_br_att, m_w_br_ssm, m_w_mix_out, m_g_post_mix, m_g_pre_xa, m_g_mem, m_w_xq, m_w_xkv, m_w_xo, m_g_post_xa, m_g_pre_ffn, m_w_gu, m_w_down, m_g_post_ffn, v_g_pre_mix, v_w_in, v_conv_w, v_conv_b, v_dt_bias, v_a_log, v_d_skip, v_g_ssm_norm, v_w_br_att, v_w_br_ssm, v_w_mix_out, v_g_post_mix, v_g_pre_xa, v_g_mem, v_w_xq, v_w_xkv, v_w_xo, v_g_post_xa, v_g_pre_ffn, v_w_gu, v_w_down, v_g_post_ffn):
    vals = dict(locals())
    cfg = _Cfg()
    nb = x.shape[0]
    t = nb * SEQ
    d = cfg.d
    depth = g_pre_mix.shape[0]

    def wire(l):
        return [vals[n][l].astype(WIRE_DTYPE) for n in _BIG] + [conv_w[l]]

    def layer_weights(gathered):
        w = {n: _unshard(gw, n in _COL_SHARDED) for n, gw in zip(_BIG + ("conv_w",), gathered)}
        w["w_in"] = _permute_in(w["w_in"], cfg)
        w["conv_w"] = w["conv_w"][None]
        return w

    p = {n: _vec3(vals[n], LANES if n in ("dt_bias", "a_log", "d_skip") else None) for n in _SMALL}
    p["d_skip_x"] = _vec3(jnp.repeat(d_skip, SSM_HEAD_DIM, axis=1))
    weights = [None] * depth
    weights[0] = layer_weights(_all_gather(wire(0), "ag_weights_first"))

    xf = x.reshape(t, d)
    memf = mem.reshape(nb * MEM_LEN, d)
    tgt = loss_target.reshape(t, d)
    h = _rowwise(_rms, "pre_norm_first", t, [_full(xf)], [(p["g_pre_mix"], 0)], [(d, ACT_DTYPE)])[0]
    saved = []
    xcur = xf
    for l in range(depth):
        last = l == depth - 1
        s, nxt, gathered = _forward_layer(l, xcur, h, memf, tgt, weights[l], p, cfg, nb, last,
                                          () if last else wire(l + 1))
        saved.append(s)
        if not last:
            weights[l + 1] = layer_weights(gathered)
            xcur, h = nxt
    dx, loss_row = nxt
    loss = lax.psum(0.5 * jnp.sum(loss_row) / d, AXES)

    top = saved[-1]
    d_dn, g_post_top = _rowwise(lambda ysub, dxo, gp: _rms_bwd(ysub, gp, dxo), "post_bwd_last", t,
                                [_full(top["dn"]), _full(dx)], [(p["g_post_ffn"], depth - 1)], [(d, ACT_DTYPE)],
                                acc_out=[(1, d)])
    grads = [None] * depth
    landed = [dict() for _ in range(depth)]
    post_ffn = [None] * depth
    post_ffn[depth - 1] = g_post_top
    pending = []
    for l in reversed(range(depth)):
        prev_dn = saved[l - 1]["dn"] if l > 0 else None
        grads[l], dx, d_dn, g_post_prev, (got_above, got_own) = _backward_layer(
            l, saved[l], dx, d_dn, memf, weights[l], p, cfg, nb, prev_dn, pending)
        if pending:
            landed[l + 1]["w_in"] = got_above[0]
        landed[l].update(zip([n for n in _BIG if n != "w_in"], got_own))
        pending = [_shard(_unpermute_in(grads[l]["w_in"], cfg), True).astype(WIRE_DTYPE)]
        if l > 0:
            post_ffn[l - 1] = g_post_prev
    landed[0]["w_in"] = _scatter_blocks(pending, "scatter_grads_last")[0]
    for l in range(depth):
        grads[l]["g_post_ffn"] = post_ffn[l]
    grad_x = dx.reshape(x.shape)
    stacked = {n: jnp.stack([grads[l][n] for l in range(depth)]) for n in _SMALL + ("conv_w",)}

    out = {}
    for n in _BIG:
        parts = jnp.stack([landed[l][n] for l in range(depth)], axis=1)
        out[n] = _adamw(parts, vals[n], vals["m_" + n], vals["v_" + n], "adamw_" + n)

    small_shapes = [vals[n].shape for n in _SMALL]
    pack_g = _pack([stacked[n] for n in _SMALL])
    conv_g = stacked["conv_w"].reshape(depth * SSM_CONV, cfg.conv_dim)
    parts_small, parts_conv = _all_gather([pack_g, conv_g], "ag_small_grads")
    packed = lambda prefix: _pack([vals[prefix + n] for n in _SMALL])[None]
    res = _adamw(parts_small[:, None], packed(""), packed("m_"), packed("v_"), "adamw_small")
    unpacked = [_unpack(r, small_shapes) for r in res]
    for i, n in enumerate(_SMALL):
        out[n] = [unpacked[j][i] for j in range(4)]
    cs = conv_w.shape[2]
    me = _flat_index(lax.axis_index("x"), lax.axis_index("y"), lax.axis_index("c"))
    parts_conv = lax.dynamic_slice_in_dim(parts_conv, me * cs, cs, axis=2)
    flat = lambda a: a.reshape(1, depth * SSM_CONV, cs)
    res = _adamw(parts_conv[:, None], flat(conv_w), flat(m_conv_w), flat(v_conv_w), "adamw_conv_w")
    out["conv_w"] = [r.reshape(conv_w.shape) for r in res]

    return (loss, grad_x, *[out[n][0] for n in _WEIGHTS], *[out[n][1] for n in _WEIGHTS],
            *[out[n][2] for n in _WEIGHTS], *[out[n][3] for n in _WEIGHTS])
```

```python
import functools
import math

import jax
import jax.numpy as jnp
from jax import lax
from jax.experimental import pallas as pl
from jax.experimental.pallas import tpu as pltpu

F32 = jnp.float32
BF16 = jnp.bfloat16
MXU_DTYPE = BF16
ACT_DTYPE = BF16
WIRE_DTYPE = BF16

D_MODEL = 1024
SEQ = 2048
DEPTH = 4
MEM_LEN = 256
RMS_EPS = 1e-6
SB_HEADS = 16
SB_HEAD_DIM = 64
SB_BLOCK = 128
SSM_INNER = 2 * D_MODEL
SSM_HEAD_DIM = 64
SSM_GROUPS = 4
SSM_STATE = 128
SSM_CONV = 4
SSM_CHUNK = 128
XA_HEADS = 4
FFN_HIDDEN = ((8 * D_MODEL + 767) // 768) * 256
ADAM_LR = 0.001
ADAM_B1 = 0.9
ADAM_B2 = 0.999
ADAM_EPS = 1e-08
ADAM_WD = 0.01
ADAM_STEP = 10

N_DEV = 8
LANES = 128
VMEM_LIMIT_BYTES = 56 * 1024 * 1024

AXES = ("x", "y", "c")
MESH = pl.DeviceIdType.MESH


class _Cfg:
    def __init__(self):
        self.d = D_MODEL
        self.sbw = SB_HEADS * SB_HEAD_DIM
        self.inner = SSM_INNER
        self.heads = SSM_INNER // SSM_HEAD_DIM
        self.epg = self.heads // SSM_GROUPS
        self.gn = SSM_GROUPS * SSM_STATE
        self.conv_dim = SSM_INNER + 2 * self.gn
        self.ffn = FFN_HIDDEN
        self.xa_dim = D_MODEL // XA_HEADS
        self.in_sizes = (self.sbw, self.sbw, self.sbw, self.inner, self.conv_dim, self.heads, self.d, self.d)
        self.in_width = sum(self.in_sizes)
        self.z0 = 0
        self.ga0 = self.inner
        self.gs0 = self.ga0 + self.d
        self.q0 = self.gs0 + self.d
        self.k0 = self.q0 + self.sbw
        self.v0 = self.k0 + self.sbw
        self.xbc0 = self.v0 + self.sbw
        self.dt0 = self.xbc0 + self.conv_dim
        self.proj_w = self.dt0 + LANES
        assert self.heads <= LANES


def _cparams(sem=None):
    return pltpu.CompilerParams(dimension_semantics=sem, vmem_limit_bytes=VMEM_LIMIT_BYTES)


def _tile(n, pref, mult):
    if n <= pref:
        return n
    t = (pref // mult) * mult
    while t >= mult:
        if n % t == 0:
            return t
        t -= mult
    return n


MM_VMEM_BUDGET = 40 * 1024 * 1024


def _mm(a, b, *, ta=False, tb=False, out_dtype=F32, name, tm=1024, tn=1152, tk=2048):
    kk, m = (a.shape if ta else a.shape[::-1])
    if tb:
        n, k2 = b.shape
    else:
        k2, n = b.shape
    assert kk == k2, (name, a.shape, b.shape)
    size = lambda dt: jnp.dtype(dt).itemsize
    tn = _tile(n, tn, LANES)
    tk = _tile(kk, tk, LANES if (tb or not ta) else 16)
    nk = kk // tk
    while True:
        tm_ = _tile(m, tm, LANES if ta else 16)
        need = (2 * (tm_ * tk * size(a.dtype) + tk * tn * size(b.dtype) + tm_ * tn * size(out_dtype))
                + (tm_ * tk + tk * tn) * size(MXU_DTYPE) + tm_ * tn * 4 * (2 if nk > 1 else 1))
        if need <= MM_VMEM_BUDGET or tm <= 128:
            break
        tm //= 2
    tm = tm_
    gi, gj = m // tm, n // tn
    j_outer = nk == 1 and b.size * size(b.dtype) * (gi - 1) > a.size * size(a.dtype) * (gj - 1)
    dims = (((0 if ta else 1,), (1 if tb else 0,)), ((), ()))

    def ij(g0, g1):
        return (g1, g0) if j_outer else (g0, g1)

    def body(a_ref, b_ref, o_ref, *scratch):
        av = a_ref[...].astype(MXU_DTYPE)
        bv = b_ref[...].astype(MXU_DTYPE)
        part = lax.dot_general(av, bv, dims, preferred_element_type=F32)
        if nk == 1:
            o_ref[...] = part.astype(out_dtype)
        else:
            acc_ref, = scratch
            k = pl.program_id(2)

            @pl.when(k == 0)
            def _():
                acc_ref[...] = part

            @pl.when(k > 0)
            def _():
                acc_ref[...] += part

            @pl.when(k == nk - 1)
            def _():
                o_ref[...] = acc_ref[...].astype(out_dtype)

    if ta:
        a_spec = pl.BlockSpec((tk, tm), lambda g0, g1, k: (k, ij(g0, g1)[0]))
    else:
        a_spec = pl.BlockSpec((tm, tk), lambda g0, g1, k: (ij(g0, g1)[0], k))
    if tb:
        b_spec = pl.BlockSpec((tn, tk), lambda g0, g1, k: (ij(g0, g1)[1], k))
    else:
        b_spec = pl.BlockSpec((tk, tn), lambda g0, g1, k: (k, ij(g0, g1)[1]))
    return pl.pallas_call(
        body,
        out_shape=jax.ShapeDtypeStruct((m, n), out_dtype),
        grid=(gj, gi, nk) if j_outer else (gi, gj, nk),
        in_specs=[a_spec, b_spec],
        out_specs=pl.BlockSpec((tm, tn), lambda g0, g1, k: ij(g0, g1)),
        scratch_shapes=[] if nk == 1 else [pltpu.VMEM((tm, tn), F32)],
        compiler_params=_cparams(("parallel", "parallel", "arbitrary")),
        name=name,
    )(a, b)


def _rowwise(fn, name, rows, row_in, vec_in, row_out, acc_out=(), tr=256):
    tr = _tile(rows, tr, 16)
    n_in = len(row_in) + len(vec_in)
    n_ro = len(row_out)

    def body(*refs):
        ins = [r[...].astype(F32) for r in refs[:n_in]]
        outs = fn(*ins)
        if not isinstance(outs, (tuple, list)):
            outs = (outs,)
        out_refs = refs[n_in:]
        for o_ref, val in zip(out_refs[:n_ro], outs[:n_ro]):
            o_ref[...] = val.astype(o_ref.dtype)
        if acc_out:
            i = pl.program_id(0)
            for o_ref, val in zip(out_refs[n_ro:], outs[n_ro:]):
                @pl.when(i == 0)
                def _(o_ref=o_ref, val=val):
                    o_ref[...] = val

                @pl.when(i > 0)
                def _(o_ref=o_ref, val=val):
                    o_ref[...] += val

    in_specs = [pl.BlockSpec((tr, w), functools.partial(lambda i, cb: (i, cb), cb=cb)) for (_, w, cb) in row_in]
    in_specs += [pl.BlockSpec((None,) + v.shape[1:], functools.partial(lambda i, l: (l, 0, 0), l=l)) for (v, l) in vec_in]
    out_shape = [jax.ShapeDtypeStruct((rows, w), dt) for (w, dt) in row_out]
    out_shape += [jax.ShapeDtypeStruct(s, F32) for s in acc_out]
    out_specs = [pl.BlockSpec((tr, w), lambda i: (i, 0)) for (w, _) in row_out]
    out_specs += [pl.BlockSpec(s, lambda i: (0, 0)) for s in acc_out]
    res = pl.pallas_call(
        body,
        out_shape=out_shape,
        grid=(rows // tr,),
        in_specs=in_specs,
        out_specs=out_specs,
        compiler_params=_cparams(("arbitrary",) if acc_out else ("parallel",)),
        name=name,
    )(*[a for (a, _, _) in row_in], *[v for (v, _) in vec_in])
    return res


def _rms(x, g):
    r = lax.rsqrt(jnp.mean(x * x, axis=-1, keepdims=True) + RMS_EPS)
    return x * r * g


def _rms_bwd(x, g, dy):
    r = lax.rsqrt(jnp.mean(x * x, axis=-1, keepdims=True) + RMS_EPS)
    xh = x * r
    dxh = dy * g
    dx = r * (dxh - xh * jnp.mean(dxh * xh, axis=-1, keepdims=True))
    return dx, jnp.sum(dy * xh, axis=0, keepdims=True)


def _silu(x):
    return x * jax.nn.sigmoid(x)


def _silu_grad(x):
    s = jax.nn.sigmoid(x)
    return s * (1.0 + x * (1.0 - s))


def _softplus(x):
    return jnp.maximum(x, 0.0) + jnp.log1p(jnp.exp(-jnp.abs(x)))


def _full(a):
    return (a, a.shape[1], 0)


def _f_post_pre(x, ysub, g_post, g_pre):
    xn = x + _rms(ysub, g_post)
    return xn, _rms(xn, g_pre)


def _f_final(x, ysub, tgt, g_post):
    err = x + _rms(ysub, g_post) - tgt
    return err * (1.0 / D_MODEL), jnp.sum(err * err, axis=0, keepdims=True)


def _f_pre_post_bwd(xmid, dh, dxo, ysub, g_pre, g_post):
    d1, dg_pre = _rms_bwd(xmid, g_pre, dh)
    dxm = dxo + d1
    dys, dg_post = _rms_bwd(ysub, g_post, dxm)
    return dxm, dys, dg_pre, dg_post


def _f_pre_bwd(x, dh, dxo, g_pre):
    d1, dg_pre = _rms_bwd(x, g_pre, dh)
    return dxo + d1, dg_pre


def _f_gain_bwd(x, dy, g):
    return _rms_bwd(x, g, dy)[1]


def _group_norm_parts(u):
    gw = u.shape[1] // SSM_GROUPS
    parts = []
    for gi in range(SSM_GROUPS):
        ug = u[:, gi * gw:(gi + 1) * gw]
        r = lax.rsqrt(jnp.mean(ug * ug, axis=-1, keepdims=True) + RMS_EPS)
        parts.append((ug * r, r))
    return gw, parts


def _f_gate_norm(y, z, g):
    _, parts = _group_norm_parts(y * _silu(z))
    return jnp.concatenate([uh for uh, _ in parts], axis=1) * g


def _f_gate_norm_bwd(y, z, do, g):
    sz = _silu(z)
    gw, parts = _group_norm_parts(y * sz)
    dxh = do * g
    du = []
    for gi, (uh, r) in enumerate(parts):
        dg_ = dxh[:, gi * gw:(gi + 1) * gw]
        du.append(r * (dg_ - uh * jnp.mean(dg_ * uh, axis=-1, keepdims=True)))
    du = jnp.concatenate(du, axis=1)
    uh_all = jnp.concatenate([uh for uh, _ in parts], axis=1)
    return du * sz, du * y * _silu_grad(z), jnp.sum(do * uh_all, axis=0, keepdims=True)


def _f_merge(ga, gs, ba, bs):
    return jax.nn.sigmoid(ga) * ba + jax.nn.sigmoid(gs) * bs


def _f_merge_bwd(ga, gs, ba, bs, dm):
    sa, ss = jax.nn.sigmoid(ga), jax.nn.sigmoid(gs)
    dgg = jnp.concatenate([dm * ba * sa * (1.0 - sa), dm * bs * ss * (1.0 - ss)], axis=1)
    return dgg, dm * sa, dm * ss


def _f_swiglu(gate, up):
    return _silu(gate) * up


def _f_swiglu_bwd(gate, up, da):
    return jnp.concatenate([da * up * _silu_grad(gate), da * _silu(gate)], axis=1)


def _split_dot(x, u):
    hi = x.astype(BF16)
    lo = (x - hi.astype(F32)).astype(BF16)
    return jnp.dot(hi, u, preferred_element_type=F32) + jnp.dot(lo, u, preferred_element_type=F32)


SB_ROWS = 512
SB_UNROLL = 4
C00 = (((0,), (0,)), ((), ()))
C11 = (((1,), (1,)), ((), ()))


def _sb_setup(q_ref, tq):
    hp = LANES // SB_HEAD_DIM
    lane = lax.broadcasted_iota(jnp.int32, (1, LANES), 1)
    heads = [jnp.logical_and(lane >= h * SB_HEAD_DIM, lane < (h + 1) * SB_HEAD_DIM) for h in range(hp)]
    qs = q_ref[...] * (SB_HEAD_DIM ** -0.5)
    q_h = [jnp.where(hm, qs, 0.0).astype(MXU_DTYPE) for hm in heads]
    row = lax.broadcasted_iota(jnp.int32, (tq, SB_BLOCK), 0)
    col = lax.broadcasted_iota(jnp.int32, (tq, SB_BLOCK), 1)
    sq_row = lax.broadcasted_iota(jnp.int32, (SB_BLOCK, SB_BLOCK), 0)
    sq_col = lax.broadcasted_iota(jnp.int32, (SB_BLOCK, SB_BLOCK), 1)
    return heads, q_h, col - row, sq_row, sq_col


def _sb_scores(q, kj, mask):
    z = lax.dot_general(q, kj, C11, preferred_element_type=F32)
    lm = -(jnp.maximum(z, 0.0) + jnp.log(1.0 + jnp.exp(-jnp.abs(z))))
    return z, lm if mask is None else jnp.where(mask, lm, 0.0)


def _add_rows(x, r0, upd):
    return x + upd if r0 == 0 else jnp.concatenate([x[:r0], x[r0:] + upd], axis=0)


def _grid_step3(nb, ncb, nq):
    return (pl.program_id(0) * ncb + pl.program_id(1)) * nq + pl.program_id(2), nb * ncb * nq


def _sb_fwd(proj, cfg, nb, gather=()):
    blk = SB_BLOCK
    tq = _tile(SEQ, SB_ROWS, blk)
    nq = SEQ // tq
    kpq = tq // blk
    unr = math.gcd(kpq, SB_UNROLL)
    hp = LANES // SB_HEAD_DIM
    ncb = cfg.sbw // LANES
    qb, kb, vb = cfg.q0 // LANES, cfg.k0 // LANES, cfg.v0 // LANES
    ng = len(gather)

    def body(q_ref, k_ref, v_ref, *rest):
        o_ref = rest[ng]
        if ng:
            step_id, n_steps = _grid_step3(nb, ncb, nq)
            start, forward, finish = _gather_phases(rest[:ng], rest[ng + 1:2 * ng + 1], *rest[2 * ng + 1:])
            pl.when(step_id == 0)(start)
            pl.when(step_id == (3 * n_steps) // 4)(forward)
        _sb_fwd_block(q_ref, k_ref, v_ref, o_ref)
        if ng:
            pl.when(step_id == n_steps - 1)(finish)

    def _sb_fwd_block(q_ref, k_ref, v_ref, o_ref):
        i = pl.program_id(2)
        heads, q_h, cmr, sq_row, sq_col = _sb_setup(q_ref, tq)
        u_rev = (sq_row >= sq_col).astype(BF16)

        def key_block(j, r0, acc, runs, diagonal):
            rows = pl.ds(pl.multiple_of(j * blk, blk), blk)
            kj = k_ref[rows, :].astype(MXU_DTYPE)
            vj = v_ref[rows, :].astype(MXU_DTYPE)
            mask = cmr[:tq - r0] < 0 if diagonal else None
            for h in range(hp):
                z, lm = _sb_scores(q_h[h][r0:], kj, mask)
                cs = _split_dot(lm, u_rev)
                w = jnp.exp(z + cs + runs[h][r0:])
                if diagonal:
                    w = jnp.where(mask, w, 0.0)
                upd = jnp.dot(w.astype(MXU_DTYPE), jnp.where(heads[h], vj, 0), preferred_element_type=F32)
                acc = _add_rows(acc, r0, upd)
                runs[h] = _add_rows(runs[h], r0, cs[:, 0:1])
            return acc

        acc = jnp.zeros((tq, LANES), F32)
        runs = [jnp.zeros((tq, 1), F32) for _ in range(hp)]
        for r in reversed(range(kpq)):
            acc = key_block(i * kpq + r, r * blk, acc, runs, True)

        def step(n, carry):
            acc, runs = carry
            runs = list(runs)
            for jj in range(unr):
                acc = key_block(i * kpq - 1 - (n * unr + jj), 0, acc, runs, False)
            return acc, tuple(runs)

        acc, _ = lax.fori_loop(0, i * (kpq // unr), step, (acc, tuple(runs)))
        o_ref[...] = acc

    res = pl.pallas_call(
        body,
        out_shape=[jax.ShapeDtypeStruct((nb * SEQ, cfg.sbw), F32)] + _gather_shapes(gather),
        grid=(nb, ncb, nq),
        in_specs=[
            pl.BlockSpec((tq, LANES), lambda b, c, i: (b * nq + i, qb + c)),
            pl.BlockSpec((SEQ, LANES), lambda b, c, i: (b, kb + c)),
            pl.BlockSpec((SEQ, LANES), lambda b, c, i: (b, vb + c)),
        ] + [ANY_SPEC] * ng,
        out_specs=[pl.BlockSpec((tq, LANES), lambda b, c, i: (b * nq + i, c))] + [ANY_SPEC] * ng,
        scratch_shapes=_comm_sems(ng) if ng else [],
        compiler_params=_cparams(("arbitrary",) * 3 if ng else ("parallel", "parallel", "arbitrary")),
        name="sb_fwd_gather" if ng else "sb_fwd",
    )(proj, proj, proj, *gather)
    return res[0], res[1:]


def _sb_bwd(proj, do_att, cfg, nb, scatter=()):
    blk = SB_BLOCK
    tq = _tile(SEQ, SB_ROWS, blk)
    nq = SEQ // tq
    kpq = tq // blk
    unr = math.gcd(kpq, SB_UNROLL)
    hp = LANES // SB_HEAD_DIM
    ncb = cfg.sbw // LANES
    scale = SB_HEAD_DIM ** -0.5
    qb, kb, vb = cfg.q0 // LANES, cfg.k0 // LANES, cfg.v0 // LANES
    ns = len(scatter)

    def body(q_ref, k_ref, v_ref, do_ref, *rest):
        dq_ref, dk_ref, dv_ref = rest[ns:ns + 3]
        g_ref, z_ref = rest[2 * ns + 3:2 * ns + 5]
        if ns:
            step_id, n_steps = _grid_step3(nb, ncb, nq)
            start, finish = _scatter_phases(rest[:ns], rest[ns + 3:2 * ns + 3], *rest[2 * ns + 5:])
            pl.when(step_id == 0)(start)
        _sb_bwd_block(q_ref, k_ref, v_ref, do_ref, dq_ref, dk_ref, dv_ref, g_ref, z_ref)
        if ns:
            pl.when(step_id == n_steps - 1)(finish)

    def _sb_bwd_block(q_ref, k_ref, v_ref, do_ref, dq_ref, dk_ref, dv_ref, g_ref, z_ref):
        i = pl.program_id(2)

        @pl.when(i == 0)
        def _():
            dk_ref[...] = jnp.zeros_like(dk_ref)
            dv_ref[...] = jnp.zeros_like(dv_ref)

        heads, q_h, cmr, sq_row, sq_col = _sb_setup(q_ref, tq)
        u_rev = (sq_row >= sq_col).astype(BF16)
        u_fwd = (sq_row <= sq_col).astype(BF16)
        do = do_ref[...]
        do_h = [jnp.where(hm, do, 0.0).astype(MXU_DTYPE) for hm in heads]


        def left_block(j, r0, runs, diagonal):
            rows = pl.ds(pl.multiple_of(j * blk, blk), blk)
            kj = k_ref[rows, :].astype(MXU_DTYPE)
            vj = v_ref[rows, :].astype(MXU_DTYPE)
            mask = cmr[:tq - r0] < 0 if diagonal else None
            dv = jnp.zeros((blk, LANES), F32)
            for h in range(hp):
                z, lm = _sb_scores(q_h[h][r0:], kj, mask)
                cs = _split_dot(lm, u_rev)
                a = jnp.exp(z + cs + runs[h][r0:])
                if diagonal:
                    a = jnp.where(mask, a, 0.0)
                da = lax.dot_general(do_h[h][r0:], vj, C11, preferred_element_type=F32)
                dv = dv + lax.dot_general(a.astype(MXU_DTYPE), do_h[h][r0:], C00, preferred_element_type=F32)
                g_ref[h, j, r0:, :] = a * da
                z_ref[h, j, r0:, :] = jax.nn.sigmoid(z)
                runs[h] = _add_rows(runs[h], r0, cs[:, 0:1])
            dv_ref[rows, :] += dv

        runs = [jnp.zeros((tq, 1), F32) for _ in range(hp)]
        for r in reversed(range(kpq)):
            left_block(i * kpq + r, r * blk, runs, True)

        def sweep_left(n, runs):
            runs = list(runs)
            for jj in range(unr):
                left_block(i * kpq - 1 - (n * unr + jj), 0, runs, False)
            return tuple(runs)

        trips = i * (kpq // unr)
        lax.fori_loop(0, trips, sweep_left, tuple(runs))

        def right_block(j, r0, dq, runs, diagonal):
            rows = pl.ds(pl.multiple_of(j * blk, blk), blk)
            kj = k_ref[rows, :].astype(MXU_DTYPE)
            dk = jnp.zeros((blk, LANES), F32)
            for h in range(hp):
                g = g_ref[h, j, r0:, :]
                g_upto = _split_dot(g, u_fwd) + runs[h][r0:]
                dz = g - z_ref[h, j, r0:, :] * g_upto
                if diagonal:
                    dz = jnp.where(cmr[:tq - r0] < 0, dz, 0.0)
                dz = dz.astype(MXU_DTYPE)
                dq = _add_rows(dq, r0, jnp.dot(dz, jnp.where(heads[h], kj, 0), preferred_element_type=F32))
                dk = dk + lax.dot_general(dz, q_h[h][r0:], C00, preferred_element_type=F32)
                runs[h] = _add_rows(runs[h], r0, jnp.sum(g, axis=1, keepdims=True))
            dk_ref[rows, :] += dk
            return dq

        def sweep_right(n, carry):
            dq, runs = carry
            runs = list(runs)
            for jj in range(unr):
                dq = right_block(n * unr + jj, 0, dq, runs, False)
            return dq, tuple(runs)

        init = (jnp.zeros((tq, LANES), F32), tuple(jnp.zeros((tq, 1), F32) for _ in range(hp)))
        dq, runs = lax.fori_loop(0, trips, sweep_right, init)
        runs = list(runs)
        for r in range(kpq):
            dq = right_block(i * kpq + r, r * blk, dq, runs, True)
        dq_ref[...] = (dq * scale).astype(dq_ref.dtype)

    kv_spec_out = pl.BlockSpec((SEQ, LANES), lambda b, c, i: (b, c))
    q_spec_out = pl.BlockSpec((tq, LANES), lambda b, c, i: (b * nq + i, c))
    res = pl.pallas_call(
        body,
        out_shape=[
            jax.ShapeDtypeStruct((nb * SEQ, cfg.sbw), ACT_DTYPE),
            jax.ShapeDtypeStruct((nb * SEQ, cfg.sbw), F32),
            jax.ShapeDtypeStruct((nb * SEQ, cfg.sbw), F32),
        ] + [jax.ShapeDtypeStruct(a.shape, a.dtype) for a in scatter],
        grid=(nb, ncb, nq),
        in_specs=[
            pl.BlockSpec((tq, LANES), lambda b, c, i: (b * nq + i, qb + c)),
            pl.BlockSpec((SEQ, LANES), lambda b, c, i: (b, kb + c)),
            pl.BlockSpec((SEQ, LANES), lambda b, c, i: (b, vb + c)),
            q_spec_out,
        ] + [ANY_SPEC] * ns,
        out_specs=[q_spec_out, kv_spec_out, kv_spec_out] + [ANY_SPEC] * ns,
        scratch_shapes=[pltpu.VMEM((hp, SEQ // blk, tq, blk), F32), pltpu.VMEM((hp, SEQ // blk, tq, blk), F32)]
        + (_comm_sems(ns) if ns else []),
        compiler_params=_cparams(("arbitrary",) * 3 if ns else ("parallel", "parallel", "arbitrary")),
        name="sb_bwd_scatter" if ns else "sb_bwd",
    )(proj, proj, proj, do_att, *scatter)
    return res[0], res[1], res[2], res[3:]


CONV_COLS = 256


def _conv_pre(x, w, b, t):
    kw = SSM_CONV
    shifted = []
    pre = b + w[kw - 1:kw, :] * x
    for k in range(kw - 1):
        d = kw - 1 - k
        xs = jnp.where(t >= d, pltpu.roll(x, d, 0), 0.0)
        shifted.append(xs)
        pre = pre + w[k:k + 1, :] * xs
    shifted.append(x)
    return pre, shifted


def _conv_fwd(proj, conv_w, conv_b, l, cfg, nb):
    cw = CONV_COLS
    ncb = cfg.conv_dim // cw
    xb = cfg.xbc0 // cw

    def body(x_ref, w_ref, b_ref, o_ref):
        x = x_ref[...]
        t = lax.broadcasted_iota(jnp.int32, x.shape, 0)
        pre, _ = _conv_pre(x, w_ref[...], b_ref[...], t)
        o_ref[...] = _silu(pre)

    return pl.pallas_call(
        body,
        out_shape=jax.ShapeDtypeStruct((nb * SEQ, cfg.conv_dim), F32),
        grid=(ncb, nb),
        in_specs=[
            pl.BlockSpec((SEQ, cw), lambda j, b: (b, xb + j)),
            pl.BlockSpec((None, SSM_CONV, cw), lambda j, b: (0, 0, j)),
            pl.BlockSpec((None, 1, cw), lambda j, b: (l, 0, j)),
        ],
        out_specs=pl.BlockSpec((SEQ, cw), lambda j, b: (b, j)),
        compiler_params=_cparams(("parallel", "parallel")),
        name="conv_fwd",
    )(proj, conv_w, conv_b)


def _conv_bwd(proj, dact, conv_w, conv_b, l, cfg, nb):
    cw = CONV_COLS
    ncb = cfg.conv_dim // cw
    xb = cfg.xbc0 // cw
    kw = SSM_CONV

    def body(x_ref, da_ref, w_ref, b_ref, dx_ref, dw_ref, db_ref):
        b_id = pl.program_id(1)
        x = x_ref[...]
        w = w_ref[...]
        t = lax.broadcasted_iota(jnp.int32, x.shape, 0)
        pre, shifted = _conv_pre(x, w, b_ref[...], t)
        dpre = da_ref[...] * _silu_grad(pre)
        dx = w[kw - 1:kw, :] * dpre
        for k in range(kw - 1):
            d = kw - 1 - k
            dx = dx + w[k:k + 1, :] * jnp.where(t < SEQ - d, pltpu.roll(dpre, SEQ - d, 0), 0.0)
        dx_ref[...] = dx.astype(dx_ref.dtype)
        dw = jnp.concatenate([jnp.sum(dpre * s, axis=0, keepdims=True) for s in shifted], axis=0)
        db = jnp.sum(dpre, axis=0, keepdims=True)

        @pl.when(b_id == 0)
        def _():
            dw_ref[...] = dw
            db_ref[...] = db

        @pl.when(b_id > 0)
        def _():
            dw_ref[...] += dw
            db_ref[...] += db

    return pl.pallas_call(
        body,
        out_shape=[
            jax.ShapeDtypeStruct((nb * SEQ, cfg.conv_dim), ACT_DTYPE),
            jax.ShapeDtypeStruct((kw, cfg.conv_dim), F32),
            jax.ShapeDtypeStruct((1, cfg.conv_dim), F32),
        ],
        grid=(ncb, nb),
        in_specs=[
            pl.BlockSpec((SEQ, cw), lambda j, b: (b, xb + j)),
            pl.BlockSpec((SEQ, cw), lambda j, b: (b, j)),
            pl.BlockSpec((None, kw, cw), lambda j, b: (0, 0, j)),
            pl.BlockSpec((None, 1, cw), lambda j, b: (l, 0, j)),
        ],
        out_specs=[
            pl.BlockSpec((SEQ, cw), lambda j, b: (b, j)),
            pl.BlockSpec((kw, cw), lambda j, b: (0, j)),
            pl.BlockSpec((1, cw), lambda j, b: (0, j)),
        ],
        compiler_params=_cparams(("parallel", "arbitrary")),
        name="conv_bwd",
    )(proj, dact, conv_w, conv_b)


def _ssd_common(dt_raw, dt_bias, a_log, tri):
    ln = SSM_CHUNK
    dt = _softplus(dt_raw + dt_bias)
    a = -jnp.exp(a_log)
    a_cs = jnp.dot(tri, dt * a, preferred_element_type=F32, precision=lax.Precision.HIGHEST)
    a_last = a_cs[ln - 1:ln, :]
    return dt, a, a_cs, a_cs.T, jnp.exp(a_cs), jnp.exp(a_last - a_cs), jnp.exp(a_last)


def _ssd_head_v1(h, xs, dt, a_cs, a_t, cb, tril):
    p = SSM_HEAD_DIM
    x_h = xs[:, h * p:(h + 1) * p]
    xd = x_h * dt[:, h:h + 1]
    lmat = jnp.exp(jnp.where(tril, a_cs[:, h:h + 1] - a_t[h:h + 1, :], -jnp.inf))
    return x_h, xd, lmat


def _ssd_specs(cfg, nc, rev):
    ln = SSM_CHUNK
    cidx = (lambda c: nc - 1 - c) if rev else (lambda c: c)
    bmb = cfg.inner // cfg.gn
    return [
        pl.BlockSpec((ln, cfg.inner), lambda b, c: (b * nc + cidx(c), 0)),
        pl.BlockSpec((ln, cfg.gn), lambda b, c: (b * nc + cidx(c), bmb)),
        pl.BlockSpec((ln, cfg.gn), lambda b, c: (b * nc + cidx(c), bmb + 1)),
        pl.BlockSpec((ln, LANES), lambda b, c: (b * nc + cidx(c), cfg.dt0 // LANES)),
    ]


def _ssd_fwd_v1(xbc, proj, dt_bias, a_log, d_skip, l, cfg, nb):
    ln, p, n = SSM_CHUNK, SSM_HEAD_DIM, SSM_STATE
    nc = SEQ // ln
    g_, e_ = SSM_GROUPS, cfg.epg
    assert cfg.inner % cfg.gn == 0

    def body(xs_ref, bm_ref, cm_ref, dtr_ref, bias_ref, alog_ref, dsk_ref, y_ref, st_ref, s_ref):
        c = pl.program_id(1)

        @pl.when(c == 0)
        def _():
            s_ref[...] = jnp.zeros_like(s_ref)

        st_ref[...] = s_ref[...]
        row = lax.broadcasted_iota(jnp.int32, (ln, ln), 0)
        col = lax.broadcasted_iota(jnp.int32, (ln, ln), 1)
        tril = row >= col
        dt, _, a_cs, a_t, e_a, dte, cd = _ssd_common(dtr_ref[...], bias_ref[...], alog_ref[...], tril.astype(F32))
        dsk = dsk_ref[...]
        xs = xs_ref[...]
        for g in range(g_):
            bm = bm_ref[:, g * n:(g + 1) * n].astype(MXU_DTYPE)
            cm = cm_ref[:, g * n:(g + 1) * n].astype(MXU_DTYPE)
            cb = lax.dot_general(cm, bm, (((1,), (1,)), ((), ())), preferred_element_type=F32)
            for e in range(e_):
                h = g * e_ + e
                x_h, xd, lmat = _ssd_head_v1(h, xs, dt, a_cs, a_t, cb, tril)
                s_prev = s_ref[g * n:(g + 1) * n, e * p:(e + 1) * p]
                y = jnp.dot((cb * lmat).astype(MXU_DTYPE), xd.astype(MXU_DTYPE), preferred_element_type=F32)
                y = y + jnp.dot(cm, s_prev.astype(MXU_DTYPE), preferred_element_type=F32) * e_a[:, h:h + 1]
                y_ref[:, h * p:(h + 1) * p] = y + dsk[:, h:h + 1] * x_h
                upd = lax.dot_general(bm, (xd * dte[:, h:h + 1]).astype(MXU_DTYPE), (((0,), (0,)), ((), ())),
                                      preferred_element_type=F32)
                s_ref[g * n:(g + 1) * n, e * p:(e + 1) * p] = cd[:, h:h + 1] * s_prev + upd

    vec = lambda b, c: (l, 0, 0)
    return pl.pallas_call(
        body,
        out_shape=[
            jax.ShapeDtypeStruct((nb * SEQ, cfg.inner), F32),
            jax.ShapeDtypeStruct((nb * nc * g_ * n, e_ * p), F32),
        ],
        grid=(nb, nc),
        in_specs=_ssd_specs(cfg, nc, False) + [pl.BlockSpec((None, 1, LANES), vec)] * 3,
        out_specs=[
            pl.BlockSpec((ln, cfg.inner), lambda b, c: (b * nc + c, 0)),
            pl.BlockSpec((g_ * n, e_ * p), lambda b, c: (b * nc + c, 0)),
        ],
        scratch_shapes=[pltpu.VMEM((g_ * n, e_ * p), F32)],
        compiler_params=_cparams(("parallel", "arbitrary")),
        name="ssd_fwd",
    )(xbc, xbc, xbc, proj, dt_bias, a_log, d_skip)


def _ssd_bwd_v1(xbc, proj, states, dy, dt_bias, a_log, d_skip, l, cfg, nb):
    ln, p, n = SSM_CHUNK, SSM_HEAD_DIM, SSM_STATE
    nc = SEQ // ln
    g_, e_ = SSM_GROUPS, cfg.epg
    c00 = (((0,), (0,)), ((), ()))
    c11 = (((1,), (1,)), ((), ()))

    def body(xs_ref, bm_ref, cm_ref, dtr_ref, st_ref, dy_ref, bias_ref, alog_ref, dsk_ref,
             dxbc_ref, ddt_ref, dvec_ref, ds_ref):
        first = jnp.logical_and(pl.program_id(0) == 0, pl.program_id(1) == 0)

        @pl.when(pl.program_id(1) == 0)
        def _():
            ds_ref[...] = jnp.zeros_like(ds_ref)

        row = lax.broadcasted_iota(jnp.int32, (ln, ln), 0)
        col = lax.broadcasted_iota(jnp.int32, (ln, ln), 1)
        tril = row >= col
        tri_f = tril.astype(F32)
        dtr = dtr_ref[...]
        bias = bias_ref[...]
        dt, a, a_cs, a_t, e_a, dte, cd = _ssd_common(dtr, bias, alog_ref[...], tri_f)
        dsk = dsk_ref[...]
        xs = xs_ref[...]
        lane = lax.broadcasted_iota(jnp.int32, (1, LANES), 1)
        sub = lax.broadcasted_iota(jnp.int32, (LANES, 1), 0)
        da_col = jnp.zeros((ln, LANES), F32)
        da_row_t = jnp.zeros((LANES, ln), F32)
        ddt = jnp.zeros((ln, LANES), F32)
        da_last = jnp.zeros((1, LANES), F32)
        ddsk = jnp.zeros((1, LANES), F32)
        for g in range(g_):
            bm = bm_ref[:, g * n:(g + 1) * n].astype(MXU_DTYPE)
            cm = cm_ref[:, g * n:(g + 1) * n].astype(MXU_DTYPE)
            cb = lax.dot_general(cm, bm, c11, preferred_element_type=F32)
            dbm = jnp.zeros((ln, n), F32)
            dcm = jnp.zeros((ln, n), F32)
            for e in range(e_):
                h = g * e_ + e
                hs = slice(h * p, (h + 1) * p)
                oh = (lane == h).astype(F32)
                x_h, xd, lmat = _ssd_head_v1(h, xs, dt, a_cs, a_t, cb, tril)
                d_y = dy_ref[:, hs]
                s_prev = st_ref[g * n:(g + 1) * n, e * p:(e + 1) * p]
                d_s = ds_ref[g * n:(g + 1) * n, e * p:(e + 1) * p]
                dy_m = d_y.astype(MXU_DTYPE)
                xd_m = xd.astype(MXU_DTYPE)
                sp_m = s_prev.astype(MXU_DTYPE)
                ds_m = d_s.astype(MXU_DTYPE)
                e_a_h, dte_h, cd_h = e_a[:, h:h + 1], dte[:, h:h + 1], cd[:, h:h + 1]
                m_mat = (cb * lmat).astype(MXU_DTYPE)
                bds = jnp.dot(bm, ds_m, preferred_element_type=F32)
                d_xd = lax.dot_general(m_mat, dy_m, c00, preferred_element_type=F32) + dte_h * bds
                d_m = lax.dot_general(dy_m, xd_m, c11, preferred_element_type=F32)
                d_cb = (d_m * lmat).astype(MXU_DTYPE)
                w_mat = d_m * cb * lmat
                dy_e = (d_y * e_a_h).astype(MXU_DTYPE)
                xd_e = (xd * dte_h).astype(MXU_DTYPE)
                dcm = dcm + jnp.dot(d_cb, bm, preferred_element_type=F32)
                dcm = dcm + lax.dot_general(dy_e, sp_m, c11, preferred_element_type=F32)
                dbm = dbm + lax.dot_general(d_cb, cm, c00, preferred_element_type=F32)
                dbm = dbm + lax.dot_general(xd_e, ds_m, c11, preferred_element_type=F32)
                ds_ref[g * n:(g + 1) * n, e * p:(e + 1) * p] = (
                    cd_h * d_s + lax.dot_general(cm, dy_e, c00, preferred_element_type=F32))
                y_off = jnp.dot(cm, sp_m, preferred_element_type=F32) * e_a_h
                q_h = jnp.sum(bds * xd, axis=1, keepdims=True) * dte_h
                da_col_h = (jnp.sum(w_mat, axis=1, keepdims=True)
                            + jnp.sum(d_y * y_off, axis=1, keepdims=True) - q_h)
                da_col = da_col + da_col_h * oh
                da_row_t = da_row_t - (sub == h).astype(F32) * jnp.sum(w_mat, axis=0, keepdims=True)
                da_last = da_last + (jnp.sum(q_h, keepdims=True) + cd_h * jnp.sum(d_s * s_prev, keepdims=True)) * oh
                dxbc_ref[:, hs] = d_xd * dt[:, h:h + 1] + dsk[:, h:h + 1] * d_y
                ddt = ddt + jnp.sum(d_xd * x_h, axis=1, keepdims=True) * oh
                ddsk = ddsk + jnp.sum(d_y * x_h, keepdims=True) * oh
            dxbc_ref[:, cfg.inner + g * n:cfg.inner + (g + 1) * n] = dbm
            dxbc_ref[:, cfg.inner + cfg.gn + g * n:cfg.inner + cfg.gn + (g + 1) * n] = dcm
        d_acs = da_col + da_row_t.T + jnp.where(row[:, 0:1] == ln - 1, da_last, 0.0)
        da_dt = lax.dot_general(tri_f, d_acs, c00, preferred_element_type=F32, precision=lax.Precision.HIGHEST)
        ddt = ddt + da_dt * a
        ddt_raw = ddt * jax.nn.sigmoid(dtr + bias)
        ddt_ref[...] = ddt_raw.astype(ddt_ref.dtype)
        da_log = jnp.sum(da_dt * dt, axis=0, keepdims=True) * a
        dvec = jnp.concatenate([jnp.sum(ddt_raw, axis=0, keepdims=True), da_log, ddsk,
                                jnp.zeros((5, LANES), F32)], axis=0)

        @pl.when(first)
        def _():
            dvec_ref[...] = dvec

        @pl.when(jnp.logical_not(first))
        def _():
            dvec_ref[...] += dvec

    vec = lambda b, c: (l, 0, 0)
    rblk = lambda b, c: (b * nc + nc - 1 - c, 0)
    return pl.pallas_call(
        body,
        out_shape=[
            jax.ShapeDtypeStruct((nb * SEQ, cfg.conv_dim), F32),
            jax.ShapeDtypeStruct((nb * SEQ, LANES), ACT_DTYPE),
            jax.ShapeDtypeStruct((8, LANES), F32),
        ],
        grid=(nb, nc),
        in_specs=_ssd_specs(cfg, nc, True) + [
            pl.BlockSpec((g_ * n, e_ * p), rblk),
            pl.BlockSpec((ln, cfg.inner), rblk),
        ] + [pl.BlockSpec((None, 1, LANES), vec)] * 3,
        out_specs=[
            pl.BlockSpec((ln, cfg.conv_dim), rblk),
            pl.BlockSpec((ln, LANES), rblk),
            pl.BlockSpec((8, LANES), lambda b, c: (0, 0)),
        ],
        scratch_shapes=[pltpu.VMEM((g_ * n, e_ * p), F32)],
        compiler_params=_cparams(("arbitrary", "arbitrary")),
        name="ssd_bwd",
    )(xbc, xbc, xbc, proj, states, dy, dt_bias, a_log, d_skip)


def _split3_dot(x, u):
    hi = x.astype(BF16)
    r1 = x - hi.astype(F32)
    mid = r1.astype(BF16)
    lo = (r1 - mid.astype(F32)).astype(BF16)
    dot = lambda a: jnp.dot(a, u, preferred_element_type=F32)
    return dot(hi) + dot(mid) + dot(lo)


def _ssd_consts(cfg):
    p = SSM_HEAD_DIM
    hrow = jnp.arange(LANES)[:, None]
    spread = (jnp.arange(cfg.inner)[None, :] // p == hrow).astype(BF16)
    spread_tile = (jnp.arange(cfg.heads * LANES)[None, :] // LANES == hrow).astype(BF16)
    return spread, spread_tile, spread.T


def _ssd_chunk_terms(dtr, bias, alog, spread, spread_tile):
    ln = SSM_CHUNK
    row = lax.broadcasted_iota(jnp.int32, (ln, ln), 0)
    col = lax.broadcasted_iota(jnp.int32, (ln, ln), 1)
    tril = row >= col
    dt, a, a_cs, a_t, e_a, dte, cd = _ssd_common(dtr, bias, alog, tril.astype(F32))
    ex = lambda v: _split_dot(v, spread)
    cd_x = ex(jnp.broadcast_to(cd, (8, LANES)))[0:1]
    colb = _split3_dot(a_cs, spread_tile)
    return dict(tril=tril, row=row, col=col, dt=dt, a=a, a_cs=a_cs, a_t=a_t, e_a=e_a, dte=dte, cd=cd,
                dt_x=ex(dt), ea_x=ex(e_a), dte_x=ex(dte), cd_x=cd_x, colb=colb)


def _head_masks():
    lane = lax.broadcasted_iota(jnp.int32, (1, LANES), 1)
    hpt = LANES // SSM_HEAD_DIM
    return [jnp.logical_and(lane >= i * SSM_HEAD_DIM, lane < (i + 1) * SSM_HEAD_DIM) for i in range(hpt)]


def _ssd_fwd(xbc, proj, dt_bias, a_log, d_skip_x, l, cfg, nb):
    ln, p, n = SSM_CHUNK, SSM_HEAD_DIM, SSM_STATE
    nc = SEQ // ln
    g_, e_ = SSM_GROUPS, cfg.epg
    gw = e_ * p
    hpt = LANES // p
    assert cfg.inner % cfg.gn == 0 and gw % LANES == 0
    spread, spread_tile, _ = _ssd_consts(cfg)

    def body(xs_ref, bm_ref, cm_ref, dtr_ref, bias_ref, alog_ref, dskx_ref, sp_ref, spt_ref, y_ref, st_ref, s_ref):
        c = pl.program_id(1)

        @pl.when(c == 0)
        def _():
            s_ref[...] = jnp.zeros_like(s_ref)

        st_ref[...] = s_ref[...]
        t = _ssd_chunk_terms(dtr_ref[...], bias_ref[...], alog_ref[...], sp_ref[...], spt_ref[...])
        hm = _head_masks()
        xs = xs_ref[...]
        xd = xs * t["dt_x"]
        xde = (xd * t["dte_x"]).astype(MXU_DTYPE)
        for g in range(g_):
            gc = slice(g * gw, (g + 1) * gw)
            bm = bm_ref[:, g * n:(g + 1) * n].astype(MXU_DTYPE)
            cm = cm_ref[:, g * n:(g + 1) * n].astype(MXU_DTYPE)
            cb = lax.dot_general(cm, bm, C11, preferred_element_type=F32)
            sg = s_ref[g * n:(g + 1) * n, :]
            y_off = jnp.dot(cm, sg.astype(MXU_DTYPE), preferred_element_type=F32) * t["ea_x"][:, gc]
            s_ref[g * n:(g + 1) * n, :] = t["cd_x"][:, gc] * sg + lax.dot_general(
                bm, xde[:, gc], C00, preferred_element_type=F32)
            for k in range(gw // LANES):
                lanes = slice(g * gw + k * LANES, g * gw + (k + 1) * LANES)
                xp = xd[:, lanes]
                acc = y_off[:, k * LANES:(k + 1) * LANES] + dskx_ref[:, lanes] * xs[:, lanes]
                for i in range(hpt):
                    h = (g * gw + k * LANES) // p + i
                    lmat = jnp.exp(jnp.where(t["tril"], t["colb"][:, h * LANES:(h + 1) * LANES] - t["a_t"][h:h + 1, :],
                                             -jnp.inf))
                    acc = acc + jnp.dot((cb * lmat).astype(MXU_DTYPE),
                                        jnp.where(hm[i], xp, 0.0).astype(MXU_DTYPE), preferred_element_type=F32)
                y_ref[:, lanes] = acc

    vec = lambda b, c: (l, 0, 0)
    whole = lambda a: pl.BlockSpec(a.shape, lambda b, c: (0, 0))
    return pl.pallas_call(
        body,
        out_shape=[
            jax.ShapeDtypeStruct((nb * SEQ, cfg.inner), F32),
            jax.ShapeDtypeStruct((nb * nc * g_ * n, gw), F32),
        ],
        grid=(nb, nc),
        in_specs=_ssd_specs(cfg, nc, False) + [pl.BlockSpec((None, 1, LANES), vec)] * 2
        + [pl.BlockSpec((None, 1, cfg.inner), vec), whole(spread), whole(spread_tile)],
        out_specs=[
            pl.BlockSpec((ln, cfg.inner), lambda b, c: (b * nc + c, 0)),
            pl.BlockSpec((g_ * n, gw), lambda b, c: (b * nc + c, 0)),
        ],
        scratch_shapes=[pltpu.VMEM((g_ * n, gw), F32)],
        compiler_params=_cparams(("parallel", "arbitrary")),
        name="ssd_fwd",
    )(xbc, xbc, xbc, proj, dt_bias, a_log, d_skip_x, spread, spread_tile)


def _ssd_bwd(xbc, proj, states, dy, dt_bias, a_log, d_skip_x, l, cfg, nb):
    ln, p, n = SSM_CHUNK, SSM_HEAD_DIM, SSM_STATE
    nc = SEQ // ln
    g_, e_ = SSM_GROUPS, cfg.epg
    gw = e_ * p
    hpt = LANES // p
    spread, spread_tile, gather_t = _ssd_consts(cfg)

    def body(xs_ref, bm_ref, cm_ref, dtr_ref, st_ref, dy_ref, bias_ref, alog_ref, dskx_ref, sp_ref, spt_ref, gt_ref,
             dxbc_ref, ddt_ref, dvec_ref, ds_ref, r1_ref, r2_ref, r4_ref, ss_ref):
        first = jnp.logical_and(pl.program_id(0) == 0, pl.program_id(1) == 0)

        @pl.when(pl.program_id(1) == 0)
        def _():
            ds_ref[...] = jnp.zeros_like(ds_ref)

        dtr = dtr_ref[...]
        bias = bias_ref[...]
        t = _ssd_chunk_terms(dtr, bias, alog_ref[...], sp_ref[...], spt_ref[...])
        tril = t["tril"]
        triu = t["row"] <= t["col"]
        hm = _head_masks()
        lane = lax.broadcasted_iota(jnp.int32, (1, LANES), 1)
        sub = lax.broadcasted_iota(jnp.int32, (LANES, 1), 0)
        xs = xs_ref[...]
        dyv = dy_ref[...]
        xd = xs * t["dt_x"]
        xde = xd * t["dte_x"]
        xde_m = xde.astype(MXU_DTYPE)
        dye_m = (dyv * t["ea_x"]).astype(MXU_DTYPE)
        da_col = jnp.zeros((ln, LANES), F32)
        da_row_t = jnp.zeros((LANES, ln), F32)
        ss_ref[...] = jnp.zeros_like(ss_ref)
        for g in range(g_):
            gc = slice(g * gw, (g + 1) * gw)
            gr = slice(g * n, (g + 1) * n)
            bm = bm_ref[:, gr].astype(MXU_DTYPE)
            cm = cm_ref[:, gr].astype(MXU_DTYPE)
            cb = lax.dot_general(cm, bm, C11, preferred_element_type=F32)
            cb_t = lax.dot_general(bm, cm, C11, preferred_element_type=F32)
            sp = st_ref[gr, :]
            dsg = ds_ref[gr, :]
            sp_m = sp.astype(MXU_DTYPE)
            dsg_m = dsg.astype(MXU_DTYPE)
            bds = jnp.dot(bm, dsg_m, preferred_element_type=F32)
            y_off = jnp.dot(cm, sp_m, preferred_element_type=F32) * t["ea_x"][:, gc]
            dcm = lax.dot_general(dye_m[:, gc], sp_m, C11, preferred_element_type=F32)
            dbm = lax.dot_general(xde_m[:, gc], dsg_m, C11, preferred_element_type=F32)
            ds_ref[gr, :] = t["cd_x"][:, gc] * dsg + lax.dot_general(cm, dye_m[:, gc], C00, preferred_element_type=F32)
            r4 = bds * xde[:, gc]
            r4_ref[:, gc] = r4
            r1_ref[:, gc] = dyv[:, gc] * y_off - r4
            ss_ref[0:1, gc] = jnp.sum(dsg * sp, axis=0, keepdims=True)
            dcb = jnp.zeros((ln, ln), F32)
            for k in range(gw // LANES):
                lanes = slice(g * gw + k * LANES, g * gw + (k + 1) * LANES)
                xp = xd[:, lanes]
                xp_m = xp.astype(MXU_DTYPE)
                dyp = dyv[:, lanes]
                dxp = t["dte_x"][:, lanes] * bds[:, k * LANES:(k + 1) * LANES]
                for i in range(hpt):
                    h = (g * gw + k * LANES) // p + i
                    diff = t["colb"][:, h * LANES:(h + 1) * LANES] - t["a_t"][h:h + 1, :]
                    lmat = jnp.exp(jnp.where(tril, diff, -jnp.inf))
                    lmat_t = jnp.exp(jnp.where(triu, -diff, -jnp.inf))
                    dy_h = jnp.where(hm[i], dyp, 0.0).astype(MXU_DTYPE)
                    d_ml = lax.dot_general(dy_h, xp_m, C11, preferred_element_type=F32) * lmat
                    dcb = dcb + d_ml
                    w_mat = d_ml * cb
                    dxp = dxp + jnp.dot((cb_t * lmat_t).astype(MXU_DTYPE), dy_h, preferred_element_type=F32)
                    da_col = da_col + jnp.sum(w_mat, axis=1, keepdims=True) * (lane == h).astype(F32)
                    da_row_t = da_row_t - (sub == h).astype(F32) * jnp.sum(w_mat, axis=0, keepdims=True)
                dxbc_ref[:, lanes] = dxp * t["dt_x"][:, lanes] + dskx_ref[:, lanes] * dyp
                r2_ref[:, lanes] = dxp * xs[:, lanes]
            dcb_m = dcb.astype(MXU_DTYPE)
            dxbc_ref[:, cfg.inner + g * n:cfg.inner + (g + 1) * n] = dbm + lax.dot_general(
                dcb_m, cm, C00, preferred_element_type=F32)
            dxbc_ref[:, cfg.inner + cfg.gn + g * n:cfg.inner + cfg.gn + (g + 1) * n] = dcm + jnp.dot(
                dcb_m, bm, preferred_element_type=F32)
        gt = gt_ref[...]
        rd = lambda v: _split_dot(v, gt)
        red4 = rd(r4_ref[...])
        da_last = jnp.sum(red4, axis=0, keepdims=True) + t["cd"] * rd(ss_ref[...])[0:1]
        d_acs = da_col + rd(r1_ref[...]) + da_row_t.T + jnp.where(t["row"][:, 0:1] == ln - 1, da_last, 0.0)
        da_dt = lax.dot_general(tril.astype(F32), d_acs, C00, preferred_element_type=F32,
                                precision=lax.Precision.HIGHEST)
        ddt = rd(r2_ref[...]) + da_dt * t["a"]
        ddt_raw = ddt * jax.nn.sigmoid(dtr + bias)
        ddt_ref[...] = ddt_raw.astype(ddt_ref.dtype)
        da_log = jnp.sum(da_dt * t["dt"], axis=0, keepdims=True) * t["a"]
        ddsk = jnp.sum(rd(dyv * xs), axis=0, keepdims=True)
        dvec = jnp.concatenate([jnp.sum(ddt_raw, axis=0, keepdims=True), da_log, ddsk,
                                jnp.zeros((5, LANES), F32)], axis=0)

        @pl.when(first)
        def _():
            dvec_ref[...] = dvec

        @pl.when(jnp.logical_not(first))
        def _():
            dvec_ref[...] += dvec

    vec = lambda b, c: (l, 0, 0)
    rblk = lambda b, c: (b * nc + nc - 1 - c, 0)
    whole = lambda a: pl.BlockSpec(a.shape, lambda b, c: (0, 0))
    return pl.pallas_call(
        body,
        out_shape=[
            jax.ShapeDtypeStruct((nb * SEQ, cfg.conv_dim), F32),
            jax.ShapeDtypeStruct((nb * SEQ, LANES), ACT_DTYPE),
            jax.ShapeDtypeStruct((8, LANES), F32),
        ],
        grid=(nb, nc),
        in_specs=_ssd_specs(cfg, nc, True) + [
            pl.BlockSpec((g_ * n, gw), rblk),
            pl.BlockSpec((ln, cfg.inner), rblk),
        ] + [pl.BlockSpec((None, 1, LANES), vec)] * 2 + [pl.BlockSpec((None, 1, cfg.inner), vec),
                                                           whole(spread), whole(spread_tile), whole(gather_t)],
        out_specs=[
            pl.BlockSpec((ln, cfg.conv_dim), rblk),
            pl.BlockSpec((ln, LANES), rblk),
            pl.BlockSpec((8, LANES), lambda b, c: (0, 0)),
        ],
        scratch_shapes=[pltpu.VMEM((g_ * n, gw), F32)] + [pltpu.VMEM((ln, cfg.inner), F32)] * 3
        + [pltpu.VMEM((8, cfg.inner), F32)],
        compiler_params=_cparams(("arbitrary", "arbitrary")),
        name="ssd_bwd",
    )(xbc, xbc, xbc, proj, states, dy, dt_bias, a_log, d_skip_x, spread, spread_tile, gather_t)


XA_ROWS = 256


def _xa_probs(q_ref, kv_ref, h, dh):
    c11 = (((1,), (1,)), ((), ()))
    qh = q_ref[:, h * dh:(h + 1) * dh].astype(MXU_DTYPE)
    kh = kv_ref[:, h * dh:(h + 1) * dh].astype(MXU_DTYPE)
    vh = kv_ref[:, D_MODEL + h * dh:D_MODEL + (h + 1) * dh].astype(MXU_DTYPE)
    s = lax.dot_general(qh, kh, c11, preferred_element_type=F32) * (dh ** -0.5)
    s = s - jnp.max(s, axis=1, keepdims=True)
    pr = jnp.exp(s)
    return qh, kh, vh, pr / jnp.sum(pr, axis=1, keepdims=True)


def _xa_fwd(q, kv, cfg, nb):
    tq = _tile(SEQ, XA_ROWS, 16)
    nq = SEQ // tq
    dh = cfg.xa_dim

    def body(q_ref, kv_ref, o_ref):
        for h in range(XA_HEADS):
            _, _, vh, pr = _xa_probs(q_ref, kv_ref, h, dh)
            o_ref[:, h * dh:(h + 1) * dh] = jnp.dot(pr.astype(MXU_DTYPE), vh, preferred_element_type=F32).astype(o_ref.dtype)

    return pl.pallas_call(
        body,
        out_shape=jax.ShapeDtypeStruct((nb * SEQ, D_MODEL), ACT_DTYPE),
        grid=(nb, nq),
        in_specs=[
            pl.BlockSpec((tq, D_MODEL), lambda b, i: (b * nq + i, 0)),
            pl.BlockSpec((MEM_LEN, 2 * D_MODEL), lambda b, i: (b, 0)),
        ],
        out_specs=pl.BlockSpec((tq, D_MODEL), lambda b, i: (b * nq + i, 0)),
        compiler_params=_cparams(("parallel", "parallel")),
        name="xa_fwd",
    )(q, kv)


def _xa_bwd(q, kv, do, cfg, nb):
    tq = _tile(SEQ, XA_ROWS, 16)
    nq = SEQ // tq
    dh = cfg.xa_dim
    c00 = (((0,), (0,)), ((), ()))
    c11 = (((1,), (1,)), ((), ()))
    scale = dh ** -0.5

    def body(q_ref, kv_ref, do_ref, dq_ref, dkv_ref, acc_ref):
        i = pl.program_id(1)

        @pl.when(i == 0)
        def _():
            acc_ref[...] = jnp.zeros_like(acc_ref)

        for h in range(XA_HEADS):
            hs = slice(h * dh, (h + 1) * dh)
            vs = slice(D_MODEL + h * dh, D_MODEL + (h + 1) * dh)
            qh, kh, vh, pr = _xa_probs(q_ref, kv_ref, h, dh)
            do_h = do_ref[:, hs].astype(MXU_DTYPE)
            dp = lax.dot_general(do_h, vh, c11, preferred_element_type=F32)
            ds = (pr * (dp - jnp.sum(dp * pr, axis=1, keepdims=True))).astype(MXU_DTYPE)
            dq_ref[:, hs] = (jnp.dot(ds, kh, preferred_element_type=F32) * scale).astype(dq_ref.dtype)
            acc_ref[:, hs] += lax.dot_general(ds, qh, c00, preferred_element_type=F32) * scale
            acc_ref[:, vs] += lax.dot_general(pr.astype(MXU_DTYPE), do_h, c00, preferred_element_type=F32)

        @pl.when(i == nq - 1)
        def _():
            dkv_ref[...] = acc_ref[...].astype(dkv_ref.dtype)

    return pl.pallas_call(
        body,
        out_shape=[
            jax.ShapeDtypeStruct((nb * SEQ, D_MODEL), ACT_DTYPE),
            jax.ShapeDtypeStruct((nb * MEM_LEN, 2 * D_MODEL), ACT_DTYPE),
        ],
        grid=(nb, nq),
        in_specs=[
            pl.BlockSpec((tq, D_MODEL), lambda b, i: (b * nq + i, 0)),
            pl.BlockSpec((MEM_LEN, 2 * D_MODEL), lambda b, i: (b, 0)),
            pl.BlockSpec((tq, D_MODEL), lambda b, i: (b * nq + i, 0)),
        ],
        out_specs=[
            pl.BlockSpec((tq, D_MODEL), lambda b, i: (b * nq + i, 0)),
            pl.BlockSpec((MEM_LEN, 2 * D_MODEL), lambda b, i: (b, 0)),
        ],
        scratch_shapes=[pltpu.VMEM((MEM_LEN, 2 * D_MODEL), F32)],
        compiler_params=_cparams(("parallel", "arbitrary")),
        name="xa_bwd",
    )(q, kv, do)


def _adamw(parts, w, m, v, name, tr=128):
    n, nl, r, c = parts.shape
    tr = _tile(r, tr, 16)

    def body(p_ref, w_ref, m_ref, v_ref, g_ref, d_ref, nm_ref, nv_ref):
        g = p_ref[0].astype(F32)
        for i in range(1, n):
            g = g + p_ref[i].astype(F32)
        m2 = ADAM_B1 * m_ref[...] + (1.0 - ADAM_B1) * g
        v2 = ADAM_B2 * v_ref[...] + (1.0 - ADAM_B2) * (g * g)
        m_hat = m2 / (1.0 - ADAM_B1 ** ADAM_STEP)
        v_hat = v2 / (1.0 - ADAM_B2 ** ADAM_STEP)
        g_ref[...] = g
        d_ref[...] = -ADAM_LR * (m_hat / (jnp.sqrt(v_hat) + ADAM_EPS) + ADAM_WD * w_ref[...])
        nm_ref[...] = m2
        nv_ref[...] = v2

    blk = pl.BlockSpec((None, tr, c), lambda l, i: (l, i, 0))
    return pl.pallas_call(
        body,
        out_shape=[jax.ShapeDtypeStruct((nl, r, c), F32)] * 4,
        grid=(nl, r // tr),
        in_specs=[pl.BlockSpec((n, None, tr, c), lambda l, i: (0, l, i, 0)), blk, blk, blk],
        out_specs=[blk] * 4,
        compiler_params=_cparams(("parallel", "parallel")),
        name=name,
    )(parts, w, m, v)


def _flat_index(px, py, pc):
    return 4 * px + 2 * py + pc


def _all_gather(arrs, name):
    n = len(arrs)

    def body(*refs):
        start, forward, finish = _gather_phases(refs[:n], refs[n:2 * n], *refs[2 * n:])
        start()
        forward()
        finish()

    return pl.pallas_call(
        body,
        out_shape=_gather_shapes(arrs),
        in_specs=[ANY_SPEC] * n,
        out_specs=[ANY_SPEC] * n,
        scratch_shapes=_comm_sems(n),
        name=name,
    )(*arrs)


ANY_SPEC = pl.BlockSpec(memory_space=pl.ANY)


def _comm_sems(n):
    return [pltpu.SemaphoreType.DMA((n, N_DEV - 1)), pltpu.SemaphoreType.DMA((n, N_DEV - 1)),
            pltpu.SemaphoreType.DMA((n,))]


def _gather_shapes(arrs):
    return [jax.ShapeDtypeStruct((N_DEV,) + a.shape, a.dtype) for a in arrs]


def _gather_phases(ins, outs, send_sems, recv_sems, local_sems):
    n = len(ins)
    x, y, c = lax.axis_index("x"), lax.axis_index("y"), lax.axis_index("c")
    me, sibling = (x, y, c), (x, y, 1 - c)
    chips = [(1 - x, y), (x, 1 - y), (1 - x, 1 - y)]

    def copy(a, k, block, to, src=None):
        slot = outs[a].at[_flat_index(*block)]
        return pltpu.make_async_remote_copy(
            src_ref=slot if src is None else src, dst_ref=slot,
            send_sem=send_sems.at[a, k], recv_sem=recv_sems.at[a, k],
            device_id=to, device_id_type=MESH)

    def mine(a):
        return pltpu.make_async_copy(ins[a], outs[a].at[_flat_index(*me)], local_sems.at[a])

    def first(a):
        return [copy(a, 0, me, sibling, src=ins[a])] + [
            copy(a, 1 + j, me, (*chip, c), src=ins[a]) for j, chip in enumerate(chips)]

    def start():
        for a in range(n):
            mine(a).start()
            for cp in first(a):
                cp.start()

    def forward():
        for j, chip in enumerate(chips):
            for a in range(n):
                copy(a, 1 + j, (*chip, c), me).wait_recv()
                copy(a, 4 + j, (*chip, c), sibling).start()

    def finish():
        for a in range(n):
            copy(a, 0, sibling, me).wait_recv()
            for j, chip in enumerate(chips):
                copy(a, 4 + j, (*chip, 1 - c), me).wait_recv()
        for a in range(n):
            for cp in first(a):
                cp.wait_send()
            for j, chip in enumerate(chips):
                copy(a, 4 + j, (*chip, c), sibling).wait_send()
            mine(a).wait()

    return start, forward, finish


def _scatter_blocks(arrs, name):
    n = len(arrs)

    def body(*refs):
        start, finish = _scatter_phases(refs[:n], refs[n:2 * n], *refs[2 * n:])
        start()
        finish()

    return pl.pallas_call(
        body,
        out_shape=[jax.ShapeDtypeStruct(a.shape, a.dtype) for a in arrs],
        in_specs=[ANY_SPEC] * n,
        out_specs=[ANY_SPEC] * n,
        scratch_shapes=_comm_sems(n),
        name=name,
    )(*arrs)


def _scatter_phases(ins, outs, send_sems, recv_sems, local_sems):
    n = len(ins)
    x, y, c = lax.axis_index("x"), lax.axis_index("y"), lax.axis_index("c")
    me = _flat_index(x, y, c)

    def peer(k):
        return (1 - x if k & 4 else x, 1 - y if k & 2 else y, 1 - c if k & 1 else c)

    def copy(a, k):
        p = peer(k)
        return pltpu.make_async_remote_copy(
            src_ref=ins[a].at[_flat_index(*p)], dst_ref=outs[a].at[me],
            send_sem=send_sems.at[a, k - 1], recv_sem=recv_sems.at[a, k - 1],
            device_id=p, device_id_type=MESH)

    def landed(a, k):
        slot = outs[a].at[_flat_index(*peer(k))]
        return pltpu.make_async_remote_copy(
            src_ref=slot, dst_ref=slot, send_sem=send_sems.at[a, k - 1], recv_sem=recv_sems.at[a, k - 1],
            device_id=peer(k), device_id_type=MESH)

    def mine(a):
        return pltpu.make_async_copy(ins[a].at[me], outs[a].at[me], local_sems.at[a])

    def start():
        for a in range(n):
            mine(a).start()
            for k in range(1, N_DEV):
                copy(a, k).start()

    def finish():
        for a in range(n):
            for k in range(1, N_DEV):
                landed(a, k).wait_recv()
        for a in range(n):
            for k in range(1, N_DEV):
                copy(a, k).wait_send()
            mine(a).wait()

    return start, finish


_BIG = ("w_in", "w_br_att", "w_br_ssm", "w_mix_out", "w_xq", "w_xkv", "w_xo", "w_gu", "w_down")
_COL_SHARDED = ("w_in", "w_xkv", "w_gu", "conv_w")
_SMALL = ("g_pre_mix", "conv_b", "dt_bias", "a_log", "d_skip", "g_ssm_norm", "g_post_mix", "g_pre_xa",
          "g_mem", "g_post_xa", "g_pre_ffn", "g_post_ffn")
_WEIGHTS = ("g_pre_mix", "w_in", "conv_w", "conv_b", "dt_bias", "a_log", "d_skip", "g_ssm_norm", "w_br_att",
            "w_br_ssm", "w_mix_out", "g_post_mix", "g_pre_xa", "g_mem", "w_xq", "w_xkv", "w_xo", "g_post_xa",
            "g_pre_ffn", "w_gu", "w_down", "g_post_ffn")
PACK_W = 8 * LANES


def _unshard(g, col):
    n, r, c = g.shape
    if col:
        return jnp.transpose(g, (1, 0, 2)).reshape(r, n * c)
    return g.reshape(n * r, c)


def _shard(w, col):
    r, c = w.shape
    if col:
        return jnp.transpose(w.reshape(r, N_DEV, c // N_DEV), (1, 0, 2))
    return w.reshape(N_DEV, r // N_DEV, c)


def _permute_in(w, cfg):
    parts, off = [], 0
    for size in cfg.in_sizes:
        parts.append(w[..., off:off + size])
        off += size
    q, k, v, z, xbc, dt, ga, gs = parts
    pad = jnp.zeros(w.shape[:-1] + (LANES - cfg.heads,), w.dtype)
    return jnp.concatenate([z, ga, gs, q, k, v, xbc, dt, pad], axis=-1)


def _unpermute_in(w, cfg):
    c = cfg
    sl = lambda a, n: w[..., a:a + n]
    return jnp.concatenate([sl(c.q0, c.sbw), sl(c.k0, c.sbw), sl(c.v0, c.sbw), sl(c.z0, c.inner),
                            sl(c.xbc0, c.conv_dim), sl(c.dt0, c.heads), sl(c.ga0, c.d), sl(c.gs0, c.d)], axis=-1)


def _pack(arrs):
    flat = jnp.concatenate([a.reshape(-1).astype(F32) for a in arrs])
    rows = -(-flat.shape[0] // PACK_W)
    rows = -(-rows // 8) * 8
    return jnp.pad(flat, (0, rows * PACK_W - flat.shape[0])).reshape(rows, PACK_W)


def _unpack(p, shapes):
    flat = p.reshape(-1)
    out, off = [], 0
    for s in shapes:
        size = math.prod(s)
        out.append(flat[off:off + size].reshape(s))
        off += size
    return out


def _vec3(a, width=None):
    if width is not None and a.shape[1] < width:
        a = jnp.pad(a, ((0, 0), (0, width - a.shape[1])))
    return a[:, None, :]


def _forward_layer(l, xin, h1, memf, tgt, w, p, cfg, nb, last, gather):
    t = xin.shape[0]
    d = cfg.d
    s = {"x_in": xin, "h1": h1}
    proj = _mm(h1, w["w_in"], name="mm_proj")
    s["proj"] = proj
    s["o_att"], gathered = _sb_fwd(proj, cfg, nb, gather)
    s["xbc"] = _conv_fwd(proj, w["conv_w"], p["conv_b"], l, cfg, nb)
    s["y"], s["states"] = _ssd_fwd(s["xbc"], proj, p["dt_bias"], p["a_log"], p["d_skip_x"], l, cfg, nb)
    s["o_ssm"] = _rowwise(_f_gate_norm, "gate_norm_fwd", t, [_full(s["y"]), (proj, cfg.inner, 0)],
                          [(p["g_ssm_norm"], l)], [(cfg.inner, ACT_DTYPE)])[0]
    s["ba"] = _mm(s["o_att"], w["w_br_att"], name="mm_br_att")
    s["bs"] = _mm(s["o_ssm"], w["w_br_ssm"], name="mm_br_ssm")
    s["merged"] = _rowwise(_f_merge, "merge_fwd", t,
                           [(proj, d, cfg.ga0 // d), (proj, d, cfg.gs0 // d), _full(s["ba"]), _full(s["bs"])],
                           [], [(d, ACT_DTYPE)])[0]
    s["mo"] = _mm(s["merged"], w["w_mix_out"], name="mm_mix_out")
    s["x1"], s["h2"] = _rowwise(_f_post_pre, "post_pre_mix", t, [_full(xin), _full(s["mo"])],
                                [(p["g_post_mix"], l), (p["g_pre_xa"], l)], [(d, F32), (d, ACT_DTYPE)])
    s["mem_n"] = _rowwise(_rms, "mem_norm", memf.shape[0], [_full(memf)], [(p["g_mem"], l)], [(d, ACT_DTYPE)])[0]
    s["q"] = _mm(s["h2"], w["w_xq"], name="mm_xq")
    s["kv"] = _mm(s["mem_n"], w["w_xkv"], name="mm_xkv")
    s["o_xa"] = _xa_fwd(s["q"], s["kv"], cfg, nb)
    s["xo"] = _mm(s["o_xa"], w["w_xo"], name="mm_xo")
    s["x2"], s["h3"] = _rowwise(_f_post_pre, "post_pre_xa", t, [_full(s["x1"]), _full(s["xo"])],
                                [(p["g_post_xa"], l), (p["g_pre_ffn"], l)], [(d, F32), (d, ACT_DTYPE)])
    s["gu"] = _mm(s["h3"], w["w_gu"], name="mm_gu")
    s["act"] = _rowwise(_f_swiglu, "swiglu_fwd", t, [(s["gu"], cfg.ffn, 0), (s["gu"], cfg.ffn, 1)], [],
                        [(cfg.ffn, ACT_DTYPE)])[0]
    s["dn"] = _mm(s["act"], w["w_down"], name="mm_down")
    if last:
        nxt = _rowwise(_f_final, "final_loss", t, [_full(s["x2"]), _full(s["dn"]), _full(tgt)],
                       [(p["g_post_ffn"], l)], [(d, F32)], acc_out=[(1, d)])
    else:
        nxt = _rowwise(_f_post_pre, "post_pre_ffn", t, [_full(s["x2"]), _full(s["dn"])],
                       [(p["g_post_ffn"], l), (p["g_pre_mix"], l + 1)], [(d, F32), (d, ACT_DTYPE)])
    return s, nxt, gathered


def _backward_layer(l, s, dx, d_dn, memf, w, p, cfg, nb, prev_dn, scatter):
    t = dx.shape[0]
    d = cfg.d
    g = {}
    dact = _mm(d_dn, w["w_down"], tb=True, out_dtype=ACT_DTYPE, name="mm_d_act", tn=1408)
    g["w_down"] = _mm(s["act"], d_dn, ta=True, name="mm_dw_down")
    dgu = _rowwise(_f_swiglu_bwd, "swiglu_bwd", t, [(s["gu"], cfg.ffn, 0), (s["gu"], cfg.ffn, 1), _full(dact)], [],
                   [(2 * cfg.ffn, ACT_DTYPE)])[0]
    dh3 = _mm(dgu, w["w_gu"], tb=True, name="mm_d_h3", tk=1408)
    g["w_gu"] = _mm(s["h3"], dgu, ta=True, name="mm_dw_gu")
    dx2, d_xo, g["g_pre_ffn"], g["g_post_xa"] = _rowwise(
        _f_pre_post_bwd, "pre_post_bwd_ffn", t, [_full(s["x2"]), _full(dh3), _full(dx), _full(s["xo"])],
        [(p["g_pre_ffn"], l), (p["g_post_xa"], l)], [(d, F32), (d, ACT_DTYPE)], acc_out=[(1, d), (1, d)])
    do_xa = _mm(d_xo, w["w_xo"], tb=True, out_dtype=ACT_DTYPE, name="mm_d_oxa")
    g["w_xo"] = _mm(s["o_xa"], d_xo, ta=True, name="mm_dw_xo")
    dq, dkv = _xa_bwd(s["q"], s["kv"], do_xa, cfg, nb)
    dh2 = _mm(dq, w["w_xq"], tb=True, name="mm_d_h2")
    g["w_xq"] = _mm(s["h2"], dq, ta=True, name="mm_dw_xq")
    dmem_n = _mm(dkv, w["w_xkv"], tb=True, name="mm_d_mem")
    g["w_xkv"] = _mm(s["mem_n"], dkv, ta=True, name="mm_dw_xkv")
    g["g_mem"] = _rowwise(_f_gain_bwd, "mem_norm_bwd", memf.shape[0], [_full(memf), _full(dmem_n)],
                          [(p["g_mem"], l)], [], acc_out=[(1, d)])[0]
    dx1, d_mo, g["g_pre_xa"], g["g_post_mix"] = _rowwise(
        _f_pre_post_bwd, "pre_post_bwd_xa", t, [_full(s["x1"]), _full(dh2), _full(dx2), _full(s["mo"])],
        [(p["g_pre_xa"], l), (p["g_post_mix"], l)], [(d, F32), (d, ACT_DTYPE)], acc_out=[(1, d), (1, d)])
    dmerged = _mm(d_mo, w["w_mix_out"], tb=True, out_dtype=ACT_DTYPE, name="mm_d_merged")
    g["w_mix_out"] = _mm(s["merged"], d_mo, ta=True, name="mm_dw_mix_out")
    proj = s["proj"]
    dgg, dba, dbs = _rowwise(
        _f_merge_bwd, "merge_bwd", t,
        [(proj, d, cfg.ga0 // d), (proj, d, cfg.gs0 // d), _full(s["ba"]), _full(s["bs"]), _full(dmerged)], [],
        [(2 * d, ACT_DTYPE), (d, ACT_DTYPE), (d, ACT_DTYPE)])
    do_att = _mm(dba, w["w_br_att"], tb=True, name="mm_d_oatt")
    g["w_br_att"] = _mm(s["o_att"], dba, ta=True, name="mm_dw_br_att")
    do_ssm = _mm(dbs, w["w_br_ssm"], tb=True, out_dtype=ACT_DTYPE, name="mm_d_ossm")
    g["w_br_ssm"] = _mm(s["o_ssm"], dbs, ta=True, name="mm_dw_br_ssm")
    dy, dz, g["g_ssm_norm"] = _rowwise(
        _f_gate_norm_bwd, "gate_norm_bwd", t, [_full(s["y"]), (proj, cfg.inner, 0), _full(do_ssm)],
        [(p["g_ssm_norm"], l)], [(cfg.inner, F32), (cfg.inner, ACT_DTYPE)], acc_out=[(1, cfg.inner)])
    dxbc, ddt_raw, dvec = _ssd_bwd(s["xbc"], proj, s["states"], dy, p["dt_bias"], p["a_log"], p["d_skip_x"], l, cfg, nb)
    g["dt_bias"], g["a_log"], g["d_skip"] = (dvec[i:i + 1, :cfg.heads] for i in range(3))
    dxbc_raw, g["conv_w"], g["conv_b"] = _conv_bwd(proj, dxbc, w["conv_w"], p["conv_b"], l, cfg, nb)
    own = [_shard(g[n], n in _COL_SHARDED).astype(WIRE_DTYPE) for n in _BIG if n != "w_in"]
    dq_sb, dk_sb, dv_sb, landed = _sb_bwd(proj, do_att, cfg, nb, list(scatter) + own)
    landed = (landed[:len(scatter)], landed[len(scatter):])
    dproj = jnp.concatenate([dz, dgg, dq_sb, dk_sb.astype(ACT_DTYPE), dv_sb.astype(ACT_DTYPE), dxbc_raw, ddt_raw], axis=1)
    dh1 = _mm(dproj, w["w_in"], tb=True, name="mm_d_h1", tk=1152)
    g["w_in"] = _mm(s["h1"], dproj, ta=True, name="mm_dw_in")
    if prev_dn is None:
        dx0, g["g_pre_mix"] = _rowwise(_f_pre_bwd, "pre_bwd_first", t, [_full(s["x_in"]), _full(dh1), _full(dx1)],
                                       [(p["g_pre_mix"], l)], [(d, F32)], acc_out=[(1, d)])
        return g, dx0, None, None, landed
    dx0, d_dn_prev, g["g_pre_mix"], g_post_prev = _rowwise(
        _f_pre_post_bwd, "pre_post_bwd_mix", t, [_full(s["x_in"]), _full(dh1), _full(dx1), _full(prev_dn)],
        [(p["g_pre_mix"], l), (p["g_post_ffn"], l - 1)], [(d, F32), (d, ACT_DTYPE)], acc_out=[(1, d), (1, d)])
    return g, dx0, d_dn_prev, g_post_prev, landed


def kernel(x, mem, g_pre_mix, w_in, conv_w, conv_b, dt_bias, a_log, d_skip, g_ssm_norm, w_br_att, w_br_ssm, w_mix_out, g_post_mix, g_pre_xa, g_mem, w_xq, w_xkv, w_xo, g_post_xa, g_pre_ffn, w_gu, w_down, g_post_ffn, loss_target, m_g_pre_mix, m_w_in, m_conv_w, m_conv_b, m_dt_bias, m_a_log, m_d_skip, m_g_ssm_norm, m_w_br_att, m_w_br_ssm, m_w_mix_out, m_g_post_mix, m_g_pre_xa, m_g_mem, m_w_xq, m_w_xkv, m_w_xo, m_g_post_xa, m_g_pre_ffn, m_w_gu, m_w_down, m_g_post_ffn, v_g_pre_mix, v_w_in, v_conv_w, v_conv_b, v_dt_bias, v_a_log, v_d_skip, v_g_ssm_norm, v_w_br_att, v_w_br_ssm, v_w_mix_out, v_g_post_mix, v_g_pre_xa, v_g_mem, v_w_xq, v_w_xkv, v_w_xo, v_g_post_xa, v_g_pre_ffn, v_w_gu, v_w_down, v_g_post_ffn):
    vals = dict(locals())
    cfg = _Cfg()
    nb = x.shape[0]
    t = nb * SEQ
    d = cfg.d
    depth = g_pre_mix.shape[0]

    def wire(l):
        return [vals[n][l].astype(WIRE_DTYPE) for n in _BIG] + [conv_w[l]]

    def layer_weights(gathered):
        w = {n: _unshard(gw, n in _COL_SHARDED) for n, gw in zip(_BIG + ("conv_w",), gathered)}
        w["w_in"] = _permute_in(w["w_in"], cfg)
        w["conv_w"] = w["conv_w"][None]
        return w

    p = {n: _vec3(vals[n], LANES if n in ("dt_bias", "a_log", "d_skip") else None) for n in _SMALL}
    p["d_skip_x"] = _vec3(jnp.repeat(d_skip, SSM_HEAD_DIM, axis=1))
    weights = [None] * depth
    weights[0] = layer_weights(_all_gather(wire(0), "ag_weights_first"))

    xf = x.reshape(t, d)
    memf = mem.reshape(nb * MEM_LEN, d)
    tgt = loss_target.reshape(t, d)
    h = _rowwise(_rms, "pre_norm_first", t, [_full(xf)], [(p["g_pre_mix"], 0)], [(d, ACT_DTYPE)])[0]
    saved = []
    xcur = xf
    for l in range(depth):
        last = l == depth - 1
        s, nxt, gathered = _forward_layer(l, xcur, h, memf, tgt, weights[l], p, cfg, nb, last,
                                          () if last else wire(l + 1))
        saved.append(s)
        if not last:
            weights[l + 1] = layer_weights(gathered)
            xcur, h = nxt
    dx, loss_row = nxt
    loss = lax.psum(0.5 * jnp.sum(loss_row) / d, AXES)

    top = saved[-1]
    d_dn, g_post_top = _rowwise(lambda ysub, dxo, gp: _rms_bwd(ysub, gp, dxo), "post_bwd_last", t,
                                [_full(top["dn"]), _full(dx)], [(p["g_post_ffn"], depth - 1)], [(d, ACT_DTYPE)],
                                acc_out=[(1, d)])
    grads = [None] * depth
    landed = [dict() for _ in range(depth)]
    post_ffn = [None] * depth
    post_ffn[depth - 1] = g_post_top
    pending = []
    for l in reversed(range(depth)):
        prev_dn = saved[l - 1]["dn"] if l > 0 else None
        grads[l], dx, d_dn, g_post_prev, (got_above, got_own) = _backward_layer(
            l, saved[l], dx, d_dn, memf, weights[l], p, cfg, nb, prev_dn, pending)
        if pending:
            landed[l + 1]["w_in"] = got_above[0]
        landed[l].update(zip([n for n in _BIG if n != "w_in"], got_own))
        pending = [_shard(_unpermute_in(grads[l]["w_in"], cfg), True).astype(WIRE_DTYPE)]
        if l > 0:
            post_ffn[l - 1] = g_post_prev
    landed[0]["w_in"] = _scatter_blocks(pending, "scatter_grads_last")[0]
    for l in range(depth):
        grads[l]["g_post_ffn"] = post_ffn[l]
    grad_x = dx.reshape(x.shape)
    stacked = {n: jnp.stack([grads[l][n] for l in range(depth)]) for n in _SMALL + ("conv_w",)}

    out = {}
    for n in _BIG:
        parts = jnp.stack([landed[l][n] for l in range(depth)], axis=1)
        out[n] = _adamw(parts, vals[n], vals["m_" + n], vals["v_" + n], "adamw_" + n)

    small_shapes = [vals[n].shape for n in _SMALL]
    pack_g = _pack([stacked[n] for n in _SMALL])
    conv_g = stacked["conv_w"].reshape(depth * SSM_CONV, cfg.conv_dim)
    parts_small, parts_conv = _all_gather([pack_g, conv_g], "ag_small_grads")
    packed = lambda prefix: _pack([vals[prefix + n] for n in _SMALL])[None]
    res = _adamw(parts_small[:, None], packed(""), packed("m_"), packed("v_"), "adamw_small")
    unpacked = [_unpack(r, small_shapes) for r in res]
    for i, n in enumerate(_SMALL):
        out[n] = [unpacked[j][i] for j in range(4)]
    cs = conv_w.shape[2]
    me = _flat_index(lax.axis_index("x"), lax.axis_index("y"), lax.axis_index("c"))
    parts_conv = lax.dynamic_slice_in_dim(parts_conv, me * cs, cs, axis=2)
    flat = lambda a: a.reshape(1, depth * SSM_CONV, cs)
    res = _adamw(parts_conv[:, None], flat(conv_w), flat(m_conv_w), flat(v_conv_w), "adamw_conv_w")
    out["conv_w"] = [r.reshape(conv_w.shape) for r in res]

    return (loss, grad_x, *[out[n][0] for n in _WEIGHTS], *[out[n][1] for n in _WEIGHTS],
            *[out[n][2] for n in _WEIGHTS], *[out[n][3] for n in _WEIGHTS])
```

```python
import functools
import math

import jax
import jax.numpy as jnp
from jax import lax
from jax.experimental import pallas as pl
from jax.experimental.pallas import tpu as pltpu

F32 = jnp.float32
BF16 = jnp.bfloat16
MXU_DTYPE = BF16
ACT_DTYPE = BF16
WIRE_DTYPE = BF16

D_MODEL = 1024
SEQ = 2048
DEPTH = 4
MEM_LEN = 256
RMS_EPS = 1e-6
SB_HEADS = 16
SB_HEAD_DIM = 64
SB_BLOCK = 128
SSM_INNER = 2 * D_MODEL
SSM_HEAD_DIM = 64
SSM_GROUPS = 4
SSM_STATE = 128
SSM_CONV = 4
SSM_CHUNK = 128
XA_HEADS = 4
FFN_HIDDEN = ((8 * D_MODEL + 767) // 768) * 256
ADAM_LR = 0.001
ADAM_B1 = 0.9
ADAM_B2 = 0.999
ADAM_EPS = 1e-08
ADAM_WD = 0.01
ADAM_STEP = 10

N_DEV = 8
LANES = 128
VMEM_LIMIT_BYTES = 56 * 1024 * 1024

AXES = ("x", "y", "c")
MESH = pl.DeviceIdType.MESH


class _Cfg:
    def __init__(self):
        self.d = D_MODEL
        self.sbw = SB_HEADS * SB_HEAD_DIM
        self.inner = SSM_INNER
        self.heads = SSM_INNER // SSM_HEAD_DIM
        self.epg = self.heads // SSM_GROUPS
        self.gn = SSM_GROUPS * SSM_STATE
        self.conv_dim = SSM_INNER + 2 * self.gn
        self.ffn = FFN_HIDDEN
        self.xa_dim = D_MODEL // XA_HEADS
        self.in_sizes = (self.sbw, self.sbw, self.sbw, self.inner, self.conv_dim, self.heads, self.d, self.d)
        self.in_width = sum(self.in_sizes)
        self.z0 = 0
        self.ga0 = self.inner
        self.gs0 = self.ga0 + self.d
        self.q0 = self.gs0 + self.d
        self.k0 = self.q0 + self.sbw
        self.v0 = self.k0 + self.sbw
        self.xbc0 = self.v0 + self.sbw
        self.dt0 = self.xbc0 + self.conv_dim
        self.proj_w = self.dt0 + LANES
        assert self.heads <= LANES


def _cparams(sem=None):
    return pltpu.CompilerParams(dimension_semantics=sem, vmem_limit_bytes=VMEM_LIMIT_BYTES)


def _tile(n, pref, mult):
    if n <= pref:
        return n
    t = (pref // mult) * mult
    while t >= mult:
        if n % t == 0:
            return t
        t -= mult
    return n


MM_VMEM_BUDGET = 40 * 1024 * 1024


def _mm(a, b, *, ta=False, tb=False, out_dtype=F32, name, tm=1024, tn=1152, tk=2048):
    kk, m = (a.shape if ta else a.shape[::-1])
    if tb:
        n, k2 = b.shape
    else:
        k2, n = b.shape
    assert kk == k2, (name, a.shape, b.shape)
    size = lambda dt: jnp.dtype(dt).itemsize
    tn = _tile(n, tn, LANES)
    tk = _tile(kk, tk, LANES if (tb or not ta) else 16)
    nk = kk // tk
    while True:
        tm_ = _tile(m, tm, LANES if ta else 16)
        need = (2 * (tm_ * tk * size(a.dtype) + tk * tn * size(b.dtype) + tm_ * tn * size(out_dtype))
                + (tm_ * tk + tk * tn) * size(MXU_DTYPE) + tm_ * tn * 4 * (2 if nk > 1 else 1))
        if need <= MM_VMEM_BUDGET or tm <= 128:
            break
        tm //= 2
    tm = tm_
    gi, gj = m // tm, n // tn
    j_outer = nk == 1 and b.size * size(b.dtype) * (gi - 1) > a.size * size(a.dtype) * (gj - 1)
    dims = (((0 if ta else 1,), (1 if tb else 0,)), ((), ()))

    def ij(g0, g1):
        return (g1, g0) if j_outer else (g0, g1)

    def body(a_ref, b_ref, o_ref, *scratch):
        av = a_ref[...].astype(MXU_DTYPE)
        bv = b_ref[...].astype(MXU_DTYPE)
        part = lax.dot_general(av, bv, dims, preferred_element_type=F32)
        if nk == 1:
            o_ref[...] = part.astype(out_dtype)
        else:
            acc_ref, = scratch
            k = pl.program_id(2)

            @pl.when(k == 0)
            def _():
                acc_ref[...] = part

            @pl.when(k > 0)
            def _():
                acc_ref[...] += part

            @pl.when(k == nk - 1)
            def _():
                o_ref[...] = acc_ref[...].astype(out_dtype)

    if ta:
        a_spec = pl.BlockSpec((tk, tm), lambda g0, g1, k: (k, ij(g0, g1)[0]))
    else:
        a_spec = pl.BlockSpec((tm, tk), lambda g0, g1, k: (ij(g0, g1)[0], k))
    if tb:
        b_spec = pl.BlockSpec((tn, tk), lambda g0, g1, k: (ij(g0, g1)[1], k))
    else:
        b_spec = pl.BlockSpec((tk, tn), lambda g0, g1, k: (k, ij(g0, g1)[1]))
    return pl.pallas_call(
        body,
        out_shape=jax.ShapeDtypeStruct((m, n), out_dtype),
        grid=(gj, gi, nk) if j_outer else (gi, gj, nk),
        in_specs=[a_spec, b_spec],
        out_specs=pl.BlockSpec((tm, tn), lambda g0, g1, k: ij(g0, g1)),
        scratch_shapes=[] if nk == 1 else [pltpu.VMEM((tm, tn), F32)],
        compiler_params=_cparams(("parallel", "parallel", "arbitrary")),
        name=name,
    )(a, b)


def _rowwise(fn, name, rows, row_in, vec_in, row_out, acc_out=(), tr=256):
    tr = _tile(rows, tr, 16)
    n_in = len(row_in) + len(vec_in)
    n_ro = len(row_out)

    def body(*refs):
        ins = [r[...].astype(F32) for r in refs[:n_in]]
        outs = fn(*ins)
        if not isinstance(outs, (tuple, list)):
            outs = (outs,)
        out_refs = refs[n_in:]
        for o_ref, val in zip(out_refs[:n_ro], outs[:n_ro]):
            o_ref[...] = val.astype(o_ref.dtype)
        if acc_out:
            i = pl.program_id(0)
            for o_ref, val in zip(out_refs[n_ro:], outs[n_ro:]):
                @pl.when(i == 0)
                def _(o_ref=o_ref, val=val):
                    o_ref[...] = val

                @pl.when(i > 0)
                def _(o_ref=o_ref, val=val):
                    o_ref[...] += val

    in_specs = [pl.BlockSpec((tr, w), functools.partial(lambda i, cb: (i, cb), cb=cb)) for (_, w, cb) in row_in]
    in_specs += [pl.BlockSpec((None,) + v.shape[1:], functools.partial(lambda i, l: (l, 0, 0), l=l)) for (v, l) in vec_in]
    out_shape = [jax.ShapeDtypeStruct((rows, w), dt) for (w, dt) in row_out]
    out_shape += [jax.ShapeDtypeStruct(s, F32) for s in acc_out]
    out_specs = [pl.BlockSpec((tr, w), lambda i: (i, 0)) for (w, _) in row_out]
    out_specs += [pl.BlockSpec(s, lambda i: (0, 0)) for s in acc_out]
    res = pl.pallas_call(
        body,
        out_shape=out_shape,
        grid=(rows // tr,),
        in_specs=in_specs,
        out_specs=out_specs,
        compiler_params=_cparams(("arbitrary",) if acc_out else ("parallel",)),
        name=name,
    )(*[a for (a, _, _) in row_in], *[v for (v, _) in vec_in])
    return res


def _rms(x, g):
    r = lax.rsqrt(jnp.mean(x * x, axis=-1, keepdims=True) + RMS_EPS)
    return x * r * g


def _rms_bwd(x, g, dy):
    r = lax.rsqrt(jnp.mean(x * x, axis=-1, keepdims=True) + RMS_EPS)
    xh = x * r
    dxh = dy * g
    dx = r * (dxh - xh * jnp.mean(dxh * xh, axis=-1, keepdims=True))
    return dx, jnp.sum(dy * xh, axis=0, keepdims=True)


def _silu(x):
    return x * jax.nn.sigmoid(x)


def _silu_grad(x):
    s = jax.nn.sigmoid(x)
    return s * (1.0 + x * (1.0 - s))


def _softplus(x):
    return jnp.maximum(x, 0.0) + jnp.log1p(jnp.exp(-jnp.abs(x)))


def _full(a):
    return (a, a.shape[1], 0)


def _f_post_pre(x, ysub, g_post, g_pre):
    xn = x + _rms(ysub, g_post)
    return xn, _rms(xn, g_pre)


def _f_final(x, ysub, tgt, g_post):
    err = x + _rms(ysub, g_post) - tgt
    return err * (1.0 / D_MODEL), jnp.sum(err * err, axis=0, keepdims=True)


def _f_pre_post_bwd(xmid, dh, dxo, ysub, g_pre, g_post):
    d1, dg_pre = _rms_bwd(xmid, g_pre, dh)
    dxm = dxo + d1
    dys, dg_post = _rms_bwd(ysub, g_post, dxm)
    return dxm, dys, dg_pre, dg_post


def _f_pre_bwd(x, dh, dxo, g_pre):
    d1, dg_pre = _rms_bwd(x, g_pre, dh)
    return dxo + d1, dg_pre


def _f_gain_bwd(x, dy, g):
    return _rms_bwd(x, g, dy)[1]


def _group_norm_parts(u):
    gw = u.shape[1] // SSM_GROUPS
    parts = []
    for gi in range(SSM_GROUPS):
        ug = u[:, gi * gw:(gi + 1) * gw]
        r = lax.rsqrt(jnp.mean(ug * ug, axis=-1, keepdims=True) + RMS_EPS)
        parts.append((ug * r, r))
    return gw, parts


def _f_gate_norm(y, z, g):
    _, parts = _group_norm_parts(y * _silu(z))
    return jnp.concatenate([uh for uh, _ in parts], axis=1) * g


def _f_gate_norm_bwd(y, z, do, g):
    sz = _silu(z)
    gw, parts = _group_norm_parts(y * sz)
    dxh = do * g
    du = []
    for gi, (uh, r) in enumerate(parts):
        dg_ = dxh[:, gi * gw:(gi + 1) * gw]
        du.append(r * (dg_ - uh * jnp.mean(dg_ * uh, axis=-1, keepdims=True)))
    du = jnp.concatenate(du, axis=1)
    uh_all = jnp.concatenate([uh for uh, _ in parts], axis=1)
    return du * sz, du * y * _silu_grad(z), jnp.sum(do * uh_all, axis=0, keepdims=True)


def _f_merge(ga, gs, ba, bs):
    return jax.nn.sigmoid(ga) * ba + jax.nn.sigmoid(gs) * bs


def _f_merge_bwd(ga, gs, ba, bs, dm):
    sa, ss = jax.nn.sigmoid(ga), jax.nn.sigmoid(gs)
    dgg = jnp.concatenate([dm * ba * sa * (1.0 - sa), dm * bs * ss * (1.0 - ss)], axis=1)
    return dgg, dm * sa, dm * ss


def _f_swiglu(gate, up):
    return _silu(gate) * up


def _f_swiglu_bwd(gate, up, da):
    return jnp.concatenate([da * up * _silu_grad(gate), da * _silu(gate)], axis=1)


def _split_dot(x, u):
    hi = x.astype(BF16)
    lo = (x - hi.astype(F32)).astype(BF16)
    return jnp.dot(hi, u, preferred_element_type=F32) + jnp.dot(lo, u, preferred_element_type=F32)


SB_ROWS = 512
SB_UNROLL = 4
C00 = (((0,), (0,)), ((), ()))
C11 = (((1,), (1,)), ((), ()))


def _sb_setup(q_ref, tq):
    hp = LANES // SB_HEAD_DIM
    lane = lax.broadcasted_iota(jnp.int32, (1, LANES), 1)
    heads = [jnp.logical_and(lane >= h * SB_HEAD_DIM, lane < (h + 1) * SB_HEAD_DIM) for h in range(hp)]
    qs = q_ref[...] * (SB_HEAD_DIM ** -0.5)
    q_h = [jnp.where(hm, qs, 0.0).astype(MXU_DTYPE) for hm in heads]
    row = lax.broadcasted_iota(jnp.int32, (tq, SB_BLOCK), 0)
    col = lax.broadcasted_iota(jnp.int32, (tq, SB_BLOCK), 1)
    sq_row = lax.broadcasted_iota(jnp.int32, (SB_BLOCK, SB_BLOCK), 0)
    sq_col = lax.broadcasted_iota(jnp.int32, (SB_BLOCK, SB_BLOCK), 1)
    return heads, q_h, col - row, sq_row, sq_col


def _sb_scores(q, kj, mask):
    z = lax.dot_general(q, kj, C11, preferred_element_type=F32)
    lm = -(jnp.maximum(z, 0.0) + jnp.log(1.0 + jnp.exp(-jnp.abs(z))))
    return z, lm if mask is None else jnp.where(mask, lm, 0.0)


def _add_rows(x, r0, upd):
    return x + upd if r0 == 0 else jnp.concatenate([x[:r0], x[r0:] + upd], axis=0)


def _grid_step3(nb, ncb, nq):
    return (pl.program_id(0) * ncb + pl.program_id(1)) * nq + pl.program_id(2), nb * ncb * nq


def _sb_fwd(proj, cfg, nb, gather=()):
    blk = SB_BLOCK
    tq = _tile(SEQ, SB_ROWS, blk)
    nq = SEQ // tq
    kpq = tq // blk
    unr = math.gcd(kpq, SB_UNROLL)
    hp = LANES // SB_HEAD_DIM
    ncb = cfg.sbw // LANES
    qb, kb, vb = cfg.q0 // LANES, cfg.k0 // LANES, cfg.v0 // LANES
    ng = len(gather)

    def body(q_ref, k_ref, v_ref, *rest):
        o_ref = rest[ng]
        if ng:
            step_id, n_steps = _grid_step3(nb, ncb, nq)
            start, forward, finish = _gather_phases(rest[:ng], rest[ng + 1:2 * ng + 1], *rest[2 * ng + 1:])
            pl.when(step_id == 0)(start)
            pl.when(step_id == (3 * n_steps) // 4)(forward)
        _sb_fwd_block(q_ref, k_ref, v_ref, o_ref)
        if ng:
            pl.when(step_id == n_steps - 1)(finish)

    def _sb_fwd_block(q_ref, k_ref, v_ref, o_ref):
        i = pl.program_id(2)
        heads, q_h, cmr, sq_row, sq_col = _sb_setup(q_ref, tq)
        u_rev = (sq_row >= sq_col).astype(BF16)

        def key_block(j, r0, acc, runs, diagonal):
            rows = pl.ds(pl.multiple_of(j * blk, blk), blk)
            kj = k_ref[rows, :].astype(MXU_DTYPE)
            vj = v_ref[rows, :].astype(MXU_DTYPE)
            mask = cmr[:tq - r0] < 0 if diagonal else None
            for h in range(hp):
                z, lm = _sb_scores(q_h[h][r0:], kj, mask)
                cs = _split_dot(lm, u_rev)
                w = jnp.exp(z + cs + runs[h][r0:])
                if diagonal:
                    w = jnp.where(mask, w, 0.0)
                upd = jnp.dot(w.astype(MXU_DTYPE), jnp.where(heads[h], vj, 0), preferred_element_type=F32)
                acc = _add_rows(acc, r0, upd)
                runs[h] = _add_rows(runs[h], r0, cs[:, 0:1])
            return acc

        acc = jnp.zeros((tq, LANES), F32)
        runs = [jnp.zeros((tq, 1), F32) for _ in range(hp)]
        for r in reversed(range(kpq)):
            acc = key_block(i * kpq + r, r * blk, acc, runs, True)

        def step(n, carry):
            acc, runs = carry
            runs = list(runs)
            for jj in range(unr):
                acc = key_block(i * kpq - 1 - (n * unr + jj), 0, acc, runs, False)
            return acc, tuple(runs)

        acc, _ = lax.fori_loop(0, i * (kpq // unr), step, (acc, tuple(runs)))
        o_ref[...] = acc

    res = pl.pallas_call(
        body,
        out_shape=[jax.ShapeDtypeStruct((nb * SEQ, cfg.sbw), F32)] + _gather_shapes(gather),
        grid=(nb, ncb, nq),
        in_specs=[
            pl.BlockSpec((tq, LANES), lambda b, c, i: (b * nq + i, qb + c)),
            pl.BlockSpec((SEQ, LANES), lambda b, c, i: (b, kb + c)),
            pl.BlockSpec((SEQ, LANES), lambda b, c, i: (b, vb + c)),
        ] + [ANY_SPEC] * ng,
        out_specs=[pl.BlockSpec((tq, LANES), lambda b, c, i: (b * nq + i, c))] + [ANY_SPEC] * ng,
        scratch_shapes=_comm_sems(ng) if ng else [],
        compiler_params=_cparams(("arbitrary",) * 3 if ng else ("parallel", "parallel", "arbitrary")),
        name="sb_fwd_gather" if ng else "sb_fwd",
    )(proj, proj, proj, *gather)
    return res[0], res[1:]


def _sb_bwd(proj, do_att, cfg, nb, scatter=()):
    blk = SB_BLOCK
    tq = _tile(SEQ, SB_ROWS, blk)
    nq = SEQ // tq
    kpq = tq // blk
    unr = math.gcd(kpq, SB_UNROLL)
    hp = LANES // SB_HEAD_DIM
    ncb = cfg.sbw // LANES
    scale = SB_HEAD_DIM ** -0.5
    qb, kb, vb = cfg.q0 // LANES, cfg.k0 // LANES, cfg.v0 // LANES
    ns = len(scatter)

    def body(q_ref, k_ref, v_ref, do_ref, *rest):
        dq_ref, dk_ref, dv_ref = rest[ns:ns + 3]
        g_ref, z_ref = rest[2 * ns + 3:2 * ns + 5]
        if ns:
            step_id, n_steps = _grid_step3(nb, ncb, nq)
            start, finish = _scatter_phases(rest[:ns], rest[ns + 3:2 * ns + 3], *rest[2 * ns + 5:])
            pl.when(step_id == 0)(start)
        _sb_bwd_block(q_ref, k_ref, v_ref, do_ref, dq_ref, dk_ref, dv_ref, g_ref, z_ref)
        if ns:
            pl.when(step_id == n_steps - 1)(finish)

    def _sb_bwd_block(q_ref, k_ref, v_ref, do_ref, dq_ref, dk_ref, dv_ref, g_ref, z_ref):
        i = pl.program_id(2)

        @pl.when(i == 0)
        def _():
            dk_ref[...] = jnp.zeros_like(dk_ref)
            dv_ref[...] = jnp.zeros_like(dv_ref)

        heads, q_h, cmr, sq_row, sq_col = _sb_setup(q_ref, tq)
        u_rev = (sq_row >= sq_col).astype(BF16)
        u_fwd = (sq_row <= sq_col).astype(BF16)
        do = do_ref[...]
        do_h = [jnp.where(hm, do, 0.0).astype(MXU_DTYPE) for hm in heads]


        def left_block(j, r0, runs, diagonal):
            rows = pl.ds(pl.multiple_of(j * blk, blk), blk)
            kj = k_ref[rows, :].astype(MXU_DTYPE)
            vj = v_ref[rows, :].astype(MXU_DTYPE)
            mask = cmr[:tq - r0] < 0 if diagonal else None
            dv = jnp.zeros((blk, LANES), F32)
            for h in range(hp):
                z, lm = _sb_scores(q_h[h][r0:], kj, mask)
                cs = _split_dot(lm, u_rev)
                a = jnp.exp(z + cs + runs[h][r0:])
                if diagonal:
                    a = jnp.where(mask, a, 0.0)
                da = lax.dot_general(do_h[h][r0:], vj, C11, preferred_element_type=F32)
                dv = dv + lax.dot_general(a.astype(MXU_DTYPE), do_h[h][r0:], C00, preferred_element_type=F32)
                g_ref[h, j, r0:, :] = a * da
                z_ref[h, j, r0:, :] = jax.nn.sigmoid(z)
                runs[h] = _add_rows(runs[h], r0, cs[:, 0:1])
            dv_ref[rows, :] += dv

        runs = [jnp.zeros((tq, 1), F32) for _ in range(hp)]
        for r in reversed(range(kpq)):
            left_block(i * kpq + r, r * blk, runs, True)

        def sweep_left(n, runs):
            runs = list(runs)
            for jj in range(unr):
                left_block(i * kpq - 1 - (n * unr + jj), 0, runs, False)
            return tuple(runs)

        trips = i * (kpq // unr)
        lax.fori_loop(0, trips, sweep_left, tuple(runs))

        def right_block(j, r0, dq, runs, diagonal):
            rows = pl.ds(pl.multiple_of(j * blk, blk), blk)
            kj = k_ref[rows, :].astype(MXU_DTYPE)
            dk = jnp.zeros((blk, LANES), F32)
            for h in range(hp):
                g = g_ref[h, j, r0:, :]
                g_upto = _split_dot(g, u_fwd) + runs[h][r0:]
                dz = g - z_ref[h, j, r0:, :] * g_upto
                if diagonal:
                    dz = jnp.where(cmr[:tq - r0] < 0, dz, 0.0)
                dz = dz.astype(MXU_DTYPE)
                dq = _add_rows(dq, r0, jnp.dot(dz, jnp.where(heads[h], kj, 0), preferred_element_type=F32))
                dk = dk + lax.dot_general(dz, q_h[h][r0:], C00, preferred_element_type=F32)
                runs[h] = _add_rows(runs[h], r0, jnp.sum(g, axis=1, keepdims=True))
            dk_ref[rows, :] += dk
            return dq

        def sweep_right(n, carry):
            dq, runs = carry
            runs = list(runs)
            for jj in range(unr):
                dq = right_block(n * unr + jj, 0, dq, runs, False)
            return dq, tuple(runs)

        init = (jnp.zeros((tq, LANES), F32), tuple(jnp.zeros((tq, 1), F32) for _ in range(hp)))
        dq, runs = lax.fori_loop(0, trips, sweep_right, init)
        runs = list(runs)
        for r in range(kpq):
            dq = right_block(i * kpq + r, r * blk, dq, runs, True)
        dq_ref[...] = (dq * scale).astype(dq_ref.dtype)

    kv_spec_out = pl.BlockSpec((SEQ, LANES), lambda b, c, i: (b, c))
    q_spec_out = pl.BlockSpec((tq, LANES), lambda b, c, i: (b * nq + i, c))
    res = pl.pallas_call(
        body,
        out_shape=[
            jax.ShapeDtypeStruct((nb * SEQ, cfg.sbw), ACT_DTYPE),
            jax.ShapeDtypeStruct((nb * SEQ, cfg.sbw), F32),
            jax.ShapeDtypeStruct((nb * SEQ, cfg.sbw), F32),
        ] + [jax.ShapeDtypeStruct(a.shape, a.dtype) for a in scatter],
        grid=(nb, ncb, nq),
        in_specs=[
            pl.BlockSpec((tq, LANES), lambda b, c, i: (b * nq + i, qb + c)),
            pl.BlockSpec((SEQ, LANES), lambda b, c, i: (b, kb + c)),
            pl.BlockSpec((SEQ, LANES), lambda b, c, i: (b, vb + c)),
            q_spec_out,
        ] + [ANY_SPEC] * ns,
        out_specs=[q_spec_out, kv_spec_out, kv_spec_out] + [ANY_SPEC] * ns,
        scratch_shapes=[pltpu.VMEM((hp, SEQ // blk, tq, blk), F32), pltpu.VMEM((hp, SEQ // blk, tq, blk), F32)]
        + (_comm_sems(ns) if ns else []),
        compiler_params=_cparams(("arbitrary",) * 3 if ns else ("parallel", "parallel", "arbitrary")),
        name="sb_bwd_scatter" if ns else "sb_bwd",
    )(proj, proj, proj, do_att, *scatter)
    return res[0], res[1], res[2], res[3:]


CONV_COLS = 256


def _conv_pre(x, w, b, t):
    kw = SSM_CONV
    shifted = []
    pre = b + w[kw - 1:kw, :] * x
    for k in range(kw - 1):
        d = kw - 1 - k
        xs = jnp.where(t >= d, pltpu.roll(x, d, 0), 0.0)
        shifted.append(xs)
        pre = pre + w[k:k + 1, :] * xs
    shifted.append(x)
    return pre, shifted


def _conv_fwd(proj, conv_w, conv_b, l, cfg, nb):
    cw = CONV_COLS
    ncb = cfg.conv_dim // cw
    xb = cfg.xbc0 // cw

    def body(x_ref, w_ref, b_ref, o_ref):
        x = x_ref[...]
        t = lax.broadcasted_iota(jnp.int32, x.shape, 0)
        pre, _ = _conv_pre(x, w_ref[...], b_ref[...], t)
        o_ref[...] = _silu(pre)

    return pl.pallas_call(
        body,
        out_shape=jax.ShapeDtypeStruct((nb * SEQ, cfg.conv_dim), F32),
        grid=(ncb, nb),
        in_specs=[
            pl.BlockSpec((SEQ, cw), lambda j, b: (b, xb + j)),
            pl.BlockSpec((None, SSM_CONV, cw), lambda j, b: (0, 0, j)),
            pl.BlockSpec((None, 1, cw), lambda j, b: (l, 0, j)),
        ],
        out_specs=pl.BlockSpec((SEQ, cw), lambda j, b: (b, j)),
        compiler_params=_cparams(("parallel", "parallel")),
        name="conv_fwd",
    )(proj, conv_w, conv_b)


def _conv_bwd(proj, dact, conv_w, conv_b, l, cfg, nb):
    cw = CONV_COLS
    ncb = cfg.conv_dim // cw
    xb = cfg.xbc0 // cw
    kw = SSM_CONV

    def body(x_ref, da_ref, w_ref, b_ref, dx_ref, dw_ref, db_ref):
        b_id = pl.program_id(1)
        x = x_ref[...]
        w = w_ref[...]
        t = lax.broadcasted_iota(jnp.int32, x.shape, 0)
        pre, shifted = _conv_pre(x, w, b_ref[...], t)
        dpre = da_ref[...] * _silu_grad(pre)
        dx = w[kw - 1:kw, :] * dpre
        for k in range(kw - 1):
            d = kw - 1 - k
            dx = dx + w[k:k + 1, :] * jnp.where(t < SEQ - d, pltpu.roll(dpre, SEQ - d, 0), 0.0)
        dx_ref[...] = dx.astype(dx_ref.dtype)
        dw = jnp.concatenate([jnp.sum(dpre * s, axis=0, keepdims=True) for s in shifted], axis=0)
        db = jnp.sum(dpre, axis=0, keepdims=True)

        @pl.when(b_id == 0)
        def _():
            dw_ref[...] = dw
            db_ref[...] = db

        @pl.when(b_id > 0)
        def _():
            dw_ref[...] += dw
            db_ref[...] += db

    return pl.pallas_call(
        body,
        out_shape=[
            jax.ShapeDtypeStruct((nb * SEQ, cfg.conv_dim), ACT_DTYPE),
            jax.ShapeDtypeStruct((kw, cfg.conv_dim), F32),
            jax.ShapeDtypeStruct((1, cfg.conv_dim), F32),
        ],
        grid=(ncb, nb),
        in_specs=[
            pl.BlockSpec((SEQ, cw), lambda j, b: (b, xb + j)),
            pl.BlockSpec((SEQ, cw), lambda j, b: (b, j)),
            pl.BlockSpec((None, kw, cw), lambda j, b: (0, 0, j)),
            pl.BlockSpec((None, 1, cw), lambda j, b: (l, 0, j)),
        ],
        out_specs=[
            pl.BlockSpec((SEQ, cw), lambda j, b: (b, j)),
            pl.BlockSpec((kw, cw), lambda j, b: (0, j)),
            pl.BlockSpec((1, cw), lambda j, b: (0, j)),
        ],
        compiler_params=_cparams(("parallel", "arbitrary")),
        name="conv_bwd",
    )(proj, dact, conv_w, conv_b)


def _ssd_common(dt_raw, dt_bias, a_log, tri):
    ln = SSM_CHUNK
    dt = _softplus(dt_raw + dt_bias)
    a = -jnp.exp(a_log)
    a_cs = jnp.dot(tri, dt * a, preferred_element_type=F32, precision=lax.Precision.HIGHEST)
    a_last = a_cs[ln - 1:ln, :]
    return dt, a, a_cs, a_cs.T, jnp.exp(a_cs), jnp.exp(a_last - a_cs), jnp.exp(a_last)


def _ssd_specs(cfg, nc, rev):
    ln = SSM_CHUNK
    cidx = (lambda c: nc - 1 - c) if rev else (lambda c: c)
    bmb = cfg.inner // cfg.gn
    return [
        pl.BlockSpec((ln, cfg.inner), lambda b, c: (b * nc + cidx(c), 0)),
        pl.BlockSpec((ln, cfg.gn), lambda b, c: (b * nc + cidx(c), bmb)),
        pl.BlockSpec((ln, cfg.gn), lambda b, c: (b * nc + cidx(c), bmb + 1)),
        pl.BlockSpec((ln, LANES), lambda b, c: (b * nc + cidx(c), cfg.dt0 // LANES)),
    ]


def _split3_dot(x, u):
    hi = x.astype(BF16)
    r1 = x - hi.astype(F32)
    mid = r1.astype(BF16)
    lo = (r1 - mid.astype(F32)).astype(BF16)
    dot = lambda a: jnp.dot(a, u, preferred_element_type=F32)
    return dot(hi) + dot(mid) + dot(lo)


def _ssd_consts(cfg):
    p = SSM_HEAD_DIM
    hrow = jnp.arange(LANES)[:, None]
    spread = (jnp.arange(cfg.inner)[None, :] // p == hrow).astype(BF16)
    spread_tile = (jnp.arange(cfg.heads * LANES)[None, :] // LANES == hrow).astype(BF16)
    return spread, spread_tile, spread.T


def _ssd_chunk_terms(dtr, bias, alog, spread, spread_tile):
    ln = SSM_CHUNK
    row = lax.broadcasted_iota(jnp.int32, (ln, ln), 0)
    col = lax.broadcasted_iota(jnp.int32, (ln, ln), 1)
    tril = row >= col
    dt, a, a_cs, a_t, e_a, dte, cd = _ssd_common(dtr, bias, alog, tril.astype(F32))
    ex = lambda v: _split_dot(v, spread)
    cd_x = ex(jnp.broadcast_to(cd, (8, LANES)))[0:1]
    colb = _split3_dot(a_cs, spread_tile)
    return dict(tril=tril, row=row, col=col, dt=dt, a=a, a_cs=a_cs, a_t=a_t, e_a=e_a, dte=dte, cd=cd,
                dt_x=ex(dt), ea_x=ex(e_a), dte_x=ex(dte), cd_x=cd_x, colb=colb)


def _head_masks():
    lane = lax.broadcasted_iota(jnp.int32, (1, LANES), 1)
    hpt = LANES // SSM_HEAD_DIM
    return [jnp.logical_and(lane >= i * SSM_HEAD_DIM, lane < (i + 1) * SSM_HEAD_DIM) for i in range(hpt)]


def _ssd_fwd(xbc, proj, dt_bias, a_log, d_skip_x, l, cfg, nb):
    ln, p, n = SSM_CHUNK, SSM_HEAD_DIM, SSM_STATE
    nc = SEQ // ln
    g_, e_ = SSM_GROUPS, cfg.epg
    gw = e_ * p
    hpt = LANES // p
    assert cfg.inner % cfg.gn == 0 and gw % LANES == 0
    spread, spread_tile, _ = _ssd_consts(cfg)

    def body(xs_ref, bm_ref, cm_ref, dtr_ref, bias_ref, alog_ref, dskx_ref, sp_ref, spt_ref, y_ref, st_ref, s_ref):
        c = pl.program_id(1)

        @pl.when(c == 0)
        def _():
            s_ref[...] = jnp.zeros_like(s_ref)

        st_ref[...] = s_ref[...]
        t = _ssd_chunk_terms(dtr_ref[...], bias_ref[...], alog_ref[...], sp_ref[...], spt_ref[...])
        hm = _head_masks()
        xs = xs_ref[...]
        xd = xs * t["dt_x"]
        xde = (xd * t["dte_x"]).astype(MXU_DTYPE)
        for g in range(g_):
            gc = slice(g * gw, (g + 1) * gw)
            bm = bm_ref[:, g * n:(g + 1) * n].astype(MXU_DTYPE)
            cm = cm_ref[:, g * n:(g + 1) * n].astype(MXU_DTYPE)
            cb = lax.dot_general(cm, bm, C11, preferred_element_type=F32)
            sg = s_ref[g * n:(g + 1) * n, :]
            y_off = jnp.dot(cm, sg.astype(MXU_DTYPE), preferred_element_type=F32) * t["ea_x"][:, gc]
            s_ref[g * n:(g + 1) * n, :] = t["cd_x"][:, gc] * sg + lax.dot_general(
                bm, xde[:, gc], C00, preferred_element_type=F32)
            for k in range(gw // LANES):
                lanes = slice(g * gw + k * LANES, g * gw + (k + 1) * LANES)
                xp = xd[:, lanes]
                acc = y_off[:, k * LANES:(k + 1) * LANES] + dskx_ref[:, lanes] * xs[:, lanes]
                for i in range(hpt):
                    h = (g * gw + k * LANES) // p + i
                    lmat = jnp.exp(jnp.where(t["tril"], t["colb"][:, h * LANES:(h + 1) * LANES] - t["a_t"][h:h + 1, :],
                                             -jnp.inf))
                    acc = acc + jnp.dot((cb * lmat).astype(MXU_DTYPE),
                                        jnp.where(hm[i], xp, 0.0).astype(MXU_DTYPE), preferred_element_type=F32)
                y_ref[:, lanes] = acc

    vec = lambda b, c: (l, 0, 0)
    whole = lambda a: pl.BlockSpec(a.shape, lambda b, c: (0, 0))
    return pl.pallas_call(
        body,
        out_shape=[
            jax.ShapeDtypeStruct((nb * SEQ, cfg.inner), F32),
            jax.ShapeDtypeStruct((nb * nc * g_ * n, gw), F32),
        ],
        grid=(nb, nc),
        in_specs=_ssd_specs(cfg, nc, False) + [pl.BlockSpec((None, 1, LANES), vec)] * 2
        + [pl.BlockSpec((None, 1, cfg.inner), vec), whole(spread), whole(spread_tile)],
        out_specs=[
            pl.BlockSpec((ln, cfg.inner), lambda b, c: (b * nc + c, 0)),
            pl.BlockSpec((g_ * n, gw), lambda b, c: (b * nc + c, 0)),
        ],
        scratch_shapes=[pltpu.VMEM((g_ * n, gw), F32)],
        compiler_params=_cparams(("parallel", "arbitrary")),
        name="ssd_fwd",
    )(xbc, xbc, xbc, proj, dt_bias, a_log, d_skip_x, spread, spread_tile)


def _ssd_bwd(xbc, proj, states, dy, dt_bias, a_log, d_skip_x, l, cfg, nb):
    ln, p, n = SSM_CHUNK, SSM_HEAD_DIM, SSM_STATE
    nc = SEQ // ln
    g_, e_ = SSM_GROUPS, cfg.epg
    gw = e_ * p
    hpt = LANES // p
    spread, spread_tile, gather_t = _ssd_consts(cfg)

    def body(xs_ref, bm_ref, cm_ref, dtr_ref, st_ref, dy_ref, bias_ref, alog_ref, dskx_ref, sp_ref, spt_ref, gt_ref,
             dxbc_ref, ddt_ref, dvec_ref, ds_ref, r1_ref, r2_ref, r4_ref, ss_ref):
        first = jnp.logical_and(pl.program_id(0) == 0, pl.program_id(1) == 0)

        @pl.when(pl.program_id(1) == 0)
        def _():
            ds_ref[...] = jnp.zeros_like(ds_ref)

        dtr = dtr_ref[...]
        bias = bias_ref[...]
        t = _ssd_chunk_terms(dtr, bias, alog_ref[...], sp_ref[...], spt_ref[...])
        tril = t["tril"]
        triu = t["row"] <= t["col"]
        hm = _head_masks()
        lane = lax.broadcasted_iota(jnp.int32, (1, LANES), 1)
        sub = lax.broadcasted_iota(jnp.int32, (LANES, 1), 0)
        xs = xs_ref[...]
        dyv = dy_ref[...]
        xd = xs * t["dt_x"]
        xde = xd * t["dte_x"]
        xde_m = xde.astype(MXU_DTYPE)
        dye_m = (dyv * t["ea_x"]).astype(MXU_DTYPE)
        da_col = jnp.zeros((ln, LANES), F32)
        da_row_t = jnp.zeros((LANES, ln), F32)
        ss_ref[...] = jnp.zeros_like(ss_ref)
        for g in range(g_):
            gc = slice(g * gw, (g + 1) * gw)
            gr = slice(g * n, (g + 1) * n)
            bm = bm_ref[:, gr].astype(MXU_DTYPE)
            cm = cm_ref[:, gr].astype(MXU_DTYPE)
            cb = lax.dot_general(cm, bm, C11, preferred_element_type=F32)
            cb_t = lax.dot_general(bm, cm, C11, preferred_element_type=F32)
            sp = st_ref[gr, :]
            dsg = ds_ref[gr, :]
            sp_m = sp.astype(MXU_DTYPE)
            dsg_m = dsg.astype(MXU_DTYPE)
            bds = jnp.dot(bm, dsg_m, preferred_element_type=F32)
            y_off = jnp.dot(cm, sp_m, preferred_element_type=F32) * t["ea_x"][:, gc]
            dcm = lax.dot_general(dye_m[:, gc], sp_m, C11, preferred_element_type=F32)
            dbm = lax.dot_general(xde_m[:, gc], dsg_m, C11, preferred_element_type=F32)
            ds_ref[gr, :] = t["cd_x"][:, gc] * dsg + lax.dot_general(cm, dye_m[:, gc], C00, preferred_element_type=F32)
            r4 = bds * xde[:, gc]
            r4_ref[:, gc] = r4
            r1_ref[:, gc] = dyv[:, gc] * y_off - r4
            ss_ref[0:1, gc] = jnp.sum(dsg * sp, axis=0, keepdims=True)
            dcb = jnp.zeros((ln, ln), F32)
            for k in range(gw // LANES):
                lanes = slice(g * gw + k * LANES, g * gw + (k + 1) * LANES)
                xp = xd[:, lanes]
                xp_m = xp.astype(MXU_DTYPE)
                dyp = dyv[:, lanes]
                dxp = t["dte_x"][:, lanes] * bds[:, k * LANES:(k + 1) * LANES]
                for i in range(hpt):
                    h = (g * gw + k * LANES) // p + i
                    diff = t["colb"][:, h * LANES:(h + 1) * LANES] - t["a_t"][h:h + 1, :]
                    lmat = jnp.exp(jnp.where(tril, diff, -jnp.inf))
                    lmat_t = jnp.exp(jnp.where(triu, -diff, -jnp.inf))
                    dy_h = jnp.where(hm[i], dyp, 0.0).astype(MXU_DTYPE)
                    d_ml = lax.dot_general(dy_h, xp_m, C11, preferred_element_type=F32) * lmat
                    dcb = dcb + d_ml
                    w_mat = d_ml * cb
                    dxp = dxp + jnp.dot((cb_t * lmat_t).astype(MXU_DTYPE), dy_h, preferred_element_type=F32)
                    da_col = da_col + jnp.sum(w_mat, axis=1, keepdims=True) * (lane == h).astype(F32)
                    da_row_t = da_row_t - (sub == h).astype(F32) * jnp.sum(w_mat, axis=0, keepdims=True)
                dxbc_ref[:, lanes] = dxp * t["dt_x"][:, lanes] + dskx_ref[:, lanes] * dyp
                r2_ref[:, lanes] = dxp * xs[:, lanes]
            dcb_m = dcb.astype(MXU_DTYPE)
            dxbc_ref[:, cfg.inner + g * n:cfg.inner + (g + 1) * n] = dbm + lax.dot_general(
                dcb_m, cm, C00, preferred_element_type=F32)
            dxbc_ref[:, cfg.inner + cfg.gn + g * n:cfg.inner + cfg.gn + (g + 1) * n] = dcm + jnp.dot(
                dcb_m, bm, preferred_element_type=F32)
        gt = gt_ref[...]
        rd = lambda v: _split_dot(v, gt)
        red4 = rd(r4_ref[...])
        da_last = jnp.sum(red4, axis=0, keepdims=True) + t["cd"] * rd(ss_ref[...])[0:1]
        d_acs = da_col + rd(r1_ref[...]) + da_row_t.T + jnp.where(t["row"][:, 0:1] == ln - 1, da_last, 0.0)
        da_dt = lax.dot_general(tril.astype(F32), d_acs, C00, preferred_element_type=F32,
                                precision=lax.Precision.HIGHEST)
        ddt = rd(r2_ref[...]) + da_dt * t["a"]
        ddt_raw = ddt * jax.nn.sigmoid(dtr + bias)
        ddt_ref[...] = ddt_raw.astype(ddt_ref.dtype)
        da_log = jnp.sum(da_dt * t["dt"], axis=0, keepdims=True) * t["a"]
        ddsk = jnp.sum(rd(dyv * xs), axis=0, keepdims=True)
        dvec = jnp.concatenate([jnp.sum(ddt_raw, axis=0, keepdims=True), da_log, ddsk,
                                jnp.zeros((5, LANES), F32)], axis=0)

        @pl.when(first)
        def _():
            dvec_ref[...] = dvec

        @pl.when(jnp.logical_not(first))
        def _():
            dvec_ref[...] += dvec

    vec = lambda b, c: (l, 0, 0)
    rblk = lambda b, c: (b * nc + nc - 1 - c, 0)
    whole = lambda a: pl.BlockSpec(a.shape, lambda b, c: (0, 0))
    return pl.pallas_call(
        body,
        out_shape=[
            jax.ShapeDtypeStruct((nb * SEQ, cfg.conv_dim), F32),
            jax.ShapeDtypeStruct((nb * SEQ, LANES), ACT_DTYPE),
            jax.ShapeDtypeStruct((8, LANES), F32),
        ],
        grid=(nb, nc),
        in_specs=_ssd_specs(cfg, nc, True) + [
            pl.BlockSpec((g_ * n, gw), rblk),
            pl.BlockSpec((ln, cfg.inner), rblk),
        ] + [pl.BlockSpec((None, 1, LANES), vec)] * 2 + [pl.BlockSpec((None, 1, cfg.inner), vec),
                                                           whole(spread), whole(spread_tile), whole(gather_t)],
        out_specs=[
            pl.BlockSpec((ln, cfg.conv_dim), rblk),
            pl.BlockSpec((ln, LANES), rblk),
            pl.BlockSpec((8, LANES), lambda b, c: (0, 0)),
        ],
        scratch_shapes=[pltpu.VMEM((g_ * n, gw), F32)] + [pltpu.VMEM((ln, cfg.inner), F32)] * 3
        + [pltpu.VMEM((8, cfg.inner), F32)],
        compiler_params=_cparams(("arbitrary", "arbitrary")),
        name="ssd_bwd",
    )(xbc, xbc, xbc, proj, states, dy, dt_bias, a_log, d_skip_x, spread, spread_tile, gather_t)


XA_ROWS = 256


def _xa_probs(q_ref, kv_ref, h, dh):
    c11 = (((1,), (1,)), ((), ()))
    qh = q_ref[:, h * dh:(h + 1) * dh].astype(MXU_DTYPE)
    kh = kv_ref[:, h * dh:(h + 1) * dh].astype(MXU_DTYPE)
    vh = kv_ref[:, D_MODEL + h * dh:D_MODEL + (h + 1) * dh].astype(MXU_DTYPE)
    s = lax.dot_general(qh, kh, c11, preferred_element_type=F32) * (dh ** -0.5)
    s = s - jnp.max(s, axis=1, keepdims=True)
    pr = jnp.exp(s)
    return qh, kh, vh, pr / jnp.sum(pr, axis=1, keepdims=True)


def _xa_fwd(q, kv, cfg, nb):
    tq = _tile(SEQ, XA_ROWS, 16)
    nq = SEQ // tq
    dh = cfg.xa_dim

    def body(q_ref, kv_ref, o_ref):
        for h in range(XA_HEADS):
            _, _, vh, pr = _xa_probs(q_ref, kv_ref, h, dh)
            o_ref[:, h * dh:(h + 1) * dh] = jnp.dot(pr.astype(MXU_DTYPE), vh, preferred_element_type=F32).astype(o_ref.dtype)

    return pl.pallas_call(
        body,
        out_shape=jax.ShapeDtypeStruct((nb * SEQ, D_MODEL), ACT_DTYPE),
        grid=(nb, nq),
        in_specs=[
            pl.BlockSpec((tq, D_MODEL), lambda b, i: (b * nq + i, 0)),
            pl.BlockSpec((MEM_LEN, 2 * D_MODEL), lambda b, i: (b, 0)),
        ],
        out_specs=pl.BlockSpec((tq, D_MODEL), lambda b, i: (b * nq + i, 0)),
        compiler_params=_cparams(("parallel", "parallel")),
        name="xa_fwd",
    )(q, kv)


def _xa_bwd(q, kv, do, cfg, nb):
    tq = _tile(SEQ, XA_ROWS, 16)
    nq = SEQ // tq
    dh = cfg.xa_dim
    c00 = (((0,), (0,)), ((), ()))
    c11 = (((1,), (1,)), ((), ()))
    scale = dh ** -0.5

    def body(q_ref, kv_ref, do_ref, dq_ref, dkv_ref, acc_ref):
        i = pl.program_id(1)

        @pl.when(i == 0)
        def _():
            acc_ref[...] = jnp.zeros_like(acc_ref)

        for h in range(XA_HEADS):
            hs = slice(h * dh, (h + 1) * dh)
            vs = slice(D_MODEL + h * dh, D_MODEL + (h + 1) * dh)
            qh, kh, vh, pr = _xa_probs(q_ref, kv_ref, h, dh)
            do_h = do_ref[:, hs].astype(MXU_DTYPE)
            dp = lax.dot_general(do_h, vh, c11, preferred_element_type=F32)
            ds = (pr * (dp - jnp.sum(dp * pr, axis=1, keepdims=True))).astype(MXU_DTYPE)
            dq_ref[:, hs] = (jnp.dot(ds, kh, preferred_element_type=F32) * scale).astype(dq_ref.dtype)
            acc_ref[:, hs] += lax.dot_general(ds, qh, c00, preferred_element_type=F32) * scale
            acc_ref[:, vs] += lax.dot_general(pr.astype(MXU_DTYPE), do_h, c00, preferred_element_type=F32)

        @pl.when(i == nq - 1)
        def _():
            dkv_ref[...] = acc_ref[...].astype(dkv_ref.dtype)

    return pl.pallas_call(
        body,
        out_shape=[
            jax.ShapeDtypeStruct((nb * SEQ, D_MODEL), ACT_DTYPE),
            jax.ShapeDtypeStruct((nb * MEM_LEN, 2 * D_MODEL), ACT_DTYPE),
        ],
        grid=(nb, nq),
        in_specs=[
            pl.BlockSpec((tq, D_MODEL), lambda b, i: (b * nq + i, 0)),
            pl.BlockSpec((MEM_LEN, 2 * D_MODEL), lambda b, i: (b, 0)),
            pl.BlockSpec((tq, D_MODEL), lambda b, i: (b * nq + i, 0)),
        ],
        out_specs=[
            pl.BlockSpec((tq, D_MODEL), lambda b, i: (b * nq + i, 0)),
            pl.BlockSpec((MEM_LEN, 2 * D_MODEL), lambda b, i: (b, 0)),
        ],
        scratch_shapes=[pltpu.VMEM((MEM_LEN, 2 * D_MODEL), F32)],
        compiler_params=_cparams(("parallel", "arbitrary")),
        name="xa_bwd",
    )(q, kv, do)


def _adamw(parts, w, m, v, name, tr=128):
    n, nl, r, c = parts.shape
    tr = _tile(r, tr, 16)

    def body(p_ref, w_ref, m_ref, v_ref, g_ref, d_ref, nm_ref, nv_ref):
        g = p_ref[0].astype(F32)
        for i in range(1, n):
            g = g + p_ref[i].astype(F32)
        m2 = ADAM_B1 * m_ref[...] + (1.0 - ADAM_B1) * g
        v2 = ADAM_B2 * v_ref[...] + (1.0 - ADAM_B2) * (g * g)
        m_hat = m2 / (1.0 - ADAM_B1 ** ADAM_STEP)
        v_hat = v2 / (1.0 - ADAM_B2 ** ADAM_STEP)
        g_ref[...] = g
        d_ref[...] = -ADAM_LR * (m_hat / (jnp.sqrt(v_hat) + ADAM_EPS) + ADAM_WD * w_ref[...])
        nm_ref[...] = m2
        nv_ref[...] = v2

    blk = pl.BlockSpec((None, tr, c), lambda l, i: (l, i, 0))
    return pl.pallas_call(
        body,
        out_shape=[jax.ShapeDtypeStruct((nl, r, c), F32)] * 4,
        grid=(nl, r // tr),
        in_specs=[pl.BlockSpec((n, None, tr, c), lambda l, i: (0, l, i, 0)), blk, blk, blk],
        out_specs=[blk] * 4,
        compiler_params=_cparams(("parallel", "parallel")),
        name=name,
    )(parts, w, m, v)


def _flat_index(px, py, pc):
    return 4 * px + 2 * py + pc


def _all_gather(arrs, name):
    n = len(arrs)

    def body(*refs):
        start, forward, finish = _gather_phases(refs[:n], refs[n:2 * n], *refs[2 * n:])
        start()
        forward()
        finish()

    return pl.pallas_call(
        body,
        out_shape=_gather_shapes(arrs),
        in_specs=[ANY_SPEC] * n,
        out_specs=[ANY_SPEC] * n,
        scratch_shapes=_comm_sems(n),
        name=name,
    )(*arrs)


ANY_SPEC = pl.BlockSpec(memory_space=pl.ANY)


def _comm_sems(n):
    return [pltpu.SemaphoreType.DMA((n, N_DEV - 1)), pltpu.SemaphoreType.DMA((n, N_DEV - 1)),
            pltpu.SemaphoreType.DMA((n,))]


def _gather_shapes(arrs):
    return [jax.ShapeDtypeStruct((N_DEV,) + a.shape, a.dtype) for a in arrs]


def _gather_phases(ins, outs, send_sems, recv_sems, local_sems):
    n = len(ins)
    x, y, c = lax.axis_index("x"), lax.axis_index("y"), lax.axis_index("c")
    me, sibling = (x, y, c), (x, y, 1 - c)
    chips = [(1 - x, y), (x, 1 - y), (1 - x, 1 - y)]

    def copy(a, k, block, to, src=None):
        slot = outs[a].at[_flat_index(*block)]
        return pltpu.make_async_remote_copy(
            src_ref=slot if src is None else src, dst_ref=slot,
            send_sem=send_sems.at[a, k], recv_sem=recv_sems.at[a, k],
            device_id=to, device_id_type=MESH)

    def mine(a):
        return pltpu.make_async_copy(ins[a], outs[a].at[_flat_index(*me)], local_sems.at[a])

    def first(a):
        return [copy(a, 0, me, sibling, src=ins[a])] + [
            copy(a, 1 + j, me, (*chip, c), src=ins[a]) for j, chip in enumerate(chips)]

    def start():
        for a in range(n):
            mine(a).start()
            for cp in first(a):
                cp.start()

    def forward():
        for j, chip in enumerate(chips):
            for a in range(n):
                copy(a, 1 + j, (*chip, c), me).wait_recv()
                copy(a, 4 + j, (*chip, c), sibling).start()

    def finish():
        for a in range(n):
            copy(a, 0, sibling, me).wait_recv()
            for j, chip in enumerate(chips):
                copy(a, 4 + j, (*chip, 1 - c), me).wait_recv()
        for a in range(n):
            for cp in first(a):
                cp.wait_send()
            for j, chip in enumerate(chips):
                copy(a, 4 + j, (*chip, c), sibling).wait_send()
            mine(a).wait()

    return start, forward, finish


def _scatter_blocks(arrs, name):
    n = len(arrs)

    def body(*refs):
        start, finish = _scatter_phases(refs[:n], refs[n:2 * n], *refs[2 * n:])
        start()
        finish()

    return pl.pallas_call(
        body,
        out_shape=[jax.ShapeDtypeStruct(a.shape, a.dtype) for a in arrs],
        in_specs=[ANY_SPEC] * n,
        out_specs=[ANY_SPEC] * n,
        scratch_shapes=_comm_sems(n),
        name=name,
    )(*arrs)


def _scatter_phases(ins, outs, send_sems, recv_sems, local_sems):
    n = len(ins)
    x, y, c = lax.axis_index("x"), lax.axis_index("y"), lax.axis_index("c")
    me = _flat_index(x, y, c)

    def peer(k):
        return (1 - x if k & 4 else x, 1 - y if k & 2 else y, 1 - c if k & 1 else c)

    def copy(a, k):
        p = peer(k)
        return pltpu.make_async_remote_copy(
            src_ref=ins[a].at[_flat_index(*p)], dst_ref=outs[a].at[me],
            send_sem=send_sems.at[a, k - 1], recv_sem=recv_sems.at[a, k - 1],
            device_id=p, device_id_type=MESH)

    def landed(a, k):
        slot = outs[a].at[_flat_index(*peer(k))]
        return pltpu.make_async_remote_copy(
            src_ref=slot, dst_ref=slot, send_sem=send_sems.at[a, k - 1], recv_sem=recv_sems.at[a, k - 1],
            device_id=peer(k), device_id_type=MESH)

    def mine(a):
        return pltpu.make_async_copy(ins[a].at[me], outs[a].at[me], local_sems.at[a])

    def start():
        for a in range(n):
            mine(a).start()
            for k in range(1, N_DEV):
                copy(a, k).start()

    def finish():
        for a in range(n):
            for k in range(1, N_DEV):
                landed(a, k).wait_recv()
        for a in range(n):
            for k in range(1, N_DEV):
                copy(a, k).wait_send()
            mine(a).wait()

    return start, finish


_BIG = ("w_in", "w_br_att", "w_br_ssm", "w_mix_out", "w_xq", "w_xkv", "w_xo", "w_gu", "w_down")
_COL_SHARDED = ("w_in", "w_xkv", "w_gu", "conv_w")
_SMALL = ("g_pre_mix", "conv_b", "dt_bias", "a_log", "d_skip", "g_ssm_norm", "g_post_mix", "g_pre_xa",
          "g_mem", "g_post_xa", "g_pre_ffn", "g_post_ffn")
_WEIGHTS = ("g_pre_mix", "w_in", "conv_w", "conv_b", "dt_bias", "a_log", "d_skip", "g_ssm_norm", "w_br_att",
            "w_br_ssm", "w_mix_out", "g_post_mix", "g_pre_xa", "g_mem", "w_xq", "w_xkv", "w_xo", "g_post_xa",
            "g_pre_ffn", "w_gu", "w_down", "g_post_ffn")
PACK_W = 8 * LANES


def _unshard(g, col):
    n, r, c = g.shape
    if col:
        return jnp.transpose(g, (1, 0, 2)).reshape(r, n * c)
    return g.reshape(n * r, c)


def _shard(w, col):
    r, c = w.shape
    if col:
        return jnp.transpose(w.reshape(r, N_DEV, c // N_DEV), (1, 0, 2))
    return w.reshape(N_DEV, r // N_DEV, c)


def _permute_in(w, cfg):
    parts, off = [], 0
    for size in cfg.in_sizes:
        parts.append(w[..., off:off + size])
        off += size
    q, k, v, z, xbc, dt, ga, gs = parts
    pad = jnp.zeros(w.shape[:-1] + (LANES - cfg.heads,), w.dtype)
    return jnp.concatenate([z, ga, gs, q, k, v, xbc, dt, pad], axis=-1)


def _unpermute_in(w, cfg):
    c = cfg
    sl = lambda a, n: w[..., a:a + n]
    return jnp.concatenate([sl(c.q0, c.sbw), sl(c.k0, c.sbw), sl(c.v0, c.sbw), sl(c.z0, c.inner),
                            sl(c.xbc0, c.conv_dim), sl(c.dt0, c.heads), sl(c.ga0, c.d), sl(c.gs0, c.d)], axis=-1)


def _pack(arrs):
    flat = jnp.concatenate([a.reshape(-1).astype(F32) for a in arrs])
    rows = -(-flat.shape[0] // PACK_W)
    rows = -(-rows // 8) * 8
    return jnp.pad(flat, (0, rows * PACK_W - flat.shape[0])).reshape(rows, PACK_W)


def _unpack(p, shapes):
    flat = p.reshape(-1)
    out, off = [], 0
    for s in shapes:
        size = math.prod(s)
        out.append(flat[off:off + size].reshape(s))
        off += size
    return out


def _vec3(a, width=None):
    if width is not None and a.shape[1] < width:
        a = jnp.pad(a, ((0, 0), (0, width - a.shape[1])))
    return a[:, None, :]


def _forward_layer(l, xin, h1, memf, tgt, w, p, cfg, nb, last, gather, n_late=0, complete=None):
    t = xin.shape[0]
    d = cfg.d
    s = {"x_in": xin, "h1": h1}
    proj = _mm(h1, w["w_in"], name="mm_proj")
    s["proj"] = proj
    s["o_att"], gathered = _sb_fwd(proj, cfg, nb, gather)
    if n_late:
        w.update(complete(gathered[:n_late]))
        gathered = gathered[n_late:]
    s["xbc"] = _conv_fwd(proj, w["conv_w"], p["conv_b"], l, cfg, nb)
    s["y"], s["states"] = _ssd_fwd(s["xbc"], proj, p["dt_bias"], p["a_log"], p["d_skip_x"], l, cfg, nb)
    s["o_ssm"] = _rowwise(_f_gate_norm, "gate_norm_fwd", t, [_full(s["y"]), (proj, cfg.inner, 0)],
                          [(p["g_ssm_norm"], l)], [(cfg.inner, ACT_DTYPE)])[0]
    s["ba"] = _mm(s["o_att"], w["w_br_att"], name="mm_br_att")
    s["bs"] = _mm(s["o_ssm"], w["w_br_ssm"], name="mm_br_ssm")
    s["merged"] = _rowwise(_f_merge, "merge_fwd", t,
                           [(proj, d, cfg.ga0 // d), (proj, d, cfg.gs0 // d), _full(s["ba"]), _full(s["bs"])],
                           [], [(d, ACT_DTYPE)])[0]
    s["mo"] = _mm(s["merged"], w["w_mix_out"], name="mm_mix_out")
    s["x1"], s["h2"] = _rowwise(_f_post_pre, "post_pre_mix", t, [_full(xin), _full(s["mo"])],
                                [(p["g_post_mix"], l), (p["g_pre_xa"], l)], [(d, F32), (d, ACT_DTYPE)])
    s["mem_n"] = _rowwise(_rms, "mem_norm", memf.shape[0], [_full(memf)], [(p["g_mem"], l)], [(d, ACT_DTYPE)])[0]
    s["q"] = _mm(s["h2"], w["w_xq"], name="mm_xq")
    s["kv"] = _mm(s["mem_n"], w["w_xkv"], name="mm_xkv")
    s["o_xa"] = _xa_fwd(s["q"], s["kv"], cfg, nb)
    s["xo"] = _mm(s["o_xa"], w["w_xo"], name="mm_xo")
    s["x2"], s["h3"] = _rowwise(_f_post_pre, "post_pre_xa", t, [_full(s["x1"]), _full(s["xo"])],
                                [(p["g_post_xa"], l), (p["g_pre_ffn"], l)], [(d, F32), (d, ACT_DTYPE)])
    s["gu"] = _mm(s["h3"], w["w_gu"], name="mm_gu")
    s["act"] = _rowwise(_f_swiglu, "swiglu_fwd", t, [(s["gu"], cfg.ffn, 0), (s["gu"], cfg.ffn, 1)], [],
                        [(cfg.ffn, ACT_DTYPE)])[0]
    s["dn"] = _mm(s["act"], w["w_down"], name="mm_down")
    if last:
        nxt = _rowwise(_f_final, "final_loss", t, [_full(s["x2"]), _full(s["dn"]), _full(tgt)],
                       [(p["g_post_ffn"], l)], [(d, F32)], acc_out=[(1, d)])
    else:
        nxt = _rowwise(_f_post_pre, "post_pre_ffn", t, [_full(s["x2"]), _full(s["dn"])],
                       [(p["g_post_ffn"], l), (p["g_pre_mix"], l + 1)], [(d, F32), (d, ACT_DTYPE)])
    return s, nxt, gathered


def _backward_layer(l, s, dx, d_dn, memf, w, p, cfg, nb, prev_dn, scatter):
    t = dx.shape[0]
    d = cfg.d
    g = {}
    dact = _mm(d_dn, w["w_down"], tb=True, out_dtype=ACT_DTYPE, name="mm_d_act", tn=1408)
    g["w_down"] = _mm(s["act"], d_dn, ta=True, name="mm_dw_down")
    dgu = _rowwise(_f_swiglu_bwd, "swiglu_bwd", t, [(s["gu"], cfg.ffn, 0), (s["gu"], cfg.ffn, 1), _full(dact)], [],
                   [(2 * cfg.ffn, ACT_DTYPE)])[0]
    dh3 = _mm(dgu, w["w_gu"], tb=True, name="mm_d_h3", tk=1408)
    g["w_gu"] = _mm(s["h3"], dgu, ta=True, name="mm_dw_gu")
    dx2, d_xo, g["g_pre_ffn"], g["g_post_xa"] = _rowwise(
        _f_pre_post_bwd, "pre_post_bwd_ffn", t, [_full(s["x2"]), _full(dh3), _full(dx), _full(s["xo"])],
        [(p["g_pre_ffn"], l), (p["g_post_xa"], l)], [(d, F32), (d, ACT_DTYPE)], acc_out=[(1, d), (1, d)])
    do_xa = _mm(d_xo, w["w_xo"], tb=True, out_dtype=ACT_DTYPE, name="mm_d_oxa")
    g["w_xo"] = _mm(s["o_xa"], d_xo, ta=True, name="mm_dw_xo")
    dq, dkv = _xa_bwd(s["q"], s["kv"], do_xa, cfg, nb)
    dh2 = _mm(dq, w["w_xq"], tb=True, name="mm_d_h2")
    g["w_xq"] = _mm(s["h2"], dq, ta=True, name="mm_dw_xq")
    dmem_n = _mm(dkv, w["w_xkv"], tb=True, name="mm_d_mem")
    g["w_xkv"] = _mm(s["mem_n"], dkv, ta=True, name="mm_dw_xkv")
    g["g_mem"] = _rowwise(_f_gain_bwd, "mem_norm_bwd", memf.shape[0], [_full(memf), _full(dmem_n)],
                          [(p["g_mem"], l)], [], acc_out=[(1, d)])[0]
    dx1, d_mo, g["g_pre_xa"], g["g_post_mix"] = _rowwise(
        _f_pre_post_bwd, "pre_post_bwd_xa", t, [_full(s["x1"]), _full(dh2), _full(dx2), _full(s["mo"])],
        [(p["g_pre_xa"], l), (p["g_post_mix"], l)], [(d, F32), (d, ACT_DTYPE)], acc_out=[(1, d), (1, d)])
    dmerged = _mm(d_mo, w["w_mix_out"], tb=True, out_dtype=ACT_DTYPE, name="mm_d_merged")
    g["w_mix_out"] = _mm(s["merged"], d_mo, ta=True, name="mm_dw_mix_out")
    proj = s["proj"]
    dgg, dba, dbs = _rowwise(
        _f_merge_bwd, "merge_bwd", t,
        [(proj, d, cfg.ga0 // d), (proj, d, cfg.gs0 // d), _full(s["ba"]), _full(s["bs"]), _full(dmerged)], [],
        [(2 * d, ACT_DTYPE), (d, ACT_DTYPE), (d, ACT_DTYPE)])
    do_att = _mm(dba, w["w_br_att"], tb=True, name="mm_d_oatt")
    g["w_br_att"] = _mm(s["o_att"], dba, ta=True, name="mm_dw_br_att")
    do_ssm = _mm(dbs, w["w_br_ssm"], tb=True, out_dtype=ACT_DTYPE, name="mm_d_ossm")
    g["w_br_ssm"] = _mm(s["o_ssm"], dbs, ta=True, name="mm_dw_br_ssm")
    dy, dz, g["g_ssm_norm"] = _rowwise(
        _f_gate_norm_bwd, "gate_norm_bwd", t, [_full(s["y"]), (proj, cfg.inner, 0), _full(do_ssm)],
        [(p["g_ssm_norm"], l)], [(cfg.inner, F32), (cfg.inner, ACT_DTYPE)], acc_out=[(1, cfg.inner)])
    dxbc, ddt_raw, dvec = _ssd_bwd(s["xbc"], proj, s["states"], dy, p["dt_bias"], p["a_log"], p["d_skip_x"], l, cfg, nb)
    g["dt_bias"], g["a_log"], g["d_skip"] = (dvec[i:i + 1, :cfg.heads] for i in range(3))
    dxbc_raw, g["conv_w"], g["conv_b"] = _conv_bwd(proj, dxbc, w["conv_w"], p["conv_b"], l, cfg, nb)
    own = [_shard(g[n], n in _COL_SHARDED).astype(WIRE_DTYPE) for n in _BIG if n != "w_in"]
    dq_sb, dk_sb, dv_sb, landed = _sb_bwd(proj, do_att, cfg, nb, list(scatter) + own)
    landed = (landed[:len(scatter)], landed[len(scatter):])
    dproj = jnp.concatenate([dz, dgg, dq_sb, dk_sb.astype(ACT_DTYPE), dv_sb.astype(ACT_DTYPE), dxbc_raw, ddt_raw], axis=1)
    dh1 = _mm(dproj, w["w_in"], tb=True, name="mm_d_h1", tk=1152)
    g["w_in"] = _mm(s["h1"], dproj, ta=True, name="mm_dw_in")
    if prev_dn is None:
        dx0, g["g_pre_mix"] = _rowwise(_f_pre_bwd, "pre_bwd_first", t, [_full(s["x_in"]), _full(dh1), _full(dx1)],
                                       [(p["g_pre_mix"], l)], [(d, F32)], acc_out=[(1, d)])
        return g, dx0, None, None, landed
    dx0, d_dn_prev, g["g_pre_mix"], g_post_prev = _rowwise(
        _f_pre_post_bwd, "pre_post_bwd_mix", t, [_full(s["x_in"]), _full(dh1), _full(dx1), _full(prev_dn)],
        [(p["g_pre_mix"], l), (p["g_post_ffn"], l - 1)], [(d, F32), (d, ACT_DTYPE)], acc_out=[(1, d), (1, d)])
    return g, dx0, d_dn_prev, g_post_prev, landed


def kernel(x, mem, g_pre_mix, w_in, conv_w, conv_b, dt_bias, a_log, d_skip, g_ssm_norm, w_br_att, w_br_ssm, w_mix_out, g_post_mix, g_pre_xa, g_mem, w_xq, w_xkv, w_xo, g_post_xa, g_pre_ffn, w_gu, w_down, g_post_ffn, loss_target, m_g_pre_mix, m_w_in, m_conv_w, m_conv_b, m_dt_bias, m_a_log, m_d_skip, m_g_ssm_norm, m_w_br_att, m_w_br_ssm, m_w_mix_out, m_g_post_mix, m_g_pre_xa, m_g_mem, m_w_xq, m_w_xkv, m_w_xo, m_g_post_xa, m_g_pre_ffn, m_w_gu, m_w_down, m_g_post_ffn, v_g_pre_mix, v_w_in, v_conv_w, v_conv_b, v_dt_bias, v_a_log, v_d_skip, v_g_ssm_norm, v_w_br_att, v_w_br_ssm, v_w_mix_out, v_g_post_mix, v_g_pre_xa, v_g_mem, v_w_xq, v_w_xkv, v_w_xo, v_g_post_xa, v_g_pre_ffn, v_w_gu, v_w_down, v_g_post_ffn):
    vals = dict(locals())
    cfg = _Cfg()
    nb = x.shape[0]
    t = nb * SEQ
    d = cfg.d
    depth = g_pre_mix.shape[0]

    gathered_names = _BIG + ("conv_w",)
    late_names = gathered_names[1:]

    def wire(l, names=gathered_names):
        return [conv_w[l] if n == "conv_w" else vals[n][l].astype(WIRE_DTYPE) for n in names]

    def layer_weights(gathered, names=gathered_names):
        w = {n: _unshard(gw, n in _COL_SHARDED) for n, gw in zip(names, gathered)}
        if "w_in" in w:
            w["w_in"] = _permute_in(w["w_in"], cfg)
        if "conv_w" in w:
            w["conv_w"] = w["conv_w"][None]
        return w

    p = {n: _vec3(vals[n], LANES if n in ("dt_bias", "a_log", "d_skip") else None) for n in _SMALL}
    p["d_skip_x"] = _vec3(jnp.repeat(d_skip, SSM_HEAD_DIM, axis=1))
    weights = [None] * depth
    weights[0] = layer_weights(_all_gather(wire(0, ("w_in",)), "ag_weights_first"), ("w_in",))

    xf = x.reshape(t, d)
    memf = mem.reshape(nb * MEM_LEN, d)
    tgt = loss_target.reshape(t, d)
    h = _rowwise(_rms, "pre_norm_first", t, [_full(xf)], [(p["g_pre_mix"], 0)], [(d, ACT_DTYPE)])[0]
    saved = []
    xcur = xf
    for l in range(depth):
        last = l == depth - 1
        late = wire(0, late_names) if l == 0 else []
        complete = (lambda got: layer_weights(got, late_names)) if l == 0 else None
        s, nxt, gathered = _forward_layer(l, xcur, h, memf, tgt, weights[l], p, cfg, nb, last,
                                          late + ([] if last else wire(l + 1)), len(late), complete)
        saved.append(s)
        if not last:
            weights[l + 1] = layer_weights(gathered)
            xcur, h = nxt
    dx, loss_row = nxt
    loss = lax.psum(0.5 * jnp.sum(loss_row) / d, AXES)

    top = saved[-1]
    d_dn, g_post_top = _rowwise(lambda ysub, dxo, gp: _rms_bwd(ysub, gp, dxo), "post_bwd_last", t,
                                [_full(top["dn"]), _full(dx)], [(p["g_post_ffn"], depth - 1)], [(d, ACT_DTYPE)],
                                acc_out=[(1, d)])
    grads = [None] * depth
    landed = [dict() for _ in range(depth)]
    post_ffn = [None] * depth
    post_ffn[depth - 1] = g_post_top
    pending = []
    for l in reversed(range(depth)):
        prev_dn = saved[l - 1]["dn"] if l > 0 else None
        grads[l], dx, d_dn, g_post_prev, (got_above, got_own) = _backward_layer(
            l, saved[l], dx, d_dn, memf, weights[l], p, cfg, nb, prev_dn, pending)
        if pending:
            landed[l + 1]["w_in"] = got_above[0]
        landed[l].update(zip([n for n in _BIG if n != "w_in"], got_own))
        pending = [_shard(_unpermute_in(grads[l]["w_in"], cfg), True).astype(WIRE_DTYPE)]
        if l > 0:
            post_ffn[l - 1] = g_post_prev
    landed[0]["w_in"] = _scatter_blocks(pending, "scatter_grads_last")[0]
    for l in range(depth):
        grads[l]["g_post_ffn"] = post_ffn[l]
    grad_x = dx.reshape(x.shape)
    stacked = {n: jnp.stack([grads[l][n] for l in range(depth)]) for n in _SMALL + ("conv_w",)}

    out = {}
    for n in _BIG:
        parts = jnp.stack([landed[l][n] for l in range(depth)], axis=1)
        out[n] = _adamw(parts, vals[n], vals["m_" + n], vals["v_" + n], "adamw_" + n)

    small_shapes = [vals[n].shape for n in _SMALL]
    pack_g = _pack([stacked[n] for n in _SMALL])
    conv_g = stacked["conv_w"].reshape(depth * SSM_CONV, cfg.conv_dim)
    parts_small, parts_conv = _all_gather([pack_g, conv_g], "ag_small_grads")
    packed = lambda prefix: _pack([vals[prefix + n] for n in _SMALL])[None]
    res = _adamw(parts_small[:, None], packed(""), packed("m_"), packed("v_"), "adamw_small")
    unpacked = [_unpack(r, small_shapes) for r in res]
    for i, n in enumerate(_SMALL):
        out[n] = [unpacked[j][i] for j in range(4)]
    cs = conv_w.shape[2]
    me = _flat_index(lax.axis_index("x"), lax.axis_index("y"), lax.axis_index("c"))
    parts_conv = lax.dynamic_slice_in_dim(parts_conv, me * cs, cs, axis=2)
    flat = lambda a: a.reshape(1, depth * SSM_CONV, cs)
    res = _adamw(parts_conv[:, None], flat(conv_w), flat(m_conv_w), flat(v_conv_w), "adamw_conv_w")
    out["conv_w"] = [r.reshape(conv_w.shape) for r in res]

    return (loss, grad_x, *[out[n][0] for n in _WEIGHTS], *[out[n][1] for n in _WEIGHTS],
            *[out[n][2] for n in _WEIGHTS], *[out[n][3] for n in _WEIGHTS])
```

```python
import functools
import math

import jax
import jax.numpy as jnp
from jax import lax
from jax.experimental import pallas as pl
from jax.experimental.pallas import tpu as pltpu

F32 = jnp.float32
BF16 = jnp.bfloat16
MXU_DTYPE = BF16
ACT_DTYPE = BF16
WIRE_DTYPE = BF16

D_MODEL = 1024
SEQ = 2048
DEPTH = 4
MEM_LEN = 256
RMS_EPS = 1e-6
SB_HEADS = 16
SB_HEAD_DIM = 64
SB_BLOCK = 128
SSM_INNER = 2 * D_MODEL
SSM_HEAD_DIM = 64
SSM_GROUPS = 4
SSM_STATE = 128
SSM_CONV = 4
SSM_CHUNK = 128
XA_HEADS = 4
FFN_HIDDEN = ((8 * D_MODEL + 767) // 768) * 256
ADAM_LR = 0.001
ADAM_B1 = 0.9
ADAM_B2 = 0.999
ADAM_EPS = 1e-08
ADAM_WD = 0.01
ADAM_STEP = 10

N_DEV = 8
LANES = 128
VMEM_LIMIT_BYTES = 56 * 1024 * 1024

AXES = ("x", "y", "c")
MESH = pl.DeviceIdType.MESH


class _Cfg:
    def __init__(self):
        self.d = D_MODEL
        self.sbw = SB_HEADS * SB_HEAD_DIM
        self.inner = SSM_INNER
        self.heads = SSM_INNER // SSM_HEAD_DIM
        self.epg = self.heads // SSM_GROUPS
        self.gn = SSM_GROUPS * SSM_STATE
        self.conv_dim = SSM_INNER + 2 * self.gn
        self.ffn = FFN_HIDDEN
        self.xa_dim = D_MODEL // XA_HEADS
        self.in_sizes = (self.sbw, self.sbw, self.sbw, self.inner, self.conv_dim, self.heads, self.d, self.d)
        self.in_width = sum(self.in_sizes)
        self.z0 = 0
        self.ga0 = self.inner
        self.gs0 = self.ga0 + self.d
        self.q0 = self.gs0 + self.d
        self.k0 = self.q0 + self.sbw
        self.v0 = self.k0 + self.sbw
        self.xbc0 = self.v0 + self.sbw
        self.dt0 = self.xbc0 + self.conv_dim
        self.proj_w = self.dt0 + LANES
        assert self.heads <= LANES


def _cparams(sem=None):
    return pltpu.CompilerParams(dimension_semantics=sem, vmem_limit_bytes=VMEM_LIMIT_BYTES)


def _tile(n, pref, mult):
    if n <= pref:
        return n
    t = (pref // mult) * mult
    while t >= mult:
        if n % t == 0:
            return t
        t -= mult
    return n


MM_VMEM_BUDGET = 40 * 1024 * 1024


def _mm(a, b, *, ta=False, tb=False, out_dtype=F32, name, tm=1024, tn=1152, tk=2048):
    kk, m = (a.shape if ta else a.shape[::-1])
    if tb:
        n, k2 = b.shape
    else:
        k2, n = b.shape
    assert kk == k2, (name, a.shape, b.shape)
    size = lambda dt: jnp.dtype(dt).itemsize
    tn = _tile(n, tn, LANES)
    tk = _tile(kk, tk, LANES if (tb or not ta) else 16)
    nk = kk // tk
    while True:
        tm_ = _tile(m, tm, LANES if ta else 16)
        need = (2 * (tm_ * tk * size(a.dtype) + tk * tn * size(b.dtype) + tm_ * tn * size(out_dtype))
                + (tm_ * tk + tk * tn) * size(MXU_DTYPE) + tm_ * tn * 4 * (2 if nk > 1 else 1))
        if need <= MM_VMEM_BUDGET or tm <= 128:
            break
        tm //= 2
    tm = tm_
    gi, gj = m // tm, n // tn
    j_outer = nk == 1 and b.size * size(b.dtype) * (gi - 1) > a.size * size(a.dtype) * (gj - 1)
    dims = (((0 if ta else 1,), (1 if tb else 0,)), ((), ()))

    def ij(g0, g1):
        return (g1, g0) if j_outer else (g0, g1)

    def body(a_ref, b_ref, o_ref, *scratch):
        av = a_ref[...].astype(MXU_DTYPE)
        bv = b_ref[...].astype(MXU_DTYPE)
        part = lax.dot_general(av, bv, dims, preferred_element_type=F32)
        if nk == 1:
            o_ref[...] = part.astype(out_dtype)
        else:
            acc_ref, = scratch
            k = pl.program_id(2)

            @pl.when(k == 0)
            def _():
                acc_ref[...] = part

            @pl.when(k > 0)
            def _():
                acc_ref[...] += part

            @pl.when(k == nk - 1)
            def _():
                o_ref[...] = acc_ref[...].astype(out_dtype)

    if ta:
        a_spec = pl.BlockSpec((tk, tm), lambda g0, g1, k: (k, ij(g0, g1)[0]))
    else:
        a_spec = pl.BlockSpec((tm, tk), lambda g0, g1, k: (ij(g0, g1)[0], k))
    if tb:
        b_spec = pl.BlockSpec((tn, tk), lambda g0, g1, k: (ij(g0, g1)[1], k))
    else:
        b_spec = pl.BlockSpec((tk, tn), lambda g0, g1, k: (k, ij(g0, g1)[1]))
    return pl.pallas_call(
        body,
        out_shape=jax.ShapeDtypeStruct((m, n), out_dtype),
        grid=(gj, gi, nk) if j_outer else (gi, gj, nk),
        in_specs=[a_spec, b_spec],
        out_specs=pl.BlockSpec((tm, tn), lambda g0, g1, k: ij(g0, g1)),
        scratch_shapes=[] if nk == 1 else [pltpu.VMEM((tm, tn), F32)],
        compiler_params=_cparams(("parallel", "parallel", "arbitrary")),
        name=name,
    )(a, b)


def _rowwise(fn, name, rows, row_in, vec_in, row_out, acc_out=(), tr=256):
    tr = _tile(rows, tr, 16)
    n_in = len(row_in) + len(vec_in)
    n_ro = len(row_out)

    def body(*refs):
        ins = [r[...].astype(F32) for r in refs[:n_in]]
        outs = fn(*ins)
        if not isinstance(outs, (tuple, list)):
            outs = (outs,)
        out_refs = refs[n_in:]
        for o_ref, val in zip(out_refs[:n_ro], outs[:n_ro]):
            o_ref[...] = val.astype(o_ref.dtype)
        if acc_out:
            i = pl.program_id(0)
            for o_ref, val in zip(out_refs[n_ro:], outs[n_ro:]):
                @pl.when(i == 0)
                def _(o_ref=o_ref, val=val):
                    o_ref[...] = val

                @pl.when(i > 0)
                def _(o_ref=o_ref, val=val):
                    o_ref[...] += val

    in_specs = [pl.BlockSpec((tr, w), functools.partial(lambda i, cb: (i, cb), cb=cb)) for (_, w, cb) in row_in]
    in_specs += [pl.BlockSpec((None,) + v.shape[1:], functools.partial(lambda i, l: (l, 0, 0), l=l)) for (v, l) in vec_in]
    out_shape = [jax.ShapeDtypeStruct((rows, w), dt) for (w, dt) in row_out]
    out_shape += [jax.ShapeDtypeStruct(s, F32) for s in acc_out]
    out_specs = [pl.BlockSpec((tr, w), lambda i: (i, 0)) for (w, _) in row_out]
    out_specs += [pl.BlockSpec(s, lambda i: (0, 0)) for s in acc_out]
    res = pl.pallas_call(
        body,
        out_shape=out_shape,
        grid=(rows // tr,),
        in_specs=in_specs,
        out_specs=out_specs,
        compiler_params=_cparams(("arbitrary",) if acc_out else ("parallel",)),
        name=name,
    )(*[a for (a, _, _) in row_in], *[v for (v, _) in vec_in])
    return res


def _rms(x, g):
    r = lax.rsqrt(jnp.mean(x * x, axis=-1, keepdims=True) + RMS_EPS)
    return x * r * g


def _rms_bwd(x, g, dy):
    r = lax.rsqrt(jnp.mean(x * x, axis=-1, keepdims=True) + RMS_EPS)
    xh = x * r
    dxh = dy * g
    dx = r * (dxh - xh * jnp.mean(dxh * xh, axis=-1, keepdims=True))
    return dx, jnp.sum(dy * xh, axis=0, keepdims=True)


def _silu(x):
    return x * jax.nn.sigmoid(x)


def _silu_grad(x):
    s = jax.nn.sigmoid(x)
    return s * (1.0 + x * (1.0 - s))


def _softplus(x):
    return jnp.maximum(x, 0.0) + jnp.log1p(jnp.exp(-jnp.abs(x)))


def _full(a):
    return (a, a.shape[1], 0)


def _f_post_pre(x, ysub, g_post, g_pre):
    xn = x + _rms(ysub, g_post)
    return xn, _rms(xn, g_pre)


def _f_final(x, ysub, tgt, g_post):
    err = x + _rms(ysub, g_post) - tgt
    return err * (1.0 / D_MODEL), jnp.sum(err * err, axis=0, keepdims=True)


def _f_pre_post_bwd(xmid, dh, dxo, ysub, g_pre, g_post):
    d1, dg_pre = _rms_bwd(xmid, g_pre, dh)
    dxm = dxo + d1
    dys, dg_post = _rms_bwd(ysub, g_post, dxm)
    return dxm, dys, dg_pre, dg_post


def _f_pre_bwd(x, dh, dxo, g_pre):
    d1, dg_pre = _rms_bwd(x, g_pre, dh)
    return dxo + d1, dg_pre


def _f_gain_bwd(x, dy, g):
    return _rms_bwd(x, g, dy)[1]


def _group_norm_parts(u):
    gw = u.shape[1] // SSM_GROUPS
    parts = []
    for gi in range(SSM_GROUPS):
        ug = u[:, gi * gw:(gi + 1) * gw]
        r = lax.rsqrt(jnp.mean(ug * ug, axis=-1, keepdims=True) + RMS_EPS)
        parts.append((ug * r, r))
    return gw, parts


def _f_gate_norm(y, z, g):
    _, parts = _group_norm_parts(y * _silu(z))
    return jnp.concatenate([uh for uh, _ in parts], axis=1) * g


def _f_gate_norm_bwd(y, z, do, g):
    sz = _silu(z)
    gw, parts = _group_norm_parts(y * sz)
    dxh = do * g
    du = []
    for gi, (uh, r) in enumerate(parts):
        dg_ = dxh[:, gi * gw:(gi + 1) * gw]
        du.append(r * (dg_ - uh * jnp.mean(dg_ * uh, axis=-1, keepdims=True)))
    du = jnp.concatenate(du, axis=1)
    uh_all = jnp.concatenate([uh for uh, _ in parts], axis=1)
    return du * sz, du * y * _silu_grad(z), jnp.sum(do * uh_all, axis=0, keepdims=True)


def _f_merge(ga, gs, ba, bs):
    return jax.nn.sigmoid(ga) * ba + jax.nn.sigmoid(gs) * bs


def _f_merge_bwd(ga, gs, ba, bs, dm):
    sa, ss = jax.nn.sigmoid(ga), jax.nn.sigmoid(gs)
    dgg = jnp.concatenate([dm * ba * sa * (1.0 - sa), dm * bs * ss * (1.0 - ss)], axis=1)
    return dgg, dm * sa, dm * ss


def _f_swiglu(gate, up):
    return _silu(gate) * up


def _f_swiglu_bwd(gate, up, da):
    return jnp.concatenate([da * up * _silu_grad(gate), da * _silu(gate)], axis=1)


def _split_dot(x, u):
    hi = x.astype(BF16)
    lo = (x - hi.astype(F32)).astype(BF16)
    return jnp.dot(hi, u, preferred_element_type=F32) + jnp.dot(lo, u, preferred_element_type=F32)


SB_ROWS = 512
SB_UNROLL = 4
C00 = (((0,), (0,)), ((), ()))
C11 = (((1,), (1,)), ((), ()))


def _sb_setup(q_ref, tq):
    hp = LANES // SB_HEAD_DIM
    lane = lax.broadcasted_iota(jnp.int32, (1, LANES), 1)
    heads = [jnp.logical_and(lane >= h * SB_HEAD_DIM, lane < (h + 1) * SB_HEAD_DIM) for h in range(hp)]
    qs = q_ref[...] * (SB_HEAD_DIM ** -0.5)
    q_h = [jnp.where(hm, qs, 0.0).astype(MXU_DTYPE) for hm in heads]
    row = lax.broadcasted_iota(jnp.int32, (tq, SB_BLOCK), 0)
    col = lax.broadcasted_iota(jnp.int32, (tq, SB_BLOCK), 1)
    sq_row = lax.broadcasted_iota(jnp.int32, (SB_BLOCK, SB_BLOCK), 0)
    sq_col = lax.broadcasted_iota(jnp.int32, (SB_BLOCK, SB_BLOCK), 1)
    return heads, q_h, col - row, sq_row, sq_col


def _sb_scores(q, kj, mask):
    z = lax.dot_general(q, kj, C11, preferred_element_type=F32)
    lm = -(jnp.maximum(z, 0.0) + jnp.log(1.0 + jnp.exp(-jnp.abs(z))))
    return z, lm if mask is None else jnp.where(mask, lm, 0.0)


def _add_rows(x, r0, upd):
    return x + upd if r0 == 0 else jnp.concatenate([x[:r0], x[r0:] + upd], axis=0)


def _grid_step3(nb, ncb, nq):
    return (pl.program_id(0) * ncb + pl.program_id(1)) * nq + pl.program_id(2), nb * ncb * nq


def _sb_fwd(proj, cfg, nb, gather=(), gather_layers=None):
    blk = SB_BLOCK
    tq = _tile(SEQ, SB_ROWS, blk)
    nq = SEQ // tq
    kpq = tq // blk
    unr = math.gcd(kpq, SB_UNROLL)
    hp = LANES // SB_HEAD_DIM
    ncb = cfg.sbw // LANES
    qb, kb, vb = cfg.q0 // LANES, cfg.k0 // LANES, cfg.v0 // LANES
    ng = len(gather)

    def body(q_ref, k_ref, v_ref, *rest):
        o_ref = rest[ng]
        if ng:
            step_id, n_steps = _grid_step3(nb, ncb, nq)
            start, forward, finish = _gather_phases(rest[:ng], rest[ng + 1:2 * ng + 1], *rest[2 * ng + 1:],
                                                    layers=gather_layers)
            pl.when(step_id == 0)(start)
            pl.when(step_id == (3 * n_steps) // 4)(forward)
        _sb_fwd_block(q_ref, k_ref, v_ref, o_ref)
        if ng:
            pl.when(step_id == n_steps - 1)(finish)

    def _sb_fwd_block(q_ref, k_ref, v_ref, o_ref):
        i = pl.program_id(2)
        heads, q_h, cmr, sq_row, sq_col = _sb_setup(q_ref, tq)
        u_rev = (sq_row >= sq_col).astype(BF16)

        def key_block(j, r0, acc, runs, diagonal):
            rows = pl.ds(pl.multiple_of(j * blk, blk), blk)
            kj = k_ref[rows, :].astype(MXU_DTYPE)
            vj = v_ref[rows, :].astype(MXU_DTYPE)
            mask = cmr[:tq - r0] < 0 if diagonal else None
            for h in range(hp):
                z, lm = _sb_scores(q_h[h][r0:], kj, mask)
                cs = _split_dot(lm, u_rev)
                w = jnp.exp(z + cs + runs[h][r0:])
                if diagonal:
                    w = jnp.where(mask, w, 0.0)
                upd = jnp.dot(w.astype(MXU_DTYPE), jnp.where(heads[h], vj, 0), preferred_element_type=F32)
                acc = _add_rows(acc, r0, upd)
                runs[h] = _add_rows(runs[h], r0, cs[:, 0:1])
            return acc

        acc = jnp.zeros((tq, LANES), F32)
        runs = [jnp.zeros((tq, 1), F32) for _ in range(hp)]
        for r in reversed(range(kpq)):
            acc = key_block(i * kpq + r, r * blk, acc, runs, True)

        def step(n, carry):
            acc, runs = carry
            runs = list(runs)
            for jj in range(unr):
                acc = key_block(i * kpq - 1 - (n * unr + jj), 0, acc, runs, False)
            return acc, tuple(runs)

        acc, _ = lax.fori_loop(0, i * (kpq // unr), step, (acc, tuple(runs)))
        o_ref[...] = acc

    res = pl.pallas_call(
        body,
        out_shape=[jax.ShapeDtypeStruct((nb * SEQ, cfg.sbw), F32)] + _gather_shapes(gather, gather_layers),
        grid=(nb, ncb, nq),
        in_specs=[
            pl.BlockSpec((tq, LANES), lambda b, c, i: (b * nq + i, qb + c)),
            pl.BlockSpec((SEQ, LANES), lambda b, c, i: (b, kb + c)),
            pl.BlockSpec((SEQ, LANES), lambda b, c, i: (b, vb + c)),
        ] + [ANY_SPEC] * ng,
        out_specs=[pl.BlockSpec((tq, LANES), lambda b, c, i: (b * nq + i, c))] + [ANY_SPEC] * ng,
        scratch_shapes=_comm_sems(ng) if ng else [],
        compiler_params=_cparams(("arbitrary",) * 3 if ng else ("parallel", "parallel", "arbitrary")),
        name="sb_fwd_gather" if ng else "sb_fwd",
    )(proj, proj, proj, *gather)
    return res[0], res[1:]


def _sb_bwd(proj, do_att, cfg, nb, scatter=()):
    blk = SB_BLOCK
    tq = _tile(SEQ, SB_ROWS, blk)
    nq = SEQ // tq
    kpq = tq // blk
    unr = math.gcd(kpq, SB_UNROLL)
    hp = LANES // SB_HEAD_DIM
    ncb = cfg.sbw // LANES
    scale = SB_HEAD_DIM ** -0.5
    qb, kb, vb = cfg.q0 // LANES, cfg.k0 // LANES, cfg.v0 // LANES
    ns = len(scatter)

    def body(q_ref, k_ref, v_ref, do_ref, *rest):
        dq_ref, dk_ref, dv_ref = rest[ns:ns + 3]
        g_ref, z_ref = rest[2 * ns + 3:2 * ns + 5]
        if ns:
            step_id, n_steps = _grid_step3(nb, ncb, nq)
            start, finish = _scatter_phases(rest[:ns], rest[ns + 3:2 * ns + 3], *rest[2 * ns + 5:])
            pl.when(step_id == 0)(start)
        _sb_bwd_block(q_ref, k_ref, v_ref, do_ref, dq_ref, dk_ref, dv_ref, g_ref, z_ref)
        if ns:
            pl.when(step_id == n_steps - 1)(finish)

    def _sb_bwd_block(q_ref, k_ref, v_ref, do_ref, dq_ref, dk_ref, dv_ref, g_ref, z_ref):
        i = pl.program_id(2)

        @pl.when(i == 0)
        def _():
            dk_ref[...] = jnp.zeros_like(dk_ref)
            dv_ref[...] = jnp.zeros_like(dv_ref)

        heads, q_h, cmr, sq_row, sq_col = _sb_setup(q_ref, tq)
        u_rev = (sq_row >= sq_col).astype(BF16)
        u_fwd = (sq_row <= sq_col).astype(BF16)
        do = do_ref[...]
        do_h = [jnp.where(hm, do, 0.0).astype(MXU_DTYPE) for hm in heads]


        def left_block(j, r0, runs, diagonal):
            rows = pl.ds(pl.multiple_of(j * blk, blk), blk)
            kj = k_ref[rows, :].astype(MXU_DTYPE)
            vj = v_ref[rows, :].astype(MXU_DTYPE)
            mask = cmr[:tq - r0] < 0 if diagonal else None
            dv = jnp.zeros((blk, LANES), F32)
            for h in range(hp):
                z, lm = _sb_scores(q_h[h][r0:], kj, mask)
                cs = _split_dot(lm, u_rev)
                a = jnp.exp(z + cs + runs[h][r0:])
                if diagonal:
                    a = jnp.where(mask, a, 0.0)
                da = lax.dot_general(do_h[h][r0:], vj, C11, preferred_element_type=F32)
                dv = dv + lax.dot_general(a.astype(MXU_DTYPE), do_h[h][r0:], C00, preferred_element_type=F32)
                g_ref[h, j, r0:, :] = a * da
                z_ref[h, j, r0:, :] = jax.nn.sigmoid(z)
                runs[h] = _add_rows(runs[h], r0, cs[:, 0:1])
            dv_ref[rows, :] += dv

        runs = [jnp.zeros((tq, 1), F32) for _ in range(hp)]
        for r in reversed(range(kpq)):
            left_block(i * kpq + r, r * blk, runs, True)

        def sweep_left(n, runs):
            runs = list(runs)
            for jj in range(unr):
                left_block(i * kpq - 1 - (n * unr + jj), 0, runs, False)
            return tuple(runs)

        trips = i * (kpq // unr)
        lax.fori_loop(0, trips, sweep_left, tuple(runs))

        def right_block(j, r0, dq, runs, diagonal):
            rows = pl.ds(pl.multiple_of(j * blk, blk), blk)
            kj = k_ref[rows, :].astype(MXU_DTYPE)
            dk = jnp.zeros((blk, LANES), F32)
            for h in range(hp):
                g = g_ref[h, j, r0:, :]
                g_upto = _split_dot(g, u_fwd) + runs[h][r0:]
                dz = g - z_ref[h, j, r0:, :] * g_upto
                if diagonal:
                    dz = jnp.where(cmr[:tq - r0] < 0, dz, 0.0)
                dz = dz.astype(MXU_DTYPE)
                dq = _add_rows(dq, r0, jnp.dot(dz, jnp.where(heads[h], kj, 0), preferred_element_type=F32))
                dk = dk + lax.dot_general(dz, q_h[h][r0:], C00, preferred_element_type=F32)
                runs[h] = _add_rows(runs[h], r0, jnp.sum(g, axis=1, keepdims=True))
            dk_ref[rows, :] += dk
            return dq

        def sweep_right(n, carry):
            dq, runs = carry
            runs = list(runs)
            for jj in range(unr):
                dq = right_block(n * unr + jj, 0, dq, runs, False)
            return dq, tuple(runs)

        init = (jnp.zeros((tq, LANES), F32), tuple(jnp.zeros((tq, 1), F32) for _ in range(hp)))
        dq, runs = lax.fori_loop(0, trips, sweep_right, init)
        runs = list(runs)
        for r in range(kpq):
            dq = right_block(i * kpq + r, r * blk, dq, runs, True)
        dq_ref[...] = (dq * scale).astype(dq_ref.dtype)

    kv_spec_out = pl.BlockSpec((SEQ, LANES), lambda b, c, i: (b, c))
    q_spec_out = pl.BlockSpec((tq, LANES), lambda b, c, i: (b * nq + i, c))
    res = pl.pallas_call(
        body,
        out_shape=[
            jax.ShapeDtypeStruct((nb * SEQ, cfg.sbw), ACT_DTYPE),
            jax.ShapeDtypeStruct((nb * SEQ, cfg.sbw), F32),
            jax.ShapeDtypeStruct((nb * SEQ, cfg.sbw), F32),
        ] + [jax.ShapeDtypeStruct(a.shape, a.dtype) for a in scatter],
        grid=(nb, ncb, nq),
        in_specs=[
            pl.BlockSpec((tq, LANES), lambda b, c, i: (b * nq + i, qb + c)),
            pl.BlockSpec((SEQ, LANES), lambda b, c, i: (b, kb + c)),
            pl.BlockSpec((SEQ, LANES), lambda b, c, i: (b, vb + c)),
            q_spec_out,
        ] + [ANY_SPEC] * ns,
        out_specs=[q_spec_out, kv_spec_out, kv_spec_out] + [ANY_SPEC] * ns,
        scratch_shapes=[pltpu.VMEM((hp, SEQ // blk, tq, blk), F32), pltpu.VMEM((hp, SEQ // blk, tq, blk), F32)]
        + (_comm_sems(ns) if ns else []),
        compiler_params=_cparams(("arbitrary",) * 3 if ns else ("parallel", "parallel", "arbitrary")),
        name="sb_bwd_scatter" if ns else "sb_bwd",
    )(proj, proj, proj, do_att, *scatter)
    return res[0], res[1], res[2], res[3:]


CONV_COLS = 256


def _conv_pre(x, w, b, t):
    kw = SSM_CONV
    shifted = []
    pre = b + w[kw - 1:kw, :] * x
    for k in range(kw - 1):
        d = kw - 1 - k
        xs = jnp.where(t >= d, pltpu.roll(x, d, 0), 0.0)
        shifted.append(xs)
        pre = pre + w[k:k + 1, :] * xs
    shifted.append(x)
    return pre, shifted


def _conv_fwd(proj, conv_w, conv_b, l, cfg, nb):
    cw = CONV_COLS
    ncb = cfg.conv_dim // cw
    xb = cfg.xbc0 // cw

    def body(x_ref, w_ref, b_ref, o_ref):
        x = x_ref[...]
        t = lax.broadcasted_iota(jnp.int32, x.shape, 0)
        pre, _ = _conv_pre(x, w_ref[...], b_ref[...], t)
        o_ref[...] = _silu(pre)

    return pl.pallas_call(
        body,
        out_shape=jax.ShapeDtypeStruct((nb * SEQ, cfg.conv_dim), F32),
        grid=(ncb, nb),
        in_specs=[
            pl.BlockSpec((SEQ, cw), lambda j, b: (b, xb + j)),
            pl.BlockSpec((None, SSM_CONV, cw), lambda j, b: (0, 0, j)),
            pl.BlockSpec((None, 1, cw), lambda j, b: (l, 0, j)),
        ],
        out_specs=pl.BlockSpec((SEQ, cw), lambda j, b: (b, j)),
        compiler_params=_cparams(("parallel", "parallel")),
        name="conv_fwd",
    )(proj, conv_w, conv_b)


def _conv_bwd(proj, dact, conv_w, conv_b, l, cfg, nb):
    cw = CONV_COLS
    ncb = cfg.conv_dim // cw
    xb = cfg.xbc0 // cw
    kw = SSM_CONV

    def body(x_ref, da_ref, w_ref, b_ref, dx_ref, dw_ref, db_ref):
        b_id = pl.program_id(1)
        x = x_ref[...]
        w = w_ref[...]
        t = lax.broadcasted_iota(jnp.int32, x.shape, 0)
        pre, shifted = _conv_pre(x, w, b_ref[...], t)
        dpre = da_ref[...] * _silu_grad(pre)
        dx = w[kw - 1:kw, :] * dpre
        for k in range(kw - 1):
            d = kw - 1 - k
            dx = dx + w[k:k + 1, :] * jnp.where(t < SEQ - d, pltpu.roll(dpre, SEQ - d, 0), 0.0)
        dx_ref[...] = dx.astype(dx_ref.dtype)
        dw = jnp.concatenate([jnp.sum(dpre * s, axis=0, keepdims=True) for s in shifted], axis=0)
        db = jnp.sum(dpre, axis=0, keepdims=True)

        @pl.when(b_id == 0)
        def _():
            dw_ref[...] = dw
            db_ref[...] = db

        @pl.when(b_id > 0)
        def _():
            dw_ref[...] += dw
            db_ref[...] += db

    return pl.pallas_call(
        body,
        out_shape=[
            jax.ShapeDtypeStruct((nb * SEQ, cfg.conv_dim), ACT_DTYPE),
            jax.ShapeDtypeStruct((kw, cfg.conv_dim), F32),
            jax.ShapeDtypeStruct((1, cfg.conv_dim), F32),
        ],
        grid=(ncb, nb),
        in_specs=[
            pl.BlockSpec((SEQ, cw), lambda j, b: (b, xb + j)),
            pl.BlockSpec((SEQ, cw), lambda j, b: (b, j)),
            pl.BlockSpec((None, kw, cw), lambda j, b: (0, 0, j)),
            pl.BlockSpec((None, 1, cw), lambda j, b: (l, 0, j)),
        ],
        out_specs=[
            pl.BlockSpec((SEQ, cw), lambda j, b: (b, j)),
            pl.BlockSpec((kw, cw), lambda j, b: (0, j)),
            pl.BlockSpec((1, cw), lambda j, b: (0, j)),
        ],
        compiler_params=_cparams(("parallel", "arbitrary")),
        name="conv_bwd",
    )(proj, dact, conv_w, conv_b)


def _ssd_common(dt_raw, dt_bias, a_log, tri):
    ln = SSM_CHUNK
    dt = _softplus(dt_raw + dt_bias)
    a = -jnp.exp(a_log)
    a_cs = jnp.dot(tri, dt * a, preferred_element_type=F32, precision=lax.Precision.HIGHEST)
    a_last = a_cs[ln - 1:ln, :]
    return dt, a, a_cs, a_cs.T, jnp.exp(a_cs), jnp.exp(a_last - a_cs), jnp.exp(a_last)


def _ssd_specs(cfg, nc, rev):
    ln = SSM_CHUNK
    cidx = (lambda c: nc - 1 - c) if rev else (lambda c: c)
    bmb = cfg.inner // cfg.gn
    return [
        pl.BlockSpec((ln, cfg.inner), lambda b, c: (b * nc + cidx(c), 0)),
        pl.BlockSpec((ln, cfg.gn), lambda b, c: (b * nc + cidx(c), bmb)),
        pl.BlockSpec((ln, cfg.gn), lambda b, c: (b * nc + cidx(c), bmb + 1)),
        pl.BlockSpec((ln, LANES), lambda b, c: (b * nc + cidx(c), cfg.dt0 // LANES)),
    ]


def _split3_dot(x, u):
    hi = x.astype(BF16)
    r1 = x - hi.astype(F32)
    mid = r1.astype(BF16)
    lo = (r1 - mid.astype(F32)).astype(BF16)
    dot = lambda a: jnp.dot(a, u, preferred_element_type=F32)
    return dot(hi) + dot(mid) + dot(lo)


def _ssd_consts(cfg):
    p = SSM_HEAD_DIM
    hrow = jnp.arange(LANES)[:, None]
    spread = (jnp.arange(cfg.inner)[None, :] // p == hrow).astype(BF16)
    spread_tile = (jnp.arange(cfg.heads * LANES)[None, :] // LANES == hrow).astype(BF16)
    return spread, spread_tile, spread.T


def _ssd_chunk_terms(dtr, bias, alog, spread, spread_tile):
    ln = SSM_CHUNK
    row = lax.broadcasted_iota(jnp.int32, (ln, ln), 0)
    col = lax.broadcasted_iota(jnp.int32, (ln, ln), 1)
    tril = row >= col
    dt, a, a_cs, a_t, e_a, dte, cd = _ssd_common(dtr, bias, alog, tril.astype(F32))
    ex = lambda v: _split_dot(v, spread)
    cd_x = ex(jnp.broadcast_to(cd, (8, LANES)))[0:1]
    colb = _split3_dot(a_cs, spread_tile)
    return dict(tril=tril, row=row, col=col, dt=dt, a=a, a_cs=a_cs, a_t=a_t, e_a=e_a, dte=dte, cd=cd,
                dt_x=ex(dt), ea_x=ex(e_a), dte_x=ex(dte), cd_x=cd_x, colb=colb)


def _head_masks():
    lane = lax.broadcasted_iota(jnp.int32, (1, LANES), 1)
    hpt = LANES // SSM_HEAD_DIM
    return [jnp.logical_and(lane >= i * SSM_HEAD_DIM, lane < (i + 1) * SSM_HEAD_DIM) for i in range(hpt)]


def _ssd_fwd(xbc, proj, dt_bias, a_log, d_skip_x, l, cfg, nb):
    ln, p, n = SSM_CHUNK, SSM_HEAD_DIM, SSM_STATE
    nc = SEQ // ln
    g_, e_ = SSM_GROUPS, cfg.epg
    gw = e_ * p
    hpt = LANES // p
    assert cfg.inner % cfg.gn == 0 and gw % LANES == 0
    spread, spread_tile, _ = _ssd_consts(cfg)

    def body(xs_ref, bm_ref, cm_ref, dtr_ref, bias_ref, alog_ref, dskx_ref, sp_ref, spt_ref, y_ref, st_ref, s_ref):
        c = pl.program_id(1)

        @pl.when(c == 0)
        def _():
            s_ref[...] = jnp.zeros_like(s_ref)

        st_ref[...] = s_ref[...]
        t = _ssd_chunk_terms(dtr_ref[...], bias_ref[...], alog_ref[...], sp_ref[...], spt_ref[...])
        hm = _head_masks()
        xs = xs_ref[...]
        xd = xs * t["dt_x"]
        xde = (xd * t["dte_x"]).astype(MXU_DTYPE)
        for g in range(g_):
            gc = slice(g * gw, (g + 1) * gw)
            bm = bm_ref[:, g * n:(g + 1) * n].astype(MXU_DTYPE)
            cm = cm_ref[:, g * n:(g + 1) * n].astype(MXU_DTYPE)
            cb = lax.dot_general(cm, bm, C11, preferred_element_type=F32)
            sg = s_ref[g * n:(g + 1) * n, :]
            y_off = jnp.dot(cm, sg.astype(MXU_DTYPE), preferred_element_type=F32) * t["ea_x"][:, gc]
            s_ref[g * n:(g + 1) * n, :] = t["cd_x"][:, gc] * sg + lax.dot_general(
                bm, xde[:, gc], C00, preferred_element_type=F32)
            for k in range(gw // LANES):
                lanes = slice(g * gw + k * LANES, g * gw + (k + 1) * LANES)
                xp = xd[:, lanes]
                acc = y_off[:, k * LANES:(k + 1) * LANES] + dskx_ref[:, lanes] * xs[:, lanes]
                for i in range(hpt):
                    h = (g * gw + k * LANES) // p + i
                    lmat = jnp.exp(jnp.where(t["tril"], t["colb"][:, h * LANES:(h + 1) * LANES] - t["a_t"][h:h + 1, :],
                                             -jnp.inf))
                    acc = acc + jnp.dot((cb * lmat).astype(MXU_DTYPE),
                                        jnp.where(hm[i], xp, 0.0).astype(MXU_DTYPE), preferred_element_type=F32)
                y_ref[:, lanes] = acc

    vec = lambda b, c: (l, 0, 0)
    whole = lambda a: pl.BlockSpec(a.shape, lambda b, c: (0, 0))
    return pl.pallas_call(
        body,
        out_shape=[
            jax.ShapeDtypeStruct((nb * SEQ, cfg.inner), F32),
            jax.ShapeDtypeStruct((nb * nc * g_ * n, gw), F32),
        ],
        grid=(nb, nc),
        in_specs=_ssd_specs(cfg, nc, False) + [pl.BlockSpec((None, 1, LANES), vec)] * 2
        + [pl.BlockSpec((None, 1, cfg.inner), vec), whole(spread), whole(spread_tile)],
        out_specs=[
            pl.BlockSpec((ln, cfg.inner), lambda b, c: (b * nc + c, 0)),
            pl.BlockSpec((g_ * n, gw), lambda b, c: (b * nc + c, 0)),
        ],
        scratch_shapes=[pltpu.VMEM((g_ * n, gw), F32)],
        compiler_params=_cparams(("parallel", "arbitrary")),
        name="ssd_fwd",
    )(xbc, xbc, xbc, proj, dt_bias, a_log, d_skip_x, spread, spread_tile)


def _ssd_bwd(xbc, proj, states, dy, dt_bias, a_log, d_skip_x, l, cfg, nb):
    ln, p, n = SSM_CHUNK, SSM_HEAD_DIM, SSM_STATE
    nc = SEQ // ln
    g_, e_ = SSM_GROUPS, cfg.epg
    gw = e_ * p
    hpt = LANES // p
    spread, spread_tile, gather_t = _ssd_consts(cfg)

    def body(xs_ref, bm_ref, cm_ref, dtr_ref, st_ref, dy_ref, bias_ref, alog_ref, dskx_ref, sp_ref, spt_ref, gt_ref,
             dxbc_ref, ddt_ref, dvec_ref, ds_ref, r1_ref, r2_ref, r4_ref, ss_ref):
        first = jnp.logical_and(pl.program_id(0) == 0, pl.program_id(1) == 0)

        @pl.when(pl.program_id(1) == 0)
        def _():
            ds_ref[...] = jnp.zeros_like(ds_ref)

        dtr = dtr_ref[...]
        bias = bias_ref[...]
        t = _ssd_chunk_terms(dtr, bias, alog_ref[...], sp_ref[...], spt_ref[...])
        tril = t["tril"]
        triu = t["row"] <= t["col"]
        hm = _head_masks()
        lane = lax.broadcasted_iota(jnp.int32, (1, LANES), 1)
        sub = lax.broadcasted_iota(jnp.int32, (LANES, 1), 0)
        xs = xs_ref[...]
        dyv = dy_ref[...]
        xd = xs * t["dt_x"]
        xde = xd * t["dte_x"]
        xde_m = xde.astype(MXU_DTYPE)
        dye_m = (dyv * t["ea_x"]).astype(MXU_DTYPE)
        da_col = jnp.zeros((ln, LANES), F32)
        da_row_t = jnp.zeros((LANES, ln), F32)
        ss_ref[...] = jnp.zeros_like(ss_ref)
        for g in range(g_):
            gc = slice(g * gw, (g + 1) * gw)
            gr = slice(g * n, (g + 1) * n)
            bm = bm_ref[:, gr].astype(MXU_DTYPE)
            cm = cm_ref[:, gr].astype(MXU_DTYPE)
            cb = lax.dot_general(cm, bm, C11, preferred_element_type=F32)
            cb_t = lax.dot_general(bm, cm, C11, preferred_element_type=F32)
            sp = st_ref[gr, :]
            dsg = ds_ref[gr, :]
            sp_m = sp.astype(MXU_DTYPE)
            dsg_m = dsg.astype(MXU_DTYPE)
            bds = jnp.dot(bm, dsg_m, preferred_element_type=F32)
            y_off = jnp.dot(cm, sp_m, preferred_element_type=F32) * t["ea_x"][:, gc]
            dcm = lax.dot_general(dye_m[:, gc], sp_m, C11, preferred_element_type=F32)
            dbm = lax.dot_general(xde_m[:, gc], dsg_m, C11, preferred_element_type=F32)
            ds_ref[gr, :] = t["cd_x"][:, gc] * dsg + lax.dot_general(cm, dye_m[:, gc], C00, preferred_element_type=F32)
            r4 = bds * xde[:, gc]
            r4_ref[:, gc] = r4
            r1_ref[:, gc] = dyv[:, gc] * y_off - r4
            ss_ref[0:1, gc] = jnp.sum(dsg * sp, axis=0, keepdims=True)
            dcb = jnp.zeros((ln, ln), F32)
            for k in range(gw // LANES):
                lanes = slice(g * gw + k * LANES, g * gw + (k + 1) * LANES)
                xp = xd[:, lanes]
                xp_m = xp.astype(MXU_DTYPE)
                dyp = dyv[:, lanes]
                dxp = t["dte_x"][:, lanes] * bds[:, k * LANES:(k + 1) * LANES]
                for i in range(hpt):
                    h = (g * gw + k * LANES) // p + i
                    diff = t["colb"][:, h * LANES:(h + 1) * LANES] - t["a_t"][h:h + 1, :]
                    lmat = jnp.exp(jnp.where(tril, diff, -jnp.inf))
                    lmat_t = jnp.exp(jnp.where(triu, -diff, -jnp.inf))
                    dy_h = jnp.where(hm[i], dyp, 0.0).astype(MXU_DTYPE)
                    d_ml = lax.dot_general(dy_h, xp_m, C11, preferred_element_type=F32) * lmat
                    dcb = dcb + d_ml
                    w_mat = d_ml * cb
                    dxp = dxp + jnp.dot((cb_t * lmat_t).astype(MXU_DTYPE), dy_h, preferred_element_type=F32)
                    da_col = da_col + jnp.sum(w_mat, axis=1, keepdims=True) * (lane == h).astype(F32)
                    da_row_t = da_row_t - (sub == h).astype(F32) * jnp.sum(w_mat, axis=0, keepdims=True)
                dxbc_ref[:, lanes] = dxp * t["dt_x"][:, lanes] + dskx_ref[:, lanes] * dyp
                r2_ref[:, lanes] = dxp * xs[:, lanes]
            dcb_m = dcb.astype(MXU_DTYPE)
            dxbc_ref[:, cfg.inner + g * n:cfg.inner + (g + 1) * n] = dbm + lax.dot_general(
                dcb_m, cm, C00, preferred_element_type=F32)
            dxbc_ref[:, cfg.inner + cfg.gn + g * n:cfg.inner + cfg.gn + (g + 1) * n] = dcm + jnp.dot(
                dcb_m, bm, preferred_element_type=F32)
        gt = gt_ref[...]
        rd = lambda v: _split_dot(v, gt)
        red4 = rd(r4_ref[...])
        da_last = jnp.sum(red4, axis=0, keepdims=True) + t["cd"] * rd(ss_ref[...])[0:1]
        d_acs = da_col + rd(r1_ref[...]) + da_row_t.T + jnp.where(t["row"][:, 0:1] == ln - 1, da_last, 0.0)
        da_dt = lax.dot_general(tril.astype(F32), d_acs, C00, preferred_element_type=F32,
                                precision=lax.Precision.HIGHEST)
        ddt = rd(r2_ref[...]) + da_dt * t["a"]
        ddt_raw = ddt * jax.nn.sigmoid(dtr + bias)
        ddt_ref[...] = ddt_raw.astype(ddt_ref.dtype)
        da_log = jnp.sum(da_dt * t["dt"], axis=0, keepdims=True) * t["a"]
        ddsk = jnp.sum(rd(dyv * xs), axis=0, keepdims=True)
        dvec = jnp.concatenate([jnp.sum(ddt_raw, axis=0, keepdims=True), da_log, ddsk,
                                jnp.zeros((5, LANES), F32)], axis=0)

        @pl.when(first)
        def _():
            dvec_ref[...] = dvec

        @pl.when(jnp.logical_not(first))
        def _():
            dvec_ref[...] += dvec

    vec = lambda b, c: (l, 0, 0)
    rblk = lambda b, c: (b * nc + nc - 1 - c, 0)
    whole = lambda a: pl.BlockSpec(a.shape, lambda b, c: (0, 0))
    return pl.pallas_call(
        body,
        out_shape=[
            jax.ShapeDtypeStruct((nb * SEQ, cfg.conv_dim), F32),
            jax.ShapeDtypeStruct((nb * SEQ, LANES), ACT_DTYPE),
            jax.ShapeDtypeStruct((8, LANES), F32),
        ],
        grid=(nb, nc),
        in_specs=_ssd_specs(cfg, nc, True) + [
            pl.BlockSpec((g_ * n, gw), rblk),
            pl.BlockSpec((ln, cfg.inner), rblk),
        ] + [pl.BlockSpec((None, 1, LANES), vec)] * 2 + [pl.BlockSpec((None, 1, cfg.inner), vec),
                                                           whole(spread), whole(spread_tile), whole(gather_t)],
        out_specs=[
            pl.BlockSpec((ln, cfg.conv_dim), rblk),
            pl.BlockSpec((ln, LANES), rblk),
            pl.BlockSpec((8, LANES), lambda b, c: (0, 0)),
        ],
        scratch_shapes=[pltpu.VMEM((g_ * n, gw), F32)] + [pltpu.VMEM((ln, cfg.inner), F32)] * 3
        + [pltpu.VMEM((8, cfg.inner), F32)],
        compiler_params=_cparams(("arbitrary", "arbitrary")),
        name="ssd_bwd",
    )(xbc, xbc, xbc, proj, states, dy, dt_bias, a_log, d_skip_x, spread, spread_tile, gather_t)


XA_ROWS = 256


def _xa_probs(q_ref, kv_ref, h, dh):
    c11 = (((1,), (1,)), ((), ()))
    qh = q_ref[:, h * dh:(h + 1) * dh].astype(MXU_DTYPE)
    kh = kv_ref[:, h * dh:(h + 1) * dh].astype(MXU_DTYPE)
    vh = kv_ref[:, D_MODEL + h * dh:D_MODEL + (h + 1) * dh].astype(MXU_DTYPE)
    s = lax.dot_general(qh, kh, c11, preferred_element_type=F32) * (dh ** -0.5)
    s = s - jnp.max(s, axis=1, keepdims=True)
    pr = jnp.exp(s)
    return qh, kh, vh, pr / jnp.sum(pr, axis=1, keepdims=True)


def _xa_fwd(q, kv, cfg, nb):
    tq = _tile(SEQ, XA_ROWS, 16)
    nq = SEQ // tq
    dh = cfg.xa_dim

    def body(q_ref, kv_ref, o_ref):
        for h in range(XA_HEADS):
            _, _, vh, pr = _xa_probs(q_ref, kv_ref, h, dh)
            o_ref[:, h * dh:(h + 1) * dh] = jnp.dot(pr.astype(MXU_DTYPE), vh, preferred_element_type=F32).astype(o_ref.dtype)

    return pl.pallas_call(
        body,
        out_shape=jax.ShapeDtypeStruct((nb * SEQ, D_MODEL), ACT_DTYPE),
        grid=(nb, nq),
        in_specs=[
            pl.BlockSpec((tq, D_MODEL), lambda b, i: (b * nq + i, 0)),
            pl.BlockSpec((MEM_LEN, 2 * D_MODEL), lambda b, i: (b, 0)),
        ],
        out_specs=pl.BlockSpec((tq, D_MODEL), lambda b, i: (b * nq + i, 0)),
        compiler_params=_cparams(("parallel", "parallel")),
        name="xa_fwd",
    )(q, kv)


def _xa_bwd(q, kv, do, cfg, nb):
    tq = _tile(SEQ, XA_ROWS, 16)
    nq = SEQ // tq
    dh = cfg.xa_dim
    c00 = (((0,), (0,)), ((), ()))
    c11 = (((1,), (1,)), ((), ()))
    scale = dh ** -0.5

    def body(q_ref, kv_ref, do_ref, dq_ref, dkv_ref, acc_ref):
        i = pl.program_id(1)

        @pl.when(i == 0)
        def _():
            acc_ref[...] = jnp.zeros_like(acc_ref)

        for h in range(XA_HEADS):
            hs = slice(h * dh, (h + 1) * dh)
            vs = slice(D_MODEL + h * dh, D_MODEL + (h + 1) * dh)
            qh, kh, vh, pr = _xa_probs(q_ref, kv_ref, h, dh)
            do_h = do_ref[:, hs].astype(MXU_DTYPE)
            dp = lax.dot_general(do_h, vh, c11, preferred_element_type=F32)
            ds = (pr * (dp - jnp.sum(dp * pr, axis=1, keepdims=True))).astype(MXU_DTYPE)
            dq_ref[:, hs] = (jnp.dot(ds, kh, preferred_element_type=F32) * scale).astype(dq_ref.dtype)
            acc_ref[:, hs] += lax.dot_general(ds, qh, c00, preferred_element_type=F32) * scale
            acc_ref[:, vs] += lax.dot_general(pr.astype(MXU_DTYPE), do_h, c00, preferred_element_type=F32)

        @pl.when(i == nq - 1)
        def _():
            dkv_ref[...] = acc_ref[...].astype(dkv_ref.dtype)

    return pl.pallas_call(
        body,
        out_shape=[
            jax.ShapeDtypeStruct((nb * SEQ, D_MODEL), ACT_DTYPE),
            jax.ShapeDtypeStruct((nb * MEM_LEN, 2 * D_MODEL), ACT_DTYPE),
        ],
        grid=(nb, nq),
        in_specs=[
            pl.BlockSpec((tq, D_MODEL), lambda b, i: (b * nq + i, 0)),
            pl.BlockSpec((MEM_LEN, 2 * D_MODEL), lambda b, i: (b, 0)),
            pl.BlockSpec((tq, D_MODEL), lambda b, i: (b * nq + i, 0)),
        ],
        out_specs=[
            pl.BlockSpec((tq, D_MODEL), lambda b, i: (b * nq + i, 0)),
            pl.BlockSpec((MEM_LEN, 2 * D_MODEL), lambda b, i: (b, 0)),
        ],
        scratch_shapes=[pltpu.VMEM((MEM_LEN, 2 * D_MODEL), F32)],
        compiler_params=_cparams(("parallel", "arbitrary")),
        name="xa_bwd",
    )(q, kv, do)


def _adamw(parts, w, m, v, name, tr=128):
    n, nl, r, c = parts.shape
    tr = _tile(r, tr, 16)

    def body(p_ref, w_ref, m_ref, v_ref, g_ref, d_ref, nm_ref, nv_ref):
        g = p_ref[0].astype(F32)
        for i in range(1, n):
            g = g + p_ref[i].astype(F32)
        m2 = ADAM_B1 * m_ref[...] + (1.0 - ADAM_B1) * g
        v2 = ADAM_B2 * v_ref[...] + (1.0 - ADAM_B2) * (g * g)
        m_hat = m2 / (1.0 - ADAM_B1 ** ADAM_STEP)
        v_hat = v2 / (1.0 - ADAM_B2 ** ADAM_STEP)
        g_ref[...] = g
        d_ref[...] = -ADAM_LR * (m_hat / (jnp.sqrt(v_hat) + ADAM_EPS) + ADAM_WD * w_ref[...])
        nm_ref[...] = m2
        nv_ref[...] = v2

    blk = pl.BlockSpec((None, tr, c), lambda l, i: (l, i, 0))
    return pl.pallas_call(
        body,
        out_shape=[jax.ShapeDtypeStruct((nl, r, c), F32)] * 4,
        grid=(nl, r // tr),
        in_specs=[pl.BlockSpec((n, None, tr, c), lambda l, i: (0, l, i, 0)), blk, blk, blk],
        out_specs=[blk] * 4,
        compiler_params=_cparams(("parallel", "parallel")),
        name=name,
    )(parts, w, m, v)


def _flat_index(px, py, pc):
    return 4 * px + 2 * py + pc


def _all_gather(arrs, name, layers=None):
    n = len(arrs)

    def body(*refs):
        start, forward, finish = _gather_phases(refs[:n], refs[n:2 * n], *refs[2 * n:], layers=layers)
        start()
        forward()
        finish()

    return pl.pallas_call(
        body,
        out_shape=_gather_shapes(arrs, layers),
        in_specs=[ANY_SPEC] * n,
        out_specs=[ANY_SPEC] * n,
        scratch_shapes=_comm_sems(n),
        name=name,
    )(*arrs)


ANY_SPEC = pl.BlockSpec(memory_space=pl.ANY)


def _comm_sems(n):
    return [pltpu.SemaphoreType.DMA((n, N_DEV - 1)), pltpu.SemaphoreType.DMA((n, N_DEV - 1)),
            pltpu.SemaphoreType.DMA((n,))]


def _gather_shapes(arrs, layers=None):
    layers = layers or [None] * len(arrs)
    return [jax.ShapeDtypeStruct((N_DEV,) + (a.shape if lay is None else a.shape[1:]), a.dtype)
            for a, lay in zip(arrs, layers)]


def _gather_phases(ins, outs, send_sems, recv_sems, local_sems, layers=None):
    n = len(ins)
    if layers is not None:
        ins = [r if lay is None else r.at[lay] for r, lay in zip(ins, layers)]
    x, y, c = lax.axis_index("x"), lax.axis_index("y"), lax.axis_index("c")
    me, sibling = (x, y, c), (x, y, 1 - c)
    chips = [(1 - x, y), (x, 1 - y), (1 - x, 1 - y)]

    def copy(a, k, block, to, src=None):
        slot = outs[a].at[_flat_index(*block)]
        return pltpu.make_async_remote_copy(
            src_ref=slot if src is None else src, dst_ref=slot,
            send_sem=send_sems.at[a, k], recv_sem=recv_sems.at[a, k],
            device_id=to, device_id_type=MESH)

    def mine(a):
        return pltpu.make_async_copy(ins[a], outs[a].at[_flat_index(*me)], local_sems.at[a])

    def first(a):
        return [copy(a, 0, me, sibling, src=ins[a])] + [
            copy(a, 1 + j, me, (*chip, c), src=ins[a]) for j, chip in enumerate(chips)]

    def start():
        for a in range(n):
            mine(a).start()
            for cp in first(a):
                cp.start()

    def forward():
        for j, chip in enumerate(chips):
            for a in range(n):
                copy(a, 1 + j, (*chip, c), me).wait_recv()
                copy(a, 4 + j, (*chip, c), sibling).start()

    def finish():
        for a in range(n):
            copy(a, 0, sibling, me).wait_recv()
            for j, chip in enumerate(chips):
                copy(a, 4 + j, (*chip, 1 - c), me).wait_recv()
        for a in range(n):
            for cp in first(a):
                cp.wait_send()
            for j, chip in enumerate(chips):
                copy(a, 4 + j, (*chip, c), sibling).wait_send()
            mine(a).wait()

    return start, forward, finish


def _scatter_blocks(arrs, name):
    n = len(arrs)

    def body(*refs):
        start, finish = _scatter_phases(refs[:n], refs[n:2 * n], *refs[2 * n:])
        start()
        finish()

    return pl.pallas_call(
        body,
        out_shape=[jax.ShapeDtypeStruct(a.shape, a.dtype) for a in arrs],
        in_specs=[ANY_SPEC] * n,
        out_specs=[ANY_SPEC] * n,
        scratch_shapes=_comm_sems(n),
        name=name,
    )(*arrs)


def _scatter_phases(ins, outs, send_sems, recv_sems, local_sems):
    n = len(ins)
    x, y, c = lax.axis_index("x"), lax.axis_index("y"), lax.axis_index("c")
    me = _flat_index(x, y, c)

    def peer(k):
        return (1 - x if k & 4 else x, 1 - y if k & 2 else y, 1 - c if k & 1 else c)

    def copy(a, k):
        p = peer(k)
        return pltpu.make_async_remote_copy(
            src_ref=ins[a].at[_flat_index(*p)], dst_ref=outs[a].at[me],
            send_sem=send_sems.at[a, k - 1], recv_sem=recv_sems.at[a, k - 1],
            device_id=p, device_id_type=MESH)

    def landed(a, k):
        slot = outs[a].at[_flat_index(*peer(k))]
        return pltpu.make_async_remote_copy(
            src_ref=slot, dst_ref=slot, send_sem=send_sems.at[a, k - 1], recv_sem=recv_sems.at[a, k - 1],
            device_id=peer(k), device_id_type=MESH)

    def mine(a):
        return pltpu.make_async_copy(ins[a].at[me], outs[a].at[me], local_sems.at[a])

    def start():
        for a in range(n):
            mine(a).start()
            for k in range(1, N_DEV):
                copy(a, k).start()

    def finish():
        for a in range(n):
            for k in range(1, N_DEV):
                landed(a, k).wait_recv()
        for a in range(n):
            for k in range(1, N_DEV):
                copy(a, k).wait_send()
            mine(a).wait()

    return start, finish


_BIG = ("w_in", "w_br_att", "w_br_ssm", "w_mix_out", "w_xq", "w_xkv", "w_xo", "w_gu", "w_down")
_COL_SHARDED = ("w_in", "w_xkv", "w_gu", "conv_w")
_SMALL = ("g_pre_mix", "conv_b", "dt_bias", "a_log", "d_skip", "g_ssm_norm", "g_post_mix", "g_pre_xa",
          "g_mem", "g_post_xa", "g_pre_ffn", "g_post_ffn")
_WEIGHTS = ("g_pre_mix", "w_in", "conv_w", "conv_b", "dt_bias", "a_log", "d_skip", "g_ssm_norm", "w_br_att",
            "w_br_ssm", "w_mix_out", "g_post_mix", "g_pre_xa", "g_mem", "w_xq", "w_xkv", "w_xo", "g_post_xa",
            "g_pre_ffn", "w_gu", "w_down", "g_post_ffn")
PACK_W = 8 * LANES


def _unshard(g, col):
    n, r, c = g.shape
    if col:
        return jnp.transpose(g, (1, 0, 2)).reshape(r, n * c)
    return g.reshape(n * r, c)


def _shard(w, col):
    r, c = w.shape
    if col:
        return jnp.transpose(w.reshape(r, N_DEV, c // N_DEV), (1, 0, 2))
    return w.reshape(N_DEV, r // N_DEV, c)


def _permute_in(w, cfg):
    parts, off = [], 0
    for size in cfg.in_sizes:
        parts.append(w[..., off:off + size])
        off += size
    q, k, v, z, xbc, dt, ga, gs = parts
    pad = jnp.zeros(w.shape[:-1] + (LANES - cfg.heads,), w.dtype)
    return jnp.concatenate([z, ga, gs, q, k, v, xbc, dt, pad], axis=-1)


def _unpermute_in(w, cfg):
    c = cfg
    sl = lambda a, n: w[..., a:a + n]
    return jnp.concatenate([sl(c.q0, c.sbw), sl(c.k0, c.sbw), sl(c.v0, c.sbw), sl(c.z0, c.inner),
                            sl(c.xbc0, c.conv_dim), sl(c.dt0, c.heads), sl(c.ga0, c.d), sl(c.gs0, c.d)], axis=-1)


def _pack(arrs):
    flat = jnp.concatenate([a.reshape(-1).astype(F32) for a in arrs])
    rows = -(-flat.shape[0] // PACK_W)
    rows = -(-rows // 8) * 8
    return jnp.pad(flat, (0, rows * PACK_W - flat.shape[0])).reshape(rows, PACK_W)


def _unpack(p, shapes):
    flat = p.reshape(-1)
    out, off = [], 0
    for s in shapes:
        size = math.prod(s)
        out.append(flat[off:off + size].reshape(s))
        off += size
    return out


def _vec3(a, width=None):
    if width is not None and a.shape[1] < width:
        a = jnp.pad(a, ((0, 0), (0, width - a.shape[1])))
    return a[:, None, :]


def _forward_layer(l, xin, h1, memf, tgt, w, p, cfg, nb, last, gather, n_late=0, complete=None):
    t = xin.shape[0]
    d = cfg.d
    s = {"x_in": xin, "h1": h1}
    proj = _mm(h1, w["w_in"], name="mm_proj")
    s["proj"] = proj
    s["o_att"], gathered = _sb_fwd(proj, cfg, nb, *gather)
    if n_late:
        w.update(complete(gathered[:n_late]))
        gathered = gathered[n_late:]
    s["xbc"] = _conv_fwd(proj, w["conv_w"], p["conv_b"], l, cfg, nb)
    s["y"], s["states"] = _ssd_fwd(s["xbc"], proj, p["dt_bias"], p["a_log"], p["d_skip_x"], l, cfg, nb)
    s["o_ssm"] = _rowwise(_f_gate_norm, "gate_norm_fwd", t, [_full(s["y"]), (proj, cfg.inner, 0)],
                          [(p["g_ssm_norm"], l)], [(cfg.inner, ACT_DTYPE)])[0]
    s["ba"] = _mm(s["o_att"], w["w_br_att"], name="mm_br_att")
    s["bs"] = _mm(s["o_ssm"], w["w_br_ssm"], name="mm_br_ssm")
    s["merged"] = _rowwise(_f_merge, "merge_fwd", t,
                           [(proj, d, cfg.ga0 // d), (proj, d, cfg.gs0 // d), _full(s["ba"]), _full(s["bs"])],
                           [], [(d, ACT_DTYPE)])[0]
    s["mo"] = _mm(s["merged"], w["w_mix_out"], name="mm_mix_out")
    s["x1"], s["h2"] = _rowwise(_f_post_pre, "post_pre_mix", t, [_full(xin), _full(s["mo"])],
                                [(p["g_post_mix"], l), (p["g_pre_xa"], l)], [(d, F32), (d, ACT_DTYPE)])
    s["mem_n"] = _rowwise(_rms, "mem_norm", memf.shape[0], [_full(memf)], [(p["g_mem"], l)], [(d, ACT_DTYPE)])[0]
    s["q"] = _mm(s["h2"], w["w_xq"], name="mm_xq")
    s["kv"] = _mm(s["mem_n"], w["w_xkv"], name="mm_xkv")
    s["o_xa"] = _xa_fwd(s["q"], s["kv"], cfg, nb)
    s["xo"] = _mm(s["o_xa"], w["w_xo"], name="mm_xo")
    s["x2"], s["h3"] = _rowwise(_f_post_pre, "post_pre_xa", t, [_full(s["x1"]), _full(s["xo"])],
                                [(p["g_post_xa"], l), (p["g_pre_ffn"], l)], [(d, F32), (d, ACT_DTYPE)])
    s["gu"] = _mm(s["h3"], w["w_gu"], name="mm_gu")
    s["act"] = _rowwise(_f_swiglu, "swiglu_fwd", t, [(s["gu"], cfg.ffn, 0), (s["gu"], cfg.ffn, 1)], [],
                        [(cfg.ffn, ACT_DTYPE)])[0]
    s["dn"] = _mm(s["act"], w["w_down"], name="mm_down")
    if last:
        nxt = _rowwise(_f_final, "final_loss", t, [_full(s["x2"]), _full(s["dn"]), _full(tgt)],
                       [(p["g_post_ffn"], l)], [(d, F32)], acc_out=[(1, d)])
    else:
        nxt = _rowwise(_f_post_pre, "post_pre_ffn", t, [_full(s["x2"]), _full(s["dn"])],
                       [(p["g_post_ffn"], l), (p["g_pre_mix"], l + 1)], [(d, F32), (d, ACT_DTYPE)])
    return s, nxt, gathered


def _backward_layer(l, s, dx, d_dn, memf, w, p, cfg, nb, prev_dn, scatter):
    t = dx.shape[0]
    d = cfg.d
    g = {}
    dact = _mm(d_dn, w["w_down"], tb=True, out_dtype=ACT_DTYPE, name="mm_d_act", tn=1408)
    g["w_down"] = _mm(s["act"], d_dn, ta=True, name="mm_dw_down")
    dgu = _rowwise(_f_swiglu_bwd, "swiglu_bwd", t, [(s["gu"], cfg.ffn, 0), (s["gu"], cfg.ffn, 1), _full(dact)], [],
                   [(2 * cfg.ffn, ACT_DTYPE)])[0]
    dh3 = _mm(dgu, w["w_gu"], tb=True, name="mm_d_h3", tk=1408)
    g["w_gu"] = _mm(s["h3"], dgu, ta=True, name="mm_dw_gu")
    dx2, d_xo, g["g_pre_ffn"], g["g_post_xa"] = _rowwise(
        _f_pre_post_bwd, "pre_post_bwd_ffn", t, [_full(s["x2"]), _full(dh3), _full(dx), _full(s["xo"])],
        [(p["g_pre_ffn"], l), (p["g_post_xa"], l)], [(d, F32), (d, ACT_DTYPE)], acc_out=[(1, d), (1, d)])
    do_xa = _mm(d_xo, w["w_xo"], tb=True, out_dtype=ACT_DTYPE, name="mm_d_oxa")
    g["w_xo"] = _mm(s["o_xa"], d_xo, ta=True, name="mm_dw_xo")
    dq, dkv = _xa_bwd(s["q"], s["kv"], do_xa, cfg, nb)
    dh2 = _mm(dq, w["w_xq"], tb=True, name="mm_d_h2")
    g["w_xq"] = _mm(s["h2"], dq, ta=True, name="mm_dw_xq")
    dmem_n = _mm(dkv, w["w_xkv"], tb=True, name="mm_d_mem")
    g["w_xkv"] = _mm(s["mem_n"], dkv, ta=True, name="mm_dw_xkv")
    g["g_mem"] = _rowwise(_f_gain_bwd, "mem_norm_bwd", memf.shape[0], [_full(memf), _full(dmem_n)],
                          [(p["g_mem"], l)], [], acc_out=[(1, d)])[0]
    dx1, d_mo, g["g_pre_xa"], g["g_post_mix"] = _rowwise(
        _f_pre_post_bwd, "pre_post_bwd_xa", t, [_full(s["x1"]), _full(dh2), _full(dx2), _full(s["mo"])],
        [(p["g_pre_xa"], l), (p["g_post_mix"], l)], [(d, F32), (d, ACT_DTYPE)], acc_out=[(1, d), (1, d)])
    dmerged = _mm(d_mo, w["w_mix_out"], tb=True, out_dtype=ACT_DTYPE, name="mm_d_merged")
    g["w_mix_out"] = _mm(s["merged"], d_mo, ta=True, name="mm_dw_mix_out")
    proj = s["proj"]
    dgg, dba, dbs = _rowwise(
        _f_merge_bwd, "merge_bwd", t,
        [(proj, d, cfg.ga0 // d), (proj, d, cfg.gs0 // d), _full(s["ba"]), _full(s["bs"]), _full(dmerged)], [],
        [(2 * d, ACT_DTYPE), (d, ACT_DTYPE), (d, ACT_DTYPE)])
    do_att = _mm(dba, w["w_br_att"], tb=True, name="mm_d_oatt")
    g["w_br_att"] = _mm(s["o_att"], dba, ta=True, name="mm_dw_br_att")
    do_ssm = _mm(dbs, w["w_br_ssm"], tb=True, out_dtype=ACT_DTYPE, name="mm_d_ossm")
    g["w_br_ssm"] = _mm(s["o_ssm"], dbs, ta=True, name="mm_dw_br_ssm")
    dy, dz, g["g_ssm_norm"] = _rowwise(
        _f_gate_norm_bwd, "gate_norm_bwd", t, [_full(s["y"]), (proj, cfg.inner, 0), _full(do_ssm)],
        [(p["g_ssm_norm"], l)], [(cfg.inner, F32), (cfg.inner, ACT_DTYPE)], acc_out=[(1, cfg.inner)])
    dxbc, ddt_raw, dvec = _ssd_bwd(s["xbc"], proj, s["states"], dy, p["dt_bias"], p["a_log"], p["d_skip_x"], l, cfg, nb)
    g["dt_bias"], g["a_log"], g["d_skip"] = (dvec[i:i + 1, :cfg.heads] for i in range(3))
    dxbc_raw, g["conv_w"], g["conv_b"] = _conv_bwd(proj, dxbc, w["conv_w"], p["conv_b"], l, cfg, nb)
    own = [_shard(g[n], n in _COL_SHARDED).astype(WIRE_DTYPE) for n in _BIG if n != "w_in"]
    dq_sb, dk_sb, dv_sb, landed = _sb_bwd(proj, do_att, cfg, nb, list(scatter) + own)
    landed = (landed[:len(scatter)], landed[len(scatter):])
    dproj = jnp.concatenate([dz, dgg, dq_sb, dk_sb.astype(ACT_DTYPE), dv_sb.astype(ACT_DTYPE), dxbc_raw, ddt_raw], axis=1)
    dh1 = _mm(dproj, w["w_in"], tb=True, name="mm_d_h1", tk=1152)
    g["w_in"] = _mm(s["h1"], dproj, ta=True, name="mm_dw_in")
    if prev_dn is None:
        dx0, g["g_pre_mix"] = _rowwise(_f_pre_bwd, "pre_bwd_first", t, [_full(s["x_in"]), _full(dh1), _full(dx1)],
                                       [(p["g_pre_mix"], l)], [(d, F32)], acc_out=[(1, d)])
        return g, dx0, None, None, landed
    dx0, d_dn_prev, g["g_pre_mix"], g_post_prev = _rowwise(
        _f_pre_post_bwd, "pre_post_bwd_mix", t, [_full(s["x_in"]), _full(dh1), _full(dx1), _full(prev_dn)],
        [(p["g_pre_mix"], l), (p["g_post_ffn"], l - 1)], [(d, F32), (d, ACT_DTYPE)], acc_out=[(1, d), (1, d)])
    return g, dx0, d_dn_prev, g_post_prev, landed


def kernel(x, mem, g_pre_mix, w_in, conv_w, conv_b, dt_bias, a_log, d_skip, g_ssm_norm, w_br_att, w_br_ssm, w_mix_out, g_post_mix, g_pre_xa, g_mem, w_xq, w_xkv, w_xo, g_post_xa, g_pre_ffn, w_gu, w_down, g_post_ffn, loss_target, m_g_pre_mix, m_w_in, m_conv_w, m_conv_b, m_dt_bias, m_a_log, m_d_skip, m_g_ssm_norm, m_w_br_att, m_w_br_ssm, m_w_mix_out, m_g_post_mix, m_g_pre_xa, m_g_mem, m_w_xq, m_w_xkv, m_w_xo, m_g_post_xa, m_g_pre_ffn, m_w_gu, m_w_down, m_g_post_ffn, v_g_pre_mix, v_w_in, v_conv_w, v_conv_b, v_dt_bias, v_a_log, v_d_skip, v_g_ssm_norm, v_w_br_att, v_w_br_ssm, v_w_mix_out, v_g_post_mix, v_g_pre_xa, v_g_mem, v_w_xq, v_w_xkv, v_w_xo, v_g_post_xa, v_g_pre_ffn, v_w_gu, v_w_down, v_g_post_ffn):
    vals = dict(locals())
    cfg = _Cfg()
    nb = x.shape[0]
    t = nb * SEQ
    d = cfg.d
    depth = g_pre_mix.shape[0]

    gathered_names = _BIG + ("conv_w",)
    late_names = gathered_names[1:]

    on_wire = {n: conv_w if n == "conv_w" else vals[n].astype(WIRE_DTYPE) for n in gathered_names}

    def wire(l, names=gathered_names):
        return [on_wire[n] for n in names], [l] * len(names)

    def layer_weights(gathered, names=gathered_names):
        w = {n: _unshard(gw, n in _COL_SHARDED) for n, gw in zip(names, gathered)}
        if "w_in" in w:
            w["w_in"] = _permute_in(w["w_in"], cfg)
        if "conv_w" in w:
            w["conv_w"] = w["conv_w"][None]
        return w

    p = {n: _vec3(vals[n], LANES if n in ("dt_bias", "a_log", "d_skip") else None) for n in _SMALL}
    p["d_skip_x"] = _vec3(jnp.repeat(d_skip, SSM_HEAD_DIM, axis=1))
    weights = [None] * depth
    first_arrs, first_layers = wire(0, ("w_in",))
    weights[0] = layer_weights(_all_gather(first_arrs, "ag_weights_first", first_layers), ("w_in",))

    xf = x.reshape(t, d)
    memf = mem.reshape(nb * MEM_LEN, d)
    tgt = loss_target.reshape(t, d)
    h = _rowwise(_rms, "pre_norm_first", t, [_full(xf)], [(p["g_pre_mix"], 0)], [(d, ACT_DTYPE)])[0]
    saved = []
    xcur = xf
    for l in range(depth):
        last = l == depth - 1
        late = wire(0, late_names) if l == 0 else ([], [])
        ahead = ([], []) if last else wire(l + 1)
        complete = (lambda got: layer_weights(got, late_names)) if l == 0 else None
        s, nxt, gathered = _forward_layer(l, xcur, h, memf, tgt, weights[l], p, cfg, nb, last,
                                          (late[0] + ahead[0], late[1] + ahead[1]), len(late[0]), complete)
        saved.append(s)
        if not last:
            weights[l + 1] = layer_weights(gathered)
            xcur, h = nxt
    dx, loss_row = nxt
    loss = lax.psum(0.5 * jnp.sum(loss_row) / d, AXES)

    top = saved[-1]
    d_dn, g_post_top = _rowwise(lambda ysub, dxo, gp: _rms_bwd(ysub, gp, dxo), "post_bwd_last", t,
                                [_full(top["dn"]), _full(dx)], [(p["g_post_ffn"], depth - 1)], [(d, ACT_DTYPE)],
                                acc_out=[(1, d)])
    grads = [None] * depth
    landed = [dict() for _ in range(depth)]
    post_ffn = [None] * depth
    post_ffn[depth - 1] = g_post_top
    pending = []
    for l in reversed(range(depth)):
        prev_dn = saved[l - 1]["dn"] if l > 0 else None
        grads[l], dx, d_dn, g_post_prev, (got_above, got_own) = _backward_layer(
            l, saved[l], dx, d_dn, memf, weights[l], p, cfg, nb, prev_dn, pending)
        if pending:
            landed[l + 1]["w_in"] = got_above[0]
        landed[l].update(zip([n for n in _BIG if n != "w_in"], got_own))
        pending = [_shard(_unpermute_in(grads[l]["w_in"], cfg), True).astype(WIRE_DTYPE)]
        if l > 0:
            post_ffn[l - 1] = g_post_prev
    landed[0]["w_in"] = _scatter_blocks(pending, "scatter_grads_last")[0]
    for l in range(depth):
        grads[l]["g_post_ffn"] = post_ffn[l]
    grad_x = dx.reshape(x.shape)

    out = {}
    for n in _BIG:
        parts = jnp.stack([landed[l][n] for l in range(depth)], axis=1)
        out[n] = _adamw(parts, vals[n], vals["m_" + n], vals["v_" + n], "adamw_" + n)

    small_shapes = [vals[n].shape for n in _SMALL]
    pack_g = _pack([grads[l][n] for n in _SMALL for l in range(depth)])
    conv_g = jnp.concatenate([grads[l]["conv_w"] for l in range(depth)], axis=0)
    parts_small, parts_conv = _all_gather([pack_g, conv_g], "ag_small_grads")
    packed = lambda prefix: _pack([vals[prefix + n] for n in _SMALL])[None]
    res = _adamw(parts_small[:, None], packed(""), packed("m_"), packed("v_"), "adamw_small")
    unpacked = [_unpack(r, small_shapes) for r in res]
    for i, n in enumerate(_SMALL):
        out[n] = [unpacked[j][i] for j in range(4)]
    cs = conv_w.shape[2]
    me = _flat_index(lax.axis_index("x"), lax.axis_index("y"), lax.axis_index("c"))
    parts_conv = lax.dynamic_slice_in_dim(parts_conv, me * cs, cs, axis=2)
    flat = lambda a: a.reshape(1, depth * SSM_CONV, cs)
    res = _adamw(parts_conv[:, None], flat(conv_w), flat(m_conv_w), flat(v_conv_w), "adamw_conv_w")
    out["conv_w"] = [r.reshape(conv_w.shape) for r in res]

    return (loss, grad_x, *[out[n][0] for n in _WEIGHTS], *[out[n][1] for n in _WEIGHTS],
            *[out[n][2] for n in _WEIGHTS], *[out[n][3] for n in _WEIGHTS])
```

```python
import functools
import math

import jax
import jax.numpy as jnp
from jax import lax
from jax.experimental import pallas as pl
from jax.experimental.pallas import tpu as pltpu

F32 = jnp.float32
BF16 = jnp.bfloat16
MXU_DTYPE = BF16
ACT_DTYPE = BF16
WIRE_DTYPE = BF16

D_MODEL = 1024
SEQ = 2048
DEPTH = 4
MEM_LEN = 256
RMS_EPS = 1e-6
SB_HEADS = 16
SB_HEAD_DIM = 64
SB_BLOCK = 128
SSM_INNER = 2 * D_MODEL
SSM_HEAD_DIM = 64
SSM_GROUPS = 4
SSM_STATE = 128
SSM_CONV = 4
SSM_CHUNK = 128
XA_HEADS = 4
FFN_HIDDEN = ((8 * D_MODEL + 767) // 768) * 256
ADAM_LR = 0.001
ADAM_B1 = 0.9
ADAM_B2 = 0.999
ADAM_EPS = 1e-08
ADAM_WD = 0.01
ADAM_STEP = 10

N_DEV = 8
LANES = 128
VMEM_LIMIT_BYTES = 56 * 1024 * 1024

AXES = ("x", "y", "c")
MESH = pl.DeviceIdType.MESH


class _Cfg:
    def __init__(self):
        self.d = D_MODEL
        self.sbw = SB_HEADS * SB_HEAD_DIM
        self.inner = SSM_INNER
        self.heads = SSM_INNER // SSM_HEAD_DIM
        self.epg = self.heads // SSM_GROUPS
        self.gn = SSM_GROUPS * SSM_STATE
        self.conv_dim = SSM_INNER + 2 * self.gn
        self.ffn = FFN_HIDDEN
        self.xa_dim = D_MODEL // XA_HEADS
        self.in_sizes = (self.sbw, self.sbw, self.sbw, self.inner, self.conv_dim, self.heads, self.d, self.d)
        self.in_width = sum(self.in_sizes)
        self.z0 = 0
        self.ga0 = self.inner
        self.gs0 = self.ga0 + self.d
        self.q0 = self.gs0 + self.d
        self.k0 = self.q0 + self.sbw
        self.v0 = self.k0 + self.sbw
        self.xbc0 = self.v0 + self.sbw
        self.dt0 = self.xbc0 + self.conv_dim
        self.proj_w = self.dt0 + LANES
        assert self.heads <= LANES


def _cparams(sem=None):
    return pltpu.CompilerParams(dimension_semantics=sem, vmem_limit_bytes=VMEM_LIMIT_BYTES)


def _tile(n, pref, mult):
    if n <= pref:
        return n
    t = (pref // mult) * mult
    while t >= mult:
        if n % t == 0:
            return t
        t -= mult
    return n


MM_VMEM_BUDGET = 40 * 1024 * 1024


def _mm(a, b, *, ta=False, tb=False, out_dtype=F32, name, tm=1024, tn=1152, tk=2048):
    kk, m = (a.shape if ta else a.shape[::-1])
    if tb:
        n, k2 = b.shape
    else:
        k2, n = b.shape
    assert kk == k2, (name, a.shape, b.shape)
    size = lambda dt: jnp.dtype(dt).itemsize
    tn = _tile(n, tn, LANES)
    tk = _tile(kk, tk, LANES if (tb or not ta) else 16)
    nk = kk // tk
    while True:
        tm_ = _tile(m, tm, LANES if ta else 16)
        need = (2 * (tm_ * tk * size(a.dtype) + tk * tn * size(b.dtype) + tm_ * tn * size(out_dtype))
                + (tm_ * tk + tk * tn) * size(MXU_DTYPE) + tm_ * tn * 4 * (2 if nk > 1 else 1))
        if need <= MM_VMEM_BUDGET or tm <= 128:
            break
        tm //= 2
    tm = tm_
    gi, gj = m // tm, n // tn
    j_outer = nk == 1 and b.size * size(b.dtype) * (gi - 1) > a.size * size(a.dtype) * (gj - 1)
    dims = (((0 if ta else 1,), (1 if tb else 0,)), ((), ()))

    def ij(g0, g1):
        return (g1, g0) if j_outer else (g0, g1)

    def body(a_ref, b_ref, o_ref, *scratch):
        av = a_ref[...].astype(MXU_DTYPE)
        bv = b_ref[...].astype(MXU_DTYPE)
        part = lax.dot_general(av, bv, dims, preferred_element_type=F32)
        if nk == 1:
            o_ref[...] = part.astype(out_dtype)
        else:
            acc_ref, = scratch
            k = pl.program_id(2)

            @pl.when(k == 0)
            def _():
                acc_ref[...] = part

            @pl.when(k > 0)
            def _():
                acc_ref[...] += part

            @pl.when(k == nk - 1)
            def _():
                o_ref[...] = acc_ref[...].astype(out_dtype)

    if ta:
        a_spec = pl.BlockSpec((tk, tm), lambda g0, g1, k: (k, ij(g0, g1)[0]))
    else:
        a_spec = pl.BlockSpec((tm, tk), lambda g0, g1, k: (ij(g0, g1)[0], k))
    if tb:
        b_spec = pl.BlockSpec((tn, tk), lambda g0, g1, k: (ij(g0, g1)[1], k))
    else:
        b_spec = pl.BlockSpec((tk, tn), lambda g0, g1, k: (k, ij(g0, g1)[1]))
    return pl.pallas_call(
        body,
        out_shape=jax.ShapeDtypeStruct((m, n), out_dtype),
        grid=(gj, gi, nk) if j_outer else (gi, gj, nk),
        in_specs=[a_spec, b_spec],
        out_specs=pl.BlockSpec((tm, tn), lambda g0, g1, k: ij(g0, g1)),
        scratch_shapes=[] if nk == 1 else [pltpu.VMEM((tm, tn), F32)],
        compiler_params=_cparams(("parallel", "parallel", "arbitrary")),
        name=name,
    )(a, b)


def _rowwise(fn, name, rows, row_in, vec_in, row_out, acc_out=(), tr=256):
    tr = _tile(rows, tr, 16)
    n_in = len(row_in) + len(vec_in)
    n_ro = len(row_out)

    def body(*refs):
        ins = [r[...].astype(F32) for r in refs[:n_in]]
        outs = fn(*ins)
        if not isinstance(outs, (tuple, list)):
            outs = (outs,)
        out_refs = refs[n_in:]
        for o_ref, val in zip(out_refs[:n_ro], outs[:n_ro]):
            o_ref[...] = val.astype(o_ref.dtype)
        if acc_out:
            i = pl.program_id(0)
            for o_ref, val in zip(out_refs[n_ro:], outs[n_ro:]):
                @pl.when(i == 0)
                def _(o_ref=o_ref, val=val):
                    o_ref[...] = val

                @pl.when(i > 0)
                def _(o_ref=o_ref, val=val):
                    o_ref[...] += val

    in_specs = [pl.BlockSpec((tr, w), functools.partial(lambda i, cb: (i, cb), cb=cb)) for (_, w, cb) in row_in]
    in_specs += [pl.BlockSpec((None,) + v.shape[1:], functools.partial(lambda i, l: (l, 0, 0), l=l)) for (v, l) in vec_in]
    out_shape = [jax.ShapeDtypeStruct((rows, w), dt) for (w, dt) in row_out]
    out_shape += [jax.ShapeDtypeStruct(s, F32) for s in acc_out]
    out_specs = [pl.BlockSpec((tr, w), lambda i: (i, 0)) for (w, _) in row_out]
    out_specs += [pl.BlockSpec(s, lambda i: (0, 0)) for s in acc_out]
    res = pl.pallas_call(
        body,
        out_shape=out_shape,
        grid=(rows // tr,),
        in_specs=in_specs,
        out_specs=out_specs,
        compiler_params=_cparams(("arbitrary",) if acc_out else ("parallel",)),
        name=name,
    )(*[a for (a, _, _) in row_in], *[v for (v, _) in vec_in])
    return res


def _rms(x, g):
    r = lax.rsqrt(jnp.mean(x * x, axis=-1, keepdims=True) + RMS_EPS)
    return x * r * g


def _rms_bwd(x, g, dy):
    r = lax.rsqrt(jnp.mean(x * x, axis=-1, keepdims=True) + RMS_EPS)
    xh = x * r
    dxh = dy * g
    dx = r * (dxh - xh * jnp.mean(dxh * xh, axis=-1, keepdims=True))
    return dx, jnp.sum(dy * xh, axis=0, keepdims=True)


def _silu(x):
    return x * jax.nn.sigmoid(x)


def _silu_grad(x):
    s = jax.nn.sigmoid(x)
    return s * (1.0 + x * (1.0 - s))


def _softplus(x):
    return jnp.maximum(x, 0.0) + jnp.log1p(jnp.exp(-jnp.abs(x)))


def _full(a):
    return (a, a.shape[1], 0)


def _f_post_pre(x, ysub, g_post, g_pre):
    xn = x + _rms(ysub, g_post)
    return xn, _rms(xn, g_pre)


def _f_final(x, ysub, tgt, g_post):
    err = x + _rms(ysub, g_post) - tgt
    return err * (1.0 / D_MODEL), jnp.sum(err * err, axis=0, keepdims=True)


def _f_pre_post_bwd(xmid, dh, dxo, ysub, g_pre, g_post):
    d1, dg_pre = _rms_bwd(xmid, g_pre, dh)
    dxm = dxo + d1
    dys, dg_post = _rms_bwd(ysub, g_post, dxm)
    return dxm, dys, dg_pre, dg_post


def _f_pre_bwd(x, dh, dxo, g_pre):
    d1, dg_pre = _rms_bwd(x, g_pre, dh)
    return dxo + d1, dg_pre


def _f_gain_bwd(x, dy, g):
    return _rms_bwd(x, g, dy)[1]


def _group_norm_parts(u):
    gw = u.shape[1] // SSM_GROUPS
    parts = []
    for gi in range(SSM_GROUPS):
        ug = u[:, gi * gw:(gi + 1) * gw]
        r = lax.rsqrt(jnp.mean(ug * ug, axis=-1, keepdims=True) + RMS_EPS)
        parts.append((ug * r, r))
    return gw, parts


def _f_gate_norm(y, z, g):
    _, parts = _group_norm_parts(y * _silu(z))
    return jnp.concatenate([uh for uh, _ in parts], axis=1) * g


def _f_gate_norm_bwd(y, z, do, g):
    sz = _silu(z)
    gw, parts = _group_norm_parts(y * sz)
    dxh = do * g
    du = []
    for gi, (uh, r) in enumerate(parts):
        dg_ = dxh[:, gi * gw:(gi + 1) * gw]
        du.append(r * (dg_ - uh * jnp.mean(dg_ * uh, axis=-1, keepdims=True)))
    du = jnp.concatenate(du, axis=1)
    uh_all = jnp.concatenate([uh for uh, _ in parts], axis=1)
    return du * sz, du * y * _silu_grad(z), jnp.sum(do * uh_all, axis=0, keepdims=True)


def _f_merge(ga, gs, ba, bs):
    return jax.nn.sigmoid(ga) * ba + jax.nn.sigmoid(gs) * bs


def _f_merge_bwd(ga, gs, ba, bs, dm):
    sa, ss = jax.nn.sigmoid(ga), jax.nn.sigmoid(gs)
    dgg = jnp.concatenate([dm * ba * sa * (1.0 - sa), dm * bs * ss * (1.0 - ss)], axis=1)
    return dgg, dm * sa, dm * ss


def _f_swiglu(gate, up):
    return _silu(gate) * up


def _f_swiglu_bwd(gate, up, da):
    return jnp.concatenate([da * up * _silu_grad(gate), da * _silu(gate)], axis=1)


def _split_dot(x, u):
    hi = x.astype(BF16)
    lo = (x - hi.astype(F32)).astype(BF16)
    return jnp.dot(hi, u, preferred_element_type=F32) + jnp.dot(lo, u, preferred_element_type=F32)


SB_ROWS = 512
SB_UNROLL = 4
C00 = (((0,), (0,)), ((), ()))
C11 = (((1,), (1,)), ((), ()))


def _sb_setup(q_ref, tq):
    hp = LANES // SB_HEAD_DIM
    lane = lax.broadcasted_iota(jnp.int32, (1, LANES), 1)
    heads = [jnp.logical_and(lane >= h * SB_HEAD_DIM, lane < (h + 1) * SB_HEAD_DIM) for h in range(hp)]
    qs = q_ref[...] * (SB_HEAD_DIM ** -0.5)
    q_h = [jnp.where(hm, qs, 0.0).astype(MXU_DTYPE) for hm in heads]
    row = lax.broadcasted_iota(jnp.int32, (tq, SB_BLOCK), 0)
    col = lax.broadcasted_iota(jnp.int32, (tq, SB_BLOCK), 1)
    sq_row = lax.broadcasted_iota(jnp.int32, (SB_BLOCK, SB_BLOCK), 0)
    sq_col = lax.broadcasted_iota(jnp.int32, (SB_BLOCK, SB_BLOCK), 1)
    return heads, q_h, col - row, sq_row, sq_col


def _sb_scores(q, kj, mask):
    z = lax.dot_general(q, kj, C11, preferred_element_type=F32)
    lm = -(jnp.maximum(z, 0.0) + jnp.log(1.0 + jnp.exp(-jnp.abs(z))))
    return z, lm if mask is None else jnp.where(mask, lm, 0.0)


def _add_rows(x, r0, upd):
    return x + upd if r0 == 0 else jnp.concatenate([x[:r0], x[r0:] + upd], axis=0)


def _grid_step3(nb, ncb, nq):
    return (pl.program_id(0) * ncb + pl.program_id(1)) * nq + pl.program_id(2), nb * ncb * nq


def _sb_fwd(proj, cfg, nb, gather=(), gather_layers=None):
    blk = SB_BLOCK
    tq = _tile(SEQ, SB_ROWS, blk)
    nq = SEQ // tq
    kpq = tq // blk
    unr = math.gcd(kpq, SB_UNROLL)
    hp = LANES // SB_HEAD_DIM
    ncb = cfg.sbw // LANES
    qb, kb, vb = cfg.q0 // LANES, cfg.k0 // LANES, cfg.v0 // LANES
    ng = len(gather)

    def body(q_ref, k_ref, v_ref, *rest):
        o_ref = rest[ng]
        if ng:
            step_id, n_steps = _grid_step3(nb, ncb, nq)
            start, forward, finish = _gather_phases(rest[:ng], rest[ng + 1:2 * ng + 1], *rest[2 * ng + 1:],
                                                    layers=gather_layers)
            pl.when(step_id == 0)(start)
            pl.when(step_id == (3 * n_steps) // 4)(forward)
        _sb_fwd_block(q_ref, k_ref, v_ref, o_ref)
        if ng:
            pl.when(step_id == n_steps - 1)(finish)

    def _sb_fwd_block(q_ref, k_ref, v_ref, o_ref):
        i = pl.program_id(2)
        heads, q_h, cmr, sq_row, sq_col = _sb_setup(q_ref, tq)
        u_rev = (sq_row >= sq_col).astype(BF16)

        def key_block(j, r0, acc, runs, diagonal):
            rows = pl.ds(pl.multiple_of(j * blk, blk), blk)
            kj = k_ref[rows, :].astype(MXU_DTYPE)
            vj = v_ref[rows, :].astype(MXU_DTYPE)
            mask = cmr[:tq - r0] < 0 if diagonal else None
            for h in range(hp):
                z, lm = _sb_scores(q_h[h][r0:], kj, mask)
                cs = _split_dot(lm, u_rev)
                w = jnp.exp(z + cs + runs[h][r0:])
                if diagonal:
                    w = jnp.where(mask, w, 0.0)
                upd = jnp.dot(w.astype(MXU_DTYPE), jnp.where(heads[h], vj, 0), preferred_element_type=F32)
                acc = _add_rows(acc, r0, upd)
                runs[h] = _add_rows(runs[h], r0, cs[:, 0:1])
            return acc

        acc = jnp.zeros((tq, LANES), F32)
        runs = [jnp.zeros((tq, 1), F32) for _ in range(hp)]
        for r in reversed(range(kpq)):
            acc = key_block(i * kpq + r, r * blk, acc, runs, True)

        def step(n, carry):
            acc, runs = carry
            runs = list(runs)
            for jj in range(unr):
                acc = key_block(i * kpq - 1 - (n * unr + jj), 0, acc, runs, False)
            return acc, tuple(runs)

        acc, _ = lax.fori_loop(0, i * (kpq // unr), step, (acc, tuple(runs)))
        o_ref[...] = acc

    res = pl.pallas_call(
        body,
        out_shape=[jax.ShapeDtypeStruct((nb * SEQ, cfg.sbw), F32)] + _gather_shapes(gather, gather_layers),
        grid=(nb, ncb, nq),
        in_specs=[
            pl.BlockSpec((tq, LANES), lambda b, c, i: (b * nq + i, qb + c)),
            pl.BlockSpec((SEQ, LANES), lambda b, c, i: (b, kb + c)),
            pl.BlockSpec((SEQ, LANES), lambda b, c, i: (b, vb + c)),
        ] + [ANY_SPEC] * ng,
        out_specs=[pl.BlockSpec((tq, LANES), lambda b, c, i: (b * nq + i, c))] + [ANY_SPEC] * ng,
        scratch_shapes=_comm_sems(ng) if ng else [],
        compiler_params=_cparams(("arbitrary",) * 3 if ng else ("parallel", "parallel", "arbitrary")),
        name="sb_fwd_gather" if ng else "sb_fwd",
    )(proj, proj, proj, *gather)
    return res[0], res[1:]


def _sb_bwd(proj, do_att, cfg, nb, scatter=()):
    blk = SB_BLOCK
    tq = _tile(SEQ, SB_ROWS, blk)
    nq = SEQ // tq
    kpq = tq // blk
    unr = math.gcd(kpq, SB_UNROLL)
    hp = LANES // SB_HEAD_DIM
    ncb = cfg.sbw // LANES
    scale = SB_HEAD_DIM ** -0.5
    qb, kb, vb = cfg.q0 // LANES, cfg.k0 // LANES, cfg.v0 // LANES
    ns = len(scatter)

    def body(q_ref, k_ref, v_ref, do_ref, *rest):
        dq_ref, dk_ref, dv_ref = rest[ns:ns + 3]
        g_ref, z_ref = rest[2 * ns + 3:2 * ns + 5]
        if ns:
            step_id, n_steps = _grid_step3(nb, ncb, nq)
            start, finish = _scatter_phases(rest[:ns], rest[ns + 3:2 * ns + 3], *rest[2 * ns + 5:])
            pl.when(step_id == 0)(start)
        _sb_bwd_block(q_ref, k_ref, v_ref, do_ref, dq_ref, dk_ref, dv_ref, g_ref, z_ref)
        if ns:
            pl.when(step_id == n_steps - 1)(finish)

    def _sb_bwd_block(q_ref, k_ref, v_ref, do_ref, dq_ref, dk_ref, dv_ref, g_ref, z_ref):
        i = pl.program_id(2)

        @pl.when(i == 0)
        def _():
            dk_ref[...] = jnp.zeros_like(dk_ref)
            dv_ref[...] = jnp.zeros_like(dv_ref)

        heads, q_h, cmr, sq_row, sq_col = _sb_setup(q_ref, tq)
        u_rev = (sq_row >= sq_col).astype(BF16)
        u_fwd = (sq_row <= sq_col).astype(BF16)
        do = do_ref[...]
        do_h = [jnp.where(hm, do, 0.0).astype(MXU_DTYPE) for hm in heads]


        def left_block(j, r0, runs, diagonal):
            rows = pl.ds(pl.multiple_of(j * blk, blk), blk)
            kj = k_ref[rows, :].astype(MXU_DTYPE)
            vj = v_ref[rows, :].astype(MXU_DTYPE)
            mask = cmr[:tq - r0] < 0 if diagonal else None
            dv = jnp.zeros((blk, LANES), F32)
            for h in range(hp):
                z, lm = _sb_scores(q_h[h][r0:], kj, mask)
                cs = _split_dot(lm, u_rev)
                a = jnp.exp(z + cs + runs[h][r0:])
                if diagonal:
                    a = jnp.where(mask, a, 0.0)
                da = lax.dot_general(do_h[h][r0:], vj, C11, preferred_element_type=F32)
                dv = dv + lax.dot_general(a.astype(MXU_DTYPE), do_h[h][r0:], C00, preferred_element_type=F32)
                g_ref[h, j, r0:, :] = a * da
                z_ref[h, j, r0:, :] = 1.0 - jnp.exp(lm)
                runs[h] = _add_rows(runs[h], r0, cs[:, 0:1])
            dv_ref[rows, :] += dv

        runs = [jnp.zeros((tq, 1), F32) for _ in range(hp)]
        for r in reversed(range(kpq)):
            left_block(i * kpq + r, r * blk, runs, True)

        def sweep_left(n, runs):
            runs = list(runs)
            for jj in range(unr):
                left_block(i * kpq - 1 - (n * unr + jj), 0, runs, False)
            return tuple(runs)

        trips = i * (kpq // unr)
        lax.fori_loop(0, trips, sweep_left, tuple(runs))

        def right_block(j, r0, dq, runs, diagonal):
            rows = pl.ds(pl.multiple_of(j * blk, blk), blk)
            kj = k_ref[rows, :].astype(MXU_DTYPE)
            dk = jnp.zeros((blk, LANES), F32)
            for h in range(hp):
                g = g_ref[h, j, r0:, :]
                g_upto = _split_dot(g, u_fwd) + runs[h][r0:]
                dz = g - z_ref[h, j, r0:, :] * g_upto
                if diagonal:
                    dz = jnp.where(cmr[:tq - r0] < 0, dz, 0.0)
                dz = dz.astype(MXU_DTYPE)
                dq = _add_rows(dq, r0, jnp.dot(dz, jnp.where(heads[h], kj, 0), preferred_element_type=F32))
                dk = dk + lax.dot_general(dz, q_h[h][r0:], C00, preferred_element_type=F32)
                runs[h] = _add_rows(runs[h], r0, jnp.sum(g, axis=1, keepdims=True))
            dk_ref[rows, :] += dk
            return dq

        def sweep_right(n, carry):
            dq, runs = carry
            runs = list(runs)
            for jj in range(unr):
                dq = right_block(n * unr + jj, 0, dq, runs, False)
            return dq, tuple(runs)

        init = (jnp.zeros((tq, LANES), F32), tuple(jnp.zeros((tq, 1), F32) for _ in range(hp)))
        dq, runs = lax.fori_loop(0, trips, sweep_right, init)
        runs = list(runs)
        for r in range(kpq):
            dq = right_block(i * kpq + r, r * blk, dq, runs, True)
        dq_ref[...] = (dq * scale).astype(dq_ref.dtype)

    kv_spec_out = pl.BlockSpec((SEQ, LANES), lambda b, c, i: (b, c))
    q_spec_out = pl.BlockSpec((tq, LANES), lambda b, c, i: (b * nq + i, c))
    res = pl.pallas_call(
        body,
        out_shape=[
            jax.ShapeDtypeStruct((nb * SEQ, cfg.sbw), ACT_DTYPE),
            jax.ShapeDtypeStruct((nb * SEQ, cfg.sbw), F32),
            jax.ShapeDtypeStruct((nb * SEQ, cfg.sbw), F32),
        ] + [jax.ShapeDtypeStruct(a.shape, a.dtype) for a in scatter],
        grid=(nb, ncb, nq),
        in_specs=[
            pl.BlockSpec((tq, LANES), lambda b, c, i: (b * nq + i, qb + c)),
            pl.BlockSpec((SEQ, LANES), lambda b, c, i: (b, kb + c)),
            pl.BlockSpec((SEQ, LANES), lambda b, c, i: (b, vb + c)),
            q_spec_out,
        ] + [ANY_SPEC] * ns,
        out_specs=[q_spec_out, kv_spec_out, kv_spec_out] + [ANY_SPEC] * ns,
        scratch_shapes=[pltpu.VMEM((hp, SEQ // blk, tq, blk), F32), pltpu.VMEM((hp, SEQ // blk, tq, blk), F32)]
        + (_comm_sems(ns) if ns else []),
        compiler_params=_cparams(("arbitrary",) * 3 if ns else ("parallel", "parallel", "arbitrary")),
        name="sb_bwd_scatter" if ns else "sb_bwd",
    )(proj, proj, proj, do_att, *scatter)
    return res[0], res[1], res[2], res[3:]


CONV_COLS = 256


def _conv_pre(x, w, b, t):
    kw = SSM_CONV
    shifted = []
    pre = b + w[kw - 1:kw, :] * x
    for k in range(kw - 1):
        d = kw - 1 - k
        xs = jnp.where(t >= d, pltpu.roll(x, d, 0), 0.0)
        shifted.append(xs)
        pre = pre + w[k:k + 1, :] * xs
    shifted.append(x)
    return pre, shifted


def _conv_fwd(proj, conv_w, conv_b, l, cfg, nb):
    cw = CONV_COLS
    ncb = cfg.conv_dim // cw
    xb = cfg.xbc0 // cw

    def body(x_ref, w_ref, b_ref, o_ref):
        x = x_ref[...]
        t = lax.broadcasted_iota(jnp.int32, x.shape, 0)
        pre, _ = _conv_pre(x, w_ref[...], b_ref[...], t)
        o_ref[...] = _silu(pre)

    return pl.pallas_call(
        body,
        out_shape=jax.ShapeDtypeStruct((nb * SEQ, cfg.conv_dim), F32),
        grid=(ncb, nb),
        in_specs=[
            pl.BlockSpec((SEQ, cw), lambda j, b: (b, xb + j)),
            pl.BlockSpec((None, SSM_CONV, cw), lambda j, b: (0, 0, j)),
            pl.BlockSpec((None, 1, cw), lambda j, b: (l, 0, j)),
        ],
        out_specs=pl.BlockSpec((SEQ, cw), lambda j, b: (b, j)),
        compiler_params=_cparams(("parallel", "parallel")),
        name="conv_fwd",
    )(proj, conv_w, conv_b)


def _conv_bwd(proj, dact, conv_w, conv_b, l, cfg, nb):
    cw = CONV_COLS
    ncb = cfg.conv_dim // cw
    xb = cfg.xbc0 // cw
    kw = SSM_CONV

    def body(x_ref, da_ref, w_ref, b_ref, dx_ref, dw_ref, db_ref):
        b_id = pl.program_id(1)
        x = x_ref[...]
        w = w_ref[...]
        t = lax.broadcasted_iota(jnp.int32, x.shape, 0)
        pre, shifted = _conv_pre(x, w, b_ref[...], t)
        dpre = da_ref[...] * _silu_grad(pre)
        dx = w[kw - 1:kw, :] * dpre
        for k in range(kw - 1):
            d = kw - 1 - k
            dx = dx + w[k:k + 1, :] * jnp.where(t < SEQ - d, pltpu.roll(dpre, SEQ - d, 0), 0.0)
        dx_ref[...] = dx.astype(dx_ref.dtype)
        dw = jnp.concatenate([jnp.sum(dpre * s, axis=0, keepdims=True) for s in shifted], axis=0)
        db = jnp.sum(dpre, axis=0, keepdims=True)

        @pl.when(b_id == 0)
        def _():
            dw_ref[...] = dw
            db_ref[...] = db

        @pl.when(b_id > 0)
        def _():
            dw_ref[...] += dw
            db_ref[...] += db

    return pl.pallas_call(
        body,
        out_shape=[
            jax.ShapeDtypeStruct((nb * SEQ, cfg.conv_dim), ACT_DTYPE),
            jax.ShapeDtypeStruct((kw, cfg.conv_dim), F32),
            jax.ShapeDtypeStruct((1, cfg.conv_dim), F32),
        ],
        grid=(ncb, nb),
        in_specs=[
            pl.BlockSpec((SEQ, cw), lambda j, b: (b, xb + j)),
            pl.BlockSpec((SEQ, cw), lambda j, b: (b, j)),
            pl.BlockSpec((None, kw, cw), lambda j, b: (0, 0, j)),
            pl.BlockSpec((None, 1, cw), lambda j, b: (l, 0, j)),
        ],
        out_specs=[
            pl.BlockSpec((SEQ, cw), lambda j, b: (b, j)),
            pl.BlockSpec((kw, cw), lambda j, b: (0, j)),
            pl.BlockSpec((1, cw), lambda j, b: (0, j)),
        ],
        compiler_params=_cparams(("parallel", "arbitrary")),
        name="conv_bwd",
    )(proj, dact, conv_w, conv_b)


def _ssd_common(dt_raw, dt_bias, a_log, tri):
    ln = SSM_CHUNK
    dt = _softplus(dt_raw + dt_bias)
    a = -jnp.exp(a_log)
    a_cs = jnp.dot(tri, dt * a, preferred_element_type=F32, precision=lax.Precision.HIGHEST)
    a_last = a_cs[ln - 1:ln, :]
    return dt, a, a_cs, a_cs.T, jnp.exp(a_cs), jnp.exp(a_last - a_cs), jnp.exp(a_last)


def _ssd_specs(cfg, nc, rev):
    ln = SSM_CHUNK
    cidx = (lambda c: nc - 1 - c) if rev else (lambda c: c)
    bmb = cfg.inner // cfg.gn
    return [
        pl.BlockSpec((ln, cfg.inner), lambda b, c: (b * nc + cidx(c), 0)),
        pl.BlockSpec((ln, cfg.gn), lambda b, c: (b * nc + cidx(c), bmb)),
        pl.BlockSpec((ln, cfg.gn), lambda b, c: (b * nc + cidx(c), bmb + 1)),
        pl.BlockSpec((ln, LANES), lambda b, c: (b * nc + cidx(c), cfg.dt0 // LANES)),
    ]


def _split3_dot(x, u):
    hi = x.astype(BF16)
    r1 = x - hi.astype(F32)
    mid = r1.astype(BF16)
    lo = (r1 - mid.astype(F32)).astype(BF16)
    dot = lambda a: jnp.dot(a, u, preferred_element_type=F32)
    return dot(hi) + dot(mid) + dot(lo)


def _ssd_consts(cfg):
    p = SSM_HEAD_DIM
    hrow = jnp.arange(LANES)[:, None]
    spread = (jnp.arange(cfg.inner)[None, :] // p == hrow).astype(BF16)
    spread_tile = (jnp.arange(cfg.heads * LANES)[None, :] // LANES == hrow).astype(BF16)
    return spread, spread_tile, spread.T


def _ssd_chunk_terms(dtr, bias, alog, spread, spread_tile):
    ln = SSM_CHUNK
    row = lax.broadcasted_iota(jnp.int32, (ln, ln), 0)
    col = lax.broadcasted_iota(jnp.int32, (ln, ln), 1)
    tril = row >= col
    dt, a, a_cs, a_t, e_a, dte, cd = _ssd_common(dtr, bias, alog, tril.astype(F32))
    ex = lambda v: _split_dot(v, spread)
    cd_x = ex(jnp.broadcast_to(cd, (8, LANES)))[0:1]
    colb = _split3_dot(a_cs, spread_tile)
    return dict(tril=tril, row=row, col=col, dt=dt, a=a, a_cs=a_cs, a_t=a_t, e_a=e_a, dte=dte, cd=cd,
                dt_x=ex(dt), ea_x=ex(e_a), dte_x=ex(dte), cd_x=cd_x, colb=colb)


def _head_masks():
    lane = lax.broadcasted_iota(jnp.int32, (1, LANES), 1)
    hpt = LANES // SSM_HEAD_DIM
    return [jnp.logical_and(lane >= i * SSM_HEAD_DIM, lane < (i + 1) * SSM_HEAD_DIM) for i in range(hpt)]


def _ssd_fwd(xbc, proj, dt_bias, a_log, d_skip_x, l, cfg, nb):
    ln, p, n = SSM_CHUNK, SSM_HEAD_DIM, SSM_STATE
    nc = SEQ // ln
    g_, e_ = SSM_GROUPS, cfg.epg
    gw = e_ * p
    hpt = LANES // p
    assert cfg.inner % cfg.gn == 0 and gw % LANES == 0
    spread, spread_tile, _ = _ssd_consts(cfg)

    def body(xs_ref, bm_ref, cm_ref, dtr_ref, bias_ref, alog_ref, dskx_ref, sp_ref, spt_ref, y_ref, st_ref, s_ref):
        c = pl.program_id(1)

        @pl.when(c == 0)
        def _():
            s_ref[...] = jnp.zeros_like(s_ref)

        st_ref[...] = s_ref[...]
        t = _ssd_chunk_terms(dtr_ref[...], bias_ref[...], alog_ref[...], sp_ref[...], spt_ref[...])
        hm = _head_masks()
        xs = xs_ref[...]
        xd = xs * t["dt_x"]
        xde = (xd * t["dte_x"]).astype(MXU_DTYPE)
        for g in range(g_):
            gc = slice(g * gw, (g + 1) * gw)
            bm = bm_ref[:, g * n:(g + 1) * n].astype(MXU_DTYPE)
            cm = cm_ref[:, g * n:(g + 1) * n].astype(MXU_DTYPE)
            cb = lax.dot_general(cm, bm, C11, preferred_element_type=F32)
            sg = s_ref[g * n:(g + 1) * n, :]
            y_off = jnp.dot(cm, sg.astype(MXU_DTYPE), preferred_element_type=F32) * t["ea_x"][:, gc]
            s_ref[g * n:(g + 1) * n, :] = t["cd_x"][:, gc] * sg + lax.dot_general(
                bm, xde[:, gc], C00, preferred_element_type=F32)
            for k in range(gw // LANES):
                lanes = slice(g * gw + k * LANES, g * gw + (k + 1) * LANES)
                xp = xd[:, lanes]
                acc = y_off[:, k * LANES:(k + 1) * LANES] + dskx_ref[:, lanes] * xs[:, lanes]
                for i in range(hpt):
                    h = (g * gw + k * LANES) // p + i
                    lmat = jnp.exp(jnp.where(t["tril"], t["colb"][:, h * LANES:(h + 1) * LANES] - t["a_t"][h:h + 1, :],
                                             -jnp.inf))
                    acc = acc + jnp.dot((cb * lmat).astype(MXU_DTYPE),
                                        jnp.where(hm[i], xp, 0.0).astype(MXU_DTYPE), preferred_element_type=F32)
                y_ref[:, lanes] = acc

    vec = lambda b, c: (l, 0, 0)
    whole = lambda a: pl.BlockSpec(a.shape, lambda b, c: (0, 0))
    return pl.pallas_call(
        body,
        out_shape=[
            jax.ShapeDtypeStruct((nb * SEQ, cfg.inner), F32),
            jax.ShapeDtypeStruct((nb * nc * g_ * n, gw), F32),
        ],
        grid=(nb, nc),
        in_specs=_ssd_specs(cfg, nc, False) + [pl.BlockSpec((None, 1, LANES), vec)] * 2
        + [pl.BlockSpec((None, 1, cfg.inner), vec), whole(spread), whole(spread_tile)],
        out_specs=[
            pl.BlockSpec((ln, cfg.inner), lambda b, c: (b * nc + c, 0)),
            pl.BlockSpec((g_ * n, gw), lambda b, c: (b * nc + c, 0)),
        ],
        scratch_shapes=[pltpu.VMEM((g_ * n, gw), F32)],
        compiler_params=_cparams(("parallel", "arbitrary")),
        name="ssd_fwd",
    )(xbc, xbc, xbc, proj, dt_bias, a_log, d_skip_x, spread, spread_tile)


def _ssd_bwd(xbc, proj, states, dy, dt_bias, a_log, d_skip_x, l, cfg, nb):
    ln, p, n = SSM_CHUNK, SSM_HEAD_DIM, SSM_STATE
    nc = SEQ // ln
    g_, e_ = SSM_GROUPS, cfg.epg
    gw = e_ * p
    hpt = LANES // p
    spread, spread_tile, gather_t = _ssd_consts(cfg)

    def body(xs_ref, bm_ref, cm_ref, dtr_ref, st_ref, dy_ref, bias_ref, alog_ref, dskx_ref, sp_ref, spt_ref, gt_ref,
             dxbc_ref, ddt_ref, dvec_ref, ds_ref, r1_ref, r2_ref, r4_ref, ss_ref):
        first = jnp.logical_and(pl.program_id(0) == 0, pl.program_id(1) == 0)

        @pl.when(pl.program_id(1) == 0)
        def _():
            ds_ref[...] = jnp.zeros_like(ds_ref)

        dtr = dtr_ref[...]
        bias = bias_ref[...]
        t = _ssd_chunk_terms(dtr, bias, alog_ref[...], sp_ref[...], spt_ref[...])
        tril = t["tril"]
        triu = t["row"] <= t["col"]
        hm = _head_masks()
        lane = lax.broadcasted_iota(jnp.int32, (1, LANES), 1)
        sub = lax.broadcasted_iota(jnp.int32, (LANES, 1), 0)
        xs = xs_ref[...]
        dyv = dy_ref[...]
        xd = xs * t["dt_x"]
        xde = xd * t["dte_x"]
        xde_m = xde.astype(MXU_DTYPE)
        dye_m = (dyv * t["ea_x"]).astype(MXU_DTYPE)
        da_col = jnp.zeros((ln, LANES), F32)
        da_row_t = jnp.zeros((LANES, ln), F32)
        ss_ref[...] = jnp.zeros_like(ss_ref)
        for g in range(g_):
            gc = slice(g * gw, (g + 1) * gw)
            gr = slice(g * n, (g + 1) * n)
            bm = bm_ref[:, gr].astype(MXU_DTYPE)
            cm = cm_ref[:, gr].astype(MXU_DTYPE)
            cb = lax.dot_general(cm, bm, C11, preferred_element_type=F32)
            cb_t = lax.dot_general(bm, cm, C11, preferred_element_type=F32)
            sp = st_ref[gr, :]
            dsg = ds_ref[gr, :]
            sp_m = sp.astype(MXU_DTYPE)
            dsg_m = dsg.astype(MXU_DTYPE)
            bds = jnp.dot(bm, dsg_m, preferred_element_type=F32)
            y_off = jnp.dot(cm, sp_m, preferred_element_type=F32) * t["ea_x"][:, gc]
            dcm = lax.dot_general(dye_m[:, gc], sp_m, C11, preferred_element_type=F32)
            dbm = lax.dot_general(xde_m[:, gc], dsg_m, C11, preferred_element_type=F32)
            ds_ref[gr, :] = t["cd_x"][:, gc] * dsg + lax.dot_general(cm, dye_m[:, gc], C00, preferred_element_type=F32)
            r4 = bds * xde[:, gc]
            r4_ref[:, gc] = r4
            r1_ref[:, gc] = dyv[:, gc] * y_off - r4
            ss_ref[0:1, gc] = jnp.sum(dsg * sp, axis=0, keepdims=True)
            dcb = jnp.zeros((ln, ln), F32)
            for k in range(gw // LANES):
                lanes = slice(g * gw + k * LANES, g * gw + (k + 1) * LANES)
                xp = xd[:, lanes]
                xp_m = xp.astype(MXU_DTYPE)
                dyp = dyv[:, lanes]
                dxp = t["dte_x"][:, lanes] * bds[:, k * LANES:(k + 1) * LANES]
                for i in range(hpt):
                    h = (g * gw + k * LANES) // p + i
                    diff = t["colb"][:, h * LANES:(h + 1) * LANES] - t["a_t"][h:h + 1, :]
                    lmat = jnp.exp(jnp.where(tril, diff, -jnp.inf))
                    lmat_t = jnp.exp(jnp.where(triu, -diff, -jnp.inf))
                    dy_h = jnp.where(hm[i], dyp, 0.0).astype(MXU_DTYPE)
                    d_ml = lax.dot_general(dy_h, xp_m, C11, preferred_element_type=F32) * lmat
                    dcb = dcb + d_ml
                    w_mat = d_ml * cb
                    dxp = dxp + jnp.dot((cb_t * lmat_t).astype(MXU_DTYPE), dy_h, preferred_element_type=F32)
                    da_col = da_col + jnp.sum(w_mat, axis=1, keepdims=True) * (lane == h).astype(F32)
                    da_row_t = da_row_t - (sub == h).astype(F32) * jnp.sum(w_mat, axis=0, keepdims=True)
                dxbc_ref[:, lanes] = dxp * t["dt_x"][:, lanes] + dskx_ref[:, lanes] * dyp
                r2_ref[:, lanes] = dxp * xs[:, lanes]
            dcb_m = dcb.astype(MXU_DTYPE)
            dxbc_ref[:, cfg.inner + g * n:cfg.inner + (g + 1) * n] = dbm + lax.dot_general(
                dcb_m, cm, C00, preferred_element_type=F32)
            dxbc_ref[:, cfg.inner + cfg.gn + g * n:cfg.inner + cfg.gn + (g + 1) * n] = dcm + jnp.dot(
                dcb_m, bm, preferred_element_type=F32)
        gt = gt_ref[...]
        rd = lambda v: _split_dot(v, gt)
        red4 = rd(r4_ref[...])
        da_last = jnp.sum(red4, axis=0, keepdims=True) + t["cd"] * rd(ss_ref[...])[0:1]
        d_acs = da_col + rd(r1_ref[...]) + da_row_t.T + jnp.where(t["row"][:, 0:1] == ln - 1, da_last, 0.0)
        da_dt = lax.dot_general(tril.astype(F32), d_acs, C00, preferred_element_type=F32,
                                precision=lax.Precision.HIGHEST)
        ddt = rd(r2_ref[...]) + da_dt * t["a"]
        ddt_raw = ddt * jax.nn.sigmoid(dtr + bias)
        ddt_ref[...] = ddt_raw.astype(ddt_ref.dtype)
        da_log = jnp.sum(da_dt * t["dt"], axis=0, keepdims=True) * t["a"]
        ddsk = jnp.sum(rd(dyv * xs), axis=0, keepdims=True)
        dvec = jnp.concatenate([jnp.sum(ddt_raw, axis=0, keepdims=True), da_log, ddsk,
                                jnp.zeros((5, LANES), F32)], axis=0)

        @pl.when(first)
        def _():
            dvec_ref[...] = dvec

        @pl.when(jnp.logical_not(first))
        def _():
            dvec_ref[...] += dvec

    vec = lambda b, c: (l, 0, 0)
    rblk = lambda b, c: (b * nc + nc - 1 - c, 0)
    whole = lambda a: pl.BlockSpec(a.shape, lambda b, c: (0, 0))
    return pl.pallas_call(
        body,
        out_shape=[
            jax.ShapeDtypeStruct((nb * SEQ, cfg.conv_dim), F32),
            jax.ShapeDtypeStruct((nb * SEQ, LANES), ACT_DTYPE),
            jax.ShapeDtypeStruct((8, LANES), F32),
        ],
        grid=(nb, nc),
        in_specs=_ssd_specs(cfg, nc, True) + [
            pl.BlockSpec((g_ * n, gw), rblk),
            pl.BlockSpec((ln, cfg.inner), rblk),
        ] + [pl.BlockSpec((None, 1, LANES), vec)] * 2 + [pl.BlockSpec((None, 1, cfg.inner), vec),
                                                           whole(spread), whole(spread_tile), whole(gather_t)],
        out_specs=[
            pl.BlockSpec((ln, cfg.conv_dim), rblk),
            pl.BlockSpec((ln, LANES), rblk),
            pl.BlockSpec((8, LANES), lambda b, c: (0, 0)),
        ],
        scratch_shapes=[pltpu.VMEM((g_ * n, gw), F32)] + [pltpu.VMEM((ln, cfg.inner), F32)] * 3
        + [pltpu.VMEM((8, cfg.inner), F32)],
        compiler_params=_cparams(("arbitrary", "arbitrary")),
        name="ssd_bwd",
    )(xbc, xbc, xbc, proj, states, dy, dt_bias, a_log, d_skip_x, spread, spread_tile, gather_t)


XA_ROWS = 256


def _xa_probs(q_ref, kv_ref, h, dh):
    c11 = (((1,), (1,)), ((), ()))
    qh = q_ref[:, h * dh:(h + 1) * dh].astype(MXU_DTYPE)
    kh = kv_ref[:, h * dh:(h + 1) * dh].astype(MXU_DTYPE)
    vh = kv_ref[:, D_MODEL + h * dh:D_MODEL + (h + 1) * dh].astype(MXU_DTYPE)
    s = lax.dot_general(qh, kh, c11, preferred_element_type=F32) * (dh ** -0.5)
    s = s - jnp.max(s, axis=1, keepdims=True)
    pr = jnp.exp(s)
    return qh, kh, vh, pr / jnp.sum(pr, axis=1, keepdims=True)


def _xa_fwd(q, kv, cfg, nb):
    tq = _tile(SEQ, XA_ROWS, 16)
    nq = SEQ // tq
    dh = cfg.xa_dim

    def body(q_ref, kv_ref, o_ref):
        for h in range(XA_HEADS):
            _, _, vh, pr = _xa_probs(q_ref, kv_ref, h, dh)
            o_ref[:, h * dh:(h + 1) * dh] = jnp.dot(pr.astype(MXU_DTYPE), vh, preferred_element_type=F32).astype(o_ref.dtype)

    return pl.pallas_call(
        body,
        out_shape=jax.ShapeDtypeStruct((nb * SEQ, D_MODEL), ACT_DTYPE),
        grid=(nb, nq),
        in_specs=[
            pl.BlockSpec((tq, D_MODEL), lambda b, i: (b * nq + i, 0)),
            pl.BlockSpec((MEM_LEN, 2 * D_MODEL), lambda b, i: (b, 0)),
        ],
        out_specs=pl.BlockSpec((tq, D_MODEL), lambda b, i: (b * nq + i, 0)),
        compiler_params=_cparams(("parallel", "parallel")),
        name="xa_fwd",
    )(q, kv)


def _xa_bwd(q, kv, do, cfg, nb):
    tq = _tile(SEQ, XA_ROWS, 16)
    nq = SEQ // tq
    dh = cfg.xa_dim
    c00 = (((0,), (0,)), ((), ()))
    c11 = (((1,), (1,)), ((), ()))
    scale = dh ** -0.5

    def body(q_ref, kv_ref, do_ref, dq_ref, dkv_ref, acc_ref):
        i = pl.program_id(1)

        @pl.when(i == 0)
        def _():
            acc_ref[...] = jnp.zeros_like(acc_ref)

        for h in range(XA_HEADS):
            hs = slice(h * dh, (h + 1) * dh)
            vs = slice(D_MODEL + h * dh, D_MODEL + (h + 1) * dh)
            qh, kh, vh, pr = _xa_probs(q_ref, kv_ref, h, dh)
            do_h = do_ref[:, hs].astype(MXU_DTYPE)
            dp = lax.dot_general(do_h, vh, c11, preferred_element_type=F32)
            ds = (pr * (dp - jnp.sum(dp * pr, axis=1, keepdims=True))).astype(MXU_DTYPE)
            dq_ref[:, hs] = (jnp.dot(ds, kh, preferred_element_type=F32) * scale).astype(dq_ref.dtype)
            acc_ref[:, hs] += lax.dot_general(ds, qh, c00, preferred_element_type=F32) * scale
            acc_ref[:, vs] += lax.dot_general(pr.astype(MXU_DTYPE), do_h, c00, preferred_element_type=F32)

        @pl.when(i == nq - 1)
        def _():
            dkv_ref[...] = acc_ref[...].astype(dkv_ref.dtype)

    return pl.pallas_call(
        body,
        out_shape=[
            jax.ShapeDtypeStruct((nb * SEQ, D_MODEL), ACT_DTYPE),
            jax.ShapeDtypeStruct((nb * MEM_LEN, 2 * D_MODEL), ACT_DTYPE),
        ],
        grid=(nb, nq),
        in_specs=[
            pl.BlockSpec((tq, D_MODEL), lambda b, i: (b * nq + i, 0)),
            pl.BlockSpec((MEM_LEN, 2 * D_MODEL), lambda b, i: (b, 0)),
            pl.BlockSpec((tq, D_MODEL), lambda b, i: (b * nq + i, 0)),
        ],
        out_specs=[
            pl.BlockSpec((tq, D_MODEL), lambda b, i: (b * nq + i, 0)),
            pl.BlockSpec((MEM_LEN, 2 * D_MODEL), lambda b, i: (b, 0)),
        ],
        scratch_shapes=[pltpu.VMEM((MEM_LEN, 2 * D_MODEL), F32)],
        compiler_params=_cparams(("parallel", "arbitrary")),
        name="xa_bwd",
    )(q, kv, do)


def _adamw(parts, w, m, v, name, tr=128):
    n, nl, r, c = parts.shape
    tr = _tile(r, tr, 16)

    def body(p_ref, w_ref, m_ref, v_ref, g_ref, d_ref, nm_ref, nv_ref):
        g = p_ref[0].astype(F32)
        for i in range(1, n):
            g = g + p_ref[i].astype(F32)
        m2 = ADAM_B1 * m_ref[...] + (1.0 - ADAM_B1) * g
        v2 = ADAM_B2 * v_ref[...] + (1.0 - ADAM_B2) * (g * g)
        m_hat = m2 / (1.0 - ADAM_B1 ** ADAM_STEP)
        v_hat = v2 / (1.0 - ADAM_B2 ** ADAM_STEP)
        g_ref[...] = g
        d_ref[...] = -ADAM_LR * (m_hat / (jnp.sqrt(v_hat) + ADAM_EPS) + ADAM_WD * w_ref[...])
        nm_ref[...] = m2
        nv_ref[...] = v2

    blk = pl.BlockSpec((None, tr, c), lambda l, i: (l, i, 0))
    return pl.pallas_call(
        body,
        out_shape=[jax.ShapeDtypeStruct((nl, r, c), F32)] * 4,
        grid=(nl, r // tr),
        in_specs=[pl.BlockSpec((n, None, tr, c), lambda l, i: (0, l, i, 0)), blk, blk, blk],
        out_specs=[blk] * 4,
        compiler_params=_cparams(("parallel", "parallel")),
        name=name,
    )(parts, w, m, v)


def _flat_index(px, py, pc):
    return 4 * px + 2 * py + pc


def _all_gather(arrs, name, layers=None):
    n = len(arrs)

    def body(*refs):
        start, forward, finish = _gather_phases(refs[:n], refs[n:2 * n], *refs[2 * n:], layers=layers)
        start()
        forward()
        finish()

    return pl.pallas_call(
        body,
        out_shape=_gather_shapes(arrs, layers),
        in_specs=[ANY_SPEC] * n,
        out_specs=[ANY_SPEC] * n,
        scratch_shapes=_comm_sems(n),
        name=name,
    )(*arrs)


ANY_SPEC = pl.BlockSpec(memory_space=pl.ANY)


def _comm_sems(n):
    return [pltpu.SemaphoreType.DMA((n, N_DEV - 1)), pltpu.SemaphoreType.DMA((n, N_DEV - 1)),
            pltpu.SemaphoreType.DMA((n,))]


def _gather_shapes(arrs, layers=None):
    layers = layers or [None] * len(arrs)
    return [jax.ShapeDtypeStruct((N_DEV,) + (a.shape if lay is None else a.shape[1:]), a.dtype)
            for a, lay in zip(arrs, layers)]


def _gather_phases(ins, outs, send_sems, recv_sems, local_sems, layers=None):
    n = len(ins)
    if layers is not None:
        ins = [r if lay is None else r.at[lay] for r, lay in zip(ins, layers)]
    x, y, c = lax.axis_index("x"), lax.axis_index("y"), lax.axis_index("c")
    me, sibling = (x, y, c), (x, y, 1 - c)
    chips = [(1 - x, y), (x, 1 - y), (1 - x, 1 - y)]

    def copy(a, k, block, to, src=None):
        slot = outs[a].at[_flat_index(*block)]
        return pltpu.make_async_remote_copy(
            src_ref=slot if src is None else src, dst_ref=slot,
            send_sem=send_sems.at[a, k], recv_sem=recv_sems.at[a, k],
            device_id=to, device_id_type=MESH)

    def mine(a):
        return pltpu.make_async_copy(ins[a], outs[a].at[_flat_index(*me)], local_sems.at[a])

    def first(a):
        return [copy(a, 0, me, sibling, src=ins[a])] + [
            copy(a, 1 + j, me, (*chip, c), src=ins[a]) for j, chip in enumerate(chips)]

    def start():
        for a in range(n):
            mine(a).start()
            for cp in first(a):
                cp.start()

    def forward():
        for j, chip in enumerate(chips):
            for a in range(n):
                copy(a, 1 + j, (*chip, c), me).wait_recv()
                copy(a, 4 + j, (*chip, c), sibling).start()

    def finish():
        for a in range(n):
            copy(a, 0, sibling, me).wait_recv()
            for j, chip in enumerate(chips):
                copy(a, 4 + j, (*chip, 1 - c), me).wait_recv()
        for a in range(n):
            for cp in first(a):
                cp.wait_send()
            for j, chip in enumerate(chips):
                copy(a, 4 + j, (*chip, c), sibling).wait_send()
            mine(a).wait()

    return start, forward, finish


def _scatter_blocks(arrs, name):
    n = len(arrs)

    def body(*refs):
        start, finish = _scatter_phases(refs[:n], refs[n:2 * n], *refs[2 * n:])
        start()
        finish()

    return pl.pallas_call(
        body,
        out_shape=[jax.ShapeDtypeStruct(a.shape, a.dtype) for a in arrs],
        in_specs=[ANY_SPEC] * n,
        out_specs=[ANY_SPEC] * n,
        scratch_shapes=_comm_sems(n),
        name=name,
    )(*arrs)


def _scatter_phases(ins, outs, send_sems, recv_sems, local_sems):
    n = len(ins)
    x, y, c = lax.axis_index("x"), lax.axis_index("y"), lax.axis_index("c")
    me = _flat_index(x, y, c)

    def peer(k):
        return (1 - x if k & 4 else x, 1 - y if k & 2 else y, 1 - c if k & 1 else c)

    def copy(a, k):
        p = peer(k)
        return pltpu.make_async_remote_copy(
            src_ref=ins[a].at[_flat_index(*p)], dst_ref=outs[a].at[me],
            send_sem=send_sems.at[a, k - 1], recv_sem=recv_sems.at[a, k - 1],
            device_id=p, device_id_type=MESH)

    def landed(a, k):
        slot = outs[a].at[_flat_index(*peer(k))]
        return pltpu.make_async_remote_copy(
            src_ref=slot, dst_ref=slot, send_sem=send_sems.at[a, k - 1], recv_sem=recv_sems.at[a, k - 1],
            device_id=peer(k), device_id_type=MESH)

    def mine(a):
        return pltpu.make_async_copy(ins[a].at[me], outs[a].at[me], local_sems.at[a])

    def start():
        for a in range(n):
            mine(a).start()
            for k in range(1, N_DEV):
                copy(a, k).start()

    def finish():
        for a in range(n):
            for k in range(1, N_DEV):
                landed(a, k).wait_recv()
        for a in range(n):
            for k in range(1, N_DEV):
                copy(a, k).wait_send()
            mine(a).wait()

    return start, finish


_BIG = ("w_in", "w_br_att", "w_br_ssm", "w_mix_out", "w_xq", "w_xkv", "w_xo", "w_gu", "w_down")
_COL_SHARDED = ("w_in", "w_xkv", "w_gu", "conv_w")
_SMALL = ("g_pre_mix", "conv_b", "dt_bias", "a_log", "d_skip", "g_ssm_norm", "g_post_mix", "g_pre_xa",
          "g_mem", "g_post_xa", "g_pre_ffn", "g_post_ffn")
_WEIGHTS = ("g_pre_mix", "w_in", "conv_w", "conv_b", "dt_bias", "a_log", "d_skip", "g_ssm_norm", "w_br_att",
            "w_br_ssm", "w_mix_out", "g_post_mix", "g_pre_xa", "g_mem", "w_xq", "w_xkv", "w_xo", "g_post_xa",
            "g_pre_ffn", "w_gu", "w_down", "g_post_ffn")
PACK_W = 8 * LANES


def _unshard(g, col):
    n, r, c = g.shape
    if col:
        return jnp.transpose(g, (1, 0, 2)).reshape(r, n * c)
    return g.reshape(n * r, c)


def _shard(w, col):
    r, c = w.shape
    if col:
        return jnp.transpose(w.reshape(r, N_DEV, c // N_DEV), (1, 0, 2))
    return w.reshape(N_DEV, r // N_DEV, c)


def _permute_in(w, cfg):
    parts, off = [], 0
    for size in cfg.in_sizes:
        parts.append(w[..., off:off + size])
        off += size
    q, k, v, z, xbc, dt, ga, gs = parts
    pad = jnp.zeros(w.shape[:-1] + (LANES - cfg.heads,), w.dtype)
    return jnp.concatenate([z, ga, gs, q, k, v, xbc, dt, pad], axis=-1)


def _unpermute_in(w, cfg):
    c = cfg
    sl = lambda a, n: w[..., a:a + n]
    return jnp.concatenate([sl(c.q0, c.sbw), sl(c.k0, c.sbw), sl(c.v0, c.sbw), sl(c.z0, c.inner),
                            sl(c.xbc0, c.conv_dim), sl(c.dt0, c.heads), sl(c.ga0, c.d), sl(c.gs0, c.d)], axis=-1)


def _pack(arrs):
    flat = jnp.concatenate([a.reshape(-1).astype(F32) for a in arrs])
    rows = -(-flat.shape[0] // PACK_W)
    rows = -(-rows // 8) * 8
    return jnp.pad(flat, (0, rows * PACK_W - flat.shape[0])).reshape(rows, PACK_W)


def _unpack(p, shapes):
    flat = p.reshape(-1)
    out, off = [], 0
    for s in shapes:
        size = math.prod(s)
        out.append(flat[off:off + size].reshape(s))
        off += size
    return out


def _vec3(a, width=None):
    if width is not None and a.shape[1] < width:
        a = jnp.pad(a, ((0, 0), (0, width - a.shape[1])))
    return a[:, None, :]


def _forward_layer(l, xin, h1, memf, tgt, w, p, cfg, nb, last, gather, n_late=0, complete=None):
    t = xin.shape[0]
    d = cfg.d
    s = {"x_in": xin, "h1": h1}
    proj = _mm(h1, w["w_in"], name="mm_proj")
    s["proj"] = proj
    s["o_att"], gathered = _sb_fwd(proj, cfg, nb, *gather)
    if n_late:
        w.update(complete(gathered[:n_late]))
        gathered = gathered[n_late:]
    s["xbc"] = _conv_fwd(proj, w["conv_w"], p["conv_b"], l, cfg, nb)
    s["y"], s["states"] = _ssd_fwd(s["xbc"], proj, p["dt_bias"], p["a_log"], p["d_skip_x"], l, cfg, nb)
    s["o_ssm"] = _rowwise(_f_gate_norm, "gate_norm_fwd", t, [_full(s["y"]), (proj, cfg.inner, 0)],
                          [(p["g_ssm_norm"], l)], [(cfg.inner, ACT_DTYPE)])[0]
    s["ba"] = _mm(s["o_att"], w["w_br_att"], name="mm_br_att")
    s["bs"] = _mm(s["o_ssm"], w["w_br_ssm"], name="mm_br_ssm")
    s["merged"] = _rowwise(_f_merge, "merge_fwd", t,
                           [(proj, d, cfg.ga0 // d), (proj, d, cfg.gs0 // d), _full(s["ba"]), _full(s["bs"])],
                           [], [(d, ACT_DTYPE)])[0]
    s["mo"] = _mm(s["merged"], w["w_mix_out"], name="mm_mix_out")
    s["x1"], s["h2"] = _rowwise(_f_post_pre, "post_pre_mix", t, [_full(xin), _full(s["mo"])],
                                [(p["g_post_mix"], l), (p["g_pre_xa"], l)], [(d, F32), (d, ACT_DTYPE)])
    s["mem_n"] = _rowwise(_rms, "mem_norm", memf.shape[0], [_full(memf)], [(p["g_mem"], l)], [(d, ACT_DTYPE)])[0]
    s["q"] = _mm(s["h2"], w["w_xq"], name="mm_xq")
    s["kv"] = _mm(s["mem_n"], w["w_xkv"], name="mm_xkv")
    s["o_xa"] = _xa_fwd(s["q"], s["kv"], cfg, nb)
    s["xo"] = _mm(s["o_xa"], w["w_xo"], name="mm_xo")
    s["x2"], s["h3"] = _rowwise(_f_post_pre, "post_pre_xa", t, [_full(s["x1"]), _full(s["xo"])],
                                [(p["g_post_xa"], l), (p["g_pre_ffn"], l)], [(d, F32), (d, ACT_DTYPE)])
    s["gu"] = _mm(s["h3"], w["w_gu"], name="mm_gu")
    s["act"] = _rowwise(_f_swiglu, "swiglu_fwd", t, [(s["gu"], cfg.ffn, 0), (s["gu"], cfg.ffn, 1)], [],
                        [(cfg.ffn, ACT_DTYPE)])[0]
    s["dn"] = _mm(s["act"], w["w_down"], name="mm_down")
    if last:
        nxt = _rowwise(_f_final, "final_loss", t, [_full(s["x2"]), _full(s["dn"]), _full(tgt)],
                       [(p["g_post_ffn"], l)], [(d, F32)], acc_out=[(1, d)])
    else:
        nxt = _rowwise(_f_post_pre, "post_pre_ffn", t, [_full(s["x2"]), _full(s["dn"])],
                       [(p["g_post_ffn"], l), (p["g_pre_mix"], l + 1)], [(d, F32), (d, ACT_DTYPE)])
    return s, nxt, gathered


def _backward_layer(l, s, dx, d_dn, memf, w, p, cfg, nb, prev_dn, scatter):
    t = dx.shape[0]
    d = cfg.d
    g = {}
    dact = _mm(d_dn, w["w_down"], tb=True, out_dtype=ACT_DTYPE, name="mm_d_act", tn=1408)
    g["w_down"] = _mm(s["act"], d_dn, ta=True, name="mm_dw_down")
    dgu = _rowwise(_f_swiglu_bwd, "swiglu_bwd", t, [(s["gu"], cfg.ffn, 0), (s["gu"], cfg.ffn, 1), _full(dact)], [],
                   [(2 * cfg.ffn, ACT_DTYPE)])[0]
    dh3 = _mm(dgu, w["w_gu"], tb=True, name="mm_d_h3", tk=1408)
    g["w_gu"] = _mm(s["h3"], dgu, ta=True, name="mm_dw_gu")
    dx2, d_xo, g["g_pre_ffn"], g["g_post_xa"] = _rowwise(
        _f_pre_post_bwd, "pre_post_bwd_ffn", t, [_full(s["x2"]), _full(dh3), _full(dx), _full(s["xo"])],
        [(p["g_pre_ffn"], l), (p["g_post_xa"], l)], [(d, F32), (d, ACT_DTYPE)], acc_out=[(1, d), (1, d)])
    do_xa = _mm(d_xo, w["w_xo"], tb=True, out_dtype=ACT_DTYPE, name="mm_d_oxa")
    g["w_xo"] = _mm(s["o_xa"], d_xo, ta=True, name="mm_dw_xo")
    dq, dkv = _xa_bwd(s["q"], s["kv"], do_xa, cfg, nb)
    dh2 = _mm(dq, w["w_xq"], tb=True, name="mm_d_h2")
    g["w_xq"] = _mm(s["h2"], dq, ta=True, name="mm_dw_xq")
    dmem_n = _mm(dkv, w["w_xkv"], tb=True, name="mm_d_mem")
    g["w_xkv"] = _mm(s["mem_n"], dkv, ta=True, name="mm_dw_xkv")
    g["g_mem"] = _rowwise(_f_gain_bwd, "mem_norm_bwd", memf.shape[0], [_full(memf), _full(dmem_n)],
                          [(p["g_mem"], l)], [], acc_out=[(1, d)])[0]
    dx1, d_mo, g["g_pre_xa"], g["g_post_mix"] = _rowwise(
        _f_pre_post_bwd, "pre_post_bwd_xa", t, [_full(s["x1"]), _full(dh2), _full(dx2), _full(s["mo"])],
        [(p["g_pre_xa"], l), (p["g_post_mix"], l)], [(d, F32), (d, ACT_DTYPE)], acc_out=[(1, d), (1, d)])
    dmerged = _mm(d_mo, w["w_mix_out"], tb=True, out_dtype=ACT_DTYPE, name="mm_d_merged")
    g["w_mix_out"] = _mm(s["merged"], d_mo, ta=True, name="mm_dw_mix_out")
    proj = s["proj"]
    dgg, dba, dbs = _rowwise(
        _f_merge_bwd, "merge_bwd", t,
        [(proj, d, cfg.ga0 // d), (proj, d, cfg.gs0 // d), _full(s["ba"]), _full(s["bs"]), _full(dmerged)], [],
        [(2 * d, ACT_DTYPE), (d, ACT_DTYPE), (d, ACT_DTYPE)])
    do_att = _mm(dba, w["w_br_att"], tb=True, name="mm_d_oatt")
    g["w_br_att"] = _mm(s["o_att"], dba, ta=True, name="mm_dw_br_att")
    do_ssm = _mm(dbs, w["w_br_ssm"], tb=True, out_dtype=ACT_DTYPE, name="mm_d_ossm")
    g["w_br_ssm"] = _mm(s["o_ssm"], dbs, ta=True, name="mm_dw_br_ssm")
    dy, dz, g["g_ssm_norm"] = _rowwise(
        _f_gate_norm_bwd, "gate_norm_bwd", t, [_full(s["y"]), (proj, cfg.inner, 0), _full(do_ssm)],
        [(p["g_ssm_norm"], l)], [(cfg.inner, F32), (cfg.inner, ACT_DTYPE)], acc_out=[(1, cfg.inner)])
    dxbc, ddt_raw, dvec = _ssd_bwd(s["xbc"], proj, s["states"], dy, p["dt_bias"], p["a_log"], p["d_skip_x"], l, cfg, nb)
    g["dt_bias"], g["a_log"], g["d_skip"] = (dvec[i:i + 1, :cfg.heads] for i in range(3))
    dxbc_raw, g["conv_w"], g["conv_b"] = _conv_bwd(proj, dxbc, w["conv_w"], p["conv_b"], l, cfg, nb)
    own = [_shard(g[n], n in _COL_SHARDED).astype(WIRE_DTYPE) for n in _BIG if n != "w_in"]
    dq_sb, dk_sb, dv_sb, landed = _sb_bwd(proj, do_att, cfg, nb, list(scatter) + own)
    landed = (landed[:len(scatter)], landed[len(scatter):])
    dproj = jnp.concatenate([dz, dgg, dq_sb, dk_sb.astype(ACT_DTYPE), dv_sb.astype(ACT_DTYPE), dxbc_raw, ddt_raw], axis=1)
    dh1 = _mm(dproj, w["w_in"], tb=True, name="mm_d_h1", tk=1152)
    g["w_in"] = _mm(s["h1"], dproj, ta=True, name="mm_dw_in")
    if prev_dn is None:
        dx0, g["g_pre_mix"] = _rowwise(_f_pre_bwd, "pre_bwd_first", t, [_full(s["x_in"]), _full(dh1), _full(dx1)],
                                       [(p["g_pre_mix"], l)], [(d, F32)], acc_out=[(1, d)])
        return g, dx0, None, None, landed
    dx0, d_dn_prev, g["g_pre_mix"], g_post_prev = _rowwise(
        _f_pre_post_bwd, "pre_post_bwd_mix", t, [_full(s["x_in"]), _full(dh1), _full(dx1), _full(prev_dn)],
        [(p["g_pre_mix"], l), (p["g_post_ffn"], l - 1)], [(d, F32), (d, ACT_DTYPE)], acc_out=[(1, d), (1, d)])
    return g, dx0, d_dn_prev, g_post_prev, landed


def kernel(x, mem, g_pre_mix, w_in, conv_w, conv_b, dt_bias, a_log, d_skip, g_ssm_norm, w_br_att, w_br_ssm, w_mix_out, g_post_mix, g_pre_xa, g_mem, w_xq, w_xkv, w_xo, g_post_xa, g_pre_ffn, w_gu, w_down, g_post_ffn, loss_target, m_g_pre_mix, m_w_in, m_conv_w, m_conv_b, m_dt_bias, m_a_log, m_d_skip, m_g_ssm_norm, m_w_br_att, m_w_br_ssm, m_w_mix_out, m_g_post_mix, m_g_pre_xa, m_g_mem, m_w_xq, m_w_xkv, m_w_xo, m_g_post_xa, m_g_pre_ffn, m_w_gu, m_w_down, m_g_post_ffn, v_g_pre_mix, v_w_in, v_conv_w, v_conv_b, v_dt_bias, v_a_log, v_d_skip, v_g_ssm_norm, v_w_br_att, v_w_br_ssm, v_w_mix_out, v_g_post_mix, v_g_pre_xa, v_g_mem, v_w_xq, v_w_xkv, v_w_xo, v_g_post_xa, v_g_pre_ffn, v_w_gu, v_w_down, v_g_post_ffn):
    vals = dict(locals())
    cfg = _Cfg()
    nb = x.shape[0]
    t = nb * SEQ
    d = cfg.d
    depth = g_pre_mix.shape[0]

    gathered_names = _BIG + ("conv_w",)
    late_names = gathered_names[1:]

    on_wire = {n: conv_w if n == "conv_w" else vals[n].astype(WIRE_DTYPE) for n in gathered_names}

    def wire(l, names=gathered_names):
        return [on_wire[n] for n in names], [l] * len(names)

    def layer_weights(gathered, names=gathered_names):
        w = {n: _unshard(gw, n in _COL_SHARDED) for n, gw in zip(names, gathered)}
        if "w_in" in w:
            w["w_in"] = _permute_in(w["w_in"], cfg)
        if "conv_w" in w:
            w["conv_w"] = w["conv_w"][None]
        return w

    p = {n: _vec3(vals[n], LANES if n in ("dt_bias", "a_log", "d_skip") else None) for n in _SMALL}
    p["d_skip_x"] = _vec3(jnp.repeat(d_skip, SSM_HEAD_DIM, axis=1))
    weights = [None] * depth
    first_arrs, first_layers = wire(0, ("w_in",))
    weights[0] = layer_weights(_all_gather(first_arrs, "ag_weights_first", first_layers), ("w_in",))

    xf = x.reshape(t, d)
    memf = mem.reshape(nb * MEM_LEN, d)
    tgt = loss_target.reshape(t, d)
    h = _rowwise(_rms, "pre_norm_first", t, [_full(xf)], [(p["g_pre_mix"], 0)], [(d, ACT_DTYPE)])[0]
    saved = []
    xcur = xf
    for l in range(depth):
        last = l == depth - 1
        late = wire(0, late_names) if l == 0 else ([], [])
        ahead = ([], []) if last else wire(l + 1)
        complete = (lambda got: layer_weights(got, late_names)) if l == 0 else None
        s, nxt, gathered = _forward_layer(l, xcur, h, memf, tgt, weights[l], p, cfg, nb, last,
                                          (late[0] + ahead[0], late[1] + ahead[1]), len(late[0]), complete)
        saved.append(s)
        if not last:
            weights[l + 1] = layer_weights(gathered)
            xcur, h = nxt
    dx, loss_row = nxt
    loss = lax.psum(0.5 * jnp.sum(loss_row) / d, AXES)

    top = saved[-1]
    d_dn, g_post_top = _rowwise(lambda ysub, dxo, gp: _rms_bwd(ysub, gp, dxo), "post_bwd_last", t,
                                [_full(top["dn"]), _full(dx)], [(p["g_post_ffn"], depth - 1)], [(d, ACT_DTYPE)],
                                acc_out=[(1, d)])
    grads = [None] * depth
    landed = [dict() for _ in range(depth)]
    post_ffn = [None] * depth
    post_ffn[depth - 1] = g_post_top
    pending = []
    for l in reversed(range(depth)):
        prev_dn = saved[l - 1]["dn"] if l > 0 else None
        grads[l], dx, d_dn, g_post_prev, (got_above, got_own) = _backward_layer(
            l, saved[l], dx, d_dn, memf, weights[l], p, cfg, nb, prev_dn, pending)
        if pending:
            landed[l + 1]["w_in"] = got_above[0]
        landed[l].update(zip([n for n in _BIG if n != "w_in"], got_own))
        pending = [_shard(_unpermute_in(grads[l]["w_in"], cfg), True).astype(WIRE_DTYPE)]
        if l > 0:
            post_ffn[l - 1] = g_post_prev
    landed[0]["w_in"] = _scatter_blocks(pending, "scatter_grads_last")[0]
    for l in range(depth):
        grads[l]["g_post_ffn"] = post_ffn[l]
    grad_x = dx.reshape(x.shape)

    out = {}
    for n in _BIG:
        parts = jnp.stack([landed[l][n] for l in range(depth)], axis=1)
        out[n] = _adamw(parts, vals[n], vals["m_" + n], vals["v_" + n], "adamw_" + n)

    small_shapes = [vals[n].shape for n in _SMALL]
    pack_g = _pack([grads[l][n] for n in _SMALL for l in range(depth)])
    conv_g = jnp.concatenate([grads[l]["conv_w"] for l in range(depth)], axis=0)
    parts_small, parts_conv = _all_gather([pack_g, conv_g], "ag_small_grads")
    packed = lambda prefix: _pack([vals[prefix + n] for n in _SMALL])[None]
    res = _adamw(parts_small[:, None], packed(""), packed("m_"), packed("v_"), "adamw_small")
    unpacked = [_unpack(r, small_shapes) for r in res]
    for i, n in enumerate(_SMALL):
        out[n] = [unpacked[j][i] for j in range(4)]
    cs = conv_w.shape[2]
    me = _flat_index(lax.axis_index("x"), lax.axis_index("y"), lax.axis_index("c"))
    parts_conv = lax.dynamic_slice_in_dim(parts_conv, me * cs, cs, axis=2)
    flat = lambda a: a.reshape(1, depth * SSM_CONV, cs)
    res = _adamw(parts_conv[:, None], flat(conv_w), flat(m_conv_w), flat(v_conv_w), "adamw_conv_w")
    out["conv_w"] = [r.reshape(conv_w.shape) for r in res]

    return (loss, grad_x, *[out[n][0] for n in _WEIGHTS], *[out[n][1] for n in _WEIGHTS],
            *[out[n][2] for n in _WEIGHTS], *[out[n][3] for n in _WEIGHTS])
```

```python
import functools
import math

import jax
import jax.numpy as jnp
from jax import lax
from jax.experimental import pallas as pl
from jax.experimental.pallas import tpu as pltpu

F32 = jnp.float32
BF16 = jnp.bfloat16
MXU_DTYPE = BF16
ACT_DTYPE = BF16
WIRE_DTYPE = BF16

D_MODEL = 1024
SEQ = 2048
DEPTH = 4
MEM_LEN = 256
RMS_EPS = 1e-6
SB_HEADS = 16
SB_HEAD_DIM = 64
SB_BLOCK = 128
SSM_INNER = 2 * D_MODEL
SSM_HEAD_DIM = 64
SSM_GROUPS = 4
SSM_STATE = 128
SSM_CONV = 4
SSM_CHUNK = 128
XA_HEADS = 4
FFN_HIDDEN = ((8 * D_MODEL + 767) // 768) * 256
ADAM_LR = 0.001
ADAM_B1 = 0.9
ADAM_B2 = 0.999
ADAM_EPS = 1e-08
ADAM_WD = 0.01
ADAM_STEP = 10

N_DEV = 8
LANES = 128
VMEM_LIMIT_BYTES = 56 * 1024 * 1024

AXES = ("x", "y", "c")
MESH = pl.DeviceIdType.MESH


class _Cfg:
    def __init__(self):
        self.d = D_MODEL
        self.sbw = SB_HEADS * SB_HEAD_DIM
        self.inner = SSM_INNER
        self.heads = SSM_INNER // SSM_HEAD_DIM
        self.epg = self.heads // SSM_GROUPS
        self.gn = SSM_GROUPS * SSM_STATE
        self.conv_dim = SSM_INNER + 2 * self.gn
        self.ffn = FFN_HIDDEN
        self.xa_dim = D_MODEL // XA_HEADS
        self.in_sizes = (self.sbw, self.sbw, self.sbw, self.inner, self.conv_dim, self.heads, self.d, self.d)
        self.in_width = sum(self.in_sizes)
        self.z0 = 0
        self.ga0 = self.inner
        self.gs0 = self.ga0 + self.d
        self.q0 = self.gs0 + self.d
        self.k0 = self.q0 + self.sbw
        self.v0 = self.k0 + self.sbw
        self.xbc0 = self.v0 + self.sbw
        self.dt0 = self.xbc0 + self.conv_dim
        self.proj_w = self.dt0 + LANES
        assert self.heads <= LANES


def _cparams(sem=None):
    return pltpu.CompilerParams(dimension_semantics=sem, vmem_limit_bytes=VMEM_LIMIT_BYTES)


def _tile(n, pref, mult):
    if n <= pref:
        return n
    t = (pref // mult) * mult
    while t >= mult:
        if n % t == 0:
            return t
        t -= mult
    return n


MM_VMEM_BUDGET = 40 * 1024 * 1024


def _mm(a, b, *, ta=False, tb=False, out_dtype=F32, name, tm=1024, tn=1152, tk=2048):
    kk, m = (a.shape if ta else a.shape[::-1])
    if tb:
        n, k2 = b.shape
    else:
        k2, n = b.shape
    assert kk == k2, (name, a.shape, b.shape)
    size = lambda dt: jnp.dtype(dt).itemsize
    tn = _tile(n, tn, LANES)
    tk = _tile(kk, tk, LANES if (tb or not ta) else 16)
    nk = kk // tk
    while True:
        tm_ = _tile(m, tm, LANES if ta else 16)
        need = (2 * (tm_ * tk * size(a.dtype) + tk * tn * size(b.dtype) + tm_ * tn * size(out_dtype))
                + (tm_ * tk + tk * tn) * size(MXU_DTYPE) + tm_ * tn * 4 * (2 if nk > 1 else 1))
        if need <= MM_VMEM_BUDGET or tm <= 128:
            break
        tm //= 2
    tm = tm_
    gi, gj = m // tm, n // tn
    j_outer = nk == 1 and b.size * size(b.dtype) * (gi - 1) > a.size * size(a.dtype) * (gj - 1)
    dims = (((0 if ta else 1,), (1 if tb else 0,)), ((), ()))

    def ij(g0, g1):
        return (g1, g0) if j_outer else (g0, g1)

    def body(a_ref, b_ref, o_ref, *scratch):
        av = a_ref[...].astype(MXU_DTYPE)
        bv = b_ref[...].astype(MXU_DTYPE)
        part = lax.dot_general(av, bv, dims, preferred_element_type=F32)
        if nk == 1:
            o_ref[...] = part.astype(out_dtype)
        else:
            acc_ref, = scratch
            k = pl.program_id(2)

            @pl.when(k == 0)
            def _():
                acc_ref[...] = part

            @pl.when(k > 0)
            def _():
                acc_ref[...] += part

            @pl.when(k == nk - 1)
            def _():
                o_ref[...] = acc_ref[...].astype(out_dtype)

    if ta:
        a_spec = pl.BlockSpec((tk, tm), lambda g0, g1, k: (k, ij(g0, g1)[0]))
    else:
        a_spec = pl.BlockSpec((tm, tk), lambda g0, g1, k: (ij(g0, g1)[0], k))
    if tb:
        b_spec = pl.BlockSpec((tn, tk), lambda g0, g1, k: (ij(g0, g1)[1], k))
    else:
        b_spec = pl.BlockSpec((tk, tn), lambda g0, g1, k: (k, ij(g0, g1)[1]))
    return pl.pallas_call(
        body,
        out_shape=jax.ShapeDtypeStruct((m, n), out_dtype),
        grid=(gj, gi, nk) if j_outer else (gi, gj, nk),
        in_specs=[a_spec, b_spec],
        out_specs=pl.BlockSpec((tm, tn), lambda g0, g1, k: ij(g0, g1)),
        scratch_shapes=[] if nk == 1 else [pltpu.VMEM((tm, tn), F32)],
        compiler_params=_cparams(("parallel", "parallel", "arbitrary")),
        name=name,
    )(a, b)


def _rowwise(fn, name, rows, row_in, vec_in, row_out, acc_out=(), tr=256):
    tr = _tile(rows, tr, 16)
    n_in = len(row_in) + len(vec_in)
    n_ro = len(row_out)

    def body(*refs):
        ins = [r[...].astype(F32) for r in refs[:n_in]]
        outs = fn(*ins)
        if not isinstance(outs, (tuple, list)):
            outs = (outs,)
        out_refs = refs[n_in:]
        for o_ref, val in zip(out_refs[:n_ro], outs[:n_ro]):
            o_ref[...] = val.astype(o_ref.dtype)
        if acc_out:
            i = pl.program_id(0)
            for o_ref, val in zip(out_refs[n_ro:], outs[n_ro:]):
                @pl.when(i == 0)
                def _(o_ref=o_ref, val=val):
                    o_ref[...] = val

                @pl.when(i > 0)
                def _(o_ref=o_ref, val=val):
                    o_ref[...] += val

    in_specs = [pl.BlockSpec((tr, w), functools.partial(lambda i, cb: (i, cb), cb=cb)) for (_, w, cb) in row_in]
    in_specs += [pl.BlockSpec((None,) + v.shape[1:], functools.partial(lambda i, l: (l, 0, 0), l=l)) for (v, l) in vec_in]
    out_shape = [jax.ShapeDtypeStruct((rows, w), dt) for (w, dt) in row_out]
    out_shape += [jax.ShapeDtypeStruct(s, F32) for s in acc_out]
    out_specs = [pl.BlockSpec((tr, w), lambda i: (i, 0)) for (w, _) in row_out]
    out_specs += [pl.BlockSpec(s, lambda i: (0, 0)) for s in acc_out]
    res = pl.pallas_call(
        body,
        out_shape=out_shape,
        grid=(rows // tr,),
        in_specs=in_specs,
        out_specs=out_specs,
        compiler_params=_cparams(("arbitrary",) if acc_out else ("parallel",)),
        name=name,
    )(*[a for (a, _, _) in row_in], *[v for (v, _) in vec_in])
    return res


def _rms(x, g):
    r = lax.rsqrt(jnp.mean(x * x, axis=-1, keepdims=True) + RMS_EPS)
    return x * r * g


def _rms_bwd(x, g, dy):
    r = lax.rsqrt(jnp.mean(x * x, axis=-1, keepdims=True) + RMS_EPS)
    xh = x * r
    dxh = dy * g
    dx = r * (dxh - xh * jnp.mean(dxh * xh, axis=-1, keepdims=True))
    return dx, jnp.sum(dy * xh, axis=0, keepdims=True)


def _silu(x):
    return x * jax.nn.sigmoid(x)


def _silu_grad(x):
    s = jax.nn.sigmoid(x)
    return s * (1.0 + x * (1.0 - s))


def _softplus(x):
    return jnp.maximum(x, 0.0) + jnp.log1p(jnp.exp(-jnp.abs(x)))


def _full(a):
    return (a, a.shape[1], 0)


def _f_post_pre(x, ysub, g_post, g_pre):
    xn = x + _rms(ysub, g_post)
    return xn, _rms(xn, g_pre)


def _f_final(x, ysub, tgt, g_post):
    err = x + _rms(ysub, g_post) - tgt
    return err * (1.0 / D_MODEL), jnp.sum(err * err, axis=0, keepdims=True)


def _f_pre_post_bwd(xmid, dh, dxo, ysub, g_pre, g_post):
    d1, dg_pre = _rms_bwd(xmid, g_pre, dh)
    dxm = dxo + d1
    dys, dg_post = _rms_bwd(ysub, g_post, dxm)
    return dxm, dys, dg_pre, dg_post


def _f_pre_bwd(x, dh, dxo, g_pre):
    d1, dg_pre = _rms_bwd(x, g_pre, dh)
    return dxo + d1, dg_pre


def _f_gain_bwd(x, dy, g):
    return _rms_bwd(x, g, dy)[1]


def _group_norm_parts(u):
    gw = u.shape[1] // SSM_GROUPS
    parts = []
    for gi in range(SSM_GROUPS):
        ug = u[:, gi * gw:(gi + 1) * gw]
        r = lax.rsqrt(jnp.mean(ug * ug, axis=-1, keepdims=True) + RMS_EPS)
        parts.append((ug * r, r))
    return gw, parts


def _f_gate_norm(y, z, g):
    _, parts = _group_norm_parts(y * _silu(z))
    return jnp.concatenate([uh for uh, _ in parts], axis=1) * g


def _f_gate_norm_bwd(y, z, do, g):
    sz = _silu(z)
    gw, parts = _group_norm_parts(y * sz)
    dxh = do * g
    du = []
    for gi, (uh, r) in enumerate(parts):
        dg_ = dxh[:, gi * gw:(gi + 1) * gw]
        du.append(r * (dg_ - uh * jnp.mean(dg_ * uh, axis=-1, keepdims=True)))
    du = jnp.concatenate(du, axis=1)
    uh_all = jnp.concatenate([uh for uh, _ in parts], axis=1)
    return du * sz, du * y * _silu_grad(z), jnp.sum(do * uh_all, axis=0, keepdims=True)


def _f_merge(ga, gs, ba, bs):
    return jax.nn.sigmoid(ga) * ba + jax.nn.sigmoid(gs) * bs


def _f_merge_bwd(ga, gs, ba, bs, dm):
    sa, ss = jax.nn.sigmoid(ga), jax.nn.sigmoid(gs)
    dgg = jnp.concatenate([dm * ba * sa * (1.0 - sa), dm * bs * ss * (1.0 - ss)], axis=1)
    return dgg, dm * sa, dm * ss


def _f_swiglu(gate, up):
    return _silu(gate) * up


def _f_swiglu_bwd(gate, up, da):
    return jnp.concatenate([da * up * _silu_grad(gate), da * _silu(gate)], axis=1)


def _split_dot(x, u):
    hi = x.astype(BF16)
    lo = (x - hi.astype(F32)).astype(BF16)
    return jnp.dot(hi, u, preferred_element_type=F32) + jnp.dot(lo, u, preferred_element_type=F32)


SB_ROWS = 512
SB_UNROLL = 4
C00 = (((0,), (0,)), ((), ()))
C11 = (((1,), (1,)), ((), ()))


def _sb_setup(q_ref, tq):
    hp = LANES // SB_HEAD_DIM
    lane = lax.broadcasted_iota(jnp.int32, (1, LANES), 1)
    heads = [jnp.logical_and(lane >= h * SB_HEAD_DIM, lane < (h + 1) * SB_HEAD_DIM) for h in range(hp)]
    qs = q_ref[...] * (SB_HEAD_DIM ** -0.5)
    q_h = [jnp.where(hm, qs, 0.0).astype(MXU_DTYPE) for hm in heads]
    row = lax.broadcasted_iota(jnp.int32, (tq, SB_BLOCK), 0)
    col = lax.broadcasted_iota(jnp.int32, (tq, SB_BLOCK), 1)
    sq_row = lax.broadcasted_iota(jnp.int32, (SB_BLOCK, SB_BLOCK), 0)
    sq_col = lax.broadcasted_iota(jnp.int32, (SB_BLOCK, SB_BLOCK), 1)
    return heads, q_h, col - row, sq_row, sq_col


def _sb_scores(q, kj, mask):
    z = lax.dot_general(q, kj, C11, preferred_element_type=F32)
    lm = -(jnp.maximum(z, 0.0) + jnp.log(1.0 + jnp.exp(-jnp.abs(z))))
    return z, lm if mask is None else jnp.where(mask, lm, 0.0)


def _add_rows(x, r0, upd):
    return x + upd if r0 == 0 else jnp.concatenate([x[:r0], x[r0:] + upd], axis=0)


def _grid_step3(nb, ncb, nq):
    return (pl.program_id(0) * ncb + pl.program_id(1)) * nq + pl.program_id(2), nb * ncb * nq


def _sb_fwd(proj, cfg, nb, gather=(), gather_layers=None):
    blk = SB_BLOCK
    tq = _tile(SEQ, SB_ROWS, blk)
    nq = SEQ // tq
    kpq = tq // blk
    unr = math.gcd(kpq, SB_UNROLL)
    hp = LANES // SB_HEAD_DIM
    ncb = cfg.sbw // LANES
    qb, kb, vb = cfg.q0 // LANES, cfg.k0 // LANES, cfg.v0 // LANES
    ng = len(gather)

    def body(q_ref, k_ref, v_ref, *rest):
        o_ref = rest[ng]
        if ng:
            step_id, n_steps = _grid_step3(nb, ncb, nq)
            start, forward, finish = _gather_phases(rest[:ng], rest[ng + 1:2 * ng + 1], *rest[2 * ng + 1:],
                                                    layers=gather_layers)
            pl.when(step_id == 0)(start)
            pl.when(step_id == (3 * n_steps) // 4)(forward)
        _sb_fwd_block(q_ref, k_ref, v_ref, o_ref)
        if ng:
            pl.when(step_id == n_steps - 1)(finish)

    def _sb_fwd_block(q_ref, k_ref, v_ref, o_ref):
        i = pl.program_id(2)
        heads, q_h, cmr, sq_row, sq_col = _sb_setup(q_ref, tq)
        u_rev = (sq_row >= sq_col).astype(BF16)

        def key_block(j, r0, acc, runs, diagonal):
            rows = pl.ds(pl.multiple_of(j * blk, blk), blk)
            kj = k_ref[rows, :].astype(MXU_DTYPE)
            vj = v_ref[rows, :].astype(MXU_DTYPE)
            mask = cmr[:tq - r0] < 0 if diagonal else None
            for h in range(hp):
                z, lm = _sb_scores(q_h[h][r0:], kj, mask)
                cs = _split_dot(lm, u_rev)
                w = jnp.exp(z + cs + runs[h][r0:])
                if diagonal:
                    w = jnp.where(mask, w, 0.0)
                upd = jnp.dot(w.astype(MXU_DTYPE), jnp.where(heads[h], vj, 0), preferred_element_type=F32)
                acc = _add_rows(acc, r0, upd)
                runs[h] = _add_rows(runs[h], r0, cs[:, 0:1])
            return acc

        acc = jnp.zeros((tq, LANES), F32)
        runs = [jnp.zeros((tq, 1), F32) for _ in range(hp)]
        for r in reversed(range(kpq)):
            acc = key_block(i * kpq + r, r * blk, acc, runs, True)

        def step(n, carry):
            acc, runs = carry
            runs = list(runs)
            for jj in range(unr):
                acc = key_block(i * kpq - 1 - (n * unr + jj), 0, acc, runs, False)
            return acc, tuple(runs)

        acc, _ = lax.fori_loop(0, i * (kpq // unr), step, (acc, tuple(runs)))
        o_ref[...] = acc

    res = pl.pallas_call(
        body,
        out_shape=[jax.ShapeDtypeStruct((nb * SEQ, cfg.sbw), F32)] + _gather_shapes(gather, gather_layers),
        grid=(nb, ncb, nq),
        in_specs=[
            pl.BlockSpec((tq, LANES), lambda b, c, i: (b * nq + i, qb + c)),
            pl.BlockSpec((SEQ, LANES), lambda b, c, i: (b, kb + c)),
            pl.BlockSpec((SEQ, LANES), lambda b, c, i: (b, vb + c)),
        ] + [ANY_SPEC] * ng,
        out_specs=[pl.BlockSpec((tq, LANES), lambda b, c, i: (b * nq + i, c))] + [ANY_SPEC] * ng,
        scratch_shapes=_comm_sems(ng) if ng else [],
        compiler_params=_cparams(("arbitrary",) * 3 if ng else ("parallel", "parallel", "arbitrary")),
        name="sb_fwd_gather" if ng else "sb_fwd",
    )(proj, proj, proj, *gather)
    return res[0], res[1:]


def _sb_bwd(proj, do_att, cfg, nb, scatter=()):
    blk = SB_BLOCK
    tq = _tile(SEQ, SB_ROWS, blk)
    nq = SEQ // tq
    kpq = tq // blk
    unr = math.gcd(kpq, SB_UNROLL)
    hp = LANES // SB_HEAD_DIM
    ncb = cfg.sbw // LANES
    scale = SB_HEAD_DIM ** -0.5
    qb, kb, vb = cfg.q0 // LANES, cfg.k0 // LANES, cfg.v0 // LANES
    ns = len(scatter)

    def body(q_ref, k_ref, v_ref, do_ref, *rest):
        dq_ref, dk_ref, dv_ref = rest[ns:ns + 3]
        g_ref, z_ref = rest[2 * ns + 3:2 * ns + 5]
        if ns:
            step_id, n_steps = _grid_step3(nb, ncb, nq)
            start, finish = _scatter_phases(rest[:ns], rest[ns + 3:2 * ns + 3], *rest[2 * ns + 5:])
            pl.when(step_id == 0)(start)
        _sb_bwd_block(q_ref, k_ref, v_ref, do_ref, dq_ref, dk_ref, dv_ref, g_ref, z_ref)
        if ns:
            pl.when(step_id == n_steps - 1)(finish)

    def _sb_bwd_block(q_ref, k_ref, v_ref, do_ref, dq_ref, dk_ref, dv_ref, g_ref, z_ref):
        i = pl.program_id(2)

        @pl.when(i == 0)
        def _():
            dk_ref[...] = jnp.zeros_like(dk_ref)
            dv_ref[...] = jnp.zeros_like(dv_ref)

        heads, q_h, cmr, sq_row, sq_col = _sb_setup(q_ref, tq)
        u_rev = (sq_row >= sq_col).astype(BF16)
        u_fwd = (sq_row <= sq_col).astype(BF16)
        do = do_ref[...]
        do_h = [jnp.where(hm, do, 0.0).astype(MXU_DTYPE) for hm in heads]


        def left_block(j, r0, runs, diagonal):
            rows = pl.ds(pl.multiple_of(j * blk, blk), blk)
            kj = k_ref[rows, :].astype(MXU_DTYPE)
            vj = v_ref[rows, :].astype(MXU_DTYPE)
            mask = cmr[:tq - r0] < 0 if diagonal else None
            dv = jnp.zeros((blk, LANES), F32)
            for h in range(hp):
                z, lm = _sb_scores(q_h[h][r0:], kj, mask)
                cs = _split_dot(lm, u_rev)
                a = jnp.exp(z + cs + runs[h][r0:])
                if diagonal:
                    a = jnp.where(mask, a, 0.0)
                da = lax.dot_general(do_h[h][r0:], vj, C11, preferred_element_type=F32)
                dv = dv + lax.dot_general(a.astype(MXU_DTYPE), do_h[h][r0:], C00, preferred_element_type=F32)
                g_ref[h, j, r0:, :] = a * da
                z_ref[h, j, r0:, :] = 1.0 - jnp.exp(lm)
                runs[h] = _add_rows(runs[h], r0, cs[:, 0:1])
            dv_ref[rows, :] += dv

        runs = [jnp.zeros((tq, 1), F32) for _ in range(hp)]
        for r in reversed(range(kpq)):
            left_block(i * kpq + r, r * blk, runs, True)

        def sweep_left(n, runs):
            runs = list(runs)
            for jj in range(unr):
                left_block(i * kpq - 1 - (n * unr + jj), 0, runs, False)
            return tuple(runs)

        trips = i * (kpq // unr)
        lax.fori_loop(0, trips, sweep_left, tuple(runs))

        def right_block(j, r0, dq, runs, diagonal):
            rows = pl.ds(pl.multiple_of(j * blk, blk), blk)
            kj = k_ref[rows, :].astype(MXU_DTYPE)
            dk = jnp.zeros((blk, LANES), F32)
            for h in range(hp):
                g = g_ref[h, j, r0:, :]
                g_upto = _split_dot(g, u_fwd) + runs[h][r0:]
                dz = g - z_ref[h, j, r0:, :] * g_upto
                if diagonal:
                    dz = jnp.where(cmr[:tq - r0] < 0, dz, 0.0)
                dz = dz.astype(MXU_DTYPE)
                dq = _add_rows(dq, r0, jnp.dot(dz, jnp.where(heads[h], kj, 0), preferred_element_type=F32))
                dk = dk + lax.dot_general(dz, q_h[h][r0:], C00, preferred_element_type=F32)
                runs[h] = _add_rows(runs[h], r0, jnp.sum(g, axis=1, keepdims=True))
            dk_ref[rows, :] += dk
            return dq

        def sweep_right(n, carry):
            dq, runs = carry
            runs = list(runs)
            for jj in range(unr):
                dq = right_block(n * unr + jj, 0, dq, runs, False)
            return dq, tuple(runs)

        init = (jnp.zeros((tq, LANES), F32), tuple(jnp.zeros((tq, 1), F32) for _ in range(hp)))
        dq, runs = lax.fori_loop(0, trips, sweep_right, init)
        runs = list(runs)
        for r in range(kpq):
            dq = right_block(i * kpq + r, r * blk, dq, runs, True)
        dq_ref[...] = (dq * scale).astype(dq_ref.dtype)

    kv_spec_out = pl.BlockSpec((SEQ, LANES), lambda b, c, i: (b, c))
    q_spec_out = pl.BlockSpec((tq, LANES), lambda b, c, i: (b * nq + i, c))
    res = pl.pallas_call(
        body,
        out_shape=[
            jax.ShapeDtypeStruct((nb * SEQ, cfg.sbw), ACT_DTYPE),
            jax.ShapeDtypeStruct((nb * SEQ, cfg.sbw), F32),
            jax.ShapeDtypeStruct((nb * SEQ, cfg.sbw), F32),
        ] + [jax.ShapeDtypeStruct(a.shape, a.dtype) for a in scatter],
        grid=(nb, ncb, nq),
        in_specs=[
            pl.BlockSpec((tq, LANES), lambda b, c, i: (b * nq + i, qb + c)),
            pl.BlockSpec((SEQ, LANES), lambda b, c, i: (b, kb + c)),
            pl.BlockSpec((SEQ, LANES), lambda b, c, i: (b, vb + c)),
            q_spec_out,
        ] + [ANY_SPEC] * ns,
        out_specs=[q_spec_out, kv_spec_out, kv_spec_out] + [ANY_SPEC] * ns,
        scratch_shapes=[pltpu.VMEM((hp, SEQ // blk, tq, blk), F32), pltpu.VMEM((hp, SEQ // blk, tq, blk), F32)]
        + (_comm_sems(ns) if ns else []),
        compiler_params=_cparams(("arbitrary",) * 3 if ns else ("parallel", "parallel", "arbitrary")),
        name="sb_bwd_scatter" if ns else "sb_bwd",
    )(proj, proj, proj, do_att, *scatter)
    return res[0], res[1], res[2], res[3:]


CONV_COLS = 256


def _conv_pre(x, w, b, t):
    kw = SSM_CONV
    shifted = []
    pre = b + w[kw - 1:kw, :] * x
    for k in range(kw - 1):
        d = kw - 1 - k
        xs = jnp.where(t >= d, pltpu.roll(x, d, 0), 0.0)
        shifted.append(xs)
        pre = pre + w[k:k + 1, :] * xs
    shifted.append(x)
    return pre, shifted


def _conv_fwd(proj, conv_w, conv_b, l, cfg, nb):
    cw = CONV_COLS
    ncb = cfg.conv_dim // cw
    xb = cfg.xbc0 // cw

    def body(x_ref, w_ref, b_ref, o_ref):
        x = x_ref[...]
        t = lax.broadcasted_iota(jnp.int32, x.shape, 0)
        pre, _ = _conv_pre(x, w_ref[...], b_ref[...], t)
        o_ref[...] = _silu(pre)

    return pl.pallas_call(
        body,
        out_shape=jax.ShapeDtypeStruct((nb * SEQ, cfg.conv_dim), F32),
        grid=(ncb, nb),
        in_specs=[
            pl.BlockSpec((SEQ, cw), lambda j, b: (b, xb + j)),
            pl.BlockSpec((None, SSM_CONV, cw), lambda j, b: (0, 0, j)),
            pl.BlockSpec((None, 1, cw), lambda j, b: (l, 0, j)),
        ],
        out_specs=pl.BlockSpec((SEQ, cw), lambda j, b: (b, j)),
        compiler_params=_cparams(("parallel", "parallel")),
        name="conv_fwd",
    )(proj, conv_w, conv_b)


def _conv_bwd(proj, dact, conv_w, conv_b, l, cfg, nb):
    cw = CONV_COLS
    ncb = cfg.conv_dim // cw
    xb = cfg.xbc0 // cw
    kw = SSM_CONV

    def body(x_ref, da_ref, w_ref, b_ref, dx_ref, dw_ref, db_ref):
        b_id = pl.program_id(1)
        x = x_ref[...]
        w = w_ref[...]
        t = lax.broadcasted_iota(jnp.int32, x.shape, 0)
        pre, shifted = _conv_pre(x, w, b_ref[...], t)
        dpre = da_ref[...] * _silu_grad(pre)
        dx = w[kw - 1:kw, :] * dpre
        for k in range(kw - 1):
            d = kw - 1 - k
            dx = dx + w[k:k + 1, :] * jnp.where(t < SEQ - d, pltpu.roll(dpre, SEQ - d, 0), 0.0)
        dx_ref[...] = dx.astype(dx_ref.dtype)
        dw = jnp.concatenate([jnp.sum(dpre * s, axis=0, keepdims=True) for s in shifted], axis=0)
        db = jnp.sum(dpre, axis=0, keepdims=True)

        @pl.when(b_id == 0)
        def _():
            dw_ref[...] = dw
            db_ref[...] = db

        @pl.when(b_id > 0)
        def _():
            dw_ref[...] += dw
            db_ref[...] += db

    return pl.pallas_call(
        body,
        out_shape=[
            jax.ShapeDtypeStruct((nb * SEQ, cfg.conv_dim), ACT_DTYPE),
            jax.ShapeDtypeStruct((kw, cfg.conv_dim), F32),
            jax.ShapeDtypeStruct((1, cfg.conv_dim), F32),
        ],
        grid=(ncb, nb),
        in_specs=[
            pl.BlockSpec((SEQ, cw), lambda j, b: (b, xb + j)),
            pl.BlockSpec((SEQ, cw), lambda j, b: (b, j)),
            pl.BlockSpec((None, kw, cw), lambda j, b: (0, 0, j)),
            pl.BlockSpec((None, 1, cw), lambda j, b: (l, 0, j)),
        ],
        out_specs=[
            pl.BlockSpec((SEQ, cw), lambda j, b: (b, j)),
            pl.BlockSpec((kw, cw), lambda j, b: (0, j)),
            pl.BlockSpec((1, cw), lambda j, b: (0, j)),
        ],
        compiler_params=_cparams(("parallel", "arbitrary")),
        name="conv_bwd",
    )(proj, dact, conv_w, conv_b)


def _ssd_common(dt_raw, dt_bias, a_log, tri):
    ln = SSM_CHUNK
    dt = _softplus(dt_raw + dt_bias)
    a = -jnp.exp(a_log)
    a_cs = jnp.dot(tri, dt * a, preferred_element_type=F32, precision=lax.Precision.HIGHEST)
    a_last = a_cs[ln - 1:ln, :]
    return dt, a, a_cs, a_cs.T, jnp.exp(a_cs), jnp.exp(a_last - a_cs), jnp.exp(a_last)


def _ssd_specs(cfg, nc, rev):
    ln = SSM_CHUNK
    cidx = (lambda c: nc - 1 - c) if rev else (lambda c: c)
    bmb = cfg.inner // cfg.gn
    return [
        pl.BlockSpec((ln, cfg.inner), lambda b, c: (b * nc + cidx(c), 0)),
        pl.BlockSpec((ln, cfg.gn), lambda b, c: (b * nc + cidx(c), bmb)),
        pl.BlockSpec((ln, cfg.gn), lambda b, c: (b * nc + cidx(c), bmb + 1)),
        pl.BlockSpec((ln, LANES), lambda b, c: (b * nc + cidx(c), cfg.dt0 // LANES)),
    ]


def _split3_dot(x, u):
    hi = x.astype(BF16)
    r1 = x - hi.astype(F32)
    mid = r1.astype(BF16)
    lo = (r1 - mid.astype(F32)).astype(BF16)
    dot = lambda a: jnp.dot(a, u, preferred_element_type=F32)
    return dot(hi) + dot(mid) + dot(lo)


def _ssd_consts(cfg):
    p = SSM_HEAD_DIM
    hrow = jnp.arange(LANES)[:, None]
    spread = (jnp.arange(cfg.inner)[None, :] // p == hrow).astype(BF16)
    spread_tile = (jnp.arange(cfg.heads * LANES)[None, :] // LANES == hrow).astype(BF16)
    return spread, spread_tile, spread.T


def _ssd_chunk_terms(dtr, bias, alog, spread, spread_tile):
    ln = SSM_CHUNK
    row = lax.broadcasted_iota(jnp.int32, (ln, ln), 0)
    col = lax.broadcasted_iota(jnp.int32, (ln, ln), 1)
    tril = row >= col
    dt, a, a_cs, a_t, e_a, dte, cd = _ssd_common(dtr, bias, alog, tril.astype(F32))
    ex = lambda v: _split_dot(v, spread)
    cd_x = ex(jnp.broadcast_to(cd, (8, LANES)))[0:1]
    colb = _split_dot(a_cs, spread_tile)
    return dict(tril=tril, row=row, col=col, dt=dt, a=a, a_cs=a_cs, a_t=a_t, e_a=e_a, dte=dte, cd=cd,
                dt_x=ex(dt), ea_x=ex(e_a), dte_x=ex(dte), cd_x=cd_x, colb=colb)


def _head_masks():
    lane = lax.broadcasted_iota(jnp.int32, (1, LANES), 1)
    hpt = LANES // SSM_HEAD_DIM
    return [jnp.logical_and(lane >= i * SSM_HEAD_DIM, lane < (i + 1) * SSM_HEAD_DIM) for i in range(hpt)]


def _ssd_fwd(xbc, proj, dt_bias, a_log, d_skip_x, l, cfg, nb):
    ln, p, n = SSM_CHUNK, SSM_HEAD_DIM, SSM_STATE
    nc = SEQ // ln
    g_, e_ = SSM_GROUPS, cfg.epg
    gw = e_ * p
    hpt = LANES // p
    assert cfg.inner % cfg.gn == 0 and gw % LANES == 0
    spread, spread_tile, _ = _ssd_consts(cfg)

    def body(xs_ref, bm_ref, cm_ref, dtr_ref, bias_ref, alog_ref, dskx_ref, sp_ref, spt_ref, y_ref, st_ref, s_ref):
        c = pl.program_id(1)

        @pl.when(c == 0)
        def _():
            s_ref[...] = jnp.zeros_like(s_ref)

        st_ref[...] = s_ref[...]
        t = _ssd_chunk_terms(dtr_ref[...], bias_ref[...], alog_ref[...], sp_ref[...], spt_ref[...])
        hm = _head_masks()
        xs = xs_ref[...]
        xd = xs * t["dt_x"]
        xde = (xd * t["dte_x"]).astype(MXU_DTYPE)
        for g in range(g_):
            gc = slice(g * gw, (g + 1) * gw)
            bm = bm_ref[:, g * n:(g + 1) * n].astype(MXU_DTYPE)
            cm = cm_ref[:, g * n:(g + 1) * n].astype(MXU_DTYPE)
            cb = lax.dot_general(cm, bm, C11, preferred_element_type=F32)
            sg = s_ref[g * n:(g + 1) * n, :]
            y_off = jnp.dot(cm, sg.astype(MXU_DTYPE), preferred_element_type=F32) * t["ea_x"][:, gc]
            s_ref[g * n:(g + 1) * n, :] = t["cd_x"][:, gc] * sg + lax.dot_general(
                bm, xde[:, gc], C00, preferred_element_type=F32)
            for k in range(gw // LANES):
                lanes = slice(g * gw + k * LANES, g * gw + (k + 1) * LANES)
                xp = xd[:, lanes]
                acc = y_off[:, k * LANES:(k + 1) * LANES] + dskx_ref[:, lanes] * xs[:, lanes]
                for i in range(hpt):
                    h = (g * gw + k * LANES) // p + i
                    lmat = jnp.exp(jnp.where(t["tril"], t["colb"][:, h * LANES:(h + 1) * LANES] - t["a_t"][h:h + 1, :],
                                             -jnp.inf))
                    acc = acc + jnp.dot((cb * lmat).astype(MXU_DTYPE),
                                        jnp.where(hm[i], xp, 0.0).astype(MXU_DTYPE), preferred_element_type=F32)
                y_ref[:, lanes] = acc

    vec = lambda b, c: (l, 0, 0)
    whole = lambda a: pl.BlockSpec(a.shape, lambda b, c: (0, 0))
    return pl.pallas_call(
        body,
        out_shape=[
            jax.ShapeDtypeStruct((nb * SEQ, cfg.inner), F32),
            jax.ShapeDtypeStruct((nb * nc * g_ * n, gw), F32),
        ],
        grid=(nb, nc),
        in_specs=_ssd_specs(cfg, nc, False) + [pl.BlockSpec((None, 1, LANES), vec)] * 2
        + [pl.BlockSpec((None, 1, cfg.inner), vec), whole(spread), whole(spread_tile)],
        out_specs=[
            pl.BlockSpec((ln, cfg.inner), lambda b, c: (b * nc + c, 0)),
            pl.BlockSpec((g_ * n, gw), lambda b, c: (b * nc + c, 0)),
        ],
        scratch_shapes=[pltpu.VMEM((g_ * n, gw), F32)],
        compiler_params=_cparams(("parallel", "arbitrary")),
        name="ssd_fwd",
    )(xbc, xbc, xbc, proj, dt_bias, a_log, d_skip_x, spread, spread_tile)


def _ssd_bwd(xbc, proj, states, dy, dt_bias, a_log, d_skip_x, l, cfg, nb):
    ln, p, n = SSM_CHUNK, SSM_HEAD_DIM, SSM_STATE
    nc = SEQ // ln
    g_, e_ = SSM_GROUPS, cfg.epg
    gw = e_ * p
    hpt = LANES // p
    spread, spread_tile, gather_t = _ssd_consts(cfg)

    def body(xs_ref, bm_ref, cm_ref, dtr_ref, st_ref, dy_ref, bias_ref, alog_ref, dskx_ref, sp_ref, spt_ref, gt_ref,
             dxbc_ref, ddt_ref, dvec_ref, ds_ref, r1_ref, r2_ref, r4_ref, ss_ref):
        first = jnp.logical_and(pl.program_id(0) == 0, pl.program_id(1) == 0)

        @pl.when(pl.program_id(1) == 0)
        def _():
            ds_ref[...] = jnp.zeros_like(ds_ref)

        dtr = dtr_ref[...]
        bias = bias_ref[...]
        t = _ssd_chunk_terms(dtr, bias, alog_ref[...], sp_ref[...], spt_ref[...])
        tril = t["tril"]
        triu = t["row"] <= t["col"]
        hm = _head_masks()
        lane = lax.broadcasted_iota(jnp.int32, (1, LANES), 1)
        sub = lax.broadcasted_iota(jnp.int32, (LANES, 1), 0)
        xs = xs_ref[...]
        dyv = dy_ref[...]
        xd = xs * t["dt_x"]
        xde = xd * t["dte_x"]
        xde_m = xde.astype(MXU_DTYPE)
        dye_m = (dyv * t["ea_x"]).astype(MXU_DTYPE)
        da_col = jnp.zeros((ln, LANES), F32)
        da_row_t = jnp.zeros((LANES, ln), F32)
        ss_ref[...] = jnp.zeros_like(ss_ref)
        for g in range(g_):
            gc = slice(g * gw, (g + 1) * gw)
            gr = slice(g * n, (g + 1) * n)
            bm = bm_ref[:, gr].astype(MXU_DTYPE)
            cm = cm_ref[:, gr].astype(MXU_DTYPE)
            cb = lax.dot_general(cm, bm, C11, preferred_element_type=F32)
            cb_t = lax.dot_general(bm, cm, C11, preferred_element_type=F32)
            sp = st_ref[gr, :]
            dsg = ds_ref[gr, :]
            sp_m = sp.astype(MXU_DTYPE)
            dsg_m = dsg.astype(MXU_DTYPE)
            bds = jnp.dot(bm, dsg_m, preferred_element_type=F32)
            y_off = jnp.dot(cm, sp_m, preferred_element_type=F32) * t["ea_x"][:, gc]
            dcm = lax.dot_general(dye_m[:, gc], sp_m, C11, preferred_element_type=F32)
            dbm = lax.dot_general(xde_m[:, gc], dsg_m, C11, preferred_element_type=F32)
            ds_ref[gr, :] = t["cd_x"][:, gc] * dsg + lax.dot_general(cm, dye_m[:, gc], C00, preferred_element_type=F32)
            r4 = bds * xde[:, gc]
            r4_ref[:, gc] = r4
            r1_ref[:, gc] = dyv[:, gc] * y_off - r4
            ss_ref[0:1, gc] = jnp.sum(dsg * sp, axis=0, keepdims=True)
            dcb = jnp.zeros((ln, ln), F32)
            for k in range(gw // LANES):
                lanes = slice(g * gw + k * LANES, g * gw + (k + 1) * LANES)
                xp = xd[:, lanes]
                xp_m = xp.astype(MXU_DTYPE)
                dyp = dyv[:, lanes]
                dxp = t["dte_x"][:, lanes] * bds[:, k * LANES:(k + 1) * LANES]
                for i in range(hpt):
                    h = (g * gw + k * LANES) // p + i
                    diff = t["colb"][:, h * LANES:(h + 1) * LANES] - t["a_t"][h:h + 1, :]
                    lmat = jnp.exp(jnp.where(tril, diff, -jnp.inf))
                    lmat_t = jnp.exp(jnp.where(triu, -diff, -jnp.inf))
                    dy_h = jnp.where(hm[i], dyp, 0.0).astype(MXU_DTYPE)
                    d_ml = lax.dot_general(dy_h, xp_m, C11, preferred_element_type=F32) * lmat
                    dcb = dcb + d_ml
                    w_mat = d_ml * cb
                    dxp = dxp + jnp.dot((cb_t * lmat_t).astype(MXU_DTYPE), dy_h, preferred_element_type=F32)
                    da_col = da_col + jnp.sum(w_mat, axis=1, keepdims=True) * (lane == h).astype(F32)
                    da_row_t = da_row_t - (sub == h).astype(F32) * jnp.sum(w_mat, axis=0, keepdims=True)
                dxbc_ref[:, lanes] = dxp * t["dt_x"][:, lanes] + dskx_ref[:, lanes] * dyp
                r2_ref[:, lanes] = dxp * xs[:, lanes]
            dcb_m = dcb.astype(MXU_DTYPE)
            dxbc_ref[:, cfg.inner + g * n:cfg.inner + (g + 1) * n] = dbm + lax.dot_general(
                dcb_m, cm, C00, preferred_element_type=F32)
            dxbc_ref[:, cfg.inner + cfg.gn + g * n:cfg.inner + cfg.gn + (g + 1) * n] = dcm + jnp.dot(
                dcb_m, bm, preferred_element_type=F32)
        gt = gt_ref[...]
        rd = lambda v: _split_dot(v, gt)
        red4 = rd(r4_ref[...])
        da_last = jnp.sum(red4, axis=0, keepdims=True) + t["cd"] * rd(ss_ref[...])[0:1]
        d_acs = da_col + rd(r1_ref[...]) + da_row_t.T + jnp.where(t["row"][:, 0:1] == ln - 1, da_last, 0.0)
        da_dt = lax.dot_general(tril.astype(F32), d_acs, C00, preferred_element_type=F32,
                                precision=lax.Precision.HIGHEST)
        ddt = rd(r2_ref[...]) + da_dt * t["a"]
        ddt_raw = ddt * jax.nn.sigmoid(dtr + bias)
        ddt_ref[...] = ddt_raw.astype(ddt_ref.dtype)
        da_log = jnp.sum(da_dt * t["dt"], axis=0, keepdims=True) * t["a"]
        ddsk = jnp.sum(rd(dyv * xs), axis=0, keepdims=True)
        dvec = jnp.concatenate([jnp.sum(ddt_raw, axis=0, keepdims=True), da_log, ddsk,
                                jnp.zeros((5, LANES), F32)], axis=0)

        @pl.when(first)
        def _():
            dvec_ref[...] = dvec

        @pl.when(jnp.logical_not(first))
        def _():
            dvec_ref[...] += dvec

    vec = lambda b, c: (l, 0, 0)
    rblk = lambda b, c: (b * nc + nc - 1 - c, 0)
    whole = lambda a: pl.BlockSpec(a.shape, lambda b, c: (0, 0))
    return pl.pallas_call(
        body,
        out_shape=[
            jax.ShapeDtypeStruct((nb * SEQ, cfg.conv_dim), F32),
            jax.ShapeDtypeStruct((nb * SEQ, LANES), ACT_DTYPE),
            jax.ShapeDtypeStruct((8, LANES), F32),
        ],
        grid=(nb, nc),
        in_specs=_ssd_specs(cfg, nc, True) + [
            pl.BlockSpec((g_ * n, gw), rblk),
            pl.BlockSpec((ln, cfg.inner), rblk),
        ] + [pl.BlockSpec((None, 1, LANES), vec)] * 2 + [pl.BlockSpec((None, 1, cfg.inner), vec),
                                                           whole(spread), whole(spread_tile), whole(gather_t)],
        out_specs=[
            pl.BlockSpec((ln, cfg.conv_dim), rblk),
            pl.BlockSpec((ln, LANES), rblk),
            pl.BlockSpec((8, LANES), lambda b, c: (0, 0)),
        ],
        scratch_shapes=[pltpu.VMEM((g_ * n, gw), F32)] + [pltpu.VMEM((ln, cfg.inner), F32)] * 3
        + [pltpu.VMEM((8, cfg.inner), F32)],
        compiler_params=_cparams(("arbitrary", "arbitrary")),
        name="ssd_bwd",
    )(xbc, xbc, xbc, proj, states, dy, dt_bias, a_log, d_skip_x, spread, spread_tile, gather_t)


XA_ROWS = 256


def _xa_probs(q_ref, kv_ref, h, dh):
    c11 = (((1,), (1,)), ((), ()))
    qh = q_ref[:, h * dh:(h + 1) * dh].astype(MXU_DTYPE)
    kh = kv_ref[:, h * dh:(h + 1) * dh].astype(MXU_DTYPE)
    vh = kv_ref[:, D_MODEL + h * dh:D_MODEL + (h + 1) * dh].astype(MXU_DTYPE)
    s = lax.dot_general(qh, kh, c11, preferred_element_type=F32) * (dh ** -0.5)
    s = s - jnp.max(s, axis=1, keepdims=True)
    pr = jnp.exp(s)
    return qh, kh, vh, pr / jnp.sum(pr, axis=1, keepdims=True)


def _xa_fwd(q, kv, cfg, nb):
    tq = _tile(SEQ, XA_ROWS, 16)
    nq = SEQ // tq
    dh = cfg.xa_dim

    def body(q_ref, kv_ref, o_ref):
        for h in range(XA_HEADS):
            _, _, vh, pr = _xa_probs(q_ref, kv_ref, h, dh)
            o_ref[:, h * dh:(h + 1) * dh] = jnp.dot(pr.astype(MXU_DTYPE), vh, preferred_element_type=F32).astype(o_ref.dtype)

    return pl.pallas_call(
        body,
        out_shape=jax.ShapeDtypeStruct((nb * SEQ, D_MODEL), ACT_DTYPE),
        grid=(nb, nq),
        in_specs=[
            pl.BlockSpec((tq, D_MODEL), lambda b, i: (b * nq + i, 0)),
            pl.BlockSpec((MEM_LEN, 2 * D_MODEL), lambda b, i: (b, 0)),
        ],
        out_specs=pl.BlockSpec((tq, D_MODEL), lambda b, i: (b * nq + i, 0)),
        compiler_params=_cparams(("parallel", "parallel")),
        name="xa_fwd",
    )(q, kv)


def _xa_bwd(q, kv, do, cfg, nb):
    tq = _tile(SEQ, XA_ROWS, 16)
    nq = SEQ // tq
    dh = cfg.xa_dim
    c00 = (((0,), (0,)), ((), ()))
    c11 = (((1,), (1,)), ((), ()))
    scale = dh ** -0.5

    def body(q_ref, kv_ref, do_ref, dq_ref, dkv_ref, acc_ref):
        i = pl.program_id(1)

        @pl.when(i == 0)
        def _():
            acc_ref[...] = jnp.zeros_like(acc_ref)

        for h in range(XA_HEADS):
            hs = slice(h * dh, (h + 1) * dh)
            vs = slice(D_MODEL + h * dh, D_MODEL + (h + 1) * dh)
            qh, kh, vh, pr = _xa_probs(q_ref, kv_ref, h, dh)
            do_h = do_ref[:, hs].astype(MXU_DTYPE)
            dp = lax.dot_general(do_h, vh, c11, preferred_element_type=F32)
            ds = (pr * (dp - jnp.sum(dp * pr, axis=1, keepdims=True))).astype(MXU_DTYPE)
            dq_ref[:, hs] = (jnp.dot(ds, kh, preferred_element_type=F32) * scale).astype(dq_ref.dtype)
            acc_ref[:, hs] += lax.dot_general(ds, qh, c00, preferred_element_type=F32) * scale
            acc_ref[:, vs] += lax.dot_general(pr.astype(MXU_DTYPE), do_h, c00, preferred_element_type=F32)

        @pl.when(i == nq - 1)
        def _():
            dkv_ref[...] = acc_ref[...].astype(dkv_ref.dtype)

    return pl.pallas_call(
        body,
        out_shape=[
            jax.ShapeDtypeStruct((nb * SEQ, D_MODEL), ACT_DTYPE),
            jax.ShapeDtypeStruct((nb * MEM_LEN, 2 * D_MODEL), ACT_DTYPE),
        ],
        grid=(nb, nq),
        in_specs=[
            pl.BlockSpec((tq, D_MODEL), lambda b, i: (b * nq + i, 0)),
            pl.BlockSpec((MEM_LEN, 2 * D_MODEL), lambda b, i: (b, 0)),
            pl.BlockSpec((tq, D_MODEL), lambda b, i: (b * nq + i, 0)),
        ],
        out_specs=[
            pl.BlockSpec((tq, D_MODEL), lambda b, i: (b * nq + i, 0)),
            pl.BlockSpec((MEM_LEN, 2 * D_MODEL), lambda b, i: (b, 0)),
        ],
        scratch_shapes=[pltpu.VMEM((MEM_LEN, 2 * D_MODEL), F32)],
        compiler_params=_cparams(("parallel", "arbitrary")),
        name="xa_bwd",
    )(q, kv, do)


def _adamw(parts, w, m, v, name, tr=128):
    n, nl, r, c = parts.shape
    tr = _tile(r, tr, 16)

    def body(p_ref, w_ref, m_ref, v_ref, g_ref, d_ref, nm_ref, nv_ref):
        g = p_ref[0].astype(F32)
        for i in range(1, n):
            g = g + p_ref[i].astype(F32)
        m2 = ADAM_B1 * m_ref[...] + (1.0 - ADAM_B1) * g
        v2 = ADAM_B2 * v_ref[...] + (1.0 - ADAM_B2) * (g * g)
        m_hat = m2 / (1.0 - ADAM_B1 ** ADAM_STEP)
        v_hat = v2 / (1.0 - ADAM_B2 ** ADAM_STEP)
        g_ref[...] = g
        d_ref[...] = -ADAM_LR * (m_hat / (jnp.sqrt(v_hat) + ADAM_EPS) + ADAM_WD * w_ref[...])
        nm_ref[...] = m2
        nv_ref[...] = v2

    blk = pl.BlockSpec((None, tr, c), lambda l, i: (l, i, 0))
    return pl.pallas_call(
        body,
        out_shape=[jax.ShapeDtypeStruct((nl, r, c), F32)] * 4,
        grid=(nl, r // tr),
        in_specs=[pl.BlockSpec((n, None, tr, c), lambda l, i: (0, l, i, 0)), blk, blk, blk],
        out_specs=[blk] * 4,
        compiler_params=_cparams(("parallel", "parallel")),
        name=name,
    )(parts, w, m, v)


def _flat_index(px, py, pc):
    return 4 * px + 2 * py + pc


def _all_gather(arrs, name, layers=None):
    n = len(arrs)

    def body(*refs):
        start, forward, finish = _gather_phases(refs[:n], refs[n:2 * n], *refs[2 * n:], layers=layers)
        start()
        forward()
        finish()

    return pl.pallas_call(
        body,
        out_shape=_gather_shapes(arrs, layers),
        in_specs=[ANY_SPEC] * n,
        out_specs=[ANY_SPEC] * n,
        scratch_shapes=_comm_sems(n),
        name=name,
    )(*arrs)


ANY_SPEC = pl.BlockSpec(memory_space=pl.ANY)


def _comm_sems(n):
    return [pltpu.SemaphoreType.DMA((n, N_DEV - 1)), pltpu.SemaphoreType.DMA((n, N_DEV - 1)),
            pltpu.SemaphoreType.DMA((n,))]


def _gather_shapes(arrs, layers=None):
    layers = layers or [None] * len(arrs)
    return [jax.ShapeDtypeStruct((N_DEV,) + (a.shape if lay is None else a.shape[1:]), a.dtype)
            for a, lay in zip(arrs, layers)]


def _gather_phases(ins, outs, send_sems, recv_sems, local_sems, layers=None):
    n = len(ins)
    if layers is not None:
        ins = [r if lay is None else r.at[lay] for r, lay in zip(ins, layers)]
    x, y, c = lax.axis_index("x"), lax.axis_index("y"), lax.axis_index("c")
    me, sibling = (x, y, c), (x, y, 1 - c)
    chips = [(1 - x, y), (x, 1 - y), (1 - x, 1 - y)]

    def copy(a, k, block, to, src=None):
        slot = outs[a].at[_flat_index(*block)]
        return pltpu.make_async_remote_copy(
            src_ref=slot if src is None else src, dst_ref=slot,
            send_sem=send_sems.at[a, k], recv_sem=recv_sems.at[a, k],
            device_id=to, device_id_type=MESH)

    def mine(a):
        return pltpu.make_async_copy(ins[a], outs[a].at[_flat_index(*me)], local_sems.at[a])

    def first(a):
        return [copy(a, 0, me, sibling, src=ins[a])] + [
            copy(a, 1 + j, me, (*chip, c), src=ins[a]) for j, chip in enumerate(chips)]

    def start():
        for a in range(n):
            mine(a).start()
            for cp in first(a):
                cp.start()

    def forward():
        for j, chip in enumerate(chips):
            for a in range(n):
                copy(a, 1 + j, (*chip, c), me).wait_recv()
                copy(a, 4 + j, (*chip, c), sibling).start()

    def finish():
        for a in range(n):
            copy(a, 0, sibling, me).wait_recv()
            for j, chip in enumerate(chips):
                copy(a, 4 + j, (*chip, 1 - c), me).wait_recv()
        for a in range(n):
            for cp in first(a):
                cp.wait_send()
            for j, chip in enumerate(chips):
                copy(a, 4 + j, (*chip, c), sibling).wait_send()
            mine(a).wait()

    return start, forward, finish


def _scatter_blocks(arrs, name):
    n = len(arrs)

    def body(*refs):
        start, finish = _scatter_phases(refs[:n], refs[n:2 * n], *refs[2 * n:])
        start()
        finish()

    return pl.pallas_call(
        body,
        out_shape=[jax.ShapeDtypeStruct(a.shape, a.dtype) for a in arrs],
        in_specs=[ANY_SPEC] * n,
        out_specs=[ANY_SPEC] * n,
        scratch_shapes=_comm_sems(n),
        name=name,
    )(*arrs)


def _scatter_phases(ins, outs, send_sems, recv_sems, local_sems):
    n = len(ins)
    x, y, c = lax.axis_index("x"), lax.axis_index("y"), lax.axis_index("c")
    me = _flat_index(x, y, c)

    def peer(k):
        return (1 - x if k & 4 else x, 1 - y if k & 2 else y, 1 - c if k & 1 else c)

    def copy(a, k):
        p = peer(k)
        return pltpu.make_async_remote_copy(
            src_ref=ins[a].at[_flat_index(*p)], dst_ref=outs[a].at[me],
            send_sem=send_sems.at[a, k - 1], recv_sem=recv_sems.at[a, k - 1],
            device_id=p, device_id_type=MESH)

    def landed(a, k):
        slot = outs[a].at[_flat_index(*peer(k))]
        return pltpu.make_async_remote_copy(
            src_ref=slot, dst_ref=slot, send_sem=send_sems.at[a, k - 1], recv_sem=recv_sems.at[a, k - 1],
            device_id=peer(k), device_id_type=MESH)

    def mine(a):
        return pltpu.make_async_copy(ins[a].at[me], outs[a].at[me], local_sems.at[a])

    def start():
        for a in range(n):
            mine(a).start()
            for k in range(1, N_DEV):
                copy(a, k).start()

    def finish():
        for a in range(n):
            for k in range(1, N_DEV):
                landed(a, k).wait_recv()
        for a in range(n):
            for k in range(1, N_DEV):
                copy(a, k).wait_send()
            mine(a).wait()

    return start, finish


_BIG = ("w_in", "w_br_att", "w_br_ssm", "w_mix_out", "w_xq", "w_xkv", "w_xo", "w_gu", "w_down")
_COL_SHARDED = ("w_in", "w_xkv", "w_gu", "conv_w")
_SMALL = ("g_pre_mix", "conv_b", "dt_bias", "a_log", "d_skip", "g_ssm_norm", "g_post_mix", "g_pre_xa",
          "g_mem", "g_post_xa", "g_pre_ffn", "g_post_ffn")
_WEIGHTS = ("g_pre_mix", "w_in", "conv_w", "conv_b", "dt_bias", "a_log", "d_skip", "g_ssm_norm", "w_br_att",
            "w_br_ssm", "w_mix_out", "g_post_mix", "g_pre_xa", "g_mem", "w_xq", "w_xkv", "w_xo", "g_post_xa",
            "g_pre_ffn", "w_gu", "w_down", "g_post_ffn")
PACK_W = 8 * LANES


def _unshard(g, col):
    n, r, c = g.shape
    if col:
        return jnp.transpose(g, (1, 0, 2)).reshape(r, n * c)
    return g.reshape(n * r, c)


def _shard(w, col):
    r, c = w.shape
    if col:
        return jnp.transpose(w.reshape(r, N_DEV, c // N_DEV), (1, 0, 2))
    return w.reshape(N_DEV, r // N_DEV, c)


def _permute_in(w, cfg):
    parts, off = [], 0
    for size in cfg.in_sizes:
        parts.append(w[..., off:off + size])
        off += size
    q, k, v, z, xbc, dt, ga, gs = parts
    pad = jnp.zeros(w.shape[:-1] + (LANES - cfg.heads,), w.dtype)
    return jnp.concatenate([z, ga, gs, q, k, v, xbc, dt, pad], axis=-1)


def _unpermute_in(w, cfg):
    c = cfg
    sl = lambda a, n: w[..., a:a + n]
    return jnp.concatenate([sl(c.q0, c.sbw), sl(c.k0, c.sbw), sl(c.v0, c.sbw), sl(c.z0, c.inner),
                            sl(c.xbc0, c.conv_dim), sl(c.dt0, c.heads), sl(c.ga0, c.d), sl(c.gs0, c.d)], axis=-1)


def _pack(arrs):
    flat = jnp.concatenate([a.reshape(-1).astype(F32) for a in arrs])
    rows = -(-flat.shape[0] // PACK_W)
    rows = -(-rows // 8) * 8
    return jnp.pad(flat, (0, rows * PACK_W - flat.shape[0])).reshape(rows, PACK_W)


def _unpack(p, shapes):
    flat = p.reshape(-1)
    out, off = [], 0
    for s in shapes:
        size = math.prod(s)
        out.append(flat[off:off + size].reshape(s))
        off += size
    return out


def _vec3(a, width=None):
    if width is not None and a.shape[1] < width:
        a = jnp.pad(a, ((0, 0), (0, width - a.shape[1])))
    return a[:, None, :]


def _forward_layer(l, xin, h1, memf, tgt, w, p, cfg, nb, last, gather, n_late=0, complete=None):
    t = xin.shape[0]
    d = cfg.d
    s = {"x_in": xin, "h1": h1}
    proj = _mm(h1, w["w_in"], name="mm_proj")
    s["proj"] = proj
    s["o_att"], gathered = _sb_fwd(proj, cfg, nb, *gather)
    if n_late:
        w.update(complete(gathered[:n_late]))
        gathered = gathered[n_late:]
    s["xbc"] = _conv_fwd(proj, w["conv_w"], p["conv_b"], l, cfg, nb)
    s["y"], s["states"] = _ssd_fwd(s["xbc"], proj, p["dt_bias"], p["a_log"], p["d_skip_x"], l, cfg, nb)
    s["o_ssm"] = _rowwise(_f_gate_norm, "gate_norm_fwd", t, [_full(s["y"]), (proj, cfg.inner, 0)],
                          [(p["g_ssm_norm"], l)], [(cfg.inner, ACT_DTYPE)])[0]
    s["ba"] = _mm(s["o_att"], w["w_br_att"], name="mm_br_att")
    s["bs"] = _mm(s["o_ssm"], w["w_br_ssm"], name="mm_br_ssm")
    s["merged"] = _rowwise(_f_merge, "merge_fwd", t,
                           [(proj, d, cfg.ga0 // d), (proj, d, cfg.gs0 // d), _full(s["ba"]), _full(s["bs"])],
                           [], [(d, ACT_DTYPE)])[0]
    s["mo"] = _mm(s["merged"], w["w_mix_out"], name="mm_mix_out")
    s["x1"], s["h2"] = _rowwise(_f_post_pre, "post_pre_mix", t, [_full(xin), _full(s["mo"])],
                                [(p["g_post_mix"], l), (p["g_pre_xa"], l)], [(d, F32), (d, ACT_DTYPE)])
    s["mem_n"] = _rowwise(_rms, "mem_norm", memf.shape[0], [_full(memf)], [(p["g_mem"], l)], [(d, ACT_DTYPE)])[0]
    s["q"] = _mm(s["h2"], w["w_xq"], name="mm_xq")
    s["kv"] = _mm(s["mem_n"], w["w_xkv"], name="mm_xkv")
    s["o_xa"] = _xa_fwd(s["q"], s["kv"], cfg, nb)
    s["xo"] = _mm(s["o_xa"], w["w_xo"], name="mm_xo")
    s["x2"], s["h3"] = _rowwise(_f_post_pre, "post_pre_xa", t, [_full(s["x1"]), _full(s["xo"])],
                                [(p["g_post_xa"], l), (p["g_pre_ffn"], l)], [(d, F32), (d, ACT_DTYPE)])
    s["gu"] = _mm(s["h3"], w["w_gu"], name="mm_gu")
    s["act"] = _rowwise(_f_swiglu, "swiglu_fwd", t, [(s["gu"], cfg.ffn, 0), (s["gu"], cfg.ffn, 1)], [],
                        [(cfg.ffn, ACT_DTYPE)])[0]
    s["dn"] = _mm(s["act"], w["w_down"], name="mm_down")
    if last:
        nxt = _rowwise(_f_final, "final_loss", t, [_full(s["x2"]), _full(s["dn"]), _full(tgt)],
                       [(p["g_post_ffn"], l)], [(d, F32)], acc_out=[(1, d)])
    else:
        nxt = _rowwise(_f_post_pre, "post_pre_ffn", t, [_full(s["x2"]), _full(s["dn"])],
                       [(p["g_post_ffn"], l), (p["g_pre_mix"], l + 1)], [(d, F32), (d, ACT_DTYPE)])
    return s, nxt, gathered


def _backward_layer(l, s, dx, d_dn, memf, w, p, cfg, nb, prev_dn, scatter):
    t = dx.shape[0]
    d = cfg.d
    g = {}
    dact = _mm(d_dn, w["w_down"], tb=True, out_dtype=ACT_DTYPE, name="mm_d_act", tn=1408)
    g["w_down"] = _mm(s["act"], d_dn, ta=True, name="mm_dw_down")
    dgu = _rowwise(_f_swiglu_bwd, "swiglu_bwd", t, [(s["gu"], cfg.ffn, 0), (s["gu"], cfg.ffn, 1), _full(dact)], [],
                   [(2 * cfg.ffn, ACT_DTYPE)])[0]
    dh3 = _mm(dgu, w["w_gu"], tb=True, name="mm_d_h3", tk=1408)
    g["w_gu"] = _mm(s["h3"], dgu, ta=True, name="mm_dw_gu")
    dx2, d_xo, g["g_pre_ffn"], g["g_post_xa"] = _rowwise(
        _f_pre_post_bwd, "pre_post_bwd_ffn", t, [_full(s["x2"]), _full(dh3), _full(dx), _full(s["xo"])],
        [(p["g_pre_ffn"], l), (p["g_post_xa"], l)], [(d, F32), (d, ACT_DTYPE)], acc_out=[(1, d), (1, d)])
    do_xa = _mm(d_xo, w["w_xo"], tb=True, out_dtype=ACT_DTYPE, name="mm_d_oxa")
    g["w_xo"] = _mm(s["o_xa"], d_xo, ta=True, name="mm_dw_xo")
    dq, dkv = _xa_bwd(s["q"], s["kv"], do_xa, cfg, nb)
    dh2 = _mm(dq, w["w_xq"], tb=True, name="mm_d_h2")
    g["w_xq"] = _mm(s["h2"], dq, ta=True, name="mm_dw_xq")
    dmem_n = _mm(dkv, w["w_xkv"], tb=True, name="mm_d_mem")
    g["w_xkv"] = _mm(s["mem_n"], dkv, ta=True, name="mm_dw_xkv")
    g["g_mem"] = _rowwise(_f_gain_bwd, "mem_norm_bwd", memf.shape[0], [_full(memf), _full(dmem_n)],
                          [(p["g_mem"], l)], [], acc_out=[(1, d)])[0]
    dx1, d_mo, g["g_pre_xa"], g["g_post_mix"] = _rowwise(
        _f_pre_post_bwd, "pre_post_bwd_xa", t, [_full(s["x1"]), _full(dh2), _full(dx2), _full(s["mo"])],
        [(p["g_pre_xa"], l), (p["g_post_mix"], l)], [(d, F32), (d, ACT_DTYPE)], acc_out=[(1, d), (1, d)])
    dmerged = _mm(d_mo, w["w_mix_out"], tb=True, out_dtype=ACT_DTYPE, name="mm_d_merged")
    g["w_mix_out"] = _mm(s["merged"], d_mo, ta=True, name="mm_dw_mix_out")
    proj = s["proj"]
    dgg, dba, dbs = _rowwise(
        _f_merge_bwd, "merge_bwd", t,
        [(proj, d, cfg.ga0 // d), (proj, d, cfg.gs0 // d), _full(s["ba"]), _full(s["bs"]), _full(dmerged)], [],
        [(2 * d, ACT_DTYPE), (d, ACT_DTYPE), (d, ACT_DTYPE)])
    do_att = _mm(dba, w["w_br_att"], tb=True, name="mm_d_oatt")
    g["w_br_att"] = _mm(s["o_att"], dba, ta=True, name="mm_dw_br_att")
    do_ssm = _mm(dbs, w["w_br_ssm"], tb=True, out_dtype=ACT_DTYPE, name="mm_d_ossm")
    g["w_br_ssm"] = _mm(s["o_ssm"], dbs, ta=True, name="mm_dw_br_ssm")
    dy, dz, g["g_ssm_norm"] = _rowwise(
        _f_gate_norm_bwd, "gate_norm_bwd", t, [_full(s["y"]), (proj, cfg.inner, 0), _full(do_ssm)],
        [(p["g_ssm_norm"], l)], [(cfg.inner, F32), (cfg.inner, ACT_DTYPE)], acc_out=[(1, cfg.inner)])
    dxbc, ddt_raw, dvec = _ssd_bwd(s["xbc"], proj, s["states"], dy, p["dt_bias"], p["a_log"], p["d_skip_x"], l, cfg, nb)
    g["dt_bias"], g["a_log"], g["d_skip"] = (dvec[i:i + 1, :cfg.heads] for i in range(3))
    dxbc_raw, g["conv_w"], g["conv_b"] = _conv_bwd(proj, dxbc, w["conv_w"], p["conv_b"], l, cfg, nb)
    own = [_shard(g[n], n in _COL_SHARDED).astype(WIRE_DTYPE) for n in _BIG if n != "w_in"]
    dq_sb, dk_sb, dv_sb, landed = _sb_bwd(proj, do_att, cfg, nb, list(scatter) + own)
    landed = (landed[:len(scatter)], landed[len(scatter):])
    dproj = jnp.concatenate([dz, dgg, dq_sb, dk_sb.astype(ACT_DTYPE), dv_sb.astype(ACT_DTYPE), dxbc_raw, ddt_raw], axis=1)
    dh1 = _mm(dproj, w["w_in"], tb=True, name="mm_d_h1", tk=1152)
    g["w_in"] = _mm(s["h1"], dproj, ta=True, name="mm_dw_in")
    if prev_dn is None:
        dx0, g["g_pre_mix"] = _rowwise(_f_pre_bwd, "pre_bwd_first", t, [_full(s["x_in"]), _full(dh1), _full(dx1)],
                                       [(p["g_pre_mix"], l)], [(d, F32)], acc_out=[(1, d)])
        return g, dx0, None, None, landed
    dx0, d_dn_prev, g["g_pre_mix"], g_post_prev = _rowwise(
        _f_pre_post_bwd, "pre_post_bwd_mix", t, [_full(s["x_in"]), _full(dh1), _full(dx1), _full(prev_dn)],
        [(p["g_pre_mix"], l), (p["g_post_ffn"], l - 1)], [(d, F32), (d, ACT_DTYPE)], acc_out=[(1, d), (1, d)])
    return g, dx0, d_dn_prev, g_post_prev, landed


def kernel(x, mem, g_pre_mix, w_in, conv_w, conv_b, dt_bias, a_log, d_skip, g_ssm_norm, w_br_att, w_br_ssm, w_mix_out, g_post_mix, g_pre_xa, g_mem, w_xq, w_xkv, w_xo, g_post_xa, g_pre_ffn, w_gu, w_down, g_post_ffn, loss_target, m_g_pre_mix, m_w_in, m_conv_w, m_conv_b, m_dt_bias, m_a_log, m_d_skip, m_g_ssm_norm, m_w_br_att, m_w_br_ssm, m_w_mix_out, m_g_post_mix, m_g_pre_xa, m_g_mem, m_w_xq, m_w_xkv, m_w_xo, m_g_post_xa, m_g_pre_ffn, m_w_gu, m_w_down, m_g_post_ffn, v_g_pre_mix, v_w_in, v_conv_w, v_conv_b, v_dt_bias, v_a_log, v_d_skip, v_g_ssm_norm, v_w_br_att, v_w_br_ssm, v_w_mix_out, v_g_post_mix, v_g_pre_xa, v_g_mem, v_w_xq, v_w_xkv, v_w_xo, v_g_post_xa, v_g_pre_ffn, v_w_gu, v_w_down, v_g_post_ffn):
    vals = dict(locals())
    cfg = _Cfg()
    nb = x.shape[0]
    t = nb * SEQ
    d = cfg.d
    depth = g_pre_mix.shape[0]

    gathered_names = _BIG + ("conv_w",)
    late_names = gathered_names[1:]

    on_wire = {n: conv_w if n == "conv_w" else vals[n].astype(WIRE_DTYPE) for n in gathered_names}

    def wire(l, names=gathered_names):
        return [on_wire[n] for n in names], [l] * len(names)

    def layer_weights(gathered, names=gathered_names):
        w = {n: _unshard(gw, n in _COL_SHARDED) for n, gw in zip(names, gathered)}
        if "w_in" in w:
            w["w_in"] = _permute_in(w["w_in"], cfg)
        if "conv_w" in w:
            w["conv_w"] = w["conv_w"][None]
        return w

    p = {n: _vec3(vals[n], LANES if n in ("dt_bias", "a_log", "d_skip") else None) for n in _SMALL}
    p["d_skip_x"] = _vec3(jnp.repeat(d_skip, SSM_HEAD_DIM, axis=1))
    weights = [None] * depth
    first_arrs, first_layers = wire(0, ("w_in",))
    weights[0] = layer_weights(_all_gather(first_arrs, "ag_weights_first", first_layers), ("w_in",))

    xf = x.reshape(t, d)
    memf = mem.reshape(nb * MEM_LEN, d)
    tgt = loss_target.reshape(t, d)
    h = _rowwise(_rms, "pre_norm_first", t, [_full(xf)], [(p["g_pre_mix"], 0)], [(d, ACT_DTYPE)])[0]
    saved = []
    xcur = xf
    for l in range(depth):
        last = l == depth - 1
        late = wire(0, late_names) if l == 0 else ([], [])
        ahead = ([], []) if last else wire(l + 1)
        complete = (lambda got: layer_weights(got, late_names)) if l == 0 else None
        s, nxt, gathered = _forward_layer(l, xcur, h, memf, tgt, weights[l], p, cfg, nb, last,
                                          (late[0] + ahead[0], late[1] + ahead[1]), len(late[0]), complete)
        saved.append(s)
        if not last:
            weights[l + 1] = layer_weights(gathered)
            xcur, h = nxt
    dx, loss_row = nxt
    loss = lax.psum(0.5 * jnp.sum(loss_row) / d, AXES)

    top = saved[-1]
    d_dn, g_post_top = _rowwise(lambda ysub, dxo, gp: _rms_bwd(ysub, gp, dxo), "post_bwd_last", t,
                                [_full(top["dn"]), _full(dx)], [(p["g_post_ffn"], depth - 1)], [(d, ACT_DTYPE)],
                                acc_out=[(1, d)])
    grads = [None] * depth
    landed = [dict() for _ in range(depth)]
    post_ffn = [None] * depth
    post_ffn[depth - 1] = g_post_top
    pending = []
    for l in reversed(range(depth)):
        prev_dn = saved[l - 1]["dn"] if l > 0 else None
        grads[l], dx, d_dn, g_post_prev, (got_above, got_own) = _backward_layer(
            l, saved[l], dx, d_dn, memf, weights[l], p, cfg, nb, prev_dn, pending)
        if pending:
            landed[l + 1]["w_in"] = got_above[0]
        landed[l].update(zip([n for n in _BIG if n != "w_in"], got_own))
        pending = [_shard(_unpermute_in(grads[l]["w_in"], cfg), True).astype(WIRE_DTYPE)]
        if l > 0:
            post_ffn[l - 1] = g_post_prev
    landed[0]["w_in"] = _scatter_blocks(pending, "scatter_grads_last")[0]
    for l in range(depth):
        grads[l]["g_post_ffn"] = post_ffn[l]
    grad_x = dx.reshape(x.shape)

    out = {}
    for n in _BIG:
        parts = jnp.stack([landed[l][n] for l in range(depth)], axis=1)
        out[n] = _adamw(parts, vals[n], vals["m_" + n], vals["v_" + n], "adamw_" + n)

    small_shapes = [vals[n].shape for n in _SMALL]
    pack_g = _pack([grads[l][n] for n in _SMALL for l in range(depth)])
    conv_g = jnp.concatenate([grads[l]["conv_w"] for l in range(depth)], axis=0)
    parts_small, parts_conv = _all_gather([pack_g, conv_g], "ag_small_grads")
    packed = lambda prefix: _pack([vals[prefix + n] for n in _SMALL])[None]
    res = _adamw(parts_small[:, None], packed(""), packed("m_"), packed("v_"), "adamw_small")
    unpacked = [_unpack(r, small_shapes) for r in res]
    for i, n in enumerate(_SMALL):
        out[n] = [unpacked[j][i] for j in range(4)]
    cs = conv_w.shape[2]
    me = _flat_index(lax.axis_index("x"), lax.axis_index("y"), lax.axis_index("c"))
    parts_conv = lax.dynamic_slice_in_dim(parts_conv, me * cs, cs, axis=2)
    flat = lambda a: a.reshape(1, depth * SSM_CONV, cs)
    res = _adamw(parts_conv[:, None], flat(conv_w), flat(m_conv_w), flat(v_conv_w), "adamw_conv_w")
    out["conv_w"] = [r.reshape(conv_w.shape) for r in res]

    return (loss, grad_x, *[out[n][0] for n in _WEIGHTS], *[out[n][1] for n in _WEIGHTS],
            *[out[n][2] for n in _WEIGHTS], *[out[n][3] for n in _WEIGHTS])
```
